```python
import math
import jax, jax.numpy as jnp
from jax import lax
import numpy as np

D_MODEL = 1024
BATCH = 4
SEQ = 4096
DEPTH = 1

MEM_LEN = 256
RWKV_HEAD_DIM = 64
RWKV_WIDTH = D_MODEL // 2
RWKV_HEADS = RWKV_WIDTH // RWKV_HEAD_DIM
DECAY_LORA = 64
ICLR_LORA = 64
GATE_LORA = 160
RWKV_IN = 3 * RWKV_WIDTH + DECAY_LORA + ICLR_LORA + GATE_LORA
RWKV_GN_EPS = 64e-5
DIFF_HEAD_DIM = 64
DIFF_WIDTH = D_MODEL - RWKV_WIDTH
DIFF_HEADS = DIFF_WIDTH // (2 * DIFF_HEAD_DIM)
DIFF_IN = 3 * DIFF_WIDTH
MIX_IN = RWKV_IN + DIFF_IN
Q_BLOCK = 128
CROSS_HEADS = 4
CROSS_HEAD_DIM = D_MODEL // CROSS_HEADS
N_EXPERTS = 32
TOP_K = 4
D_FF_EXPERT = D_MODEL
SWIGLU_LIMIT = 7.0
SWIGLU_ALPHA = 1.702
MOE_BLOCK = 128
NORM_EPS = 1e-5

kernel_name = 'hymba_rwkv7_diffattn_moe_layer'


def rms_norm(x, g, eps=NORM_EPS):
    xf = x.astype(jnp.float32)
    y = xf * lax.rsqrt(jnp.mean(xf * xf, axis=-1, keepdims=True) + eps)
    return (y * g.astype(jnp.float32)).astype(x.dtype)


def token_shift(y, mu):
    y_prev = jnp.pad(y, ((0, 0), (1, 0), (0, 0)))[:, :-1]
    return y + mu * (y_prev - y)


def alibi_slopes(n):
    return jnp.exp2(-8.0 * jnp.arange(1, n + 1, dtype=jnp.float32) / n)


def diff_lambda_init(layer):
    return 0.8 - 0.6 * math.exp(-0.3 * layer)


def rwkv7_group(u, w0, w_decay_up, a0, w_iclr_up, w_gate_up, k_k, k_a, r_k, lnx_g, lnx_b):
    f32 = jnp.float32
    B, S, _ = u.shape
    H, N, C = RWKV_HEADS, RWKV_HEAD_DIM, RWKV_WIDTH
    r, k, v, xw, xa, xg = jnp.split(
        u, [C, 2 * C, 3 * C, 3 * C + DECAY_LORA, 3 * C + DECAY_LORA + ICLR_LORA], axis=-1)
    w_log = -jax.nn.softplus(-(w0 + jnp.tanh(xw) @ w_decay_up)) - 0.5
    decay = jnp.exp(-jnp.exp(w_log.astype(f32)))
    a = jax.nn.sigmoid(a0 + xa @ w_iclr_up)
    g = jax.nn.sigmoid(xg) @ w_gate_up
    heads = lambda t: t.reshape(B, S, H, N)
    kk = heads(k * k_k).astype(f32)
    kk = kk / jnp.maximum(jnp.linalg.norm(kk, axis=-1, keepdims=True), 1e-12)
    k = k * (1.0 + (a - 1.0) * k_a)
    r_h, k_h, v_h, a_h = heads(r), heads(k), heads(v), heads(a)
    seq_first = lambda t: jnp.moveaxis(t.astype(f32), 1, 0)

    def step(state, inp):
        r_t, w_t, k_t, v_t, kk_t, a_t = inp
        sa = jnp.einsum('bhvk,bhk->bhv', state, -kk_t)
        state = (state * w_t[:, :, None, :]
                 + sa[..., None] * (kk_t * a_t)[:, :, None, :]
                 + v_t[..., None] * k_t[:, :, None, :])
        return state, jnp.einsum('bhvk,bhk->bhv', state, r_t)

    state0 = jnp.zeros((B, H, N, N), f32)
    scan_in = (seq_first(r_h), seq_first(heads(decay)), seq_first(k_h),
               seq_first(v_h), seq_first(kk), seq_first(a_h))
    _, y = lax.scan(step, state0, scan_in)
    y = jnp.moveaxis(y, 0, 1)
    mu = jnp.mean(y, axis=-1, keepdims=True)
    var = jnp.mean(jnp.square(y - mu), axis=-1, keepdims=True)
    y = ((y - mu) * lax.rsqrt(var + RWKV_GN_EPS)).reshape(B, S, C)
    y = y * lnx_g.astype(f32) + lnx_b.astype(f32)
    bonus = jnp.sum((r_h * k_h * r_k).astype(f32), axis=-1, keepdims=True) * v_h.astype(f32)
    y = (y + bonus.reshape(B, S, C)) * g.astype(f32)
    return y.astype(u.dtype)


def diff_attention_group(u, lambda_q1, lambda_k1, lambda_q2, lambda_k2, subln_g, lambda_init):
    f32 = jnp.float32
    B, S, _ = u.shape
    H, d = DIFF_HEADS, DIFF_HEAD_DIM
    q, k, v = jnp.split(u, 3, axis=-1)
    q = q.reshape(B, S, H, 2, d) * (d ** -0.5)
    k = k.reshape(B, S, H, 2, d)
    v = v.reshape(B, S, H, 2 * d)
    lam = (jnp.exp(jnp.sum(lambda_q1 * lambda_k1).astype(f32))
           - jnp.exp(jnp.sum(lambda_q2 * lambda_k2).astype(f32)) + lambda_init)
    slopes = alibi_slopes(H)
    n_blocks = S // Q_BLOCK
    q_blocks = jnp.moveaxis(q.reshape(B, n_blocks, Q_BLOCK, H, 2, d), 1, 0)
    key_pos = jnp.arange(S)

    def block(args):
        qb, bi = args
        q_pos = bi * Q_BLOCK + jnp.arange(Q_BLOCK)
        dist = (q_pos[:, None] - key_pos[None, :]).astype(f32)
        bias = jnp.where(dist[None] >= 0, -slopes[:, None, None] * dist[None], -jnp.inf)
        s = jnp.einsum('bqhcd,bkhcd->bhcqk', qb, k).astype(f32) + bias[None, :, None]
        p = jax.nn.softmax(s, axis=-1)
        attn = p[:, :, 0] - lam * p[:, :, 1]
        return jnp.einsum('bhqk,bkhe->bqhe', attn.astype(v.dtype), v)

    o = lax.map(block, (q_blocks, jnp.arange(n_blocks)))
    o = jnp.moveaxis(o, 0, 1).reshape(B, S, H, 2 * d)
    o = rms_norm(o, subln_g) * (1.0 - lambda_init)
    return o.reshape(B, S, DIFF_WIDTH)


def cross_attention(h, m, w_cq, w_ckv, w_co):
    B, S, D = h.shape
    M = m.shape[1]
    q = (h @ w_cq).reshape(B, S, CROSS_HEADS, CROSS_HEAD_DIM)
    k, v = jnp.split(m @ w_ckv, 2, axis=-1)
    k = k.reshape(B, M, CROSS_HEADS, CROSS_HEAD_DIM)
    v = v.reshape(B, M, CROSS_HEADS, CROSS_HEAD_DIM)
    s = jnp.einsum('bqhd,bkhd->bhqk', q, k).astype(jnp.float32) * (CROSS_HEAD_DIM ** -0.5)
    p = jax.nn.softmax(s, axis=-1).astype(v.dtype)
    o = jnp.einsum('bhqk,bkhd->bqhd', p, v).reshape(B, S, D)
    return o @ w_co


def clamped_swiglu(hid):
    x_glu = jnp.minimum(hid[..., ::2], SWIGLU_LIMIT)
    x_lin = jnp.clip(hid[..., 1::2], -SWIGLU_LIMIT, SWIGLU_LIMIT)
    return x_glu * jax.nn.sigmoid(SWIGLU_ALPHA * x_glu) * (x_lin + 1.0)


def moe_ffn(h, w_router, b_router, w1, b1, w2, b2):
    B, S, D = h.shape
    T = B * S
    n_assign = T * TOP_K
    n_blocks = n_assign // MOE_BLOCK + N_EXPERTS
    n_rows = n_blocks * MOE_BLOCK
    xt = h.reshape(T, D)
    logits = (xt @ w_router + b_router).astype(jnp.float32)
    top_vals, top_idx = lax.top_k(logits, TOP_K)
    gates = jax.nn.softmax(top_vals, axis=-1)
    flat_e = top_idx.reshape(-1)
    flat_tok = jnp.repeat(jnp.arange(T, dtype=jnp.int32), TOP_K)
    flat_gate = gates.reshape(-1)
    order = jnp.argsort(flat_e)
    sorted_e = flat_e[order]
    counts = jnp.bincount(flat_e, length=N_EXPERTS)
    padded = (counts + MOE_BLOCK - 1) // MOE_BLOCK * MOE_BLOCK
    start = jnp.cumsum(counts) - counts
    pad_end = jnp.cumsum(padded)
    pad_start = pad_end - padded
    dest = pad_start[sorted_e] + jnp.arange(n_assign, dtype=jnp.int32) - start[sorted_e]
    row_tok = jnp.full((n_rows,), T, jnp.int32).at[dest].set(flat_tok[order])
    row_gate = jnp.zeros((n_rows,), jnp.float32).at[dest].set(flat_gate[order])
    block_e = jnp.minimum(
        jnp.searchsorted(pad_end, jnp.arange(n_blocks, dtype=jnp.int32) * MOE_BLOCK, side='right'),
        N_EXPERTS - 1)
    x_pad = jnp.concatenate([xt, jnp.zeros((1, D), xt.dtype)], axis=0)

    def expert_block(args):
        tok, e = args
        hid = x_pad[tok] @ w1[e] + b1[e]
        return clamped_swiglu(hid) @ w2[e] + b2[e]

    y = lax.map(expert_block, (row_tok.reshape(n_blocks, MOE_BLOCK), block_e))
    y = y.reshape(n_rows, D) * row_gate[:, None].astype(y.dtype)
    out = jax.ops.segment_sum(y, row_tok, num_segments=T + 1)[:T]
    return out.reshape(B, S, D)


def setup_inputs(seed: int = 0) -> dict:
    key = jax.random.key(seed)
    ks = iter(jax.random.split(key, 48))
    f32 = jnp.float32
    L, D, E, F = DEPTH, D_MODEL, N_EXPERTS, D_FF_EXPERT
    C, d = RWKV_WIDTH, DIFF_HEAD_DIM

    def normal(shape, scale):
        return scale * jax.random.normal(next(ks), shape, f32)

    def gain(shape):
        return 1.0 + 0.02 * jax.random.normal(next(ks), shape, f32)

    def uniform(shape, lo, hi):
        return jax.random.uniform(next(ks), shape, f32, lo, hi)

    return {
        'x': normal((BATCH, SEQ, D), 1.0),
        'mem': normal((BATCH, MEM_LEN, D), 1.0),
        'norm_mix_g': gain((L, D)),
        'w_in': normal((L, D, MIX_IN), D ** -0.5),
        'shift_mu': uniform((L, RWKV_IN), 0.0, 1.0),
        'w0': uniform((L, C), -4.0, 1.0),
        'w_decay_up': normal((L, DECAY_LORA, C), 0.5 * DECAY_LORA ** -0.5),
        'a0': normal((L, C), 0.1),
        'w_iclr_up': normal((L, ICLR_LORA, C), ICLR_LORA ** -0.5),
        'w_gate_up': normal((L, GATE_LORA, C), GATE_LORA ** -0.5),
        'k_k': 0.85 + normal((L, C), 0.02),
        'k_a': gain((L, C)),
        'r_k': normal((L, RWKV_HEADS, RWKV_HEAD_DIM), 0.1),
        'lnx_g': gain((L, C)),
        'lnx_b': normal((L, C), 0.01),
        'lambda_q1': normal((L, d), 0.1),
        'lambda_k1': normal((L, d), 0.1),
        'lambda_q2': normal((L, d), 0.1),
        'lambda_k2': normal((L, d), 0.1),
        'subln_g': gain((L, 2 * d)),
        'w_out': normal((L, D, D), D ** -0.5),
        'norm_cross_g': gain((L, D)),
        'norm_mem_g': gain((L, D)),
        'w_cq': normal((L, D, D), D ** -0.5),
        'w_ckv': normal((L, D, 2 * D), D ** -0.5),
        'w_co': normal((L, D, D), D ** -0.5),
        'norm_ffn_g': gain((L, D)),
        'w_router': normal((L, D, E), D ** -0.5),
        'b_router': normal((L, E), 0.01),
        'w1': normal((L, E, D, 2 * F), D ** -0.5),
        'b1': normal((L, E, 2 * F), 0.01),
        'w2': normal((L, E, F, D), F ** -0.5),
        'b2': normal((L, E, D), 0.01),
        'norm_final_g': gain((D,)),
    }


def reference(x, mem, norm_mix_g, w_in, shift_mu, w0, w_decay_up, a0, w_iclr_up, w_gate_up,
              k_k, k_a, r_k, lnx_g, lnx_b, lambda_q1, lambda_k1, lambda_q2, lambda_k2, subln_g,
              w_out, norm_cross_g, norm_mem_g, w_cq, w_ckv, w_co, norm_ffn_g, w_router, b_router,
              w1, b1, w2, b2, norm_final_g):
    for l in range(DEPTH):
        h = rms_norm(x, norm_mix_g[l])
        proj = h @ w_in[l]
        u_rwkv = token_shift(proj[..., :RWKV_IN], shift_mu[l])
        u_diff = proj[..., RWKV_IN:]
        y_rwkv = rwkv7_group(u_rwkv, w0[l], w_decay_up[l], a0[l], w_iclr_up[l], w_gate_up[l],
                             k_k[l], k_a[l], r_k[l], lnx_g[l], lnx_b[l])
        y_diff = diff_attention_group(u_diff, lambda_q1[l], lambda_k1[l], lambda_q2[l],
                                      lambda_k2[l], subln_g[l], diff_lambda_init(l))
        x = x + jnp.concatenate([y_rwkv, y_diff], axis=-1) @ w_out[l]
        x = x + cross_attention(rms_norm(x, norm_cross_g[l]), rms_norm(mem, norm_mem_g[l]),
                                w_cq[l], w_ckv[l], w_co[l])
        x = x + moe_ffn(rms_norm(x, norm_ffn_g[l]), w_router[l], b_router[l],
                        w1[l], b1[l], w2[l], b2[l])
    return rms_norm(x, norm_final_g)
```

```python
import functools
import math

import jax
import jax.numpy as jnp
from jax import lax
from jax.experimental import pallas as pl
from jax.experimental.pallas import tpu as pltpu

F32 = jnp.float32
BF16 = jnp.bfloat16
I32 = jnp.int32

NORM_EPS = 1e-5
HEAD = 64
RWKV_W = 512
RWKV_HEADS = RWKV_W // HEAD
DECAY_LORA, ICLR_LORA, GATE_LORA = 64, 64, 160
LORA_W = DECAY_LORA + ICLR_LORA + GATE_LORA
LORA_PAD = 384
RWKV_GN_EPS = 64e-5
DIFF_W = 512
DIFF_HEADS = DIFF_W // (2 * HEAD)
CROSS_HEADS = 4
N_EXPERTS = 32
TOP_K = 4
SWIGLU_LIMIT = 7.0
SWIGLU_ALPHA = 1.702
LANES = 128
NEG_BIG = -1e30

RWKV_CHUNK = 64
PROJ_TM, PROJ_TN = 512, 1152
ATT_BLOCK = 512
MID_TM = 256
MOE_ROWS = 256
VMEM_LIMIT = 56 * 1024 * 1024


def _dot(a, b):
    return jnp.dot(a.astype(BF16), b.astype(BF16), preferred_element_type=F32)


def _dot_nt(a, b):
    return lax.dot_general(a.astype(BF16), b.astype(BF16), (((1,), (1,)), ((), ())),
                           preferred_element_type=F32)


def _dot_tn(a, b):
    return lax.dot_general(a.astype(BF16), b.astype(BF16), (((0,), (0,)), ((), ())),
                           preferred_element_type=F32)


def _split3(x):
    hi = x.astype(BF16)
    r1 = x - hi.astype(F32)
    mid = r1.astype(BF16)
    lo = (r1 - mid.astype(F32)).astype(BF16)
    return hi, mid, lo


def _dot_exact_rhs(x, ones_bf16):
    hi, mid, lo = _split3(x)
    return (jnp.dot(hi, ones_bf16, preferred_element_type=F32)
            + jnp.dot(mid, ones_bf16, preferred_element_type=F32)
            + jnp.dot(lo, ones_bf16, preferred_element_type=F32))


def _rms(x, g):
    return x * lax.rsqrt(jnp.mean(x * x, axis=-1, keepdims=True) + NORM_EPS) * g


def _params(*sem):
    return pltpu.CompilerParams(dimension_semantics=sem, vmem_limit_bytes=VMEM_LIMIT)


def _norm_inproj_kernel(x_ref, g_ref, w_ref, o_ref, h_scr):
    @pl.when(pl.program_id(1) == 0)
    def _():
        h_scr[...] = _rms(x_ref[...], g_ref[...]).astype(BF16)

    o_ref[...] = jnp.dot(h_scr[...], w_ref[...], preferred_element_type=F32)


def _norm_inproj(x2d, g, w_bf16):
    t, d = x2d.shape
    n = w_bf16.shape[1]
    tm, tn = min(PROJ_TM, t), PROJ_TN
    return pl.pallas_call(
        _norm_inproj_kernel,
        grid=(t // tm, n // tn),
        in_specs=[pl.BlockSpec((tm, d), lambda i, j: (i, 0)),
                  pl.BlockSpec((1, d), lambda i, j: (0, 0)),
                  pl.BlockSpec((d, tn), lambda i, j: (0, j))],
        out_specs=pl.BlockSpec((tm, tn), lambda i, j: (i, j)),
        out_shape=jax.ShapeDtypeStruct((t, n), F32),
        scratch_shapes=[pltpu.VMEM((tm, d), BF16)],
        compiler_params=_params("parallel", "arbitrary"),
        name="norm_inproj",
    )(x2d, g, w_bf16)


def _rwkv_kernel(rkv_ref, lora_ref, mu_rkv_ref, mu_l_ref, wl_ref, vec_ref, bd_ref, tri_ref,
                 o_ref, st_ref, prev_rkv_ref, prev_l_ref):
    L = RWKV_CHUNK
    C = RWKV_W

    @pl.when(pl.program_id(1) == 0)
    def _():
        st_ref[...] = jnp.zeros_like(st_ref)
        prev_rkv_ref[...] = jnp.zeros_like(prev_rkv_ref)
        prev_l_ref[...] = jnp.zeros_like(prev_l_ref)

    row = lax.broadcasted_iota(I32, (L, 1), 0)

    def token_shift(raw, prev_ref, mu):
        prev = jnp.where(row == 0, prev_ref[...], pltpu.roll(raw, 1, axis=0))
        prev_ref[...] = raw[L - 1:L, :]
        return raw + mu * (prev - raw)

    u = token_shift(rkv_ref[...], prev_rkv_ref, mu_rkv_ref[...])
    ul = token_shift(lora_ref[...], prev_l_ref, mu_l_ref[...])
    r, k, v = u[:, :C], u[:, C:2 * C], u[:, 2 * C:]

    lane_l = lax.broadcasted_iota(I32, (L, LORA_PAD), 1)
    act = jnp.where(lane_l < DECAY_LORA, jnp.tanh(ul),
                    jnp.where(lane_l < DECAY_LORA + ICLR_LORA, ul, jax.nn.sigmoid(ul)))
    lo = _dot(act, wl_ref[...])
    w0, a0, k_k, k_a = vec_ref[0:1, :], vec_ref[1:2, :], vec_ref[2:3, :], vec_ref[3:4, :]
    r_k, lnx_g, lnx_b = vec_ref[4:5, :], vec_ref[5:6, :], vec_ref[6:7, :]

    w_log = -jax.nn.softplus(-(w0 + lo[:, :C])) - 0.5
    lw = -jnp.exp(w_log)
    a = jax.nn.sigmoid(a0 + lo[:, C:2 * C])
    g = lo[:, 2 * C:]
    bd = bd_ref[...]
    kk = k * k_k
    kk = kk / jnp.maximum(jnp.sqrt(_dot_exact_rhs(kk * kk, bd)), 1e-12)
    k2 = k * (1.0 + (a - 1.0) * k_a)

    tri = tri_ref[...]
    hi, mid, lo3 = _split3(lw)
    cum = (jnp.dot(tri, hi, preferred_element_type=F32) + jnp.dot(tri, mid, preferred_element_type=F32)
           + jnp.dot(tri, lo3, preferred_element_type=F32))
    total = cum[L - 1:L, :]
    e_neg = jnp.exp(-cum)
    e_rem = jnp.exp(total - cum)
    kka = kk * a
    a_t = -kk * jnp.exp(cum - lw)
    b_t = kka * e_neg
    k_t = k2 * e_neg
    r_t = r * jnp.exp(cum)
    b_bar = kka * e_rem
    k_bar = k2 * e_rem
    p_total = jnp.exp(total)

    m0 = lax.broadcasted_iota(I32, (L, LANES), 1) < HEAD
    r2i = lax.broadcasted_iota(I32, (2 * L, 2 * L), 0)
    c2i = lax.broadcasted_iota(I32, (2 * L, 2 * L), 1)
    strict, incl = c2i < r2i, c2i <= r2i

    def stack2(xp):
        return jnp.concatenate([jnp.where(m0, xp, 0.0), jnp.where(m0, 0.0, xp)], axis=0)

    ys = []
    for p in range(RWKV_HEADS // 2):
        sl = slice(p * LANES, (p + 1) * LANES)
        A2, B2, K2, R2 = stack2(a_t[:, sl]), stack2(b_t[:, sl]), stack2(k_t[:, sl]), stack2(r_t[:, sl])
        V2, Bb2, Kb2 = stack2(v[:, sl]), stack2(b_bar[:, sl]), stack2(k_bar[:, sl])
        m_ab = jnp.where(strict, _dot_nt(A2, B2), 0.0)
        m_ak = jnp.where(strict, _dot_nt(A2, K2), 0.0)
        g_b = jnp.where(incl, _dot_nt(R2, B2), 0.0)
        g_k = jnp.where(incl, _dot_nt(R2, K2), 0.0)
        tr, pw = m_ab, m_ab
        for _ in range(int(math.log2(L)) - 1):
            pw = _dot(pw, pw)
            tr = tr + pw + _dot(pw, tr)
        az = jnp.concatenate([A2, _dot(m_ak, V2)], axis=1)
        tz = az + _dot(tr, az)
        a_hat, u_loc = tz[:, :LANES], tz[:, LANES:]
        st = st_ref[p]
        u2 = _dot_nt(a_hat, st) + u_loc
        y2 = _dot_nt(R2, st) + _dot(g_b, u2) + _dot(g_k, V2)
        st_ref[p] = st * p_total[:, sl] + _dot_tn(u2, Bb2) + _dot_tn(V2, Kb2)
        ys.append(y2[:L] + y2[L:])
    y = jnp.concatenate(ys, axis=1)

    inv_n = 1.0 / HEAD
    mu = _dot_exact_rhs(y, bd) * inv_n
    d = y - mu
    var = _dot_exact_rhs(d * d, bd) * inv_n
    yn = d * lax.rsqrt(var + RWKV_GN_EPS) * lnx_g + lnx_b
    bonus = _dot_exact_rhs(r * k2 * r_k, bd) * v
    o_ref[...] = (yn + bonus) * g


def _rwkv_group(proj3, mu_rkv, mu_l, wl, vec, rkv_block, lora_block):
    b, s, _ = proj3.shape
    L, C = RWKV_CHUNK, RWKV_W
    head = jnp.arange(C, dtype=I32) // HEAD
    bd = (head[:, None] == head[None, :]).astype(BF16)
    t = jnp.arange(L, dtype=I32)
    tri = (t[None, :] <= t[:, None]).astype(BF16)
    const = lambda shape: pl.BlockSpec(shape, lambda i, j: (0,) * len(shape))
    return pl.pallas_call(
        _rwkv_kernel,
        grid=(b, s // L),
        in_specs=[pl.BlockSpec((None, L, 3 * C), lambda i, j: (i, j, rkv_block)),
                  pl.BlockSpec((None, L, LORA_PAD), lambda i, j: (i, j, lora_block)),
                  const((1, 3 * C)), const((1, LORA_PAD)), const((LORA_PAD, 3 * C)),
                  const((8, C)), const((C, C)), const((L, L))],
        out_specs=pl.BlockSpec((None, L, C), lambda i, j: (i, j, 0)),
        out_shape=jax.ShapeDtypeStruct((b, s, C), F32),
        scratch_shapes=[pltpu.VMEM((RWKV_HEADS // 2, LANES, LANES), F32),
                        pltpu.VMEM((1, 3 * C), F32), pltpu.VMEM((1, LORA_PAD), F32)],
        compiler_params=_params("parallel", "arbitrary"),
        name="rwkv_group",
    )(proj3, proj3, mu_rkv, mu_l, wl, vec, bd, tri)


def _diff_attn_kernel(q_ref, k_ref, v_ref, slope_ref, lam_ref, g_ref, o_ref,
                      q2_scr, m_scr, l_scr, acc_scr, *, lambda_init):
    blk = ATT_BLOCK
    qi, ki = pl.program_id(2), pl.program_id(3)

    @pl.when(ki == 0)
    def _():
        q = q_ref[...] * (HEAD ** -0.5)
        m0 = lax.broadcasted_iota(I32, (blk, LANES), 1) < HEAD
        q2_scr[0:blk, :] = jnp.where(m0, q, 0.0).astype(BF16)
        q2_scr[blk:, :] = jnp.where(m0, 0.0, q).astype(BF16)
        m_scr[...] = jnp.full_like(m_scr, NEG_BIG)
        l_scr[...] = jnp.zeros_like(l_scr)
        acc_scr[...] = jnp.zeros_like(acc_scr)

    def step(on_diagonal):
        s = _dot_nt(q2_scr[...], k_ref[...])
        koff = lax.broadcasted_iota(I32, (1, blk), 1)
        s = s + slope_ref[:, :1] * ((ki - qi) * blk + koff).astype(F32)
        if on_diagonal:
            qoff = lax.broadcasted_iota(I32, (2 * blk, 1), 0) % blk
            s = jnp.where(koff <= qoff, s, NEG_BIG)
        m_prev = m_scr[...]
        m_new = jnp.maximum(m_prev, jnp.max(s, axis=-1, keepdims=True))
        alpha = jnp.exp(m_prev - m_new)
        p = jnp.exp(s - m_new)
        l_scr[...] = alpha * l_scr[...] + jnp.sum(p, axis=-1, keepdims=True)
        acc_scr[...] = alpha * acc_scr[...] + _dot(p, v_ref[...])
        m_scr[...] = m_new

    @pl.when(ki < qi)
    def _():
        step(False)

    @pl.when(ki == qi)
    def _():
        step(True)
        lam_v = lam_ref[...]
        lam = (jnp.exp(jnp.sum(lam_v[0:1] * lam_v[1:2], axis=-1, keepdims=True))
               - jnp.exp(jnp.sum(lam_v[2:3] * lam_v[3:4], axis=-1, keepdims=True)) + lambda_init)
        o2 = acc_scr[...] / l_scr[...]
        o = o2[:blk] - lam * o2[blk:]
        o_ref[...] = _rms(o, g_ref[...]) * (1.0 - lambda_init)


def _diff_attention(proj3, lam_vecs, subln_g, lambda_init):
    b, s, _ = proj3.shape
    blk = ATT_BLOCK
    nb = s // blk
    nh = DIFF_HEADS
    slopes = jnp.exp2(-8.0 * jnp.arange(1, nh + 1, dtype=F32) / nh)
    slopes = jnp.broadcast_to(slopes[:, None, None], (nh, 1, LANES))
    kernel = functools.partial(_diff_attn_kernel, lambda_init=lambda_init)
    return pl.pallas_call(
        kernel,
        grid=(b, nh, nb, nb),
        in_specs=[pl.BlockSpec((None, blk, LANES), lambda bi, h, qi, ki: (bi, qi, h)),
                  pl.BlockSpec((None, blk, LANES), lambda bi, h, qi, ki: (bi, jnp.minimum(ki, qi), nh + h)),
                  pl.BlockSpec((None, blk, LANES), lambda bi, h, qi, ki: (bi, jnp.minimum(ki, qi), 2 * nh + h)),
                  pl.BlockSpec((None, 1, LANES), lambda bi, h, qi, ki: (h, 0, 0)),
                  pl.BlockSpec((4, HEAD), lambda bi, h, qi, ki: (0, 0)),
                  pl.BlockSpec((1, 2 * HEAD), lambda bi, h, qi, ki: (0, 0))],
        out_specs=pl.BlockSpec((None, blk, LANES), lambda bi, h, qi, ki: (bi, qi, h)),
        out_shape=jax.ShapeDtypeStruct((b, s, DIFF_W), F32),
        scratch_shapes=[pltpu.VMEM((2 * blk, LANES), BF16), pltpu.VMEM((2 * blk, 1), F32),
                        pltpu.VMEM((2 * blk, 1), F32), pltpu.VMEM((2 * blk, LANES), F32)],
        compiler_params=_params("parallel", "parallel", "parallel", "arbitrary"),
        name="diff_attention",
    )(proj3, proj3, proj3, slopes, lam_vecs, subln_g)


def _mem_kv_kernel(m_ref, g_ref, w_ref, k_ref, v_ref):
    d = m_ref.shape[-1]
    kv = _dot(_rms(m_ref[...], g_ref[...]), w_ref[...])
    k_ref[...] = kv[:, :d].astype(BF16)
    v_ref[...] = kv[:, d:].astype(BF16)


def _mem_kv(mem, g, w_ckv_bf16):
    b, m, d = mem.shape
    return pl.pallas_call(
        _mem_kv_kernel,
        grid=(b,),
        in_specs=[pl.BlockSpec((None, m, d), lambda i: (i, 0, 0)),
                  pl.BlockSpec((1, d), lambda i: (0, 0)),
                  pl.BlockSpec((d, 2 * d), lambda i: (0, 0))],
        out_specs=[pl.BlockSpec((None, m, d), lambda i: (i, 0, 0))] * 2,
        out_shape=[jax.ShapeDtypeStruct((b, m, d), BF16)] * 2,
        compiler_params=_params("parallel"),
        name="mem_kv",
    )(mem, g, w_ckv_bf16)


def _mid_kernel(x_ref, yr_ref, yd_ref, wo_ref, gc_ref, wcq_ref, km_ref, vm_ref, wco_ref, gf_ref,
                wr_hi_ref, wr_lo_ref, br_ref, tri_ref,
                x2_ref, hn_ref, ri_ref, rg_ref, cnt_ref, carry_scr):
    tm, d = x_ref.shape
    first = jnp.logical_and(pl.program_id(0) == 0, pl.program_id(1) == 0)

    @pl.when(first)
    def _():
        carry_scr[...] = jnp.zeros_like(carry_scr)

    half = yr_ref.shape[-1]
    x1 = x_ref[...] + _dot(yr_ref[...], wo_ref[:half, :]) + _dot(yd_ref[...], wo_ref[half:, :])

    q = _dot(_rms(x1, gc_ref[...]), wcq_ref[...])
    hd = d // CROSS_HEADS
    outs = []
    for h in range(CROSS_HEADS):
        sl = slice(h * hd, (h + 1) * hd)
        s = _dot_nt(q[:, sl], km_ref[:, sl]) * (hd ** -0.5)
        s = s - jnp.max(s, axis=-1, keepdims=True)
        p = jnp.exp(s)
        p = p / jnp.sum(p, axis=-1, keepdims=True)
        outs.append(_dot(p, vm_ref[:, sl]))
    x2 = x1 + _dot(jnp.concatenate(outs, axis=1), wco_ref[...])
    x2_ref[...] = x2

    hn = _rms(x2, gf_ref[...])
    hn_ref[...] = hn.astype(BF16)
    hi = hn.astype(BF16)
    lo = (hn - hi.astype(F32)).astype(BF16)
    logits = (jnp.dot(hi, wr_hi_ref[...], preferred_element_type=F32)
              + jnp.dot(lo, wr_hi_ref[...], preferred_element_type=F32)
              + jnp.dot(hi, wr_lo_ref[...], preferred_element_type=F32) + br_ref[...])
    lane = lax.broadcasted_iota(I32, (tm, LANES), 1)
    vals = logits
    tops, idxs, hots = [], [], []
    for _ in range(TOP_K):
        mx = jnp.max(vals, axis=-1, keepdims=True)
        idx = jnp.min(jnp.where(vals == mx, lane, LANES), axis=-1, keepdims=True)
        hot = lane == idx
        vals = jnp.where(hot, -jnp.inf, vals)
        tops.append(mx)
        idxs.append(idx)
        hots.append(hot)
    es = [jnp.exp(t - tops[0]) for t in tops]
    denom = es[0] + es[1] + es[2] + es[3]
    sel = jnp.zeros((tm, LANES), F32)
    for hot in hots:
        sel = sel + hot.astype(F32)
    before = jnp.dot(tri_ref[...], sel.astype(BF16), preferred_element_type=F32) + carry_scr[...]
    carry_scr[...] = carry_scr[...] + jnp.sum(sel, axis=0, keepdims=True)
    ri = jnp.zeros((tm, LANES), I32)
    rg = jnp.zeros((tm, LANES), F32)
    for j in range(TOP_K):
        rank = jnp.sum(jnp.where(hots[j], before, 0.0), axis=-1, keepdims=True).astype(I32)
        ri = jnp.where(lane == j, idxs[j], ri)
        ri = jnp.where(lane == TOP_K + j, rank, ri)
        rg = jnp.where(lane == j, es[j] / denom, rg)
    ri_ref[...] = ri
    rg_ref[...] = rg
    cnt_ref[...] = jnp.broadcast_to(carry_scr[...], cnt_ref.shape)


def _mid_stage(x, y_rwkv, y_diff, w_out, g_cross, w_cq, k_mem, v_mem, w_co, g_ffn, w_router, b_router):
    b, s, d = x.shape
    tm = min(MID_TM, s)
    m = k_mem.shape[1]
    half = y_rwkv.shape[-1]
    e = w_router.shape[1]
    wr = jnp.zeros((d, LANES), F32).at[:, :e].set(w_router)
    wr_hi = wr.astype(BF16)
    wr_lo = (wr - wr_hi.astype(F32)).astype(BF16)
    br = jnp.full((1, LANES), NEG_BIG, F32).at[0, :e].set(b_router)
    t = jnp.arange(tm, dtype=I32)
    tri = (t[None, :] < t[:, None]).astype(BF16)
    tile = lambda w: pl.BlockSpec((None, tm, w), lambda i, j: (i, j, 0))
    const = lambda shape: pl.BlockSpec(shape, lambda i, j: (0,) * len(shape))
    return pl.pallas_call(
        _mid_kernel,
        grid=(b, s // tm),
        in_specs=[tile(d), tile(half), tile(half), const((d, d)), const((1, d)), const((d, d)),
                  pl.BlockSpec((None, m, d), lambda i, j: (i, 0, 0)),
                  pl.BlockSpec((None, m, d), lambda i, j: (i, 0, 0)),
                  const((d, d)), const((1, d)), const((d, LANES)), const((d, LANES)),
                  const((1, LANES)), const((tm, tm))],
        out_specs=[tile(d), tile(d), tile(LANES), tile(LANES), const((8, LANES))],
        out_shape=[jax.ShapeDtypeStruct((b, s, d), F32), jax.ShapeDtypeStruct((b, s, d), BF16),
                   jax.ShapeDtypeStruct((b, s, LANES), I32), jax.ShapeDtypeStruct((b, s, LANES), F32),
                   jax.ShapeDtypeStruct((8, LANES), F32)],
        scratch_shapes=[pltpu.VMEM((1, LANES), F32)],
        compiler_params=_params("arbitrary", "arbitrary"),
        name="outproj_cross_router",
    )(x, y_rwkv, y_diff, w_out.astype(BF16), g_cross, w_cq.astype(BF16), k_mem, v_mem,
      w_co.astype(BF16), g_ffn, wr_hi, wr_lo, br, tri)


def _expert_kernel(be_ref, nused_ref, x_ref, w1_ref, b1_ref, w2_ref, b2_ref, o_ref):
    i = pl.program_id(0)

    @pl.when(i < nused_ref[0])
    def _():
        hid = jnp.dot(x_ref[...], w1_ref[...], preferred_element_type=F32) + b1_ref[...]
        f2 = hid.shape[-1]
        lin = pltpu.roll(hid, f2 - 1, axis=1)
        glu = jnp.minimum(hid, SWIGLU_LIMIT)
        lin = jnp.clip(lin, -SWIGLU_LIMIT, SWIGLU_LIMIT)
        act = glu * jax.nn.sigmoid(SWIGLU_ALPHA * glu) * (lin + 1.0)
        f = f2 // 2
        even = lax.broadcasted_iota(I32, (act.shape[0], f), 1) % 2 == 0
        packed = jnp.where(even, act[:, :f], 0.0) + pltpu.roll(jnp.where(even, act[:, f:], 0.0), 1, axis=1)
        o_ref[...] = jnp.dot(packed.astype(BF16), w2_ref[...], preferred_element_type=F32) + b2_ref[...]

    @pl.when(i >= nused_ref[0])
    def _():
        o_ref[...] = jnp.zeros_like(o_ref)


def _expert_ffn(x_sorted, block_e, n_used, w1, b1, w2p, b2):
    n_rows, d = x_sorted.shape
    e, _, f2 = w1.shape
    r = MOE_ROWS
    grid_spec = pltpu.PrefetchScalarGridSpec(
        num_scalar_prefetch=2,
        grid=(n_rows // r,),
        in_specs=[pl.BlockSpec((r, d), lambda i, be, nu: (i, 0)),
                  pl.BlockSpec((None, d, f2), lambda i, be, nu: (be[i], 0, 0)),
                  pl.BlockSpec((None, 1, f2), lambda i, be, nu: (be[i], 0, 0)),
                  pl.BlockSpec((None, f2 // 2, d), lambda i, be, nu: (be[i], 0, 0)),
                  pl.BlockSpec((None, 1, d), lambda i, be, nu: (be[i], 0, 0))],
        out_specs=pl.BlockSpec((r, d), lambda i, be, nu: (i, 0)),
    )
    return pl.pallas_call(
        _expert_kernel,
        grid_spec=grid_spec,
        out_shape=jax.ShapeDtypeStruct((n_rows, d), F32),
        compiler_params=_params("arbitrary"),
        name="expert_ffn",
    )(block_e, n_used, x_sorted, w1, b1.reshape(e, 1, f2), w2p, b2.reshape(e, 1, d))


def _combine_kernel(x_ref, moe_ref, g_ref, o_ref, *, final):
    y = x_ref[...] + moe_ref[...]
    o_ref[...] = _rms(y, g_ref[...]) if final else y


def _combine(x2d, moe2d, g, final):
    t, d = x2d.shape
    tm = min(512, t)
    return pl.pallas_call(
        functools.partial(_combine_kernel, final=final),
        grid=(t // tm,),
        in_specs=[pl.BlockSpec((tm, d), lambda i: (i, 0)), pl.BlockSpec((tm, d), lambda i: (i, 0)),
                  pl.BlockSpec((1, d), lambda i: (0, 0))],
        out_specs=pl.BlockSpec((tm, d), lambda i: (i, 0)),
        out_shape=jax.ShapeDtypeStruct((t, d), F32),
        compiler_params=_params("parallel"),
        name="combine_norm",
    )(x2d, moe2d, g)


def _layer(x, mem, lyr, final, p):
    b, s, d = x.shape
    t = b * s
    c = RWKV_W
    rwkv_in = 3 * c + LORA_W

    w_in = p["w_in"][lyr]
    w_cat = jnp.concatenate([w_in[:, rwkv_in:], w_in[:, :3 * c], w_in[:, 3 * c:rwkv_in],
                             jnp.zeros((d, LORA_PAD - LORA_W), F32)], axis=1).astype(BF16)
    proj = _norm_inproj(x.reshape(t, d), p["norm_mix_g"][lyr][None], w_cat).reshape(b, s, -1)

    mu = p["shift_mu"][lyr]
    mu_l = jnp.zeros((1, LORA_PAD), F32).at[0, :LORA_W].set(mu[3 * c:])
    wl = jnp.zeros((LORA_PAD, 3 * c), F32)
    wl = wl.at[:DECAY_LORA, :c].set(p["w_decay_up"][lyr])
    wl = wl.at[DECAY_LORA:DECAY_LORA + ICLR_LORA, c:2 * c].set(p["w_iclr_up"][lyr])
    wl = wl.at[DECAY_LORA + ICLR_LORA:LORA_W, 2 * c:].set(p["w_gate_up"][lyr])
    vec = jnp.stack([p["w0"][lyr], p["a0"][lyr], p["k_k"][lyr], p["k_a"][lyr], p["r_k"][lyr].reshape(c),
                     p["lnx_g"][lyr], p["lnx_b"][lyr], jnp.zeros((c,), F32)])
    diff_cols = 3 * DIFF_W
    y_rwkv = _rwkv_group(proj, mu[None, :3 * c], mu_l, wl.astype(BF16), vec,
                         rkv_block=diff_cols // (3 * c), lora_block=(diff_cols + 3 * c) // LORA_PAD)

    lambda_init = 0.8 - 0.6 * math.exp(-0.3 * lyr)
    lam_vecs = jnp.stack([p["lambda_q1"][lyr], p["lambda_k1"][lyr], p["lambda_q2"][lyr], p["lambda_k2"][lyr]])
    y_diff = _diff_attention(proj, lam_vecs, p["subln_g"][lyr][None], lambda_init)

    k_mem, v_mem = _mem_kv(mem, p["norm_mem_g"][lyr][None], p["w_ckv"][lyr].astype(BF16))
    x2, hn, route_i, route_g, counts = _mid_stage(
        x, y_rwkv, y_diff, p["w_out"][lyr], p["norm_cross_g"][lyr][None], p["w_cq"][lyr], k_mem, v_mem,
        p["w_co"][lyr], p["norm_ffn_g"][lyr][None], p["w_router"][lyr], p["b_router"][lyr])

    e = N_EXPERTS
    r = MOE_ROWS
    n_blocks = (t * TOP_K) // r + e
    route_i = route_i.reshape(t, LANES)
    idx, rank = route_i[:, :TOP_K], route_i[:, TOP_K:2 * TOP_K]
    gate = route_g.reshape(t, LANES)[:, :TOP_K]
    cnt = counts[0, :e].astype(I32)
    padded = (cnt + r - 1) // r * r
    pad_end = jnp.cumsum(padded)
    dest = (pad_end - padded)[idx] + rank
    tok = jnp.broadcast_to(jnp.arange(t, dtype=I32)[:, None], (t, TOP_K))
    row_tok = jnp.zeros((n_blocks * r,), I32).at[dest.reshape(-1)].set(tok.reshape(-1))
    block_e = jnp.minimum(jnp.searchsorted(pad_end, jnp.arange(n_blocks, dtype=I32) * r, side="right"),
                          e - 1).astype(I32)
    n_used = (pad_end[-1:] // r).astype(I32)

    f = p["w2"].shape[2]
    w2p = p["w2"][lyr].reshape(e, 2, f // 2, d).transpose(0, 2, 1, 3).reshape(e, f, d).astype(BF16)
    x_sorted = jnp.take(hn.reshape(t, d), row_tok, axis=0)
    y_sorted = _expert_ffn(x_sorted, block_e, n_used, p["w1"][lyr].astype(BF16), p["b1"][lyr], w2p, p["b2"][lyr])
    moe = jnp.sum(jnp.take(y_sorted, dest.reshape(-1), axis=0).reshape(t, TOP_K, d) * gate[:, :, None], axis=1)
    return _combine(x2.reshape(t, d), moe, p["norm_final_g"][None], final).reshape(b, s, d)


def kernel(x, mem, norm_mix_g, w_in, shift_mu, w0, w_decay_up, a0, w_iclr_up, w_gate_up, k_k, k_a, r_k,
           lnx_g, lnx_b, lambda_q1, lambda_k1, lambda_q2, lambda_k2, subln_g, w_out, norm_cross_g,
           norm_mem_g, w_cq, w_ckv, w_co, norm_ffn_g, w_router, b_router, w1, b1, w2, b2, norm_final_g):
    p = dict(norm_mix_g=norm_mix_g, w_in=w_in, shift_mu=shift_mu, w0=w0, w_decay_up=w_decay_up, a0=a0,
             w_iclr_up=w_iclr_up, w_gate_up=w_gate_up, k_k=k_k, k_a=k_a, r_k=r_k, lnx_g=lnx_g, lnx_b=lnx_b,
             lambda_q1=lambda_q1, lambda_k1=lambda_k1, lambda_q2=lambda_q2, lambda_k2=lambda_k2,
             subln_g=subln_g, w_out=w_out, norm_cross_g=norm_cross_g, norm_mem_g=norm_mem_g, w_cq=w_cq,
             w_ckv=w_ckv, w_co=w_co, norm_ffn_g=norm_ffn_g, w_router=w_router, b_router=b_router,
             w1=w1, b1=b1, w2=w2, b2=b2, norm_final_g=norm_final_g)
    depth = w_in.shape[0]
    for lyr in range(depth):
        x = _layer(x, mem, lyr, lyr == depth - 1, p)
    return x
```

```python
import functools
import math

import jax
import jax.numpy as jnp
from jax import lax
from jax.experimental import pallas as pl
from jax.experimental.pallas import tpu as pltpu

F32 = jnp.float32
BF16 = jnp.bfloat16
I32 = jnp.int32

NORM_EPS = 1e-5
HEAD = 64
RWKV_W = 512
RWKV_HEADS = RWKV_W // HEAD
DECAY_LORA, ICLR_LORA, GATE_LORA = 64, 64, 160
LORA_W = DECAY_LORA + ICLR_LORA + GATE_LORA
LORA_PAD = 384
RWKV_GN_EPS = 64e-5
DIFF_W = 512
DIFF_HEADS = DIFF_W // (2 * HEAD)
CROSS_HEADS = 4
N_EXPERTS = 32
TOP_K = 4
SWIGLU_LIMIT = 7.0
SWIGLU_ALPHA = 1.702
LANES = 128
TILE_ROWS = 8
NEG_BIG = -1e30

RWKV_CHUNK = 64
PROJ_TM, PROJ_TN = 512, 1152
ATT_BLOCK = 512
MID_TM = 256
MOE_ROWS = 256
DISPATCH_TM = 512
COMBINE_TM = 256
VMEM_LIMIT = 56 * 1024 * 1024


def _dot(a, b):
    return jnp.dot(a.astype(BF16), b.astype(BF16), preferred_element_type=F32)


def _dot_nt(a, b):
    return lax.dot_general(a.astype(BF16), b.astype(BF16), (((1,), (1,)), ((), ())),
                           preferred_element_type=F32)


def _dot_tn(a, b):
    return lax.dot_general(a.astype(BF16), b.astype(BF16), (((0,), (0,)), ((), ())),
                           preferred_element_type=F32)


def _split3(x):
    hi = x.astype(BF16)
    r1 = x - hi.astype(F32)
    mid = r1.astype(BF16)
    lo = (r1 - mid.astype(F32)).astype(BF16)
    return hi, mid, lo


def _dot_exact_rhs(x, ones_bf16):
    hi, mid, lo = _split3(x)
    return (jnp.dot(hi, ones_bf16, preferred_element_type=F32)
            + jnp.dot(mid, ones_bf16, preferred_element_type=F32)
            + jnp.dot(lo, ones_bf16, preferred_element_type=F32))


def _store_token_tiles(ref, x, lead=()):
    rows = x.shape[0]
    for c in range(TILE_ROWS):
        ref[lead + (pl.ds(c, rows, stride=TILE_ROWS), slice(None))] = x[:, c * LANES:(c + 1) * LANES]


def _load_token_tiles(ref, rows, lead=()):
    return jnp.concatenate([ref[lead + (pl.ds(c, rows, stride=TILE_ROWS), slice(None))]
                            for c in range(TILE_ROWS)], axis=1)


def _rms(x, g):
    return x * lax.rsqrt(jnp.mean(x * x, axis=-1, keepdims=True) + NORM_EPS) * g


def _params(*sem):
    return pltpu.CompilerParams(dimension_semantics=sem, vmem_limit_bytes=VMEM_LIMIT)


def _norm_inproj_kernel(x_ref, g_ref, w_ref, o_ref, h_scr):
    @pl.when(pl.program_id(1) == 0)
    def _():
        h_scr[...] = _rms(x_ref[...], g_ref[...]).astype(BF16)

    o_ref[...] = jnp.dot(h_scr[...], w_ref[...], preferred_element_type=F32)


def _norm_inproj(x2d, g, w_bf16):
    t, d = x2d.shape
    n = w_bf16.shape[1]
    tm, tn = min(PROJ_TM, t), PROJ_TN
    return pl.pallas_call(
        _norm_inproj_kernel,
        grid=(t // tm, n // tn),
        in_specs=[pl.BlockSpec((tm, d), lambda i, j: (i, 0)),
                  pl.BlockSpec((1, d), lambda i, j: (0, 0)),
                  pl.BlockSpec((d, tn), lambda i, j: (0, j))],
        out_specs=pl.BlockSpec((tm, tn), lambda i, j: (i, j)),
        out_shape=jax.ShapeDtypeStruct((t, n), F32),
        scratch_shapes=[pltpu.VMEM((tm, d), BF16)],
        compiler_params=_params("parallel", "arbitrary"),
        name="norm_inproj",
    )(x2d, g, w_bf16)


def _rwkv_kernel(rkv_ref, lora_ref, mu_rkv_ref, mu_l_ref, wl_ref, vec_ref, bd_ref, tri_ref,
                 o_ref, st_ref, prev_rkv_ref, prev_l_ref):
    L = RWKV_CHUNK
    C = RWKV_W

    @pl.when(pl.program_id(1) == 0)
    def _():
        st_ref[...] = jnp.zeros_like(st_ref)
        prev_rkv_ref[...] = jnp.zeros_like(prev_rkv_ref)
        prev_l_ref[...] = jnp.zeros_like(prev_l_ref)

    row = lax.broadcasted_iota(I32, (L, 1), 0)

    def token_shift(raw, prev_ref, mu):
        prev = jnp.where(row == 0, prev_ref[...], pltpu.roll(raw, 1, axis=0))
        prev_ref[...] = raw[L - 1:L, :]
        return raw + mu * (prev - raw)

    u = token_shift(rkv_ref[...], prev_rkv_ref, mu_rkv_ref[...])
    ul = token_shift(lora_ref[...], prev_l_ref, mu_l_ref[...])
    r, k, v = u[:, :C], u[:, C:2 * C], u[:, 2 * C:]

    lane_l = lax.broadcasted_iota(I32, (L, LORA_PAD), 1)
    act = jnp.where(lane_l < DECAY_LORA, jnp.tanh(ul),
                    jnp.where(lane_l < DECAY_LORA + ICLR_LORA, ul, jax.nn.sigmoid(ul)))
    lo = _dot(act, wl_ref[...])
    w0, a0, k_k, k_a = vec_ref[0:1, :], vec_ref[1:2, :], vec_ref[2:3, :], vec_ref[3:4, :]
    r_k, lnx_g, lnx_b = vec_ref[4:5, :], vec_ref[5:6, :], vec_ref[6:7, :]

    w_log = -jax.nn.softplus(-(w0 + lo[:, :C])) - 0.5
    lw = -jnp.exp(w_log)
    a = jax.nn.sigmoid(a0 + lo[:, C:2 * C])
    g = lo[:, 2 * C:]
    bd = bd_ref[...]
    kk = k * k_k
    kk = kk / jnp.maximum(jnp.sqrt(_dot_exact_rhs(kk * kk, bd)), 1e-12)
    k2 = k * (1.0 + (a - 1.0) * k_a)

    tri = tri_ref[...]
    hi, mid, lo3 = _split3(lw)
    cum = (jnp.dot(tri, hi, preferred_element_type=F32) + jnp.dot(tri, mid, preferred_element_type=F32)
           + jnp.dot(tri, lo3, preferred_element_type=F32))
    total = cum[L - 1:L, :]
    e_neg = jnp.exp(-cum)
    e_rem = jnp.exp(total - cum)
    kka = kk * a
    a_t = -kk * jnp.exp(cum - lw)
    b_t = kka * e_neg
    k_t = k2 * e_neg
    r_t = r * jnp.exp(cum)
    b_bar = kka * e_rem
    k_bar = k2 * e_rem
    p_total = jnp.exp(total)

    m0 = lax.broadcasted_iota(I32, (L, LANES), 1) < HEAD
    r2i = lax.broadcasted_iota(I32, (2 * L, 2 * L), 0)
    c2i = lax.broadcasted_iota(I32, (2 * L, 2 * L), 1)
    strict, incl = c2i < r2i, c2i <= r2i

    def stack2(xp):
        return jnp.concatenate([jnp.where(m0, xp, 0.0), jnp.where(m0, 0.0, xp)], axis=0)

    ys = []
    for p in range(RWKV_HEADS // 2):
        sl = slice(p * LANES, (p + 1) * LANES)
        A2, B2, K2, R2 = stack2(a_t[:, sl]), stack2(b_t[:, sl]), stack2(k_t[:, sl]), stack2(r_t[:, sl])
        V2, Bb2, Kb2 = stack2(v[:, sl]), stack2(b_bar[:, sl]), stack2(k_bar[:, sl])
        m_ab = jnp.where(strict, _dot_nt(A2, B2), 0.0)
        m_ak = jnp.where(strict, _dot_nt(A2, K2), 0.0)
        g_b = jnp.where(incl, _dot_nt(R2, B2), 0.0)
        g_k = jnp.where(incl, _dot_nt(R2, K2), 0.0)
        tr, pw = m_ab, m_ab
        for _ in range(int(math.log2(L)) - 1):
            pw = _dot(pw, pw)
            tr = tr + pw + _dot(pw, tr)
        az = jnp.concatenate([A2, _dot(m_ak, V2)], axis=1)
        tz = az + _dot(tr, az)
        a_hat, u_loc = tz[:, :LANES], tz[:, LANES:]
        st = st_ref[p]
        u2 = _dot_nt(a_hat, st) + u_loc
        y2 = _dot_nt(R2, st) + _dot(g_b, u2) + _dot(g_k, V2)
        st_ref[p] = st * p_total[:, sl] + _dot_tn(u2, Bb2) + _dot_tn(V2, Kb2)
        ys.append(y2[:L] + y2[L:])
    y = jnp.concatenate(ys, axis=1)

    inv_n = 1.0 / HEAD
    mu = _dot_exact_rhs(y, bd) * inv_n
    d = y - mu
    var = _dot_exact_rhs(d * d, bd) * inv_n
    yn = d * lax.rsqrt(var + RWKV_GN_EPS) * lnx_g + lnx_b
    bonus = _dot_exact_rhs(r * k2 * r_k, bd) * v
    o_ref[...] = (yn + bonus) * g


def _rwkv_group(proj3, mu_rkv, mu_l, wl, vec, rkv_block, lora_block):
    b, s, _ = proj3.shape
    L, C = RWKV_CHUNK, RWKV_W
    head = jnp.arange(C, dtype=I32) // HEAD
    bd = (head[:, None] == head[None, :]).astype(BF16)
    t = jnp.arange(L, dtype=I32)
    tri = (t[None, :] <= t[:, None]).astype(BF16)
    const = lambda shape: pl.BlockSpec(shape, lambda i, j: (0,) * len(shape))
    return pl.pallas_call(
        _rwkv_kernel,
        grid=(b, s // L),
        in_specs=[pl.BlockSpec((None, L, 3 * C), lambda i, j: (i, j, rkv_block)),
                  pl.BlockSpec((None, L, LORA_PAD), lambda i, j: (i, j, lora_block)),
                  const((1, 3 * C)), const((1, LORA_PAD)), const((LORA_PAD, 3 * C)),
                  const((8, C)), const((C, C)), const((L, L))],
        out_specs=pl.BlockSpec((None, L, C), lambda i, j: (i, j, 0)),
        out_shape=jax.ShapeDtypeStruct((b, s, C), F32),
        scratch_shapes=[pltpu.VMEM((RWKV_HEADS // 2, LANES, LANES), F32),
                        pltpu.VMEM((1, 3 * C), F32), pltpu.VMEM((1, LORA_PAD), F32)],
        compiler_params=_params("parallel", "arbitrary"),
        name="rwkv_group",
    )(proj3, proj3, mu_rkv, mu_l, wl, vec, bd, tri)


def _diff_attn_kernel(q_ref, k_ref, v_ref, slope_ref, lam_ref, g_ref, o_ref,
                      q2_scr, m_scr, l_scr, acc_scr, *, lambda_init):
    blk = ATT_BLOCK
    qi, ki = pl.program_id(2), pl.program_id(3)

    @pl.when(ki == 0)
    def _():
        q = q_ref[...] * (HEAD ** -0.5)
        m0 = lax.broadcasted_iota(I32, (blk, LANES), 1) < HEAD
        q2_scr[0:blk, :] = jnp.where(m0, q, 0.0).astype(BF16)
        q2_scr[blk:, :] = jnp.where(m0, 0.0, q).astype(BF16)
        m_scr[...] = jnp.full_like(m_scr, NEG_BIG)
        l_scr[...] = jnp.zeros_like(l_scr)
        acc_scr[...] = jnp.zeros_like(acc_scr)

    def step(on_diagonal):
        s = _dot_nt(q2_scr[...], k_ref[...])
        koff = lax.broadcasted_iota(I32, (1, blk), 1)
        s = s + slope_ref[:, :1] * ((ki - qi) * blk + koff).astype(F32)
        if on_diagonal:
            qoff = lax.broadcasted_iota(I32, (2 * blk, 1), 0) % blk
            s = jnp.where(koff <= qoff, s, NEG_BIG)
        m_prev = m_scr[...]
        m_new = jnp.maximum(m_prev, jnp.max(s, axis=-1, keepdims=True))
        alpha = jnp.exp(m_prev - m_new)
        p = jnp.exp(s - m_new)
        l_scr[...] = alpha * l_scr[...] + jnp.sum(p, axis=-1, keepdims=True)
        acc_scr[...] = alpha * acc_scr[...] + _dot(p, v_ref[...])
        m_scr[...] = m_new

    @pl.when(ki < qi)
    def _():
        step(False)

    @pl.when(ki == qi)
    def _():
        step(True)
        lam_v = lam_ref[...]
        lam = (jnp.exp(jnp.sum(lam_v[0:1] * lam_v[1:2], axis=-1, keepdims=True))
               - jnp.exp(jnp.sum(lam_v[2:3] * lam_v[3:4], axis=-1, keepdims=True)) + lambda_init)
        o2 = acc_scr[...] / l_scr[...]
        o = o2[:blk] - lam * o2[blk:]
        o_ref[...] = _rms(o, g_ref[...]) * (1.0 - lambda_init)


def _diff_attention(proj3, lam_vecs, subln_g, lambda_init):
    b, s, _ = proj3.shape
    blk = ATT_BLOCK
    nb = s // blk
    nh = DIFF_HEADS
    slopes = jnp.exp2(-8.0 * jnp.arange(1, nh + 1, dtype=F32) / nh)
    slopes = jnp.broadcast_to(slopes[:, None, None], (nh, 1, LANES))
    kernel = functools.partial(_diff_attn_kernel, lambda_init=lambda_init)
    return pl.pallas_call(
        kernel,
        grid=(b, nh, nb, nb),
        in_specs=[pl.BlockSpec((None, blk, LANES), lambda bi, h, qi, ki: (bi, qi, h)),
                  pl.BlockSpec((None, blk, LANES), lambda bi, h, qi, ki: (bi, jnp.minimum(ki, qi), nh + h)),
                  pl.BlockSpec((None, blk, LANES), lambda bi, h, qi, ki: (bi, jnp.minimum(ki, qi), 2 * nh + h)),
                  pl.BlockSpec((None, 1, LANES), lambda bi, h, qi, ki: (h, 0, 0)),
                  pl.BlockSpec((4, HEAD), lambda bi, h, qi, ki: (0, 0)),
                  pl.BlockSpec((1, 2 * HEAD), lambda bi, h, qi, ki: (0, 0))],
        out_specs=pl.BlockSpec((None, blk, LANES), lambda bi, h, qi, ki: (bi, qi, h)),
        out_shape=jax.ShapeDtypeStruct((b, s, DIFF_W), F32),
        scratch_shapes=[pltpu.VMEM((2 * blk, LANES), BF16), pltpu.VMEM((2 * blk, 1), F32),
                        pltpu.VMEM((2 * blk, 1), F32), pltpu.VMEM((2 * blk, LANES), F32)],
        compiler_params=_params("parallel", "parallel", "parallel", "arbitrary"),
        name="diff_attention",
    )(proj3, proj3, proj3, slopes, lam_vecs, subln_g)


def _mem_kv_kernel(m_ref, g_ref, w_ref, k_ref, v_ref):
    d = m_ref.shape[-1]
    kv = _dot(_rms(m_ref[...], g_ref[...]), w_ref[...])
    k_ref[...] = kv[:, :d].astype(BF16)
    v_ref[...] = kv[:, d:].astype(BF16)


def _mem_kv(mem, g, w_ckv_bf16):
    b, m, d = mem.shape
    return pl.pallas_call(
        _mem_kv_kernel,
        grid=(b,),
        in_specs=[pl.BlockSpec((None, m, d), lambda i: (i, 0, 0)),
                  pl.BlockSpec((1, d), lambda i: (0, 0)),
                  pl.BlockSpec((d, 2 * d), lambda i: (0, 0))],
        out_specs=[pl.BlockSpec((None, m, d), lambda i: (i, 0, 0))] * 2,
        out_shape=[jax.ShapeDtypeStruct((b, m, d), BF16)] * 2,
        compiler_params=_params("parallel"),
        name="mem_kv",
    )(mem, g, w_ckv_bf16)


def _mid_kernel(x_ref, yr_ref, yd_ref, wo_ref, gc_ref, wcq_ref, km_ref, vm_ref, wco_ref, gf_ref,
                wr_hi_ref, wr_lo_ref, br_ref, tri_ref,
                x2_ref, hn_ref, ri_ref, rg_ref, cnt_ref, carry_scr):
    tm, d = x_ref.shape
    first = jnp.logical_and(pl.program_id(0) == 0, pl.program_id(1) == 0)

    @pl.when(first)
    def _():
        carry_scr[...] = jnp.zeros_like(carry_scr)

    half = yr_ref.shape[-1]
    x1 = x_ref[...] + _dot(yr_ref[...], wo_ref[:half, :]) + _dot(yd_ref[...], wo_ref[half:, :])

    q = _dot(_rms(x1, gc_ref[...]), wcq_ref[...])
    hd = d // CROSS_HEADS
    outs = []
    for h in range(CROSS_HEADS):
        sl = slice(h * hd, (h + 1) * hd)
        s = _dot_nt(q[:, sl], km_ref[:, sl]) * (hd ** -0.5)
        s = s - jnp.max(s, axis=-1, keepdims=True)
        p = jnp.exp(s)
        p = p / jnp.sum(p, axis=-1, keepdims=True)
        outs.append(_dot(p, vm_ref[:, sl]))
    x2 = x1 + _dot(jnp.concatenate(outs, axis=1), wco_ref[...])
    x2_ref[...] = x2

    hn = _rms(x2, gf_ref[...])
    _store_token_tiles(hn_ref, hn)
    hi = hn.astype(BF16)
    lo = (hn - hi.astype(F32)).astype(BF16)
    logits = (jnp.dot(hi, wr_hi_ref[...], preferred_element_type=F32)
              + jnp.dot(lo, wr_hi_ref[...], preferred_element_type=F32)
              + jnp.dot(hi, wr_lo_ref[...], preferred_element_type=F32) + br_ref[...])
    lane = lax.broadcasted_iota(I32, (tm, LANES), 1)
    vals = logits
    tops, idxs, hots = [], [], []
    for _ in range(TOP_K):
        mx = jnp.max(vals, axis=-1, keepdims=True)
        idx = jnp.min(jnp.where(vals == mx, lane, LANES), axis=-1, keepdims=True)
        hot = lane == idx
        vals = jnp.where(hot, -jnp.inf, vals)
        tops.append(mx)
        idxs.append(idx)
        hots.append(hot)
    es = [jnp.exp(t - tops[0]) for t in tops]
    denom = es[0] + es[1] + es[2] + es[3]
    sel = jnp.zeros((tm, LANES), F32)
    for hot in hots:
        sel = sel + hot.astype(F32)
    before = jnp.dot(tri_ref[...], sel.astype(BF16), preferred_element_type=F32) + carry_scr[...]
    carry_scr[...] = carry_scr[...] + jnp.sum(sel, axis=0, keepdims=True)
    ri = jnp.zeros((tm, LANES), I32)
    rg = jnp.zeros((tm, LANES), F32)
    for j in range(TOP_K):
        rank = jnp.sum(jnp.where(hots[j], before, 0.0), axis=-1, keepdims=True).astype(I32)
        ri = jnp.where(lane == j, idxs[j], ri)
        ri = jnp.where(lane == TOP_K + j, rank, ri)
        rg = jnp.where(lane == j, es[j] / denom, rg)
    ri_ref[...] = ri
    rg_ref[...] = rg
    cnt_ref[...] = jnp.broadcast_to(carry_scr[...], cnt_ref.shape)


def _mid_stage(x, y_rwkv, y_diff, w_out, g_cross, w_cq, k_mem, v_mem, w_co, g_ffn, w_router, b_router):
    b, s, d = x.shape
    tm = min(MID_TM, s)
    m = k_mem.shape[1]
    half = y_rwkv.shape[-1]
    e = w_router.shape[1]
    wr = jnp.zeros((d, LANES), F32).at[:, :e].set(w_router)
    wr_hi = wr.astype(BF16)
    wr_lo = (wr - wr_hi.astype(F32)).astype(BF16)
    br = jnp.full((1, LANES), NEG_BIG, F32).at[0, :e].set(b_router)
    t = jnp.arange(tm, dtype=I32)
    tri = (t[None, :] < t[:, None]).astype(BF16)
    tile = lambda w: pl.BlockSpec((None, tm, w), lambda i, j: (i, j, 0))
    const = lambda shape: pl.BlockSpec(shape, lambda i, j: (0,) * len(shape))
    return pl.pallas_call(
        _mid_kernel,
        grid=(b, s // tm),
        in_specs=[tile(d), tile(half), tile(half), const((d, d)), const((1, d)), const((d, d)),
                  pl.BlockSpec((None, m, d), lambda i, j: (i, 0, 0)),
                  pl.BlockSpec((None, m, d), lambda i, j: (i, 0, 0)),
                  const((d, d)), const((1, d)), const((d, LANES)), const((d, LANES)),
                  const((1, LANES)), const((tm, tm))],
        out_specs=[tile(d), pl.BlockSpec((tm * TILE_ROWS, LANES), lambda i, j: (i * (s // tm) + j, 0)),
                   tile(LANES), tile(LANES), const((8, LANES))],
        out_shape=[jax.ShapeDtypeStruct((b, s, d), F32), jax.ShapeDtypeStruct((b * s * TILE_ROWS, LANES), F32),
                   jax.ShapeDtypeStruct((b, s, LANES), I32), jax.ShapeDtypeStruct((b, s, LANES), F32),
                   jax.ShapeDtypeStruct((8, LANES), F32)],
        scratch_shapes=[pltpu.VMEM((1, LANES), F32)],
        compiler_params=_params("arbitrary", "arbitrary"),
        name="outproj_cross_router",
    )(x, y_rwkv, y_diff, w_out.astype(BF16), g_cross, w_cq.astype(BF16), k_mem, v_mem,
      w_co.astype(BF16), g_ffn, wr_hi, wr_lo, br, tri)


def _expert_kernel(be_ref, nused_ref, x_ref, w1_ref, b1_ref, w2_ref, b2_ref, o_ref):
    i = pl.program_id(0)

    @pl.when(i < nused_ref[0])
    def _():
        x = _load_token_tiles(x_ref, MOE_ROWS).astype(BF16)
        hid = jnp.dot(x, w1_ref[...], preferred_element_type=F32) + b1_ref[...]
        f2 = hid.shape[-1]
        lin = pltpu.roll(hid, f2 - 1, axis=1)
        glu = jnp.minimum(hid, SWIGLU_LIMIT)
        lin = jnp.clip(lin, -SWIGLU_LIMIT, SWIGLU_LIMIT)
        act = glu * jax.nn.sigmoid(SWIGLU_ALPHA * glu) * (lin + 1.0)
        f = f2 // 2
        even = lax.broadcasted_iota(I32, (act.shape[0], f), 1) % 2 == 0
        packed = jnp.where(even, act[:, :f], 0.0) + pltpu.roll(jnp.where(even, act[:, f:], 0.0), 1, axis=1)
        y = jnp.dot(packed.astype(BF16), w2_ref[...], preferred_element_type=F32) + b2_ref[...]
        _store_token_tiles(o_ref, y)

    @pl.when(i >= nused_ref[0])
    def _():
        o_ref[...] = jnp.zeros_like(o_ref)


def _expert_ffn(x_sorted, block_e, n_used, w1, b1, w2p, b2):
    e, d, f2 = w1.shape
    r = MOE_ROWS
    n_rows = x_sorted.shape[0] // TILE_ROWS
    grid_spec = pltpu.PrefetchScalarGridSpec(
        num_scalar_prefetch=2,
        grid=(n_rows // r,),
        in_specs=[pl.BlockSpec((r * TILE_ROWS, LANES), lambda i, be, nu: (i, 0)),
                  pl.BlockSpec((None, d, f2), lambda i, be, nu: (be[i], 0, 0)),
                  pl.BlockSpec((None, 1, f2), lambda i, be, nu: (be[i], 0, 0)),
                  pl.BlockSpec((None, f2 // 2, d), lambda i, be, nu: (be[i], 0, 0)),
                  pl.BlockSpec((None, 1, d), lambda i, be, nu: (be[i], 0, 0))],
        out_specs=pl.BlockSpec((r * TILE_ROWS, LANES), lambda i, be, nu: (i, 0)),
    )
    return pl.pallas_call(
        _expert_kernel,
        grid_spec=grid_spec,
        out_shape=jax.ShapeDtypeStruct((n_rows * TILE_ROWS, LANES), F32),
        compiler_params=_params("arbitrary"),
        name="expert_ffn",
    )(block_e, n_used, x_sorted, w1, b1.reshape(e, 1, f2), w2p, b2.reshape(e, 1, d))


def _tile_at(ref, first_row):
    return ref.at[pl.ds(pl.multiple_of(first_row, TILE_ROWS), TILE_ROWS)]


def _dispatch_kernel(dest_ref, hn_hbm, xs_init_hbm, xs_hbm, sem, *, tm):
    del xs_init_hbm
    base = pl.program_id(0) * tm

    def issue(t, carry):
        src = _tile_at(hn_hbm, (base + t) * TILE_ROWS)
        for j in range(TOP_K):
            pltpu.make_async_copy(src, _tile_at(xs_hbm, dest_ref[0, t * TOP_K + j]), sem).start()
        return carry

    lax.fori_loop(0, tm, issue, 0, unroll=8)
    rows = pl.ds(0, tm * TOP_K * TILE_ROWS)
    pltpu.make_async_copy(hn_hbm.at[rows], xs_hbm.at[rows], sem).wait()


def _dispatch(hn_tiles, dest_rows, n_rows):
    t = hn_tiles.shape[0] // TILE_ROWS
    tm = min(DISPATCH_TM, t)
    steps = t // tm
    return pl.pallas_call(
        functools.partial(_dispatch_kernel, tm=tm),
        grid=(steps,),
        in_specs=[pl.BlockSpec((None, 1, tm * TOP_K), lambda i: (i, 0, 0), memory_space=pltpu.SMEM),
                  pl.BlockSpec(memory_space=pl.ANY), pl.BlockSpec(memory_space=pl.ANY)],
        out_specs=pl.BlockSpec(memory_space=pl.ANY),
        out_shape=jax.ShapeDtypeStruct((n_rows * TILE_ROWS, LANES), F32),
        scratch_shapes=[pltpu.SemaphoreType.DMA(())],
        input_output_aliases={2: 0},
        compiler_params=_params("arbitrary"),
        name="moe_dispatch",
    )(dest_rows.reshape(steps, 1, tm * TOP_K), hn_tiles, jnp.zeros((n_rows * TILE_ROWS, LANES), F32))


def _combine_kernel(dest_ref, x_ref, gate_ref, g_ref, ys_hbm, o_ref, buf, sem, *, tm, final):
    def issue(t, carry):
        for j in range(TOP_K):
            pltpu.make_async_copy(_tile_at(ys_hbm, dest_ref[0, t * TOP_K + j]),
                                  _tile_at(buf.at[j], t * TILE_ROWS), sem).start()
        return carry

    lax.fori_loop(0, tm, issue, 0, unroll=8)
    pltpu.make_async_copy(buf, buf, sem).wait()
    y = x_ref[...]
    gate = gate_ref[...]
    for j in range(TOP_K):
        y = y + gate[:, j:j + 1] * _load_token_tiles(buf, tm, lead=(j,))
    o_ref[...] = _rms(y, g_ref[...]) if final else y


def _combine(x2d, gate2d, dest_rows, y_sorted, g, final):
    t, d = x2d.shape
    tm = min(COMBINE_TM, t)
    steps = t // tm
    return pl.pallas_call(
        functools.partial(_combine_kernel, tm=tm, final=final),
        grid=(steps,),
        in_specs=[pl.BlockSpec((None, 1, tm * TOP_K), lambda i: (i, 0, 0), memory_space=pltpu.SMEM),
                  pl.BlockSpec((tm, d), lambda i: (i, 0)), pl.BlockSpec((tm, LANES), lambda i: (i, 0)),
                  pl.BlockSpec((1, d), lambda i: (0, 0)), pl.BlockSpec(memory_space=pl.ANY)],
        out_specs=pl.BlockSpec((tm, d), lambda i: (i, 0)),
        out_shape=jax.ShapeDtypeStruct((t, d), F32),
        scratch_shapes=[pltpu.VMEM((TOP_K, tm * TILE_ROWS, LANES), F32), pltpu.SemaphoreType.DMA(())],
        compiler_params=_params("arbitrary"),
        name="combine_norm",
    )(dest_rows.reshape(steps, 1, tm * TOP_K), x2d, gate2d, g, y_sorted)


def _layer(x, mem, lyr, final, p):
    b, s, d = x.shape
    t = b * s
    c = RWKV_W
    rwkv_in = 3 * c + LORA_W

    w_in = p["w_in"][lyr]
    w_cat = jnp.concatenate([w_in[:, rwkv_in:], w_in[:, :3 * c], w_in[:, 3 * c:rwkv_in],
                             jnp.zeros((d, LORA_PAD - LORA_W), F32)], axis=1).astype(BF16)
    proj = _norm_inproj(x.reshape(t, d), p["norm_mix_g"][lyr][None], w_cat).reshape(b, s, -1)

    mu = p["shift_mu"][lyr]
    mu_l = jnp.zeros((1, LORA_PAD), F32).at[0, :LORA_W].set(mu[3 * c:])
    wl = jnp.zeros((LORA_PAD, 3 * c), F32)
    wl = wl.at[:DECAY_LORA, :c].set(p["w_decay_up"][lyr])
    wl = wl.at[DECAY_LORA:DECAY_LORA + ICLR_LORA, c:2 * c].set(p["w_iclr_up"][lyr])
    wl = wl.at[DECAY_LORA + ICLR_LORA:LORA_W, 2 * c:].set(p["w_gate_up"][lyr])
    vec = jnp.stack([p["w0"][lyr], p["a0"][lyr], p["k_k"][lyr], p["k_a"][lyr], p["r_k"][lyr].reshape(c),
                     p["lnx_g"][lyr], p["lnx_b"][lyr], jnp.zeros((c,), F32)])
    diff_cols = 3 * DIFF_W
    y_rwkv = _rwkv_group(proj, mu[None, :3 * c], mu_l, wl.astype(BF16), vec,
                         rkv_block=diff_cols // (3 * c), lora_block=(diff_cols + 3 * c) // LORA_PAD)

    lambda_init = 0.8 - 0.6 * math.exp(-0.3 * lyr)
    lam_vecs = jnp.stack([p["lambda_q1"][lyr], p["lambda_k1"][lyr], p["lambda_q2"][lyr], p["lambda_k2"][lyr]])
    y_diff = _diff_attention(proj, lam_vecs, p["subln_g"][lyr][None], lambda_init)

    k_mem, v_mem = _mem_kv(mem, p["norm_mem_g"][lyr][None], p["w_ckv"][lyr].astype(BF16))
    x2, hn, route_i, route_g, counts = _mid_stage(
        x, y_rwkv, y_diff, p["w_out"][lyr], p["norm_cross_g"][lyr][None], p["w_cq"][lyr], k_mem, v_mem,
        p["w_co"][lyr], p["norm_ffn_g"][lyr][None], p["w_router"][lyr], p["b_router"][lyr])

    e = N_EXPERTS
    r = MOE_ROWS
    n_blocks = (t * TOP_K) // r + e
    route_i = route_i.reshape(t, LANES)
    idx, rank = route_i[:, :TOP_K], route_i[:, TOP_K:2 * TOP_K]
    cnt = counts[0, :e].astype(I32)
    padded = (cnt + r - 1) // r * r
    pad_end = jnp.cumsum(padded)
    pad_start = pad_end - padded
    dest = jnp.sum(jnp.where(idx[:, :, None] == jnp.arange(e, dtype=I32), pad_start, 0), axis=-1) + rank
    dest = dest.reshape(-1) * TILE_ROWS
    block_start = jnp.arange(n_blocks, dtype=I32) * r
    block_e = jnp.minimum(jnp.sum((block_start[:, None] >= pad_end[None, :]).astype(I32), axis=1), e - 1)
    n_used = pad_end[-1:] // r

    f = p["w2"].shape[2]
    w2p = jnp.stack([p["w2"][lyr][:, :f // 2], p["w2"][lyr][:, f // 2:]], axis=2).reshape(e, f, d).astype(BF16)
    x_sorted = _dispatch(hn, dest, n_blocks * r)
    y_sorted = _expert_ffn(x_sorted, block_e, n_used, p["w1"][lyr].astype(BF16), p["b1"][lyr], w2p, p["b2"][lyr])
    return _combine(x2.reshape(t, d), route_g.reshape(t, LANES), dest, y_sorted,
                    p["norm_final_g"][None], final).reshape(b, s, d)


def kernel(x, mem, norm_mix_g, w_in, shift_mu, w0, w_decay_up, a0, w_iclr_up, w_gate_up, k_k, k_a, r_k,
           lnx_g, lnx_b, lambda_q1, lambda_k1, lambda_q2, lambda_k2, subln_g, w_out, norm_cross_g,
           norm_mem_g, w_cq, w_ckv, w_co, norm_ffn_g, w_router, b_router, w1, b1, w2, b2, norm_final_g):
    p = dict(norm_mix_g=norm_mix_g, w_in=w_in, shift_mu=shift_mu, w0=w0, w_decay_up=w_decay_up, a0=a0,
             w_iclr_up=w_iclr_up, w_gate_up=w_gate_up, k_k=k_k, k_a=k_a, r_k=r_k, lnx_g=lnx_g, lnx_b=lnx_b,
             lambda_q1=lambda_q1, lambda_k1=lambda_k1, lambda_q2=lambda_q2, lambda_k2=lambda_k2,
             subln_g=subln_g, w_out=w_out, norm_cross_g=norm_cross_g, norm_mem_g=norm_mem_g, w_cq=w_cq,
             w_ckv=w_ckv, w_co=w_co, norm_ffn_g=norm_ffn_g, w_router=w_router, b_router=b_router,
             w1=w1, b1=b1, w2=w2, b2=b2, norm_final_g=norm_final_g)
    depth = w_in.shape[0]
    for lyr in range(depth):
        x = _layer(x, mem, lyr, lyr == depth - 1, p)
    return x
```

```python
import functools
import math

import jax
import jax.numpy as jnp
from jax import lax
from jax.experimental import pallas as pl
from jax.experimental.pallas import tpu as pltpu

F32 = jnp.float32
BF16 = jnp.bfloat16
I32 = jnp.int32

NORM_EPS = 1e-5
HEAD = 64
RWKV_W = 512
RWKV_HEADS = RWKV_W // HEAD
DECAY_LORA, ICLR_LORA, GATE_LORA = 64, 64, 160
LORA_W = DECAY_LORA + ICLR_LORA + GATE_LORA
LORA_PAD = 384
RWKV_GN_EPS = 64e-5
DIFF_W = 512
DIFF_HEADS = DIFF_W // (2 * HEAD)
CROSS_HEADS = 4
N_EXPERTS = 32
TOP_K = 4
SWIGLU_LIMIT = 7.0
SWIGLU_ALPHA = 1.702
LANES = 128
TILE_ROWS = 8
NEG_BIG = -1e30

RWKV_CHUNK = 64
RWKV_BATCH = 2
PROJ_TM, PROJ_TN = 512, 1152
ATT_BLOCK = 512
ATT_ROW_CHUNK = 256
MID_TM = 256
MOE_ROWS = 256
DISPATCH_TM = 512
COMBINE_TM = 256
VMEM_LIMIT = 56 * 1024 * 1024


def _dot(a, b):
    return jnp.dot(a.astype(BF16), b.astype(BF16), preferred_element_type=F32)


def _dot_nt(a, b):
    return lax.dot_general(a.astype(BF16), b.astype(BF16), (((1,), (1,)), ((), ())),
                           preferred_element_type=F32)


def _dot_tn(a, b):
    return lax.dot_general(a.astype(BF16), b.astype(BF16), (((0,), (0,)), ((), ())),
                           preferred_element_type=F32)


def _split3(x):
    hi = x.astype(BF16)
    r1 = x - hi.astype(F32)
    mid = r1.astype(BF16)
    lo = (r1 - mid.astype(F32)).astype(BF16)
    return hi, mid, lo


def _dot_exact_rhs(x, ones_bf16):
    hi, mid, lo = _split3(x)
    return (jnp.dot(hi, ones_bf16, preferred_element_type=F32)
            + jnp.dot(mid, ones_bf16, preferred_element_type=F32)
            + jnp.dot(lo, ones_bf16, preferred_element_type=F32))


def _store_token_tiles(ref, x, lead=()):
    rows = x.shape[0]
    for c in range(TILE_ROWS):
        ref[lead + (pl.ds(c, rows, stride=TILE_ROWS), slice(None))] = x[:, c * LANES:(c + 1) * LANES]


def _load_token_tiles(ref, rows, lead=()):
    return jnp.concatenate([ref[lead + (pl.ds(c, rows, stride=TILE_ROWS), slice(None))]
                            for c in range(TILE_ROWS)], axis=1)


def _rms(x, g):
    return x * lax.rsqrt(jnp.mean(x * x, axis=-1, keepdims=True) + NORM_EPS) * g


def _params(*sem):
    return pltpu.CompilerParams(dimension_semantics=sem, vmem_limit_bytes=VMEM_LIMIT)


def _norm_inproj_kernel(x_ref, g_ref, w_ref, o_ref, h_scr):
    @pl.when(pl.program_id(1) == 0)
    def _():
        h_scr[...] = _rms(x_ref[...], g_ref[...]).astype(BF16)

    o_ref[...] = jnp.dot(h_scr[...], w_ref[...], preferred_element_type=F32)


def _norm_inproj(x2d, g, w_bf16):
    t, d = x2d.shape
    n = w_bf16.shape[1]
    tm, tn = min(PROJ_TM, t), PROJ_TN
    return pl.pallas_call(
        _norm_inproj_kernel,
        grid=(t // tm, n // tn),
        in_specs=[pl.BlockSpec((tm, d), lambda i, j: (i, 0)),
                  pl.BlockSpec((1, d), lambda i, j: (0, 0)),
                  pl.BlockSpec((d, tn), lambda i, j: (0, j))],
        out_specs=pl.BlockSpec((tm, tn), lambda i, j: (i, j)),
        out_shape=jax.ShapeDtypeStruct((t, n), F32),
        scratch_shapes=[pltpu.VMEM((tm, d), BF16)],
        compiler_params=_params("parallel", "arbitrary"),
        name="norm_inproj",
    )(x2d, g, w_bf16)


def _rwkv_kernel(rkv_ref, lora_ref, mu_rkv_ref, mu_l_ref, wl_ref, vec_ref, bd_ref, tri_ref,
                 o_ref, st_ref, prev_rkv_ref, prev_l_ref):
    L = RWKV_CHUNK
    C = RWKV_W

    @pl.when(pl.program_id(1) == 0)
    def _():
        st_ref[...] = jnp.zeros_like(st_ref)
        prev_rkv_ref[...] = jnp.zeros_like(prev_rkv_ref)
        prev_l_ref[...] = jnp.zeros_like(prev_l_ref)

    row = lax.broadcasted_iota(I32, (L, 1), 0)
    w0, a0, k_k, k_a = vec_ref[0:1, :], vec_ref[1:2, :], vec_ref[2:3, :], vec_ref[3:4, :]
    r_k, lnx_g, lnx_b = vec_ref[4:5, :], vec_ref[5:6, :], vec_ref[6:7, :]
    bd = bd_ref[...]
    tri = tri_ref[...]
    lane_l = lax.broadcasted_iota(I32, (L, LORA_PAD), 1)
    nb = rkv_ref.shape[0]

    def token_shift(raw, prev_ref, bi, mu):
        prev = jnp.where(row == 0, prev_ref[bi], pltpu.roll(raw, 1, axis=0))
        prev_ref[bi] = raw[L - 1:L, :]
        return raw + mu * (prev - raw)

    def elementwise(bi):
        u = token_shift(rkv_ref[bi], prev_rkv_ref, bi, mu_rkv_ref[...])
        ul = token_shift(lora_ref[bi], prev_l_ref, bi, mu_l_ref[...])
        r, k, v = u[:, :C], u[:, C:2 * C], u[:, 2 * C:]
        act = jnp.where(lane_l < DECAY_LORA, jnp.tanh(ul),
                        jnp.where(lane_l < DECAY_LORA + ICLR_LORA, ul, jax.nn.sigmoid(ul)))
        lo = _dot(act, wl_ref[...])
        w_log = -jax.nn.softplus(-(w0 + lo[:, :C])) - 0.5
        lw = -jnp.exp(w_log)
        a = jax.nn.sigmoid(a0 + lo[:, C:2 * C])
        kk = k * k_k
        kk = kk / jnp.maximum(jnp.sqrt(_dot_exact_rhs(kk * kk, bd)), 1e-12)
        k2 = k * (1.0 + (a - 1.0) * k_a)
        hi, mid, lo3 = _split3(lw)
        cum = (jnp.dot(tri, hi, preferred_element_type=F32) + jnp.dot(tri, mid, preferred_element_type=F32)
               + jnp.dot(tri, lo3, preferred_element_type=F32))
        total = cum[L - 1:L, :]
        e_neg = jnp.exp(-cum)
        e_rem = jnp.exp(total - cum)
        kka = kk * a
        return dict(r=r, v=v, k2=k2, g=lo[:, 2 * C:], a_t=-kk * jnp.exp(cum - lw), b_t=kka * e_neg,
                    k_t=k2 * e_neg, r_t=r * jnp.exp(cum), b_bar=kka * e_rem, k_bar=k2 * e_rem,
                    p_total=jnp.exp(total))

    ew = [elementwise(bi) for bi in range(nb)]

    m0 = lax.broadcasted_iota(I32, (L, LANES), 1) < HEAD
    r2i = lax.broadcasted_iota(I32, (2 * L, 2 * L), 0)
    c2i = lax.broadcasted_iota(I32, (2 * L, 2 * L), 1)
    strict, incl = c2i < r2i, c2i <= r2i
    npairs = RWKV_HEADS // 2
    chains = [(bi, p) for bi in range(nb) for p in range(npairs)]
    nc = range(len(chains))

    def stack2(name):
        out = []
        for bi, p in chains:
            xp = ew[bi][name][:, p * LANES:(p + 1) * LANES]
            out.append(jnp.concatenate([jnp.where(m0, xp, 0.0), jnp.where(m0, 0.0, xp)], axis=0))
        return out

    A2, B2, K2, R2 = stack2("a_t"), stack2("b_t"), stack2("k_t"), stack2("r_t")
    V2, Bb2, Kb2 = stack2("v"), stack2("b_bar"), stack2("k_bar")
    m_ab = [jnp.where(strict, _dot_nt(A2[c], B2[c]), 0.0) for c in nc]
    m_ak = [jnp.where(strict, _dot_nt(A2[c], K2[c]), 0.0) for c in nc]
    g_b = [jnp.where(incl, _dot_nt(R2[c], B2[c]), 0.0) for c in nc]
    g_k = [jnp.where(incl, _dot_nt(R2[c], K2[c]), 0.0) for c in nc]
    tr, pw = list(m_ab), list(m_ab)
    for _ in range(int(math.log2(L)) - 1):
        pw = [_dot(pw[c], pw[c]) for c in nc]
        tr = [tr[c] + pw[c] + _dot(pw[c], tr[c]) for c in nc]
    az = [jnp.concatenate([A2[c], _dot(m_ak[c], V2[c])], axis=1) for c in nc]
    tz = [az[c] + _dot(tr[c], az[c]) for c in nc]
    st = [st_ref[bi, p] for bi, p in chains]
    u2 = [_dot_nt(tz[c][:, :LANES], st[c]) + tz[c][:, LANES:] for c in nc]
    y2 = [_dot_nt(R2[c], st[c]) + _dot(g_b[c], u2[c]) + _dot(g_k[c], V2[c]) for c in nc]
    for c, (bi, p) in enumerate(chains):
        st_ref[bi, p] = (st[c] * ew[bi]["p_total"][:, p * LANES:(p + 1) * LANES]
                         + _dot_tn(u2[c], Bb2[c]) + _dot_tn(V2[c], Kb2[c]))

    inv_n = 1.0 / HEAD
    for bi in range(nb):
        y = jnp.concatenate([y2[bi * npairs + p][:L] + y2[bi * npairs + p][L:] for p in range(npairs)], axis=1)
        e = ew[bi]
        mu = _dot_exact_rhs(y, bd) * inv_n
        d = y - mu
        var = _dot_exact_rhs(d * d, bd) * inv_n
        yn = d * lax.rsqrt(var + RWKV_GN_EPS) * lnx_g + lnx_b
        bonus = _dot_exact_rhs(e["r"] * e["k2"] * r_k, bd) * e["v"]
        o_ref[bi] = (yn + bonus) * e["g"]


def _rwkv_group(proj3, mu_rkv, mu_l, wl, vec, rkv_block, lora_block):
    b, s, _ = proj3.shape
    L, C = RWKV_CHUNK, RWKV_W
    head = jnp.arange(C, dtype=I32) // HEAD
    bd = (head[:, None] == head[None, :]).astype(BF16)
    t = jnp.arange(L, dtype=I32)
    tri = (t[None, :] <= t[:, None]).astype(BF16)
    const = lambda shape: pl.BlockSpec(shape, lambda i, j: (0,) * len(shape))
    nb = RWKV_BATCH if b % RWKV_BATCH == 0 else 1
    return pl.pallas_call(
        _rwkv_kernel,
        grid=(b // nb, s // L),
        in_specs=[pl.BlockSpec((nb, L, 3 * C), lambda i, j: (i, j, rkv_block)),
                  pl.BlockSpec((nb, L, LORA_PAD), lambda i, j: (i, j, lora_block)),
                  const((1, 3 * C)), const((1, LORA_PAD)), const((LORA_PAD, 3 * C)),
                  const((8, C)), const((C, C)), const((L, L))],
        out_specs=pl.BlockSpec((nb, L, C), lambda i, j: (i, j, 0)),
        out_shape=jax.ShapeDtypeStruct((b, s, C), F32),
        scratch_shapes=[pltpu.VMEM((nb, RWKV_HEADS // 2, LANES, LANES), F32),
                        pltpu.VMEM((nb, 1, 3 * C), F32), pltpu.VMEM((nb, 1, LORA_PAD), F32)],
        compiler_params=_params("parallel", "arbitrary"),
        name="rwkv_group",
    )(proj3, proj3, mu_rkv, mu_l, wl, vec, bd, tri)


def _diff_attn_kernel(qi_ref, ki_ref, q_ref, k_ref, v_ref, slope_ref, lam_ref, g_ref, o_ref,
                      q2_scr, m_scr, acc_scr, *, lambda_init):
    blk = ATT_BLOCK
    step_id = pl.program_id(2)
    qi, ki = qi_ref[step_id], ki_ref[step_id]
    log2e = math.log2(math.e)

    @pl.when(ki == 0)
    def _():
        q = q_ref[...] * (HEAD ** -0.5 * log2e)
        m0 = lax.broadcasted_iota(I32, (blk, LANES), 1) < HEAD
        q2_scr[0:blk, :] = jnp.where(m0, q, 0.0).astype(BF16)
        q2_scr[blk:, :] = jnp.where(m0, 0.0, q).astype(BF16)
        m_scr[...] = jnp.full_like(m_scr, NEG_BIG)
        acc_scr[...] = jnp.zeros_like(acc_scr)

    def step(on_diagonal):
        k = k_ref[...].astype(BF16)
        v_ext = jnp.concatenate([v_ref[...].astype(BF16), jnp.ones((blk, LANES), BF16)], axis=1)
        koff = lax.broadcasted_iota(I32, (1, blk), 1)
        col_bias = (slope_ref[:, :1] * log2e) * ((ki - qi) * blk + koff).astype(F32)
        rc = ATT_ROW_CHUNK
        for c in range(2 * blk // rc):
            rows = slice(c * rc, (c + 1) * rc)
            s = lax.dot_general(q2_scr[rows, :], k, (((1,), (1,)), ((), ())),
                                preferred_element_type=F32) + col_bias
            if on_diagonal:
                qoff = (c * rc) % blk + lax.broadcasted_iota(I32, (rc, 1), 0)
                s = jnp.where(koff <= qoff, s, NEG_BIG)
            m_prev = m_scr[rows, :]
            m_new = jnp.maximum(m_prev, jnp.max(s, axis=-1, keepdims=True))
            alpha = jnp.exp2(m_prev - m_new)
            p = jnp.exp2(s - jnp.concatenate([m_new] * (blk // LANES), axis=1))
            acc_scr[rows, :] = (jnp.concatenate([alpha, alpha], axis=1) * acc_scr[rows, :]
                                + jnp.dot(p.astype(BF16), v_ext, preferred_element_type=F32))
            m_scr[rows, :] = m_new

    @pl.when(ki < qi)
    def _():
        step(False)

    @pl.when(ki == qi)
    def _():
        step(True)
        lam_v = lam_ref[...]
        lam = (jnp.exp(jnp.sum(lam_v[0:1] * lam_v[1:2], axis=-1, keepdims=True))
               - jnp.exp(jnp.sum(lam_v[2:3] * lam_v[3:4], axis=-1, keepdims=True)) + lambda_init)
        o2 = acc_scr[:, :LANES] / acc_scr[:, LANES:]
        o = o2[:blk] - lam * o2[blk:]
        o_ref[...] = _rms(o, g_ref[...]) * (1.0 - lambda_init)


def _diff_attention(proj3, lam_vecs, subln_g, lambda_init):
    b, s, _ = proj3.shape
    blk = ATT_BLOCK
    nb = s // blk
    nh = DIFF_HEADS
    slopes = jnp.exp2(-8.0 * jnp.arange(1, nh + 1, dtype=F32) / nh)
    slopes = jnp.broadcast_to(slopes[:, None, None], (nh, 1, LANES))
    pairs = [(qi, ki) for qi in range(nb) for ki in range(qi + 1)]
    qi_tab = jnp.asarray([pq for pq, _ in pairs], I32)
    ki_tab = jnp.asarray([pk for _, pk in pairs], I32)
    kernel = functools.partial(_diff_attn_kernel, lambda_init=lambda_init)
    grid_spec = pltpu.PrefetchScalarGridSpec(
        num_scalar_prefetch=2,
        grid=(b, nh, len(pairs)),
        in_specs=[pl.BlockSpec((None, blk, LANES), lambda bi, h, t, qt, kt: (bi, qt[t], h)),
                  pl.BlockSpec((None, blk, LANES), lambda bi, h, t, qt, kt: (bi, kt[t], nh + h)),
                  pl.BlockSpec((None, blk, LANES), lambda bi, h, t, qt, kt: (bi, kt[t], 2 * nh + h)),
                  pl.BlockSpec((None, 1, LANES), lambda bi, h, t, qt, kt: (h, 0, 0)),
                  pl.BlockSpec((4, HEAD), lambda bi, h, t, qt, kt: (0, 0)),
                  pl.BlockSpec((1, 2 * HEAD), lambda bi, h, t, qt, kt: (0, 0))],
        out_specs=pl.BlockSpec((None, blk, LANES), lambda bi, h, t, qt, kt: (bi, qt[t], h)),
        scratch_shapes=[pltpu.VMEM((2 * blk, LANES), BF16), pltpu.VMEM((2 * blk, LANES), F32),
                        pltpu.VMEM((2 * blk, 2 * LANES), F32)],
    )
    return pl.pallas_call(
        kernel,
        grid_spec=grid_spec,
        out_shape=jax.ShapeDtypeStruct((b, s, DIFF_W), F32),
        compiler_params=_params("parallel", "parallel", "arbitrary"),
        name="diff_attention",
    )(qi_tab, ki_tab, proj3, proj3, proj3, slopes, lam_vecs, subln_g)


def _mem_kv_kernel(m_ref, g_ref, w_ref, k_ref, v_ref):
    d = m_ref.shape[-1]
    kv = _dot(_rms(m_ref[...], g_ref[...]), w_ref[...])
    k_ref[...] = kv[:, :d].astype(BF16)
    v_ref[...] = kv[:, d:].astype(BF16)


def _mem_kv(mem, g, w_ckv_bf16):
    b, m, d = mem.shape
    return pl.pallas_call(
        _mem_kv_kernel,
        grid=(b,),
        in_specs=[pl.BlockSpec((None, m, d), lambda i: (i, 0, 0)),
                  pl.BlockSpec((1, d), lambda i: (0, 0)),
                  pl.BlockSpec((d, 2 * d), lambda i: (0, 0))],
        out_specs=[pl.BlockSpec((None, m, d), lambda i: (i, 0, 0))] * 2,
        out_shape=[jax.ShapeDtypeStruct((b, m, d), BF16)] * 2,
        compiler_params=_params("parallel"),
        name="mem_kv",
    )(mem, g, w_ckv_bf16)


def _mid_kernel(x_ref, yr_ref, yd_ref, wo_ref, gc_ref, wcq_ref, km_ref, vm_ref, wco_ref, gf_ref,
                wr_hi_ref, wr_lo_ref, br_ref, tri_ref,
                x2_ref, hn_ref, ri_ref, rg_ref, cnt_ref, carry_scr):
    tm, d = x_ref.shape
    first = jnp.logical_and(pl.program_id(0) == 0, pl.program_id(1) == 0)

    @pl.when(first)
    def _():
        carry_scr[...] = jnp.zeros_like(carry_scr)

    half = yr_ref.shape[-1]
    x1 = x_ref[...] + _dot(yr_ref[...], wo_ref[:half, :]) + _dot(yd_ref[...], wo_ref[half:, :])

    q = _dot(_rms(x1, gc_ref[...]), wcq_ref[...])
    hd = d // CROSS_HEADS
    outs = []
    for h in range(CROSS_HEADS):
        sl = slice(h * hd, (h + 1) * hd)
        s = _dot_nt(q[:, sl], km_ref[:, sl]) * (hd ** -0.5)
        s = s - jnp.max(s, axis=-1, keepdims=True)
        p = jnp.exp(s)
        p = p / jnp.sum(p, axis=-1, keepdims=True)
        outs.append(_dot(p, vm_ref[:, sl]))
    x2 = x1 + _dot(jnp.concatenate(outs, axis=1), wco_ref[...])
    x2_ref[...] = x2

    hn = _rms(x2, gf_ref[...])
    _store_token_tiles(hn_ref, hn)
    hi = hn.astype(BF16)
    lo = (hn - hi.astype(F32)).astype(BF16)
    logits = (jnp.dot(hi, wr_hi_ref[...], preferred_element_type=F32)
              + jnp.dot(lo, wr_hi_ref[...], preferred_element_type=F32)
              + jnp.dot(hi, wr_lo_ref[...], preferred_element_type=F32) + br_ref[...])
    lane = lax.broadcasted_iota(I32, (tm, LANES), 1)
    vals = logits
    tops, idxs, hots = [], [], []
    for _ in range(TOP_K):
        mx = jnp.max(vals, axis=-1, keepdims=True)
        idx = jnp.min(jnp.where(vals == mx, lane, LANES), axis=-1, keepdims=True)
        hot = lane == idx
        vals = jnp.where(hot, -jnp.inf, vals)
        tops.append(mx)
        idxs.append(idx)
        hots.append(hot)
    es = [jnp.exp(t - tops[0]) for t in tops]
    denom = es[0] + es[1] + es[2] + es[3]
    sel = jnp.zeros((tm, LANES), F32)
    for hot in hots:
        sel = sel + hot.astype(F32)
    before = jnp.dot(tri_ref[...], sel.astype(BF16), preferred_element_type=F32) + carry_scr[...]
    carry_scr[...] = carry_scr[...] + jnp.sum(sel, axis=0, keepdims=True)
    ri = jnp.zeros((tm, LANES), I32)
    rg = jnp.zeros((tm, LANES), F32)
    for j in range(TOP_K):
        rank = jnp.sum(jnp.where(hots[j], before, 0.0), axis=-1, keepdims=True).astype(I32)
        ri = jnp.where(lane == j, idxs[j], ri)
        ri = jnp.where(lane == TOP_K + j, rank, ri)
        rg = jnp.where(lane == j, es[j] / denom, rg)
    ri_ref[...] = ri
    rg_ref[...] = rg
    cnt_ref[...] = jnp.broadcast_to(carry_scr[...], cnt_ref.shape)


def _mid_stage(x, y_rwkv, y_diff, w_out, g_cross, w_cq, k_mem, v_mem, w_co, g_ffn, w_router, b_router):
    b, s, d = x.shape
    tm = min(MID_TM, s)
    m = k_mem.shape[1]
    half = y_rwkv.shape[-1]
    e = w_router.shape[1]
    wr = jnp.zeros((d, LANES), F32).at[:, :e].set(w_router)
    wr_hi = wr.astype(BF16)
    wr_lo = (wr - wr_hi.astype(F32)).astype(BF16)
    br = jnp.full((1, LANES), NEG_BIG, F32).at[0, :e].set(b_router)
    t = jnp.arange(tm, dtype=I32)
    tri = (t[None, :] < t[:, None]).astype(BF16)
    tile = lambda w: pl.BlockSpec((None, tm, w), lambda i, j: (i, j, 0))
    const = lambda shape: pl.BlockSpec(shape, lambda i, j: (0,) * len(shape))
    return pl.pallas_call(
        _mid_kernel,
        grid=(b, s // tm),
        in_specs=[tile(d), tile(half), tile(half), const((d, d)), const((1, d)), const((d, d)),
                  pl.BlockSpec((None, m, d), lambda i, j: (i, 0, 0)),
                  pl.BlockSpec((None, m, d), lambda i, j: (i, 0, 0)),
                  const((d, d)), const((1, d)), const((d, LANES)), const((d, LANES)),
                  const((1, LANES)), const((tm, tm))],
        out_specs=[tile(d), pl.BlockSpec((tm * TILE_ROWS, LANES), lambda i, j: (i * (s // tm) + j, 0)),
                   tile(LANES), tile(LANES), const((8, LANES))],
        out_shape=[jax.ShapeDtypeStruct((b, s, d), F32), jax.ShapeDtypeStruct((b * s * TILE_ROWS, LANES), F32),
                   jax.ShapeDtypeStruct((b, s, LANES), I32), jax.ShapeDtypeStruct((b, s, LANES), F32),
                   jax.ShapeDtypeStruct((8, LANES), F32)],
        scratch_shapes=[pltpu.VMEM((1, LANES), F32)],
        compiler_params=_params("arbitrary", "arbitrary"),
        name="outproj_cross_router",
    )(x, y_rwkv, y_diff, w_out.astype(BF16), g_cross, w_cq.astype(BF16), k_mem, v_mem,
      w_co.astype(BF16), g_ffn, wr_hi, wr_lo, br, tri)


def _expert_kernel(be_ref, nused_ref, x_ref, w1_ref, b1_ref, w2_ref, b2_ref, o_ref):
    i = pl.program_id(0)

    @pl.when(i < nused_ref[0])
    def _():
        x = _load_token_tiles(x_ref, MOE_ROWS).astype(BF16)
        hid = jnp.dot(x, w1_ref[...], preferred_element_type=F32) + b1_ref[...]
        f2 = hid.shape[-1]
        lin = pltpu.roll(hid, f2 - 1, axis=1)
        glu = jnp.minimum(hid, SWIGLU_LIMIT)
        lin = jnp.clip(lin, -SWIGLU_LIMIT, SWIGLU_LIMIT)
        act = glu * jax.nn.sigmoid(SWIGLU_ALPHA * glu) * (lin + 1.0)
        f = f2 // 2
        even = lax.broadcasted_iota(I32, (act.shape[0], f), 1) % 2 == 0
        packed = jnp.where(even, act[:, :f], 0.0) + pltpu.roll(jnp.where(even, act[:, f:], 0.0), 1, axis=1)
        y = jnp.dot(packed.astype(BF16), w2_ref[...], preferred_element_type=F32) + b2_ref[...]
        _store_token_tiles(o_ref, y)

    @pl.when(i >= nused_ref[0])
    def _():
        o_ref[...] = jnp.zeros_like(o_ref)


def _expert_ffn(x_sorted, block_e, n_used, w1, b1, w2p, b2):
    e, d, f2 = w1.shape
    r = MOE_ROWS
    n_rows = x_sorted.shape[0] // TILE_ROWS
    grid_spec = pltpu.PrefetchScalarGridSpec(
        num_scalar_prefetch=2,
        grid=(n_rows // r,),
        in_specs=[pl.BlockSpec((r * TILE_ROWS, LANES), lambda i, be, nu: (i, 0)),
                  pl.BlockSpec((None, d, f2), lambda i, be, nu: (be[i], 0, 0)),
                  pl.BlockSpec((None, 1, f2), lambda i, be, nu: (be[i], 0, 0)),
                  pl.BlockSpec((None, f2 // 2, d), lambda i, be, nu: (be[i], 0, 0)),
                  pl.BlockSpec((None, 1, d), lambda i, be, nu: (be[i], 0, 0))],
        out_specs=pl.BlockSpec((r * TILE_ROWS, LANES), lambda i, be, nu: (i, 0)),
    )
    return pl.pallas_call(
        _expert_kernel,
        grid_spec=grid_spec,
        out_shape=jax.ShapeDtypeStruct((n_rows * TILE_ROWS, LANES), F32),
        compiler_params=_params("arbitrary"),
        name="expert_ffn",
    )(block_e, n_used, x_sorted, w1, b1.reshape(e, 1, f2), w2p, b2.reshape(e, 1, d))


def _tile_at(ref, first_row):
    return ref.at[pl.ds(pl.multiple_of(first_row, TILE_ROWS), TILE_ROWS)]


def _dispatch_kernel(dest_ref, hn_ref, xs_init_hbm, xs_hbm, sem, *, tm):
    del xs_init_hbm

    def issue(t, carry):
        src = _tile_at(hn_ref, t * TILE_ROWS)
        for j in range(TOP_K):
            pltpu.make_async_copy(src, _tile_at(xs_hbm, dest_ref[0, t * TOP_K + j]), sem).start()
        return carry

    lax.fori_loop(0, tm, issue, 0, unroll=8)
    for j in range(TOP_K):
        pltpu.make_async_copy(hn_ref, xs_hbm.at[pl.ds(0, tm * TILE_ROWS)], sem).wait()


def _dispatch(hn_tiles, dest_rows, n_rows):
    t = hn_tiles.shape[0] // TILE_ROWS
    tm = min(DISPATCH_TM, t)
    steps = t // tm
    return pl.pallas_call(
        functools.partial(_dispatch_kernel, tm=tm),
        grid=(steps,),
        in_specs=[pl.BlockSpec((None, 1, tm * TOP_K), lambda i: (i, 0, 0), memory_space=pltpu.SMEM),
                  pl.BlockSpec((tm * TILE_ROWS, LANES), lambda i: (i, 0)), pl.BlockSpec(memory_space=pl.ANY)],
        out_specs=pl.BlockSpec(memory_space=pl.ANY),
        out_shape=jax.ShapeDtypeStruct((n_rows * TILE_ROWS, LANES), F32),
        scratch_shapes=[pltpu.SemaphoreType.DMA(())],
        input_output_aliases={2: 0},
        compiler_params=_params("arbitrary"),
        name="moe_dispatch",
    )(dest_rows.reshape(steps, 1, tm * TOP_K), hn_tiles, jnp.zeros((n_rows * TILE_ROWS, LANES), F32))


def _combine_kernel(dest_ref, x_ref, gate_ref, g_ref, ys_hbm, o_ref, buf, sem, *, tm, final):
    def issue(t, carry):
        for j in range(TOP_K):
            pltpu.make_async_copy(_tile_at(ys_hbm, dest_ref[0, t * TOP_K + j]),
                                  _tile_at(buf.at[j], t * TILE_ROWS), sem).start()
        return carry

    lax.fori_loop(0, tm, issue, 0, unroll=8)
    pltpu.make_async_copy(buf, buf, sem).wait()
    y = x_ref[...]
    gate = gate_ref[...]
    for j in range(TOP_K):
        y = y + gate[:, j:j + 1] * _load_token_tiles(buf, tm, lead=(j,))
    o_ref[...] = _rms(y, g_ref[...]) if final else y


def _combine(x2d, gate2d, dest_rows, y_sorted, g, final):
    t, d = x2d.shape
    tm = min(COMBINE_TM, t)
    steps = t // tm
    return pl.pallas_call(
        functools.partial(_combine_kernel, tm=tm, final=final),
        grid=(steps,),
        in_specs=[pl.BlockSpec((None, 1, tm * TOP_K), lambda i: (i, 0, 0), memory_space=pltpu.SMEM),
                  pl.BlockSpec((tm, d), lambda i: (i, 0)), pl.BlockSpec((tm, LANES), lambda i: (i, 0)),
                  pl.BlockSpec((1, d), lambda i: (0, 0)), pl.BlockSpec(memory_space=pl.ANY)],
        out_specs=pl.BlockSpec((tm, d), lambda i: (i, 0)),
        out_shape=jax.ShapeDtypeStruct((t, d), F32),
        scratch_shapes=[pltpu.VMEM((TOP_K, tm * TILE_ROWS, LANES), F32), pltpu.SemaphoreType.DMA(())],
        compiler_params=_params("arbitrary"),
        name="combine_norm",
    )(dest_rows.reshape(steps, 1, tm * TOP_K), x2d, gate2d, g, y_sorted)


def _layer(x, mem, lyr, final, p):
    b, s, d = x.shape
    t = b * s
    c = RWKV_W
    rwkv_in = 3 * c + LORA_W

    w_in = p["w_in"][lyr]
    w_cat = jnp.concatenate([w_in[:, rwkv_in:], w_in[:, :3 * c], w_in[:, 3 * c:rwkv_in],
                             jnp.zeros((d, LORA_PAD - LORA_W), F32)], axis=1).astype(BF16)
    proj = _norm_inproj(x.reshape(t, d), p["norm_mix_g"][lyr][None], w_cat).reshape(b, s, -1)

    mu = p["shift_mu"][lyr]
    mu_l = jnp.zeros((1, LORA_PAD), F32).at[0, :LORA_W].set(mu[3 * c:])
    wl = jnp.zeros((LORA_PAD, 3 * c), F32)
    wl = wl.at[:DECAY_LORA, :c].set(p["w_decay_up"][lyr])
    wl = wl.at[DECAY_LORA:DECAY_LORA + ICLR_LORA, c:2 * c].set(p["w_iclr_up"][lyr])
    wl = wl.at[DECAY_LORA + ICLR_LORA:LORA_W, 2 * c:].set(p["w_gate_up"][lyr])
    vec = jnp.stack([p["w0"][lyr], p["a0"][lyr], p["k_k"][lyr], p["k_a"][lyr], p["r_k"][lyr].reshape(c),
                     p["lnx_g"][lyr], p["lnx_b"][lyr], jnp.zeros((c,), F32)])
    diff_cols = 3 * DIFF_W
    y_rwkv = _rwkv_group(proj, mu[None, :3 * c], mu_l, wl.astype(BF16), vec,
                         rkv_block=diff_cols // (3 * c), lora_block=(diff_cols + 3 * c) // LORA_PAD)

    lambda_init = 0.8 - 0.6 * math.exp(-0.3 * lyr)
    lam_vecs = jnp.stack([p["lambda_q1"][lyr], p["lambda_k1"][lyr], p["lambda_q2"][lyr], p["lambda_k2"][lyr]])
    y_diff = _diff_attention(proj, lam_vecs, p["subln_g"][lyr][None], lambda_init)

    k_mem, v_mem = _mem_kv(mem, p["norm_mem_g"][lyr][None], p["w_ckv"][lyr].astype(BF16))
    x2, hn, route_i, route_g, counts = _mid_stage(
        x, y_rwkv, y_diff, p["w_out"][lyr], p["norm_cross_g"][lyr][None], p["w_cq"][lyr], k_mem, v_mem,
        p["w_co"][lyr], p["norm_ffn_g"][lyr][None], p["w_router"][lyr], p["b_router"][lyr])

    e = N_EXPERTS
    r = MOE_ROWS
    n_blocks = (t * TOP_K) // r + e
    route_i = route_i.reshape(t, LANES)
    idx, rank = route_i[:, :TOP_K], route_i[:, TOP_K:2 * TOP_K]
    cnt = counts[0, :e].astype(I32)
    padded = (cnt + r - 1) // r * r
    pad_end = jnp.cumsum(padded)
    pad_start = pad_end - padded
    dest = jnp.sum(jnp.where(idx[:, :, None] == jnp.arange(e, dtype=I32), pad_start, 0), axis=-1) + rank
    dest = dest.reshape(-1) * TILE_ROWS
    block_start = jnp.arange(n_blocks, dtype=I32) * r
    block_e = jnp.minimum(jnp.sum((block_start[:, None] >= pad_end[None, :]).astype(I32), axis=1), e - 1)
    n_used = pad_end[-1:] // r

    f = p["w2"].shape[2]
    w2p = jnp.stack([p["w2"][lyr][:, :f // 2], p["w2"][lyr][:, f // 2:]], axis=2).reshape(e, f, d).astype(BF16)
    x_sorted = _dispatch(hn, dest, n_blocks * r)
    y_sorted = _expert_ffn(x_sorted, block_e, n_used, p["w1"][lyr].astype(BF16), p["b1"][lyr], w2p, p["b2"][lyr])
    return _combine(x2.reshape(t, d), route_g.reshape(t, LANES), dest, y_sorted,
                    p["norm_final_g"][None], final).reshape(b, s, d)


def kernel(x, mem, norm_mix_g, w_in, shift_mu, w0, w_decay_up, a0, w_iclr_up, w_gate_up, k_k, k_a, r_k,
           lnx_g, lnx_b, lambda_q1, lambda_k1, lambda_q2, lambda_k2, subln_g, w_out, norm_cross_g,
           norm_mem_g, w_cq, w_ckv, w_co, norm_ffn_g, w_router, b_router, w1, b1, w2, b2, norm_final_g):
    p = dict(norm_mix_g=norm_mix_g, w_in=w_in, shift_mu=shift_mu, w0=w0, w_decay_up=w_decay_up, a0=a0,
             w_iclr_up=w_iclr_up, w_gate_up=w_gate_up, k_k=k_k, k_a=k_a, r_k=r_k, lnx_g=lnx_g, lnx_b=lnx_b,
             lambda_q1=lambda_q1, lambda_k1=lambda_k1, lambda_q2=lambda_q2, lambda_k2=lambda_k2,
             subln_g=subln_g, w_out=w_out, norm_cross_g=norm_cross_g, norm_mem_g=norm_mem_g, w_cq=w_cq,
             w_ckv=w_ckv, w_co=w_co, norm_ffn_g=norm_ffn_g, w_router=w_router, b_router=b_router,
             w1=w1, b1=b1, w2=w2, b2=b2, norm_final_g=norm_final_g)
    depth = w_in.shape[0]
    for lyr in range(depth):
        x = _layer(x, mem, lyr, lyr == depth - 1, p)
    return x
```

```python
import functools
import math

import jax
import jax.numpy as jnp
from jax import lax
from jax.experimental import pallas as pl
from jax.experimental.pallas import tpu as pltpu

F32 = jnp.float32
BF16 = jnp.bfloat16
I32 = jnp.int32

NORM_EPS = 1e-5
HEAD = 64
RWKV_W = 512
RWKV_HEADS = RWKV_W // HEAD
DECAY_LORA, ICLR_LORA, GATE_LORA = 64, 64, 160
LORA_W = DECAY_LORA + ICLR_LORA + GATE_LORA
LORA_PAD = 384
RWKV_GN_EPS = 64e-5
DIFF_W = 512
DIFF_HEADS = DIFF_W // (2 * HEAD)
CROSS_HEADS = 4
N_EXPERTS = 32
TOP_K = 4
SWIGLU_LIMIT = 7.0
SWIGLU_ALPHA = 1.702
LANES = 128
TILE_ROWS = 8
NEG_BIG = -1e30

RWKV_CHUNK = 64
RWKV_BATCH = 2
PROJ_TM, PROJ_TN = 512, 1152
ATT_BLOCK = 512
ATT_ROW_CHUNK = 256
MID_TM = 256
MOE_ROWS = 256
DISPATCH_TM = 512
COMBINE_TM = 256
VMEM_LIMIT = 56 * 1024 * 1024


def _dot(a, b):
    return jnp.dot(a.astype(BF16), b.astype(BF16), preferred_element_type=F32)


def _dot_nt(a, b):
    return lax.dot_general(a.astype(BF16), b.astype(BF16), (((1,), (1,)), ((), ())),
                           preferred_element_type=F32)


def _dot_tn(a, b):
    return lax.dot_general(a.astype(BF16), b.astype(BF16), (((0,), (0,)), ((), ())),
                           preferred_element_type=F32)


def _split2(x):
    hi = x.astype(BF16)
    lo = (x - hi.astype(F32)).astype(BF16)
    return hi, lo


def _dot_exact_rhs(x, ones_bf16):
    hi, lo = _split2(x)
    return jnp.dot(hi, ones_bf16, preferred_element_type=F32) + jnp.dot(lo, ones_bf16, preferred_element_type=F32)


def _dot_exact_lhs(ones_bf16, x):
    hi, lo = _split2(x)
    return jnp.dot(ones_bf16, hi, preferred_element_type=F32) + jnp.dot(ones_bf16, lo, preferred_element_type=F32)


def _store_token_tiles(ref, x, lead=()):
    rows = x.shape[0]
    for c in range(TILE_ROWS):
        ref[lead + (pl.ds(c, rows, stride=TILE_ROWS), slice(None))] = x[:, c * LANES:(c + 1) * LANES]


def _load_token_tiles(ref, rows, lead=()):
    return jnp.concatenate([ref[lead + (pl.ds(c, rows, stride=TILE_ROWS), slice(None))]
                            for c in range(TILE_ROWS)], axis=1)


def _rms(x, g):
    return x * lax.rsqrt(jnp.mean(x * x, axis=-1, keepdims=True) + NORM_EPS) * g


def _params(*sem):
    return pltpu.CompilerParams(dimension_semantics=sem, vmem_limit_bytes=VMEM_LIMIT)


def _norm_inproj_kernel(x_ref, g_ref, w_ref, o_ref, h_scr):
    @pl.when(pl.program_id(1) == 0)
    def _():
        h_scr[...] = _rms(x_ref[...], g_ref[...]).astype(BF16)

    o_ref[...] = jnp.dot(h_scr[...], w_ref[...], preferred_element_type=F32)


def _norm_inproj(x2d, g, w_bf16):
    t, d = x2d.shape
    n = w_bf16.shape[1]
    tm, tn = min(PROJ_TM, t), PROJ_TN
    return pl.pallas_call(
        _norm_inproj_kernel,
        grid=(t // tm, n // tn),
        in_specs=[pl.BlockSpec((tm, d), lambda i, j: (i, 0)),
                  pl.BlockSpec((1, d), lambda i, j: (0, 0)),
                  pl.BlockSpec((d, tn), lambda i, j: (0, j))],
        out_specs=pl.BlockSpec((tm, tn), lambda i, j: (i, j)),
        out_shape=jax.ShapeDtypeStruct((t, n), F32),
        scratch_shapes=[pltpu.VMEM((tm, d), BF16)],
        compiler_params=_params("parallel", "arbitrary"),
        name="norm_inproj",
    )(x2d, g, w_bf16)


def _rwkv_kernel(rkv_ref, lora_ref, mu_rkv_ref, mu_l_ref, wl_ref, vec_ref, bd_ref, tri_ref,
                 o_ref, st_ref, prev_rkv_ref, prev_l_ref):
    L = RWKV_CHUNK
    C = RWKV_W

    @pl.when(pl.program_id(1) == 0)
    def _():
        st_ref[...] = jnp.zeros_like(st_ref)
        prev_rkv_ref[...] = jnp.zeros_like(prev_rkv_ref)
        prev_l_ref[...] = jnp.zeros_like(prev_l_ref)

    row = lax.broadcasted_iota(I32, (L, 1), 0)
    w0, a0, k_k, k_a = vec_ref[0:1, :], vec_ref[1:2, :], vec_ref[2:3, :], vec_ref[3:4, :]
    r_k, lnx_g, lnx_b = vec_ref[4:5, :], vec_ref[5:6, :], vec_ref[6:7, :]
    bd = bd_ref[...]
    tri = tri_ref[...]
    lane_l = lax.broadcasted_iota(I32, (L, LORA_PAD), 1)
    nb = rkv_ref.shape[0]

    def token_shift(raw, prev_ref, bi, mu):
        prev = jnp.where(row == 0, prev_ref[bi], pltpu.roll(raw, 1, axis=0))
        prev_ref[bi] = raw[L - 1:L, :]
        return raw + mu * (prev - raw)

    def elementwise(bi):
        u = token_shift(rkv_ref[bi], prev_rkv_ref, bi, mu_rkv_ref[...])
        ul = token_shift(lora_ref[bi], prev_l_ref, bi, mu_l_ref[...])
        r, k, v = u[:, :C], u[:, C:2 * C], u[:, 2 * C:]
        act = jnp.where(lane_l < DECAY_LORA, jnp.tanh(ul),
                        jnp.where(lane_l < DECAY_LORA + ICLR_LORA, ul, jax.nn.sigmoid(ul)))
        lo = _dot(act, wl_ref[...])
        w_log = -jax.nn.softplus(-(w0 + lo[:, :C])) - 0.5
        lw = -jnp.exp(w_log)
        a = jax.nn.sigmoid(a0 + lo[:, C:2 * C])
        kk = k * k_k
        kk = kk / jnp.maximum(jnp.sqrt(_dot_exact_rhs(kk * kk, bd)), 1e-12)
        k2 = k * (1.0 + (a - 1.0) * k_a)
        cum = _dot_exact_lhs(tri, lw)
        total = cum[L - 1:L, :]
        e_neg = jnp.exp(-cum)
        e_rem = jnp.exp(total - cum)
        kka = kk * a
        return dict(r=r, v=v, k2=k2, g=lo[:, 2 * C:], a_t=-kk * jnp.exp(cum - lw), b_t=kka * e_neg,
                    k_t=k2 * e_neg, r_t=r * jnp.exp(cum), b_bar=kka * e_rem, k_bar=k2 * e_rem,
                    p_total=jnp.exp(total))

    ew = [elementwise(bi) for bi in range(nb)]

    m0 = lax.broadcasted_iota(I32, (L, LANES), 1) < HEAD
    r2i = lax.broadcasted_iota(I32, (2 * L, 4 * L), 0)
    c2i = lax.broadcasted_iota(I32, (2 * L, 4 * L), 1) % (2 * L)
    strict, incl = c2i < r2i, c2i <= r2i
    npairs = RWKV_HEADS // 2
    chains = [(bi, p) for bi in range(nb) for p in range(npairs)]
    nc = range(len(chains))

    def stack2(name):
        out = []
        for bi, p in chains:
            xp = ew[bi][name][:, p * LANES:(p + 1) * LANES]
            out.append(jnp.concatenate([jnp.where(m0, xp, 0.0), jnp.where(m0, 0.0, xp)], axis=0))
        return out

    A2, B2, K2, R2 = stack2("a_t"), stack2("b_t"), stack2("k_t"), stack2("r_t")
    V2, Bb2, Kb2 = stack2("v"), stack2("b_bar"), stack2("k_bar")
    mg = [_dot_nt(jnp.concatenate([A2[c], R2[c]], axis=0), jnp.concatenate([B2[c], K2[c]], axis=0)) for c in nc]
    m_cat = [jnp.where(strict, mg[c][:2 * L], 0.0) for c in nc]
    g_cat = [jnp.where(incl, mg[c][2 * L:], 0.0) for c in nc]
    m_ab = [m_cat[c][:, :LANES] for c in nc]
    pw = [_dot(m_ab[c], m_ab[c]) for c in nc]
    tr = list(m_ab)
    k = 2
    while 2 * k < L:
        x = [_dot(pw[c], jnp.concatenate([pw[c], tr[c]], axis=1)) for c in nc]
        tr = [tr[c] + pw[c] + x[c][:, LANES:] for c in nc]
        pw = [x[c][:, :LANES] for c in nc]
        k *= 2
    tr = [tr[c] + pw[c] + _dot(pw[c], tr[c]) for c in nc]
    az = [jnp.concatenate([A2[c], _dot(m_cat[c][:, LANES:], V2[c])], axis=1) for c in nc]
    tz = [az[c] + _dot(tr[c], az[c]) for c in nc]
    st = [st_ref[bi, p] for bi, p in chains]
    u2 = [_dot_nt(tz[c][:, :LANES], st[c]) + tz[c][:, LANES:] for c in nc]
    uv = [jnp.concatenate([u2[c], V2[c]], axis=0) for c in nc]
    y2 = [_dot_nt(R2[c], st[c]) + _dot(g_cat[c], uv[c]) for c in nc]
    for c, (bi, p) in enumerate(chains):
        st_ref[bi, p] = (st[c] * ew[bi]["p_total"][:, p * LANES:(p + 1) * LANES]
                         + _dot_tn(uv[c], jnp.concatenate([Bb2[c], Kb2[c]], axis=0)))

    inv_n = 1.0 / HEAD
    for bi in range(nb):
        y = jnp.concatenate([y2[bi * npairs + p][:L] + y2[bi * npairs + p][L:] for p in range(npairs)], axis=1)
        e = ew[bi]
        mu = _dot_exact_rhs(y, bd) * inv_n
        d = y - mu
        var = _dot_exact_rhs(d * d, bd) * inv_n
        yn = d * lax.rsqrt(var + RWKV_GN_EPS) * lnx_g + lnx_b
        bonus = _dot_exact_rhs(e["r"] * e["k2"] * r_k, bd) * e["v"]
        o_ref[bi] = (yn + bonus) * e["g"]


def _rwkv_group(proj3, mu_rkv, mu_l, wl, vec, rkv_block, lora_block):
    b, s, _ = proj3.shape
    L, C = RWKV_CHUNK, RWKV_W
    head = jnp.arange(C, dtype=I32) // HEAD
    bd = (head[:, None] == head[None, :]).astype(BF16)
    t = jnp.arange(L, dtype=I32)
    tri = (t[None, :] <= t[:, None]).astype(BF16)
    const = lambda shape: pl.BlockSpec(shape, lambda i, j: (0,) * len(shape))
    nb = RWKV_BATCH if b % RWKV_BATCH == 0 else 1
    return pl.pallas_call(
        _rwkv_kernel,
        grid=(b // nb, s // L),
        in_specs=[pl.BlockSpec((nb, L, 3 * C), lambda i, j: (i, j, rkv_block)),
                  pl.BlockSpec((nb, L, LORA_PAD), lambda i, j: (i, j, lora_block)),
                  const((1, 3 * C)), const((1, LORA_PAD)), const((LORA_PAD, 3 * C)),
                  const((8, C)), const((C, C)), const((L, L))],
        out_specs=pl.BlockSpec((nb, L, C), lambda i, j: (i, j, 0)),
        out_shape=jax.ShapeDtypeStruct((b, s, C), F32),
        scratch_shapes=[pltpu.VMEM((nb, RWKV_HEADS // 2, LANES, LANES), F32),
                        pltpu.VMEM((nb, 1, 3 * C), F32), pltpu.VMEM((nb, 1, LORA_PAD), F32)],
        compiler_params=_params("parallel", "arbitrary"),
        name="rwkv_group",
    )(proj3, proj3, mu_rkv, mu_l, wl, vec, bd, tri)


def _diff_attn_kernel(qi_ref, ki_ref, q_ref, k_ref, v_ref, slope_ref, lam_ref, g_ref, o_ref,
                      q2_scr, m_scr, acc_scr, *, lambda_init):
    blk = ATT_BLOCK
    step_id = pl.program_id(2)
    qi, ki = qi_ref[step_id], ki_ref[step_id]
    log2e = math.log2(math.e)

    @pl.when(ki == 0)
    def _():
        q = q_ref[...] * (HEAD ** -0.5 * log2e)
        m0 = lax.broadcasted_iota(I32, (blk, LANES), 1) < HEAD
        q2_scr[0:blk, :] = jnp.where(m0, q, 0.0).astype(BF16)
        q2_scr[blk:, :] = jnp.where(m0, 0.0, q).astype(BF16)
        m_scr[...] = jnp.full_like(m_scr, NEG_BIG)
        acc_scr[...] = jnp.zeros_like(acc_scr)

    def step(on_diagonal):
        k = k_ref[...].astype(BF16)
        v_ext = jnp.concatenate([v_ref[...].astype(BF16), jnp.ones((blk, LANES), BF16)], axis=1)
        koff = lax.broadcasted_iota(I32, (1, blk), 1)
        col_bias = (slope_ref[:, :1] * log2e) * ((ki - qi) * blk + koff).astype(F32)
        rc = ATT_ROW_CHUNK
        for c in range(2 * blk // rc):
            rows = slice(c * rc, (c + 1) * rc)
            s = lax.dot_general(q2_scr[rows, :], k, (((1,), (1,)), ((), ())),
                                preferred_element_type=F32) + col_bias
            if on_diagonal:
                qoff = (c * rc) % blk + lax.broadcasted_iota(I32, (rc, 1), 0)
                s = jnp.where(koff <= qoff, s, NEG_BIG)
            m_prev = m_scr[rows, :]
            m_new = jnp.maximum(m_prev, jnp.max(s, axis=-1, keepdims=True))
            alpha = jnp.exp2(m_prev - m_new)
            p = jnp.exp2(s - jnp.concatenate([m_new] * (blk // LANES), axis=1))
            acc_scr[rows, :] = (jnp.concatenate([alpha, alpha], axis=1) * acc_scr[rows, :]
                                + jnp.dot(p.astype(BF16), v_ext, preferred_element_type=F32))
            m_scr[rows, :] = m_new

    @pl.when(ki < qi)
    def _():
        step(False)

    @pl.when(ki == qi)
    def _():
        step(True)
        lam_v = lam_ref[...]
        lam = (jnp.exp(jnp.sum(lam_v[0:1] * lam_v[1:2], axis=-1, keepdims=True))
               - jnp.exp(jnp.sum(lam_v[2:3] * lam_v[3:4], axis=-1, keepdims=True)) + lambda_init)
        o2 = acc_scr[:, :LANES] / acc_scr[:, LANES:]
        o = o2[:blk] - lam * o2[blk:]
        o_ref[...] = _rms(o, g_ref[...]) * (1.0 - lambda_init)


def _diff_attention(proj3, lam_vecs, subln_g, lambda_init):
    b, s, _ = proj3.shape
    blk = ATT_BLOCK
    nb = s // blk
    nh = DIFF_HEADS
    slopes = jnp.exp2(-8.0 * jnp.arange(1, nh + 1, dtype=F32) / nh)
    slopes = jnp.broadcast_to(slopes[:, None, None], (nh, 1, LANES))
    pairs = [(qi, ki) for qi in range(nb) for ki in range(qi + 1)]
    qi_tab = jnp.asarray([pq for pq, _ in pairs], I32)
    ki_tab = jnp.asarray([pk for _, pk in pairs], I32)
    kernel = functools.partial(_diff_attn_kernel, lambda_init=lambda_init)
    grid_spec = pltpu.PrefetchScalarGridSpec(
        num_scalar_prefetch=2,
        grid=(b, nh, len(pairs)),
        in_specs=[pl.BlockSpec((None, blk, LANES), lambda bi, h, t, qt, kt: (bi, qt[t], h)),
                  pl.BlockSpec((None, blk, LANES), lambda bi, h, t, qt, kt: (bi, kt[t], nh + h)),
                  pl.BlockSpec((None, blk, LANES), lambda bi, h, t, qt, kt: (bi, kt[t], 2 * nh + h)),
                  pl.BlockSpec((None, 1, LANES), lambda bi, h, t, qt, kt: (h, 0, 0)),
                  pl.BlockSpec((4, HEAD), lambda bi, h, t, qt, kt: (0, 0)),
                  pl.BlockSpec((1, 2 * HEAD), lambda bi, h, t, qt, kt: (0, 0))],
        out_specs=pl.BlockSpec((None, blk, LANES), lambda bi, h, t, qt, kt: (bi, qt[t], h)),
        scratch_shapes=[pltpu.VMEM((2 * blk, LANES), BF16), pltpu.VMEM((2 * blk, LANES), F32),
                        pltpu.VMEM((2 * blk, 2 * LANES), F32)],
    )
    return pl.pallas_call(
        kernel,
        grid_spec=grid_spec,
        out_shape=jax.ShapeDtypeStruct((b, s, DIFF_W), F32),
        compiler_params=_params("parallel", "parallel", "arbitrary"),
        name="diff_attention",
    )(qi_tab, ki_tab, proj3, proj3, proj3, slopes, lam_vecs, subln_g)


def _mem_kv_kernel(m_ref, g_ref, w_ref, k_ref, v_ref):
    d = m_ref.shape[-1]
    kv = _dot(_rms(m_ref[...], g_ref[...]), w_ref[...])
    k_ref[...] = kv[:, :d].astype(BF16)
    v_ref[...] = kv[:, d:].astype(BF16)


def _mem_kv(mem, g, w_ckv_bf16):
    b, m, d = mem.shape
    return pl.pallas_call(
        _mem_kv_kernel,
        grid=(b,),
        in_specs=[pl.BlockSpec((None, m, d), lambda i: (i, 0, 0)),
                  pl.BlockSpec((1, d), lambda i: (0, 0)),
                  pl.BlockSpec((d, 2 * d), lambda i: (0, 0))],
        out_specs=[pl.BlockSpec((None, m, d), lambda i: (i, 0, 0))] * 2,
        out_shape=[jax.ShapeDtypeStruct((b, m, d), BF16)] * 2,
        compiler_params=_params("parallel"),
        name="mem_kv",
    )(mem, g, w_ckv_bf16)


def _mid_kernel(x_ref, yr_ref, yd_ref, wo_ref, gc_ref, wcq_ref, km_ref, vm_ref, wco_ref, gf_ref,
                wr_hi_ref, wr_lo_ref, br_ref, tri_ref,
                x2_ref, hn_ref, ri_ref, rg_ref, cnt_ref, carry_scr):
    tm, d = x_ref.shape
    first = jnp.logical_and(pl.program_id(0) == 0, pl.program_id(1) == 0)

    @pl.when(first)
    def _():
        carry_scr[...] = jnp.zeros_like(carry_scr)

    half = yr_ref.shape[-1]
    x1 = x_ref[...] + _dot(yr_ref[...], wo_ref[:half, :]) + _dot(yd_ref[...], wo_ref[half:, :])

    q = _dot(_rms(x1, gc_ref[...]), wcq_ref[...])
    hd = d // CROSS_HEADS
    outs = []
    for h in range(CROSS_HEADS):
        sl = slice(h * hd, (h + 1) * hd)
        s = _dot_nt(q[:, sl], km_ref[:, sl]) * (hd ** -0.5)
        s = s - jnp.max(s, axis=-1, keepdims=True)
        p = jnp.exp(s)
        p = p / jnp.sum(p, axis=-1, keepdims=True)
        outs.append(_dot(p, vm_ref[:, sl]))
    x2 = x1 + _dot(jnp.concatenate(outs, axis=1), wco_ref[...])
    x2_ref[...] = x2

    hn = _rms(x2, gf_ref[...])
    _store_token_tiles(hn_ref, hn)
    hi = hn.astype(BF16)
    lo = (hn - hi.astype(F32)).astype(BF16)
    logits = (jnp.dot(hi, wr_hi_ref[...], preferred_element_type=F32)
              + jnp.dot(lo, wr_hi_ref[...], preferred_element_type=F32)
              + jnp.dot(hi, wr_lo_ref[...], preferred_element_type=F32) + br_ref[...])
    lane = lax.broadcasted_iota(I32, (tm, LANES), 1)
    vals = logits
    tops, idxs, hots = [], [], []
    for _ in range(TOP_K):
        mx = jnp.max(vals, axis=-1, keepdims=True)
        idx = jnp.min(jnp.where(vals == mx, lane, LANES), axis=-1, keepdims=True)
        hot = lane == idx
        vals = jnp.where(hot, -jnp.inf, vals)
        tops.append(mx)
        idxs.append(idx)
        hots.append(hot)
    es = [jnp.exp(t - tops[0]) for t in tops]
    denom = es[0] + es[1] + es[2] + es[3]
    sel = jnp.zeros((tm, LANES), F32)
    for hot in hots:
        sel = sel + hot.astype(F32)
    before = jnp.dot(tri_ref[...], sel.astype(BF16), preferred_element_type=F32) + carry_scr[...]
    carry_scr[...] = carry_scr[...] + jnp.sum(sel, axis=0, keepdims=True)
    ri = jnp.zeros((tm, LANES), I32)
    rg = jnp.zeros((tm, LANES), F32)
    for j in range(TOP_K):
        rank = jnp.sum(jnp.where(hots[j], before, 0.0), axis=-1, keepdims=True).astype(I32)
        ri = jnp.where(lane == j, idxs[j], ri)
        ri = jnp.where(lane == TOP_K + j, rank, ri)
        rg = jnp.where(lane == j, es[j] / denom, rg)
    ri_ref[...] = ri
    rg_ref[...] = rg
    cnt_ref[...] = jnp.broadcast_to(carry_scr[...], cnt_ref.shape)


def _mid_stage(x, y_rwkv, y_diff, w_out, g_cross, w_cq, k_mem, v_mem, w_co, g_ffn, w_router, b_router):
    b, s, d = x.shape
    tm = min(MID_TM, s)
    m = k_mem.shape[1]
    half = y_rwkv.shape[-1]
    e = w_router.shape[1]
    wr = jnp.zeros((d, LANES), F32).at[:, :e].set(w_router)
    wr_hi = wr.astype(BF16)
    wr_lo = (wr - wr_hi.astype(F32)).astype(BF16)
    br = jnp.full((1, LANES), NEG_BIG, F32).at[0, :e].set(b_router)
    t = jnp.arange(tm, dtype=I32)
    tri = (t[None, :] < t[:, None]).astype(BF16)
    tile = lambda w: pl.BlockSpec((None, tm, w), lambda i, j: (i, j, 0))
    const = lambda shape: pl.BlockSpec(shape, lambda i, j: (0,) * len(shape))
    return pl.pallas_call(
        _mid_kernel,
        grid=(b, s // tm),
        in_specs=[tile(d), tile(half), tile(half), const((d, d)), const((1, d)), const((d, d)),
                  pl.BlockSpec((None, m, d), lambda i, j: (i, 0, 0)),
                  pl.BlockSpec((None, m, d), lambda i, j: (i, 0, 0)),
                  const((d, d)), const((1, d)), const((d, LANES)), const((d, LANES)),
                  const((1, LANES)), const((tm, tm))],
        out_specs=[tile(d), pl.BlockSpec((tm * TILE_ROWS, LANES), lambda i, j: (i * (s // tm) + j, 0)),
                   tile(LANES), tile(LANES), const((8, LANES))],
        out_shape=[jax.ShapeDtypeStruct((b, s, d), F32), jax.ShapeDtypeStruct((b * s * TILE_ROWS, LANES), F32),
                   jax.ShapeDtypeStruct((b, s, LANES), I32), jax.ShapeDtypeStruct((b, s, LANES), F32),
                   jax.ShapeDtypeStruct((8, LANES), F32)],
        scratch_shapes=[pltpu.VMEM((1, LANES), F32)],
        compiler_params=_params("arbitrary", "arbitrary"),
        name="outproj_cross_router",
    )(x, y_rwkv, y_diff, w_out.astype(BF16), g_cross, w_cq.astype(BF16), k_mem, v_mem,
      w_co.astype(BF16), g_ffn, wr_hi, wr_lo, br, tri)


def _expert_kernel(be_ref, nused_ref, x_ref, w1_ref, b1_ref, w2_ref, b2_ref, o_ref, w1_scr, w2_scr, w2i_scr):
    i = pl.program_id(0)
    used = i < nused_ref[0]
    new_expert = jnp.logical_or(i == 0, be_ref[i] != be_ref[jnp.maximum(i - 1, 0)])

    @pl.when(jnp.logical_and(used, new_expert))
    def _():
        d, f = w1_ref.shape[0], w2_ref.shape[0]
        step = 128

        def cast_rows(c, carry):
            rows = pl.ds(pl.multiple_of(c * step, step), step)
            w1_scr[rows, :] = w1_ref[rows, :].astype(BF16)
            return carry

        lax.fori_loop(0, d // step, cast_rows, 0)
        for cb in range(w2_ref.shape[1] // LANES):
            cols = slice(cb * LANES, (cb + 1) * LANES)
            w2i_scr[cb, pl.ds(0, f // 2, stride=2), :] = w2_ref[:f // 2, cols]
            w2i_scr[cb, pl.ds(1, f // 2, stride=2), :] = w2_ref[f // 2:, cols]
            w2_scr[:, cols] = w2i_scr[cb].astype(BF16)

    @pl.when(used)
    def _():
        x = _load_token_tiles(x_ref, MOE_ROWS).astype(BF16)
        hid = jnp.dot(x, w1_scr[...], preferred_element_type=F32) + b1_ref[...]
        f2 = hid.shape[-1]
        lin = pltpu.roll(hid, f2 - 1, axis=1)
        glu = jnp.minimum(hid, SWIGLU_LIMIT)
        lin = jnp.clip(lin, -SWIGLU_LIMIT, SWIGLU_LIMIT)
        act = glu * jax.nn.sigmoid(SWIGLU_ALPHA * glu) * (lin + 1.0)
        f = f2 // 2
        even = lax.broadcasted_iota(I32, (act.shape[0], f), 1) % 2 == 0
        packed = jnp.where(even, act[:, :f], 0.0) + pltpu.roll(jnp.where(even, act[:, f:], 0.0), 1, axis=1)
        y = jnp.dot(packed.astype(BF16), w2_scr[...], preferred_element_type=F32) + b2_ref[...]
        _store_token_tiles(o_ref, y)

    @pl.when(jnp.logical_not(used))
    def _():
        o_ref[...] = jnp.zeros_like(o_ref)


def _expert_ffn(x_sorted, block_e, n_used, w1, b1, w2, b2):
    e, d, f2 = w1.shape
    r = MOE_ROWS
    n_rows = x_sorted.shape[0] // TILE_ROWS
    grid_spec = pltpu.PrefetchScalarGridSpec(
        num_scalar_prefetch=2,
        grid=(n_rows // r,),
        in_specs=[pl.BlockSpec((r * TILE_ROWS, LANES), lambda i, be, nu: (i, 0)),
                  pl.BlockSpec((None, d, f2), lambda i, be, nu: (be[i], 0, 0)),
                  pl.BlockSpec((None, 1, f2), lambda i, be, nu: (be[i], 0, 0)),
                  pl.BlockSpec((None, f2 // 2, d), lambda i, be, nu: (be[i], 0, 0)),
                  pl.BlockSpec((None, 1, d), lambda i, be, nu: (be[i], 0, 0))],
        out_specs=pl.BlockSpec((r * TILE_ROWS, LANES), lambda i, be, nu: (i, 0)),
        scratch_shapes=[pltpu.VMEM((d, f2), BF16), pltpu.VMEM((f2 // 2, d), BF16),
                        pltpu.VMEM((d // LANES, f2 // 2, LANES), F32)],
    )
    return pl.pallas_call(
        _expert_kernel,
        grid_spec=grid_spec,
        out_shape=jax.ShapeDtypeStruct((n_rows * TILE_ROWS, LANES), F32),
        compiler_params=_params("arbitrary"),
        name="expert_ffn",
    )(block_e, n_used, x_sorted, w1, b1.reshape(e, 1, f2), w2, b2.reshape(e, 1, d))


def _tile_at(ref, first_row):
    return ref.at[pl.ds(pl.multiple_of(first_row, TILE_ROWS), TILE_ROWS)]


def _dispatch_kernel(pad_lo_ref, pad_hi_ref, dest_ref, hn_ref, xs_hbm, zero_scr, sem, pad_sem, *, tm):
    @pl.when(pl.program_id(0) == 0)
    def _():
        zero_scr[...] = jnp.zeros_like(zero_scr)

        def pad_copy(row):
            return pltpu.make_async_copy(zero_scr, _tile_at(xs_hbm, row * TILE_ROWS), pad_sem)

        def per_expert(e, carry):
            lo, hi = pad_lo_ref[e], pad_hi_ref[e]
            lax.fori_loop(lo, hi, lambda r, c: (pad_copy(r).start(), c)[1], 0)
            lax.fori_loop(lo, hi, lambda r, c: (pad_copy(r).wait(), c)[1], 0)
            return carry

        lax.fori_loop(0, pad_lo_ref.shape[0], per_expert, 0)

    def issue(t, carry):
        src = _tile_at(hn_ref, t * TILE_ROWS)
        for j in range(TOP_K):
            pltpu.make_async_copy(src, _tile_at(xs_hbm, dest_ref[0, t * TOP_K + j]), sem).start(priority=j % 2)
        return carry

    lax.fori_loop(0, tm, issue, 0, unroll=8)
    for j in range(TOP_K):
        pltpu.make_async_copy(hn_ref, xs_hbm.at[pl.ds(0, tm * TILE_ROWS)], sem).wait()


def _dispatch(hn_tiles, dest_rows, pad_lo, pad_hi, n_rows):
    t = hn_tiles.shape[0] // TILE_ROWS
    tm = min(DISPATCH_TM, t)
    steps = t // tm
    grid_spec = pltpu.PrefetchScalarGridSpec(
        num_scalar_prefetch=2,
        grid=(steps,),
        in_specs=[pl.BlockSpec((None, 1, tm * TOP_K), lambda i, lo, hi: (i, 0, 0), memory_space=pltpu.SMEM),
                  pl.BlockSpec((tm * TILE_ROWS, LANES), lambda i, lo, hi: (i, 0))],
        out_specs=pl.BlockSpec(memory_space=pl.ANY),
        scratch_shapes=[pltpu.VMEM((TILE_ROWS, LANES), F32), pltpu.SemaphoreType.DMA(()),
                        pltpu.SemaphoreType.DMA(())],
    )
    return pl.pallas_call(
        functools.partial(_dispatch_kernel, tm=tm),
        grid_spec=grid_spec,
        out_shape=jax.ShapeDtypeStruct((n_rows * TILE_ROWS, LANES), F32),
        compiler_params=_params("arbitrary"),
        name="moe_dispatch",
    )(pad_lo, pad_hi, dest_rows.reshape(steps, 1, tm * TOP_K), hn_tiles)


def _combine_kernel(dest_ref, x_ref, gate_ref, g_ref, ys_hbm, o_ref, buf, sem, *, tm, final):
    def issue(t, carry):
        for j in range(TOP_K):
            pltpu.make_async_copy(_tile_at(ys_hbm, dest_ref[0, t * TOP_K + j]),
                                  _tile_at(buf.at[j], t * TILE_ROWS), sem).start(priority=j % 2)
        return carry

    lax.fori_loop(0, tm, issue, 0, unroll=8)
    pltpu.make_async_copy(buf, buf, sem).wait()
    y = x_ref[...]
    gate = gate_ref[...]
    for j in range(TOP_K):
        y = y + gate[:, j:j + 1] * _load_token_tiles(buf, tm, lead=(j,))
    o_ref[...] = _rms(y, g_ref[...]) if final else y


def _combine(x2d, gate2d, dest_rows, y_sorted, g, final):
    t, d = x2d.shape
    tm = min(COMBINE_TM, t)
    steps = t // tm
    return pl.pallas_call(
        functools.partial(_combine_kernel, tm=tm, final=final),
        grid=(steps,),
        in_specs=[pl.BlockSpec((None, 1, tm * TOP_K), lambda i: (i, 0, 0), memory_space=pltpu.SMEM),
                  pl.BlockSpec((tm, d), lambda i: (i, 0)), pl.BlockSpec((tm, LANES), lambda i: (i, 0)),
                  pl.BlockSpec((1, d), lambda i: (0, 0)), pl.BlockSpec(memory_space=pl.ANY)],
        out_specs=pl.BlockSpec((tm, d), lambda i: (i, 0)),
        out_shape=jax.ShapeDtypeStruct((t, d), F32),
        scratch_shapes=[pltpu.VMEM((TOP_K, tm * TILE_ROWS, LANES), F32), pltpu.SemaphoreType.DMA(())],
        compiler_params=_params("arbitrary"),
        name="combine_norm",
    )(dest_rows.reshape(steps, 1, tm * TOP_K), x2d, gate2d, g, y_sorted)


def _layer(x, mem, lyr, final, p):
    b, s, d = x.shape
    t = b * s
    c = RWKV_W
    rwkv_in = 3 * c + LORA_W

    w_in = p["w_in"][lyr]
    w_cat = jnp.concatenate([w_in[:, rwkv_in:], w_in[:, :3 * c], w_in[:, 3 * c:rwkv_in],
                             jnp.zeros((d, LORA_PAD - LORA_W), F32)], axis=1).astype(BF16)
    proj = _norm_inproj(x.reshape(t, d), p["norm_mix_g"][lyr][None], w_cat).reshape(b, s, -1)

    mu = p["shift_mu"][lyr]
    mu_l = jnp.zeros((1, LORA_PAD), F32).at[0, :LORA_W].set(mu[3 * c:])
    wl = jnp.zeros((LORA_PAD, 3 * c), F32)
    wl = wl.at[:DECAY_LORA, :c].set(p["w_decay_up"][lyr])
    wl = wl.at[DECAY_LORA:DECAY_LORA + ICLR_LORA, c:2 * c].set(p["w_iclr_up"][lyr])
    wl = wl.at[DECAY_LORA + ICLR_LORA:LORA_W, 2 * c:].set(p["w_gate_up"][lyr])
    vec = jnp.stack([p["w0"][lyr], p["a0"][lyr], p["k_k"][lyr], p["k_a"][lyr], p["r_k"][lyr].reshape(c),
                     p["lnx_g"][lyr], p["lnx_b"][lyr], jnp.zeros((c,), F32)])
    diff_cols = 3 * DIFF_W
    y_rwkv = _rwkv_group(proj, mu[None, :3 * c], mu_l, wl.astype(BF16), vec,
                         rkv_block=diff_cols // (3 * c), lora_block=(diff_cols + 3 * c) // LORA_PAD)

    lambda_init = 0.8 - 0.6 * math.exp(-0.3 * lyr)
    lam_vecs = jnp.stack([p["lambda_q1"][lyr], p["lambda_k1"][lyr], p["lambda_q2"][lyr], p["lambda_k2"][lyr]])
    y_diff = _diff_attention(proj, lam_vecs, p["subln_g"][lyr][None], lambda_init)

    k_mem, v_mem = _mem_kv(mem, p["norm_mem_g"][lyr][None], p["w_ckv"][lyr].astype(BF16))
    x2, hn, route_i, route_g, counts = _mid_stage(
        x, y_rwkv, y_diff, p["w_out"][lyr], p["norm_cross_g"][lyr][None], p["w_cq"][lyr], k_mem, v_mem,
        p["w_co"][lyr], p["norm_ffn_g"][lyr][None], p["w_router"][lyr], p["b_router"][lyr])

    e = N_EXPERTS
    r = MOE_ROWS
    n_blocks = (t * TOP_K) // r + e
    route_i = route_i.reshape(t, LANES)
    idx, rank = route_i[:, :TOP_K], route_i[:, TOP_K:2 * TOP_K]
    cnt = counts[0, :e].astype(I32)
    padded = (cnt + r - 1) // r * r
    pad_end = jnp.cumsum(padded)
    pad_start = pad_end - padded
    dest = jnp.sum(jnp.where(idx[:, :, None] == jnp.arange(e, dtype=I32), pad_start, 0), axis=-1) + rank
    dest = dest.reshape(-1) * TILE_ROWS
    block_start = jnp.arange(n_blocks, dtype=I32) * r
    block_e = jnp.minimum(jnp.sum((block_start[:, None] >= pad_end[None, :]).astype(I32), axis=1), e - 1)
    n_used = pad_end[-1:] // r

    unused_lo = jnp.concatenate([pad_start + cnt, pad_end[-1:]])
    unused_hi = jnp.concatenate([pad_end, jnp.full((1,), n_blocks * r, I32)])
    x_sorted = _dispatch(hn, dest, unused_lo, unused_hi, n_blocks * r)
    y_sorted = _expert_ffn(x_sorted, block_e, n_used, p["w1"][lyr], p["b1"][lyr], p["w2"][lyr], p["b2"][lyr])
    return _combine(x2.reshape(t, d), route_g.reshape(t, LANES), dest, y_sorted,
                    p["norm_final_g"][None], final).reshape(b, s, d)


def kernel(x, mem, norm_mix_g, w_in, shift_mu, w0, w_decay_up, a0, w_iclr_up, w_gate_up, k_k, k_a, r_k,
           lnx_g, lnx_b, lambda_q1, lambda_k1, lambda_q2, lambda_k2, subln_g, w_out, norm_cross_g,
           norm_mem_g, w_cq, w_ckv, w_co, norm_ffn_g, w_router, b_router, w1, b1, w2, b2, norm_final_g):
    p = dict(norm_mix_g=norm_mix_g, w_in=w_in, shift_mu=shift_mu, w0=w0, w_decay_up=w_decay_up, a0=a0,
             w_iclr_up=w_iclr_up, w_gate_up=w_gate_up, k_k=k_k, k_a=k_a, r_k=r_k, lnx_g=lnx_g, lnx_b=lnx_b,
             lambda_q1=lambda_q1, lambda_k1=lambda_k1, lambda_q2=lambda_q2, lambda_k2=lambda_k2,
             subln_g=subln_g, w_out=w_out, norm_cross_g=norm_cross_g, norm_mem_g=norm_mem_g, w_cq=w_cq,
             w_ckv=w_ckv, w_co=w_co, norm_ffn_g=norm_ffn_g, w_router=w_router, b_router=b_router,
             w1=w1, b1=b1, w2=w2, b2=b2, norm_final_g=norm_final_g)
    depth = w_in.shape[0]
    for lyr in range(depth):
        x = _layer(x, mem, lyr, lyr == depth - 1, p)
    return x
```

```python
import functools
import math

import jax
import jax.numpy as jnp
from jax import lax
from jax.experimental import pallas as pl
from jax.experimental.pallas import tpu as pltpu

F32 = jnp.float32
BF16 = jnp.bfloat16
I32 = jnp.int32

NORM_EPS = 1e-5
HEAD = 64
RWKV_W = 512
RWKV_HEADS = RWKV_W // HEAD
DECAY_LORA, ICLR_LORA, GATE_LORA = 64, 64, 160
LORA_W = DECAY_LORA + ICLR_LORA + GATE_LORA
LORA_PAD = 384
RWKV_GN_EPS = 64e-5
DIFF_W = 512
DIFF_HEADS = DIFF_W // (2 * HEAD)
CROSS_HEADS = 4
N_EXPERTS = 32
TOP_K = 4
SWIGLU_LIMIT = 7.0
SWIGLU_ALPHA = 1.702
LANES = 128
TILE_ROWS = 8
NEG_BIG = -1e30

RWKV_CHUNK = 64
RWKV_BATCH = 2
PROJ_TM = 512
ATT_BLOCK = 512
ATT_ROW_CHUNK = 128
MID_TM = 512
MID_CHAIN_ROWS = 256
MOE_ROWS = 256
DISPATCH_TM = 512
COMBINE_TM = 256
VMEM_LIMIT = 56 * 1024 * 1024


def _dot(a, b):
    return jnp.dot(a.astype(BF16), b.astype(BF16), preferred_element_type=F32)


def _dot_nt(a, b):
    return lax.dot_general(a.astype(BF16), b.astype(BF16), (((1,), (1,)), ((), ())),
                           preferred_element_type=F32)


def _dot_tn(a, b):
    return lax.dot_general(a.astype(BF16), b.astype(BF16), (((0,), (0,)), ((), ())),
                           preferred_element_type=F32)


def _split2(x):
    hi = x.astype(BF16)
    lo = (x - hi.astype(F32)).astype(BF16)
    return hi, lo


def _dot_exact_rhs(x, ones_bf16):
    hi, lo = _split2(x)
    return jnp.dot(hi, ones_bf16, preferred_element_type=F32) + jnp.dot(lo, ones_bf16, preferred_element_type=F32)


def _dot_exact_lhs(ones_bf16, x):
    hi, lo = _split2(x)
    return jnp.dot(ones_bf16, hi, preferred_element_type=F32) + jnp.dot(ones_bf16, lo, preferred_element_type=F32)


def _store_token_tiles(ref, x, lead=()):
    rows = x.shape[0]
    for c in range(TILE_ROWS):
        ref[lead + (pl.ds(c, rows, stride=TILE_ROWS), slice(None))] = x[:, c * LANES:(c + 1) * LANES]


def _load_token_tiles(ref, rows, lead=()):
    return jnp.concatenate([ref[lead + (pl.ds(c, rows, stride=TILE_ROWS), slice(None))]
                            for c in range(TILE_ROWS)], axis=1)


def _rms(x, g):
    return x * lax.rsqrt(jnp.mean(x * x, axis=-1, keepdims=True) + NORM_EPS) * g


def _params(*sem):
    return pltpu.CompilerParams(dimension_semantics=sem, vmem_limit_bytes=VMEM_LIMIT)


def _norm_inproj_kernel(x_ref, g_ref, w_ref, od_ref, or_ref):
    h = _rms(x_ref[...], g_ref[...]).astype(BF16)
    nd = od_ref.shape[1]
    od_ref[...] = jnp.dot(h, w_ref[:, :nd], preferred_element_type=F32).astype(BF16)
    or_ref[...] = jnp.dot(h, w_ref[:, nd:], preferred_element_type=F32)


def _norm_inproj(x2d, g, w_bf16, n_diff):
    t, d = x2d.shape
    n = w_bf16.shape[1]
    tm = min(PROJ_TM, t)
    return pl.pallas_call(
        _norm_inproj_kernel,
        grid=(t // tm,),
        in_specs=[pl.BlockSpec((tm, d), lambda i: (i, 0)),
                  pl.BlockSpec((1, d), lambda i: (0, 0)),
                  pl.BlockSpec((d, n), lambda i: (0, 0))],
        out_specs=[pl.BlockSpec((tm, n_diff), lambda i: (i, 0)), pl.BlockSpec((tm, n - n_diff), lambda i: (i, 0))],
        out_shape=[jax.ShapeDtypeStruct((t, n_diff), BF16), jax.ShapeDtypeStruct((t, n - n_diff), F32)],
        compiler_params=_params("parallel"),
        name="norm_inproj",
    )(x2d, g, w_bf16)


def _rwkv_kernel(rkv_ref, lora_ref, mu_rkv_ref, mu_l_ref, wl_ref, vec_ref, bd_ref, tri_ref,
                 o_ref, st_ref, prev_rkv_ref, prev_l_ref):
    L = RWKV_CHUNK
    C = RWKV_W

    @pl.when(pl.program_id(1) == 0)
    def _():
        st_ref[...] = jnp.zeros_like(st_ref)
        prev_rkv_ref[...] = jnp.zeros_like(prev_rkv_ref)
        prev_l_ref[...] = jnp.zeros_like(prev_l_ref)

    row = lax.broadcasted_iota(I32, (L, 1), 0)
    w0, a0, k_k, k_a = vec_ref[0:1, :], vec_ref[1:2, :], vec_ref[2:3, :], vec_ref[3:4, :]
    r_k, lnx_g, lnx_b = vec_ref[4:5, :], vec_ref[5:6, :], vec_ref[6:7, :]
    bd = bd_ref[...]
    tri = tri_ref[...]
    lane_l = lax.broadcasted_iota(I32, (L, LORA_PAD), 1)
    nb = rkv_ref.shape[0]

    def token_shift(raw, prev_ref, bi, mu):
        prev = jnp.where(row == 0, prev_ref[bi], pltpu.roll(raw, 1, axis=0))
        prev_ref[bi] = raw[L - 1:L, :]
        return raw + mu * (prev - raw)

    def elementwise(bi):
        u = token_shift(rkv_ref[bi], prev_rkv_ref, bi, mu_rkv_ref[...])
        ul = token_shift(lora_ref[bi], prev_l_ref, bi, mu_l_ref[...])
        r, k, v = u[:, :C], u[:, C:2 * C], u[:, 2 * C:]
        act = jnp.where(lane_l < DECAY_LORA, jnp.tanh(ul),
                        jnp.where(lane_l < DECAY_LORA + ICLR_LORA, ul, jax.nn.sigmoid(ul)))
        lo = _dot(act, wl_ref[...])
        w_log = -jax.nn.softplus(-(w0 + lo[:, :C])) - 0.5
        lw = -jnp.exp(w_log)
        a = jax.nn.sigmoid(a0 + lo[:, C:2 * C])
        kk = k * k_k
        kk = kk / jnp.maximum(jnp.sqrt(_dot_exact_rhs(kk * kk, bd)), 1e-12)
        k2 = k * (1.0 + (a - 1.0) * k_a)
        cum = _dot_exact_lhs(tri, lw)
        total = cum[L - 1:L, :]
        e_neg = jnp.exp(-cum)
        e_rem = jnp.exp(total - cum)
        kka = kk * a
        return dict(r=r, v=v, k2=k2, g=lo[:, 2 * C:], a_t=-kk * jnp.exp(cum - lw), b_t=kka * e_neg,
                    k_t=k2 * e_neg, r_t=r * jnp.exp(cum), b_bar=kka * e_rem, k_bar=k2 * e_rem,
                    p_total=jnp.exp(total))

    ew = [elementwise(bi) for bi in range(nb)]

    m0 = lax.broadcasted_iota(I32, (L, LANES), 1) < HEAD
    r2i = lax.broadcasted_iota(I32, (2 * L, 4 * L), 0)
    c2i = lax.broadcasted_iota(I32, (2 * L, 4 * L), 1) % (2 * L)
    strict, incl = c2i < r2i, c2i <= r2i
    npairs = RWKV_HEADS // 2
    chains = [(bi, p) for bi in range(nb) for p in range(npairs)]
    nc = range(len(chains))

    def stack2(name):
        out = []
        for bi, p in chains:
            xp = ew[bi][name][:, p * LANES:(p + 1) * LANES]
            out.append(jnp.concatenate([jnp.where(m0, xp, 0.0), jnp.where(m0, 0.0, xp)], axis=0))
        return out

    A2, B2, K2, R2 = stack2("a_t"), stack2("b_t"), stack2("k_t"), stack2("r_t")
    V2, Bb2, Kb2 = stack2("v"), stack2("b_bar"), stack2("k_bar")
    mg = [_dot_nt(jnp.concatenate([A2[c], R2[c]], axis=0), jnp.concatenate([B2[c], K2[c]], axis=0)) for c in nc]
    m_cat = [jnp.where(strict, mg[c][:2 * L], 0.0) for c in nc]
    g_cat = [jnp.where(incl, mg[c][2 * L:], 0.0) for c in nc]
    m_ab = [m_cat[c][:, :LANES] for c in nc]
    pw = [_dot(m_ab[c], m_ab[c]) for c in nc]
    tr = list(m_ab)
    k = 2
    while 2 * k < L:
        x = [_dot(pw[c], jnp.concatenate([pw[c], tr[c]], axis=1)) for c in nc]
        tr = [tr[c] + pw[c] + x[c][:, LANES:] for c in nc]
        pw = [x[c][:, :LANES] for c in nc]
        k *= 2
    tr = [tr[c] + pw[c] + _dot(pw[c], tr[c]) for c in nc]
    az = [jnp.concatenate([A2[c], _dot(m_cat[c][:, LANES:], V2[c])], axis=1) for c in nc]
    tz = [az[c] + _dot(tr[c], az[c]) for c in nc]
    st = [st_ref[bi, p] for bi, p in chains]
    u2 = [_dot_nt(tz[c][:, :LANES], st[c]) + tz[c][:, LANES:] for c in nc]
    uv = [jnp.concatenate([u2[c], V2[c]], axis=0) for c in nc]
    y2 = [_dot_nt(R2[c], st[c]) + _dot(g_cat[c], uv[c]) for c in nc]
    for c, (bi, p) in enumerate(chains):
        st_ref[bi, p] = (st[c] * ew[bi]["p_total"][:, p * LANES:(p + 1) * LANES]
                         + _dot_tn(uv[c], jnp.concatenate([Bb2[c], Kb2[c]], axis=0)))

    inv_n = 1.0 / HEAD
    for bi in range(nb):
        y = jnp.concatenate([y2[bi * npairs + p][:L] + y2[bi * npairs + p][L:] for p in range(npairs)], axis=1)
        e = ew[bi]
        mu = _dot_exact_rhs(y, bd) * inv_n
        d = y - mu
        var = _dot_exact_rhs(d * d, bd) * inv_n
        yn = d * lax.rsqrt(var + RWKV_GN_EPS) * lnx_g + lnx_b
        bonus = _dot_exact_rhs(e["r"] * e["k2"] * r_k, bd) * e["v"]
        o_ref[bi] = (yn + bonus) * e["g"]


def _rwkv_group(proj3, mu_rkv, mu_l, wl, vec, rkv_block, lora_block):
    b, s, _ = proj3.shape
    L, C = RWKV_CHUNK, RWKV_W
    head = jnp.arange(C, dtype=I32) // HEAD
    bd = (head[:, None] == head[None, :]).astype(BF16)
    t = jnp.arange(L, dtype=I32)
    tri = (t[None, :] <= t[:, None]).astype(BF16)
    const = lambda shape: pl.BlockSpec(shape, lambda i, j: (0,) * len(shape))
    nb = RWKV_BATCH if b % RWKV_BATCH == 0 else 1
    return pl.pallas_call(
        _rwkv_kernel,
        grid=(b // nb, s // L),
        in_specs=[pl.BlockSpec((nb, L, 3 * C), lambda i, j: (i, j, rkv_block)),
                  pl.BlockSpec((nb, L, LORA_PAD), lambda i, j: (i, j, lora_block)),
                  const((1, 3 * C)), const((1, LORA_PAD)), const((LORA_PAD, 3 * C)),
                  const((8, C)), const((C, C)), const((L, L))],
        out_specs=pl.BlockSpec((nb, L, C), lambda i, j: (i, j, 0)),
        out_shape=jax.ShapeDtypeStruct((b, s, C), F32),
        scratch_shapes=[pltpu.VMEM((nb, RWKV_HEADS // 2, LANES, LANES), F32),
                        pltpu.VMEM((nb, 1, 3 * C), F32), pltpu.VMEM((nb, 1, LORA_PAD), F32)],
        compiler_params=_params("parallel", "arbitrary"),
        name="rwkv_group",
    )(proj3, proj3, mu_rkv, mu_l, wl, vec, bd, tri)


def _diff_attn_kernel(qi_ref, ki_ref, q_ref, k_ref, v_ref, slope_ref, lam_ref, g_ref, o_ref,
                      q2_scr, m_scr, acc_scr, *, lambda_init):
    blk = ATT_BLOCK
    step_id = pl.program_id(2)
    qi, ki = qi_ref[step_id], ki_ref[step_id]
    log2e = math.log2(math.e)

    @pl.when(ki == 0)
    def _():
        q = q_ref[...].astype(F32) * (HEAD ** -0.5 * log2e)
        m0 = lax.broadcasted_iota(I32, (blk, LANES), 1) < HEAD
        q2_scr[0:blk, :] = jnp.where(m0, q, 0.0).astype(BF16)
        q2_scr[blk:, :] = jnp.where(m0, 0.0, q).astype(BF16)
        m_scr[...] = jnp.full_like(m_scr, NEG_BIG)
        acc_scr[...] = jnp.zeros_like(acc_scr)

    def step(on_diagonal):
        k = k_ref[...].astype(BF16)
        v_ext = jnp.concatenate([v_ref[...].astype(BF16), jnp.ones((blk, LANES), BF16)], axis=1)
        koff = lax.broadcasted_iota(I32, (1, blk), 1)
        col_bias = (slope_ref[:, :1] * log2e) * ((ki - qi) * blk + koff).astype(F32)
        rc = ATT_ROW_CHUNK
        n_chunks = 2 * blk // rc

        def scores(c):
            return lax.dot_general(q2_scr[c * rc:(c + 1) * rc, :], k, (((1,), (1,)), ((), ())),
                                   preferred_element_type=F32)

        s_next = scores(0)
        for c in range(n_chunks):
            rows = slice(c * rc, (c + 1) * rc)
            s = s_next + col_bias
            if c + 1 < n_chunks:
                s_next = scores(c + 1)
            if on_diagonal:
                qoff = (c * rc) % blk + lax.broadcasted_iota(I32, (rc, 1), 0)
                s = jnp.where(koff <= qoff, s, NEG_BIG)
            m_prev = m_scr[rows, :]
            m_new = jnp.maximum(m_prev, jnp.max(s, axis=-1, keepdims=True))
            alpha = jnp.exp2(m_prev - m_new)
            p = jnp.exp2(s - jnp.concatenate([m_new] * (blk // LANES), axis=1))
            acc_scr[rows, :] = (jnp.concatenate([alpha, alpha], axis=1) * acc_scr[rows, :]
                                + jnp.dot(p.astype(BF16), v_ext, preferred_element_type=F32))
            m_scr[rows, :] = m_new

    @pl.when(ki < qi)
    def _():
        step(False)

    @pl.when(ki == qi)
    def _():
        step(True)
        lam_v = lam_ref[...]
        lam = (jnp.exp(jnp.sum(lam_v[0:1] * lam_v[1:2], axis=-1, keepdims=True))
               - jnp.exp(jnp.sum(lam_v[2:3] * lam_v[3:4], axis=-1, keepdims=True)) + lambda_init)
        o2 = acc_scr[:, :LANES] / acc_scr[:, LANES:]
        o = o2[:blk] - lam * o2[blk:]
        o_ref[...] = _rms(o, g_ref[...]) * (1.0 - lambda_init)


def _diff_attention(proj3, lam_vecs, subln_g, lambda_init):
    b, s, _ = proj3.shape
    blk = ATT_BLOCK
    nb = s // blk
    nh = DIFF_HEADS
    slopes = jnp.exp2(-8.0 * jnp.arange(1, nh + 1, dtype=F32) / nh)
    slopes = jnp.broadcast_to(slopes[:, None, None], (nh, 1, LANES))
    pairs = [(qi, ki) for qi in range(nb) for ki in range(qi + 1)]
    qi_tab = jnp.asarray([pq for pq, _ in pairs], I32)
    ki_tab = jnp.asarray([pk for _, pk in pairs], I32)
    kernel = functools.partial(_diff_attn_kernel, lambda_init=lambda_init)
    grid_spec = pltpu.PrefetchScalarGridSpec(
        num_scalar_prefetch=2,
        grid=(b, nh, len(pairs)),
        in_specs=[pl.BlockSpec((None, blk, LANES), lambda bi, h, t, qt, kt: (bi, qt[t], h)),
                  pl.BlockSpec((None, blk, LANES), lambda bi, h, t, qt, kt: (bi, kt[t], nh + h)),
                  pl.BlockSpec((None, blk, LANES), lambda bi, h, t, qt, kt: (bi, kt[t], 2 * nh + h)),
                  pl.BlockSpec((None, 1, LANES), lambda bi, h, t, qt, kt: (h, 0, 0)),
                  pl.BlockSpec((4, HEAD), lambda bi, h, t, qt, kt: (0, 0)),
                  pl.BlockSpec((1, 2 * HEAD), lambda bi, h, t, qt, kt: (0, 0))],
        out_specs=pl.BlockSpec((None, blk, LANES), lambda bi, h, t, qt, kt: (bi, qt[t], h)),
        scratch_shapes=[pltpu.VMEM((2 * blk, LANES), BF16), pltpu.VMEM((2 * blk, LANES), F32),
                        pltpu.VMEM((2 * blk, 2 * LANES), F32)],
    )
    return pl.pallas_call(
        kernel,
        grid_spec=grid_spec,
        out_shape=jax.ShapeDtypeStruct((b, s, DIFF_W), F32),
        compiler_params=_params("parallel", "parallel", "arbitrary"),
        name="diff_attention",
    )(qi_tab, ki_tab, proj3, proj3, proj3, slopes, lam_vecs, subln_g)


def _mem_kv_kernel(m_ref, g_ref, w_ref, k_ref, v_ref):
    d = m_ref.shape[-1]
    kv = _dot(_rms(m_ref[...], g_ref[...]), w_ref[...])
    k_ref[...] = kv[:, :d].astype(BF16)
    v_ref[...] = kv[:, d:].astype(BF16)


def _mem_kv(mem, g, w_ckv_bf16):
    b, m, d = mem.shape
    return pl.pallas_call(
        _mem_kv_kernel,
        grid=(b,),
        in_specs=[pl.BlockSpec((None, m, d), lambda i: (i, 0, 0)),
                  pl.BlockSpec((1, d), lambda i: (0, 0)),
                  pl.BlockSpec((d, 2 * d), lambda i: (0, 0))],
        out_specs=[pl.BlockSpec((None, m, d), lambda i: (i, 0, 0))] * 2,
        out_shape=[jax.ShapeDtypeStruct((b, m, d), BF16)] * 2,
        compiler_params=_params("parallel"),
        name="mem_kv",
    )(mem, g, w_ckv_bf16)


def _mid_kernel(x_ref, yr_ref, yd_ref, wo_ref, gc_ref, wcq_ref, km_ref, vm_ref, wco_ref, gf_ref,
                wr_hi_ref, wr_lo_ref, br_ref, tri_ref,
                x2_ref, hn_ref, ri_ref, rg_ref, cnt_ref, carry_scr):
    tm, d = x_ref.shape
    first = jnp.logical_and(pl.program_id(0) == 0, pl.program_id(1) == 0)

    @pl.when(first)
    def _():
        carry_scr[...] = jnp.zeros_like(carry_scr)

    rc = tri_ref.shape[0]
    chains = range(tm // rc)
    rows = [slice(c * rc, (c + 1) * rc) for c in chains]
    half = yr_ref.shape[-1]
    x1 = [x_ref[r, :] + _dot(yr_ref[r, :], wo_ref[:half, :]) + _dot(yd_ref[r, :], wo_ref[half:, :]) for r in rows]

    q = [_dot(_rms(x1[c], gc_ref[...]), wcq_ref[...]) for c in chains]
    hd = d // CROSS_HEADS
    outs = [[] for _ in chains]
    for h in range(CROSS_HEADS):
        sl = slice(h * hd, (h + 1) * hd)
        s = [_dot_nt(q[c][:, sl], km_ref[:, sl]) * (hd ** -0.5) for c in chains]
        p = [jnp.exp(s[c] - jnp.max(s[c], axis=-1, keepdims=True)) for c in chains]
        p = [p[c] / jnp.sum(p[c], axis=-1, keepdims=True) for c in chains]
        for c in chains:
            outs[c].append(_dot(p[c], vm_ref[:, sl]))
    x2 = [x1[c] + _dot(jnp.concatenate(outs[c], axis=1), wco_ref[...]) for c in chains]
    for c in chains:
        x2_ref[rows[c], :] = x2[c]

    hn = [_rms(x2[c], gf_ref[...]) for c in chains]
    _store_token_tiles(hn_ref, jnp.concatenate(hn, axis=0))
    hi = [hn[c].astype(BF16) for c in chains]
    lo = [(hn[c] - hi[c].astype(F32)).astype(BF16) for c in chains]
    vals = [(jnp.dot(hi[c], wr_hi_ref[...], preferred_element_type=F32)
             + jnp.dot(lo[c], wr_hi_ref[...], preferred_element_type=F32)
             + jnp.dot(hi[c], wr_lo_ref[...], preferred_element_type=F32) + br_ref[...]) for c in chains]
    lane = lax.broadcasted_iota(I32, (rc, LANES), 1)
    tops, idxs, hots = [[] for _ in chains], [[] for _ in chains], [[] for _ in chains]
    for _ in range(TOP_K):
        for c in chains:
            mx = jnp.max(vals[c], axis=-1, keepdims=True)
            idx = jnp.min(jnp.where(vals[c] == mx, lane, LANES), axis=-1, keepdims=True)
            hot = lane == idx
            vals[c] = jnp.where(hot, -jnp.inf, vals[c])
            tops[c].append(mx)
            idxs[c].append(idx)
            hots[c].append(hot)
    for c in chains:
        es = [jnp.exp(t - tops[c][0]) for t in tops[c]]
        denom = es[0] + es[1] + es[2] + es[3]
        sel = jnp.zeros((rc, LANES), F32)
        for hot in hots[c]:
            sel = sel + hot.astype(F32)
        before = jnp.dot(tri_ref[...], sel.astype(BF16), preferred_element_type=F32) + carry_scr[...]
        carry_scr[...] = carry_scr[...] + jnp.sum(sel, axis=0, keepdims=True)
        ri = jnp.zeros((rc, LANES), I32)
        rg = jnp.zeros((rc, LANES), F32)
        for j in range(TOP_K):
            rank = jnp.sum(jnp.where(hots[c][j], before, 0.0), axis=-1, keepdims=True).astype(I32)
            ri = jnp.where(lane == j, idxs[c][j], ri)
            ri = jnp.where(lane == TOP_K + j, rank, ri)
            rg = jnp.where(lane == j, es[j] / denom, rg)
        ri_ref[rows[c], :] = ri
        rg_ref[rows[c], :] = rg
    cnt_ref[...] = jnp.broadcast_to(carry_scr[...], cnt_ref.shape)


def _mid_stage(x, y_rwkv, y_diff, w_out, g_cross, w_cq, k_mem, v_mem, w_co, g_ffn, w_router, b_router):
    b, s, d = x.shape
    tm = min(MID_TM, s)
    m = k_mem.shape[1]
    half = y_rwkv.shape[-1]
    e = w_router.shape[1]
    wr = jnp.zeros((d, LANES), F32).at[:, :e].set(w_router)
    wr_hi = wr.astype(BF16)
    wr_lo = (wr - wr_hi.astype(F32)).astype(BF16)
    br = jnp.full((1, LANES), NEG_BIG, F32).at[0, :e].set(b_router)
    rc = min(MID_CHAIN_ROWS, tm)
    t = jnp.arange(rc, dtype=I32)
    tri = (t[None, :] < t[:, None]).astype(BF16)
    tile = lambda w: pl.BlockSpec((None, tm, w), lambda i, j: (i, j, 0))
    const = lambda shape: pl.BlockSpec(shape, lambda i, j: (0,) * len(shape))
    return pl.pallas_call(
        _mid_kernel,
        grid=(b, s // tm),
        in_specs=[tile(d), tile(half), tile(half), const((d, d)), const((1, d)), const((d, d)),
                  pl.BlockSpec((None, m, d), lambda i, j: (i, 0, 0)),
                  pl.BlockSpec((None, m, d), lambda i, j: (i, 0, 0)),
                  const((d, d)), const((1, d)), const((d, LANES)), const((d, LANES)),
                  const((1, LANES)), const((rc, rc))],
        out_specs=[tile(d), pl.BlockSpec((tm * TILE_ROWS, LANES), lambda i, j: (i * (s // tm) + j, 0)),
                   tile(LANES), tile(LANES), const((8, LANES))],
        out_shape=[jax.ShapeDtypeStruct((b, s, d), F32), jax.ShapeDtypeStruct((b * s * TILE_ROWS, LANES), F32),
                   jax.ShapeDtypeStruct((b, s, LANES), I32), jax.ShapeDtypeStruct((b, s, LANES), F32),
                   jax.ShapeDtypeStruct((8, LANES), F32)],
        scratch_shapes=[pltpu.VMEM((1, LANES), F32)],
        compiler_params=_params("arbitrary", "arbitrary"),
        name="outproj_cross_router",
    )(x, y_rwkv, y_diff, w_out.astype(BF16), g_cross, w_cq.astype(BF16), k_mem, v_mem,
      w_co.astype(BF16), g_ffn, wr_hi, wr_lo, br, tri)


def _expert_kernel(be_ref, nused_ref, x_ref, w1_ref, b1_ref, w2_ref, b2_ref, o_ref, w1_scr, w2_scr, w2i_scr):
    i = pl.program_id(0)
    used = i < nused_ref[0]
    new_expert = jnp.logical_or(i == 0, be_ref[i] != be_ref[jnp.maximum(i - 1, 0)])

    @pl.when(jnp.logical_and(used, new_expert))
    def _():
        d, f = w1_ref.shape[0], w2_ref.shape[0]
        step = 128

        def cast_rows(c, carry):
            rows = pl.ds(pl.multiple_of(c * step, step), step)
            w1_scr[rows, :] = w1_ref[rows, :].astype(BF16)
            return carry

        lax.fori_loop(0, d // step, cast_rows, 0)
        for cb in range(w2_ref.shape[1] // LANES):
            cols = slice(cb * LANES, (cb + 1) * LANES)
            w2i_scr[cb, pl.ds(0, f // 2, stride=2), :] = w2_ref[:f // 2, cols]
            w2i_scr[cb, pl.ds(1, f // 2, stride=2), :] = w2_ref[f // 2:, cols]
            w2_scr[:, cols] = w2i_scr[cb].astype(BF16)

    @pl.when(used)
    def _():
        x = _load_token_tiles(x_ref, MOE_ROWS).astype(BF16)
        hid = jnp.dot(x, w1_scr[...], preferred_element_type=F32) + b1_ref[...]
        even = lax.broadcasted_iota(I32, (hid.shape[0], LANES), 1) % 2 == 0

        def act_block(cb):
            h = hid[:, cb * LANES:(cb + 1) * LANES]
            lin = jnp.clip(pltpu.roll(h, LANES - 1, axis=1), -SWIGLU_LIMIT, SWIGLU_LIMIT)
            glu = jnp.minimum(h, SWIGLU_LIMIT)
            return jnp.where(even, glu * jax.nn.sigmoid(SWIGLU_ALPHA * glu) * (lin + 1.0), 0.0)

        nbk = hid.shape[1] // (2 * LANES)
        packed = jnp.concatenate([act_block(cb) + pltpu.roll(act_block(nbk + cb), 1, axis=1)
                                  for cb in range(nbk)], axis=1)
        y = jnp.dot(packed.astype(BF16), w2_scr[...], preferred_element_type=F32) + b2_ref[...]
        _store_token_tiles(o_ref, y)

    @pl.when(jnp.logical_not(used))
    def _():
        o_ref[...] = jnp.zeros_like(o_ref)


def _expert_ffn(x_sorted, block_e, n_used, w1, b1, w2, b2):
    e, d, f2 = w1.shape
    r = MOE_ROWS
    n_rows = x_sorted.shape[0] // TILE_ROWS
    grid_spec = pltpu.PrefetchScalarGridSpec(
        num_scalar_prefetch=2,
        grid=(n_rows // r,),
        in_specs=[pl.BlockSpec((r * TILE_ROWS, LANES), lambda i, be, nu: (i, 0)),
                  pl.BlockSpec((None, d, f2), lambda i, be, nu: (be[i], 0, 0)),
                  pl.BlockSpec((None, 1, f2), lambda i, be, nu: (be[i], 0, 0)),
                  pl.BlockSpec((None, f2 // 2, d), lambda i, be, nu: (be[i], 0, 0)),
                  pl.BlockSpec((None, 1, d), lambda i, be, nu: (be[i], 0, 0))],
        out_specs=pl.BlockSpec((r * TILE_ROWS, LANES), lambda i, be, nu: (i, 0)),
        scratch_shapes=[pltpu.VMEM((d, f2), BF16), pltpu.VMEM((f2 // 2, d), BF16),
                        pltpu.VMEM((d // LANES, f2 // 2, LANES), F32)],
    )
    return pl.pallas_call(
        _expert_kernel,
        grid_spec=grid_spec,
        out_shape=jax.ShapeDtypeStruct((n_rows * TILE_ROWS, LANES), F32),
        compiler_params=_params("arbitrary"),
        name="expert_ffn",
    )(block_e, n_used, x_sorted, w1, b1.reshape(e, 1, f2), w2, b2.reshape(e, 1, d))


def _tile_at(ref, first_row):
    return ref.at[pl.ds(pl.multiple_of(first_row, TILE_ROWS), TILE_ROWS)]


def _dispatch_kernel(pad_lo_ref, pad_hi_ref, dest_ref, hn_ref, xs_hbm, zero_scr, sem, pad_sem, *, tm):
    @pl.when(pl.program_id(0) == 0)
    def _():
        zero_scr[...] = jnp.zeros_like(zero_scr)

        def pad_copy(row):
            return pltpu.make_async_copy(zero_scr, _tile_at(xs_hbm, row * TILE_ROWS), pad_sem)

        def per_expert(e, carry):
            lo, hi = pad_lo_ref[e], pad_hi_ref[e]
            lax.fori_loop(lo, hi, lambda r, c: (pad_copy(r).start(), c)[1], 0)
            lax.fori_loop(lo, hi, lambda r, c: (pad_copy(r).wait(), c)[1], 0)
            return carry

        lax.fori_loop(0, pad_lo_ref.shape[0], per_expert, 0)

    def issue(t, carry):
        src = _tile_at(hn_ref, t * TILE_ROWS)
        for j in range(TOP_K):
            pltpu.make_async_copy(src, _tile_at(xs_hbm, dest_ref[0, t * TOP_K + j]), sem).start(priority=j % 2)
        return carry

    lax.fori_loop(0, tm, issue, 0, unroll=8)
    for j in range(TOP_K):
        pltpu.make_async_copy(hn_ref, xs_hbm.at[pl.ds(0, tm * TILE_ROWS)], sem).wait()


def _dispatch(hn_tiles, dest_rows, pad_lo, pad_hi, n_rows):
    t = hn_tiles.shape[0] // TILE_ROWS
    tm = min(DISPATCH_TM, t)
    steps = t // tm
    grid_spec = pltpu.PrefetchScalarGridSpec(
        num_scalar_prefetch=2,
        grid=(steps,),
        in_specs=[pl.BlockSpec((None, 1, tm * TOP_K), lambda i, lo, hi: (i, 0, 0), memory_space=pltpu.SMEM),
                  pl.BlockSpec((tm * TILE_ROWS, LANES), lambda i, lo, hi: (i, 0))],
        out_specs=pl.BlockSpec(memory_space=pl.ANY),
        scratch_shapes=[pltpu.VMEM((TILE_ROWS, LANES), F32), pltpu.SemaphoreType.DMA(()),
                        pltpu.SemaphoreType.DMA(())],
    )
    return pl.pallas_call(
        functools.partial(_dispatch_kernel, tm=tm),
        grid_spec=grid_spec,
        out_shape=jax.ShapeDtypeStruct((n_rows * TILE_ROWS, LANES), F32),
        compiler_params=_params("arbitrary"),
        name="moe_dispatch",
    )(pad_lo, pad_hi, dest_rows.reshape(steps, 1, tm * TOP_K), hn_tiles)


def _combine_kernel(dest_ref, x_ref, gate_ref, g_ref, ys_hbm, o_ref, buf, sem, *, tm, final):
    def issue(t, carry):
        for j in range(TOP_K):
            pltpu.make_async_copy(_tile_at(ys_hbm, dest_ref[0, t * TOP_K + j]),
                                  _tile_at(buf.at[j], t * TILE_ROWS), sem).start(priority=j % 2)
        return carry

    lax.fori_loop(0, tm, issue, 0, unroll=8)
    pltpu.make_async_copy(buf, buf, sem).wait()
    y = x_ref[...]
    gate = gate_ref[...]
    for j in range(TOP_K):
        y = y + gate[:, j:j + 1] * _load_token_tiles(buf, tm, lead=(j,))
    o_ref[...] = _rms(y, g_ref[...]) if final else y


def _combine(x2d, gate2d, dest_rows, y_sorted, g, final):
    t, d = x2d.shape
    tm = min(COMBINE_TM, t)
    steps = t // tm
    return pl.pallas_call(
        functools.partial(_combine_kernel, tm=tm, final=final),
        grid=(steps,),
        in_specs=[pl.BlockSpec((None, 1, tm * TOP_K), lambda i: (i, 0, 0), memory_space=pltpu.SMEM),
                  pl.BlockSpec((tm, d), lambda i: (i, 0)), pl.BlockSpec((tm, LANES), lambda i: (i, 0)),
                  pl.BlockSpec((1, d), lambda i: (0, 0)), pl.BlockSpec(memory_space=pl.ANY)],
        out_specs=pl.BlockSpec((tm, d), lambda i: (i, 0)),
        out_shape=jax.ShapeDtypeStruct((t, d), F32),
        scratch_shapes=[pltpu.VMEM((TOP_K, tm * TILE_ROWS, LANES), F32), pltpu.SemaphoreType.DMA(())],
        compiler_params=_params("arbitrary"),
        name="combine_norm",
    )(dest_rows.reshape(steps, 1, tm * TOP_K), x2d, gate2d, g, y_sorted)


def _layer(x, mem, lyr, final, p):
    b, s, d = x.shape
    t = b * s
    c = RWKV_W
    rwkv_in = 3 * c + LORA_W

    w_in = p["w_in"][lyr]
    w_cat = jnp.concatenate([w_in[:, rwkv_in:], w_in[:, :3 * c], w_in[:, 3 * c:rwkv_in],
                             jnp.zeros((d, LORA_PAD - LORA_W), F32)], axis=1).astype(BF16)
    diff_cols = 3 * DIFF_W
    proj_diff, proj_rwkv = _norm_inproj(x.reshape(t, d), p["norm_mix_g"][lyr][None], w_cat, diff_cols)
    proj_diff, proj_rwkv = proj_diff.reshape(b, s, -1), proj_rwkv.reshape(b, s, -1)

    mu = p["shift_mu"][lyr]
    mu_l = jnp.zeros((1, LORA_PAD), F32).at[0, :LORA_W].set(mu[3 * c:])
    wl = jnp.zeros((LORA_PAD, 3 * c), F32)
    wl = wl.at[:DECAY_LORA, :c].set(p["w_decay_up"][lyr])
    wl = wl.at[DECAY_LORA:DECAY_LORA + ICLR_LORA, c:2 * c].set(p["w_iclr_up"][lyr])
    wl = wl.at[DECAY_LORA + ICLR_LORA:LORA_W, 2 * c:].set(p["w_gate_up"][lyr])
    vec = jnp.stack([p["w0"][lyr], p["a0"][lyr], p["k_k"][lyr], p["k_a"][lyr], p["r_k"][lyr].reshape(c),
                     p["lnx_g"][lyr], p["lnx_b"][lyr], jnp.zeros((c,), F32)])
    y_rwkv = _rwkv_group(proj_rwkv, mu[None, :3 * c], mu_l, wl.astype(BF16), vec,
                         rkv_block=0, lora_block=3 * c // LORA_PAD)

    lambda_init = 0.8 - 0.6 * math.exp(-0.3 * lyr)
    lam_vecs = jnp.stack([p["lambda_q1"][lyr], p["lambda_k1"][lyr], p["lambda_q2"][lyr], p["lambda_k2"][lyr]])
    y_diff = _diff_attention(proj_diff, lam_vecs, p["subln_g"][lyr][None], lambda_init)

    k_mem, v_mem = _mem_kv(mem, p["norm_mem_g"][lyr][None], p["w_ckv"][lyr].astype(BF16))
    x2, hn, route_i, route_g, counts = _mid_stage(
        x, y_rwkv, y_diff, p["w_out"][lyr], p["norm_cross_g"][lyr][None], p["w_cq"][lyr], k_mem, v_mem,
        p["w_co"][lyr], p["norm_ffn_g"][lyr][None], p["w_router"][lyr], p["b_router"][lyr])

    e = N_EXPERTS
    r = MOE_ROWS
    n_blocks = (t * TOP_K) // r + e
    route_i = route_i.reshape(t, LANES)
    idx, rank = route_i[:, :TOP_K], route_i[:, TOP_K:2 * TOP_K]
    cnt = counts[0, :e].astype(I32)
    padded = (cnt + r - 1) // r * r
    pad_end = jnp.cumsum(padded)
    pad_start = pad_end - padded
    dest = jnp.sum(jnp.where(idx[:, :, None] == jnp.arange(e, dtype=I32), pad_start, 0), axis=-1) + rank
    dest = dest.reshape(-1) * TILE_ROWS
    block_start = jnp.arange(n_blocks, dtype=I32) * r
    block_e = jnp.minimum(jnp.sum((block_start[:, None] >= pad_end[None, :]).astype(I32), axis=1), e - 1)
    n_used = pad_end[-1:] // r

    unused_lo = jnp.concatenate([pad_start + cnt, pad_end[-1:]])
    unused_hi = jnp.concatenate([pad_end, jnp.full((1,), n_blocks * r, I32)])
    x_sorted = _dispatch(hn, dest, unused_lo, unused_hi, n_blocks * r)
    y_sorted = _expert_ffn(x_sorted, block_e, n_used, p["w1"][lyr], p["b1"][lyr], p["w2"][lyr], p["b2"][lyr])
    return _combine(x2.reshape(t, d), route_g.reshape(t, LANES), dest, y_sorted,
                    p["norm_final_g"][None], final).reshape(b, s, d)


def kernel(x, mem, norm_mix_g, w_in, shift_mu, w0, w_decay_up, a0, w_iclr_up, w_gate_up, k_k, k_a, r_k,
           lnx_g, lnx_b, lambda_q1, lambda_k1, lambda_q2, lambda_k2, subln_g, w_out, norm_cross_g,
           norm_mem_g, w_cq, w_ckv, w_co, norm_ffn_g, w_router, b_router, w1, b1, w2, b2, norm_final_g):
    p = dict(norm_mix_g=norm_mix_g, w_in=w_in, shift_mu=shift_mu, w0=w0, w_decay_up=w_decay_up, a0=a0,
             w_iclr_up=w_iclr_up, w_gate_up=w_gate_up, k_k=k_k, k_a=k_a, r_k=r_k, lnx_g=lnx_g, lnx_b=lnx_b,
             lambda_q1=lambda_q1, lambda_k1=lambda_k1, lambda_q2=lambda_q2, lambda_k2=lambda_k2,
             subln_g=subln_g, w_out=w_out, norm_cross_g=norm_cross_g, norm_mem_g=norm_mem_g, w_cq=w_cq,
             w_ckv=w_ckv, w_co=w_co, norm_ffn_g=norm_ffn_g, w_router=w_router, b_router=b_router,
             w1=w1, b1=b1, w2=w2, b2=b2, norm_final_g=norm_final_g)
    depth = w_in.shape[0]
    for lyr in range(depth):
        x = _layer(x, mem, lyr, lyr == depth - 1, p)
    return x
```

```python
import functools
import math

import jax
import jax.numpy as jnp
from jax import lax
from jax.experimental import pallas as pl
from jax.experimental.pallas import tpu as pltpu

F32 = jnp.float32
BF16 = jnp.bfloat16
I32 = jnp.int32

NORM_EPS = 1e-5
HEAD = 64
RWKV_W = 512
RWKV_HEADS = RWKV_W // HEAD
DECAY_LORA, ICLR_LORA, GATE_LORA = 64, 64, 160
LORA_W = DECAY_LORA + ICLR_LORA + GATE_LORA
LORA_PAD = 384
RWKV_GN_EPS = 64e-5
DIFF_W = 512
DIFF_HEADS = DIFF_W // (2 * HEAD)
CROSS_HEADS = 4
N_EXPERTS = 32
TOP_K = 4
SWIGLU_LIMIT = 7.0
SWIGLU_ALPHA = 1.702
LANES = 128
TILE_ROWS = 8
NEG_BIG = -1e30

RWKV_CHUNK = 64
RWKV_BATCH = 4
PROJ_TM = 512
ATT_BLOCK = 512
ATT_ROW_CHUNK = 128
MID_TM = 512
MID_CHAIN_ROWS = 256
MOE_ROWS = 256
DISPATCH_TM = 512
COMBINE_TM = 256
VMEM_LIMIT = 56 * 1024 * 1024


def _dot(a, b):
    return jnp.dot(a.astype(BF16), b.astype(BF16), preferred_element_type=F32)


def _dot_nt(a, b):
    return lax.dot_general(a.astype(BF16), b.astype(BF16), (((1,), (1,)), ((), ())),
                           preferred_element_type=F32)


def _dot_tn(a, b):
    return lax.dot_general(a.astype(BF16), b.astype(BF16), (((0,), (0,)), ((), ())),
                           preferred_element_type=F32)


def _split2(x):
    hi = x.astype(BF16)
    lo = (x - hi.astype(F32)).astype(BF16)
    return hi, lo


def _dot_exact_rhs(x, ones_bf16):
    hi, lo = _split2(x)
    return jnp.dot(hi, ones_bf16, preferred_element_type=F32) + jnp.dot(lo, ones_bf16, preferred_element_type=F32)


def _dot_exact_lhs(ones_bf16, x):
    hi, lo = _split2(x)
    return jnp.dot(ones_bf16, hi, preferred_element_type=F32) + jnp.dot(ones_bf16, lo, preferred_element_type=F32)


def _store_token_tiles(ref, x, lead=()):
    rows = x.shape[0]
    for c in range(TILE_ROWS):
        ref[lead + (pl.ds(c, rows, stride=TILE_ROWS), slice(None))] = x[:, c * LANES:(c + 1) * LANES]


def _load_token_tiles(ref, rows, lead=()):
    return jnp.concatenate([ref[lead + (pl.ds(c, rows, stride=TILE_ROWS), slice(None))]
                            for c in range(TILE_ROWS)], axis=1)


def _rms(x, g):
    return x * lax.rsqrt(jnp.mean(x * x, axis=-1, keepdims=True) + NORM_EPS) * g


def _params(*sem):
    return pltpu.CompilerParams(dimension_semantics=sem, vmem_limit_bytes=VMEM_LIMIT)


def _norm_inproj_kernel(x_ref, g_ref, w_ref, od_ref, or_ref):
    h = _rms(x_ref[...], g_ref[...]).astype(BF16)
    nd = od_ref.shape[1]
    od_ref[...] = jnp.dot(h, w_ref[:, :nd], preferred_element_type=F32).astype(BF16)
    or_ref[...] = jnp.dot(h, w_ref[:, nd:], preferred_element_type=F32)


def _norm_inproj(x2d, g, w_bf16, n_diff):
    t, d = x2d.shape
    n = w_bf16.shape[1]
    tm = min(PROJ_TM, t)
    return pl.pallas_call(
        _norm_inproj_kernel,
        grid=(t // tm,),
        in_specs=[pl.BlockSpec((tm, d), lambda i: (i, 0)),
                  pl.BlockSpec((1, d), lambda i: (0, 0)),
                  pl.BlockSpec((d, n), lambda i: (0, 0))],
        out_specs=[pl.BlockSpec((tm, n_diff), lambda i: (i, 0)), pl.BlockSpec((tm, n - n_diff), lambda i: (i, 0))],
        out_shape=[jax.ShapeDtypeStruct((t, n_diff), BF16), jax.ShapeDtypeStruct((t, n - n_diff), F32)],
        compiler_params=_params("parallel"),
        name="norm_inproj",
    )(x2d, g, w_bf16)


def _rwkv_kernel(rkv_ref, lora_ref, mu_rkv_ref, mu_l_ref, wl_ref, vec_ref, bd_ref, tri_ref,
                 o_ref, st_ref, prev_rkv_ref, prev_l_ref):
    L = RWKV_CHUNK
    C = RWKV_W

    @pl.when(pl.program_id(1) == 0)
    def _():
        st_ref[...] = jnp.zeros_like(st_ref)
        prev_rkv_ref[...] = jnp.zeros_like(prev_rkv_ref)
        prev_l_ref[...] = jnp.zeros_like(prev_l_ref)

    row = lax.broadcasted_iota(I32, (L, 1), 0)
    w0, a0, k_k, k_a = vec_ref[0:1, :], vec_ref[1:2, :], vec_ref[2:3, :], vec_ref[3:4, :]
    r_k, lnx_g, lnx_b = vec_ref[4:5, :], vec_ref[5:6, :], vec_ref[6:7, :]
    bd = bd_ref[...]
    tri = tri_ref[...]
    lane_l = lax.broadcasted_iota(I32, (L, LORA_PAD), 1)
    nb = rkv_ref.shape[0]

    def token_shift(raw, prev_ref, bi, mu):
        prev = jnp.where(row == 0, prev_ref[bi], pltpu.roll(raw, 1, axis=0))
        prev_ref[bi] = raw[L - 1:L, :]
        return raw + mu * (prev - raw)

    def elementwise(bi):
        u = token_shift(rkv_ref[bi], prev_rkv_ref, bi, mu_rkv_ref[...])
        ul = token_shift(lora_ref[bi], prev_l_ref, bi, mu_l_ref[...])
        r, k, v = u[:, :C], u[:, C:2 * C], u[:, 2 * C:]
        act = jnp.where(lane_l < DECAY_LORA, jnp.tanh(ul),
                        jnp.where(lane_l < DECAY_LORA + ICLR_LORA, ul, jax.nn.sigmoid(ul)))
        lo = _dot(act, wl_ref[...])
        w_log = -jax.nn.softplus(-(w0 + lo[:, :C])) - 0.5
        lw = -jnp.exp(w_log)
        a = jax.nn.sigmoid(a0 + lo[:, C:2 * C])
        kk = k * k_k
        kk = kk / jnp.maximum(jnp.sqrt(_dot_exact_rhs(kk * kk, bd)), 1e-12)
        k2 = k * (1.0 + (a - 1.0) * k_a)
        cum = _dot_exact_lhs(tri, lw)
        total = cum[L - 1:L, :]
        e_neg = jnp.exp(-cum)
        e_rem = jnp.exp(total - cum)
        kka = kk * a
        return dict(r=r, v=v, k2=k2, g=lo[:, 2 * C:], a_t=-kk * jnp.exp(cum - lw), b_t=kka * e_neg,
                    k_t=k2 * e_neg, r_t=r * jnp.exp(cum), b_bar=kka * e_rem, k_bar=k2 * e_rem,
                    p_total=jnp.exp(total))

    ew = [elementwise(bi) for bi in range(nb)]

    m0 = lax.broadcasted_iota(I32, (L, LANES), 1) < HEAD
    r2i = lax.broadcasted_iota(I32, (2 * L, 4 * L), 0)
    c2i = lax.broadcasted_iota(I32, (2 * L, 4 * L), 1) % (2 * L)
    strict, incl = c2i < r2i, c2i <= r2i
    npairs = RWKV_HEADS // 2
    chains = [(bi, p) for bi in range(nb) for p in range(npairs)]
    nc = range(len(chains))

    def stack2(name):
        out = []
        for bi, p in chains:
            xp = ew[bi][name][:, p * LANES:(p + 1) * LANES]
            out.append(jnp.concatenate([jnp.where(m0, xp, 0.0), jnp.where(m0, 0.0, xp)], axis=0))
        return out

    A2, B2, K2, R2 = stack2("a_t"), stack2("b_t"), stack2("k_t"), stack2("r_t")
    V2, Bb2, Kb2 = stack2("v"), stack2("b_bar"), stack2("k_bar")
    mg = [_dot_nt(jnp.concatenate([A2[c], R2[c]], axis=0), jnp.concatenate([B2[c], K2[c]], axis=0)) for c in nc]
    m_cat = [jnp.where(strict, mg[c][:2 * L], 0.0) for c in nc]
    g_cat = [jnp.where(incl, mg[c][2 * L:], 0.0) for c in nc]
    m_ab = [m_cat[c][:, :LANES] for c in nc]
    pw = [_dot(m_ab[c], m_ab[c]) for c in nc]
    tr = list(m_ab)
    k = 2
    while 2 * k < L:
        x = [_dot(pw[c], jnp.concatenate([pw[c], tr[c]], axis=1)) for c in nc]
        tr = [tr[c] + pw[c] + x[c][:, LANES:] for c in nc]
        pw = [x[c][:, :LANES] for c in nc]
        k *= 2
    tr = [tr[c] + pw[c] + _dot(pw[c], tr[c]) for c in nc]
    az = [jnp.concatenate([A2[c], _dot(m_cat[c][:, LANES:], V2[c])], axis=1) for c in nc]
    tz = [az[c] + _dot(tr[c], az[c]) for c in nc]
    st = [st_ref[bi, p] for bi, p in chains]
    u2 = [_dot_nt(tz[c][:, :LANES], st[c]) + tz[c][:, LANES:] for c in nc]
    uv = [jnp.concatenate([u2[c], V2[c]], axis=0) for c in nc]
    y2 = [_dot_nt(R2[c], st[c]) + _dot(g_cat[c], uv[c]) for c in nc]
    for c, (bi, p) in enumerate(chains):
        st_ref[bi, p] = (st[c] * ew[bi]["p_total"][:, p * LANES:(p + 1) * LANES]
                         + _dot_tn(uv[c], jnp.concatenate([Bb2[c], Kb2[c]], axis=0)))

    inv_n = 1.0 / HEAD
    for bi in range(nb):
        y = jnp.concatenate([y2[bi * npairs + p][:L] + y2[bi * npairs + p][L:] for p in range(npairs)], axis=1)
        e = ew[bi]
        mu = _dot_exact_rhs(y, bd) * inv_n
        d = y - mu
        var = _dot_exact_rhs(d * d, bd) * inv_n
        yn = d * lax.rsqrt(var + RWKV_GN_EPS) * lnx_g + lnx_b
        bonus = _dot_exact_rhs(e["r"] * e["k2"] * r_k, bd) * e["v"]
        o_ref[bi] = (yn + bonus) * e["g"]


def _rwkv_group(proj3, mu_rkv, mu_l, wl, vec, rkv_block, lora_block):
    b, s, _ = proj3.shape
    L, C = RWKV_CHUNK, RWKV_W
    head = jnp.arange(C, dtype=I32) // HEAD
    bd = (head[:, None] == head[None, :]).astype(BF16)
    t = jnp.arange(L, dtype=I32)
    tri = (t[None, :] <= t[:, None]).astype(BF16)
    const = lambda shape: pl.BlockSpec(shape, lambda i, j: (0,) * len(shape))
    nb = RWKV_BATCH if b % RWKV_BATCH == 0 else 1
    return pl.pallas_call(
        _rwkv_kernel,
        grid=(b // nb, s // L),
        in_specs=[pl.BlockSpec((nb, L, 3 * C), lambda i, j: (i, j, rkv_block)),
                  pl.BlockSpec((nb, L, LORA_PAD), lambda i, j: (i, j, lora_block)),
                  const((1, 3 * C)), const((1, LORA_PAD)), const((LORA_PAD, 3 * C)),
                  const((8, C)), const((C, C)), const((L, L))],
        out_specs=pl.BlockSpec((nb, L, C), lambda i, j: (i, j, 0)),
        out_shape=jax.ShapeDtypeStruct((b, s, C), F32),
        scratch_shapes=[pltpu.VMEM((nb, RWKV_HEADS // 2, LANES, LANES), F32),
                        pltpu.VMEM((nb, 1, 3 * C), F32), pltpu.VMEM((nb, 1, LORA_PAD), F32)],
        compiler_params=_params("parallel", "arbitrary"),
        name="rwkv_group",
    )(proj3, proj3, mu_rkv, mu_l, wl, vec, bd, tri)


def _diff_attn_kernel(qi_ref, ki_ref, q_ref, k_ref, v_ref, slope_ref, lam_ref, g_ref, o_ref,
                      q2_scr, m_scr, acc_scr, *, lambda_init):
    blk = ATT_BLOCK
    step_id = pl.program_id(2)
    qi, ki = qi_ref[step_id], ki_ref[step_id]
    log2e = math.log2(math.e)

    @pl.when(ki == 0)
    def _():
        q = q_ref[...].astype(F32) * (HEAD ** -0.5 * log2e)
        m0 = lax.broadcasted_iota(I32, (blk, LANES), 1) < HEAD
        q2_scr[0:blk, :] = jnp.where(m0, q, 0.0).astype(BF16)
        q2_scr[blk:, :] = jnp.where(m0, 0.0, q).astype(BF16)
        m_scr[...] = jnp.full_like(m_scr, NEG_BIG)
        acc_scr[...] = jnp.zeros_like(acc_scr)

    def step(on_diagonal):
        k = k_ref[...].astype(BF16)
        v_ext = jnp.concatenate([v_ref[...].astype(BF16), jnp.ones((blk, LANES), BF16)], axis=1)
        koff = lax.broadcasted_iota(I32, (1, blk), 1)
        col_bias = (slope_ref[:, :1] * log2e) * ((ki - qi) * blk + koff).astype(F32)
        rc = ATT_ROW_CHUNK
        n_chunks = 2 * blk // rc

        def scores(c):
            return lax.dot_general(q2_scr[c * rc:(c + 1) * rc, :], k, (((1,), (1,)), ((), ())),
                                   preferred_element_type=F32)

        s_next = scores(0)
        for c in range(n_chunks):
            rows = slice(c * rc, (c + 1) * rc)
            s = s_next + col_bias
            if c + 1 < n_chunks:
                s_next = scores(c + 1)
            if on_diagonal:
                qoff = (c * rc) % blk + lax.broadcasted_iota(I32, (rc, 1), 0)
                s = jnp.where(koff <= qoff, s, NEG_BIG)
            m_prev = m_scr[rows, :]
            m_new = jnp.maximum(m_prev, jnp.max(s, axis=-1, keepdims=True))
            alpha = jnp.exp2(m_prev - m_new)
            p = jnp.exp2(s - jnp.concatenate([m_new] * (blk // LANES), axis=1))
            acc_scr[rows, :] = (jnp.concatenate([alpha, alpha], axis=1) * acc_scr[rows, :]
                                + jnp.dot(p.astype(BF16), v_ext, preferred_element_type=F32))
            m_scr[rows, :] = m_new

    @pl.when(ki < qi)
    def _():
        step(False)

    @pl.when(ki == qi)
    def _():
        step(True)
        lam_v = lam_ref[...]
        lam = (jnp.exp(jnp.sum(lam_v[0:1] * lam_v[1:2], axis=-1, keepdims=True))
               - jnp.exp(jnp.sum(lam_v[2:3] * lam_v[3:4], axis=-1, keepdims=True)) + lambda_init)
        o2 = acc_scr[:, :LANES] / acc_scr[:, LANES:]
        o = o2[:blk] - lam * o2[blk:]
        o_ref[...] = _rms(o, g_ref[...]) * (1.0 - lambda_init)


def _diff_attention(proj3, lam_vecs, subln_g, lambda_init):
    b, s, _ = proj3.shape
    blk = ATT_BLOCK
    nb = s // blk
    nh = DIFF_HEADS
    slopes = jnp.exp2(-8.0 * jnp.arange(1, nh + 1, dtype=F32) / nh)
    slopes = jnp.broadcast_to(slopes[:, None, None], (nh, 1, LANES))
    pairs = [(qi, ki) for qi in range(nb) for ki in range(qi + 1)]
    qi_tab = jnp.asarray([pq for pq, _ in pairs], I32)
    ki_tab = jnp.asarray([pk for _, pk in pairs], I32)
    kernel = functools.partial(_diff_attn_kernel, lambda_init=lambda_init)
    grid_spec = pltpu.PrefetchScalarGridSpec(
        num_scalar_prefetch=2,
        grid=(b, nh, len(pairs)),
        in_specs=[pl.BlockSpec((None, blk, LANES), lambda bi, h, t, qt, kt: (bi, qt[t], h)),
                  pl.BlockSpec((None, blk, LANES), lambda bi, h, t, qt, kt: (bi, kt[t], nh + h)),
                  pl.BlockSpec((None, blk, LANES), lambda bi, h, t, qt, kt: (bi, kt[t], 2 * nh + h)),
                  pl.BlockSpec((None, 1, LANES), lambda bi, h, t, qt, kt: (h, 0, 0)),
                  pl.BlockSpec((4, HEAD), lambda bi, h, t, qt, kt: (0, 0)),
                  pl.BlockSpec((1, 2 * HEAD), lambda bi, h, t, qt, kt: (0, 0))],
        out_specs=pl.BlockSpec((None, blk, LANES), lambda bi, h, t, qt, kt: (bi, qt[t], h)),
        scratch_shapes=[pltpu.VMEM((2 * blk, LANES), BF16), pltpu.VMEM((2 * blk, LANES), F32),
                        pltpu.VMEM((2 * blk, 2 * LANES), F32)],
    )
    return pl.pallas_call(
        kernel,
        grid_spec=grid_spec,
        out_shape=jax.ShapeDtypeStruct((b, s, DIFF_W), F32),
        compiler_params=_params("parallel", "parallel", "arbitrary"),
        name="diff_attention",
    )(qi_tab, ki_tab, proj3, proj3, proj3, slopes, lam_vecs, subln_g)


def _mem_kv_kernel(m_ref, g_ref, w_ref, k_ref, v_ref):
    d = m_ref.shape[-1]
    kv = _dot(_rms(m_ref[...], g_ref[...]), w_ref[...])
    k_ref[...] = kv[:, :d].astype(BF16)
    v_ref[...] = kv[:, d:].astype(BF16)


def _mem_kv(mem, g, w_ckv_bf16):
    b, m, d = mem.shape
    return pl.pallas_call(
        _mem_kv_kernel,
        grid=(b,),
        in_specs=[pl.BlockSpec((None, m, d), lambda i: (i, 0, 0)),
                  pl.BlockSpec((1, d), lambda i: (0, 0)),
                  pl.BlockSpec((d, 2 * d), lambda i: (0, 0))],
        out_specs=[pl.BlockSpec((None, m, d), lambda i: (i, 0, 0))] * 2,
        out_shape=[jax.ShapeDtypeStruct((b, m, d), BF16)] * 2,
        compiler_params=_params("parallel"),
        name="mem_kv",
    )(mem, g, w_ckv_bf16)


def _mid_kernel(x_ref, yr_ref, yd_ref, wo_ref, gc_ref, wcq_ref, km_ref, vm_ref, wco_ref, gf_ref,
                wr_hi_ref, wr_lo_ref, br_ref, tri_ref,
                x2_ref, hn_ref, ri_ref, rg_ref, cnt_ref, carry_scr):
    tm, d = x_ref.shape
    first = jnp.logical_and(pl.program_id(0) == 0, pl.program_id(1) == 0)

    @pl.when(first)
    def _():
        carry_scr[...] = jnp.zeros_like(carry_scr)

    rc = tri_ref.shape[0]
    chains = range(tm // rc)
    rows = [slice(c * rc, (c + 1) * rc) for c in chains]
    half = yr_ref.shape[-1]
    x1 = [x_ref[r, :] + _dot(yr_ref[r, :], wo_ref[:half, :]) + _dot(yd_ref[r, :], wo_ref[half:, :]) for r in rows]

    q = [_dot(_rms(x1[c], gc_ref[...]), wcq_ref[...]) for c in chains]
    hd = d // CROSS_HEADS
    outs = [[] for _ in chains]
    for h in range(CROSS_HEADS):
        sl = slice(h * hd, (h + 1) * hd)
        s = [_dot_nt(q[c][:, sl], km_ref[:, sl]) * (hd ** -0.5) for c in chains]
        p = [jnp.exp(s[c] - jnp.max(s[c], axis=-1, keepdims=True)) for c in chains]
        p = [p[c] / jnp.sum(p[c], axis=-1, keepdims=True) for c in chains]
        for c in chains:
            outs[c].append(_dot(p[c], vm_ref[:, sl]))
    x2 = [x1[c] + _dot(jnp.concatenate(outs[c], axis=1), wco_ref[...]) for c in chains]
    for c in chains:
        x2_ref[rows[c], :] = x2[c]

    hn = [_rms(x2[c], gf_ref[...]) for c in chains]
    _store_token_tiles(hn_ref, jnp.concatenate(hn, axis=0))
    hi = [hn[c].astype(BF16) for c in chains]
    lo = [(hn[c] - hi[c].astype(F32)).astype(BF16) for c in chains]
    vals = [(jnp.dot(hi[c], wr_hi_ref[...], preferred_element_type=F32)
             + jnp.dot(lo[c], wr_hi_ref[...], preferred_element_type=F32)
             + jnp.dot(hi[c], wr_lo_ref[...], preferred_element_type=F32) + br_ref[...]) for c in chains]
    lane = lax.broadcasted_iota(I32, (rc, LANES), 1)
    tops, idxs, hots = [[] for _ in chains], [[] for _ in chains], [[] for _ in chains]
    for _ in range(TOP_K):
        for c in chains:
            mx = jnp.max(vals[c], axis=-1, keepdims=True)
            idx = jnp.min(jnp.where(vals[c] == mx, lane, LANES), axis=-1, keepdims=True)
            hot = lane == idx
            vals[c] = jnp.where(hot, -jnp.inf, vals[c])
            tops[c].append(mx)
            idxs[c].append(idx)
            hots[c].append(hot)
    for c in chains:
        es = [jnp.exp(t - tops[c][0]) for t in tops[c]]
        denom = es[0] + es[1] + es[2] + es[3]
        sel = jnp.zeros((rc, LANES), F32)
        for hot in hots[c]:
            sel = sel + hot.astype(F32)
        before = jnp.dot(tri_ref[...], sel.astype(BF16), preferred_element_type=F32) + carry_scr[...]
        carry_scr[...] = carry_scr[...] + jnp.sum(sel, axis=0, keepdims=True)
        ri = jnp.zeros((rc, LANES), I32)
        rg = jnp.zeros((rc, LANES), F32)
        for j in range(TOP_K):
            rank = jnp.sum(jnp.where(hots[c][j], before, 0.0), axis=-1, keepdims=True).astype(I32)
            ri = jnp.where(lane == j, idxs[c][j], ri)
            ri = jnp.where(lane == TOP_K + j, rank, ri)
            rg = jnp.where(lane == j, es[j] / denom, rg)
        ri_ref[rows[c], :] = ri
        rg_ref[rows[c], :] = rg
    cnt_ref[...] = jnp.broadcast_to(carry_scr[...], cnt_ref.shape)


def _mid_stage(x, y_rwkv, y_diff, w_out, g_cross, w_cq, k_mem, v_mem, w_co, g_ffn, w_router, b_router):
    b, s, d = x.shape
    tm = min(MID_TM, s)
    m = k_mem.shape[1]
    half = y_rwkv.shape[-1]
    e = w_router.shape[1]
    wr = jnp.zeros((d, LANES), F32).at[:, :e].set(w_router)
    wr_hi = wr.astype(BF16)
    wr_lo = (wr - wr_hi.astype(F32)).astype(BF16)
    br = jnp.full((1, LANES), NEG_BIG, F32).at[0, :e].set(b_router)
    rc = min(MID_CHAIN_ROWS, tm)
    t = jnp.arange(rc, dtype=I32)
    tri = (t[None, :] < t[:, None]).astype(BF16)
    tile = lambda w: pl.BlockSpec((None, tm, w), lambda i, j: (i, j, 0))
    const = lambda shape: pl.BlockSpec(shape, lambda i, j: (0,) * len(shape))
    return pl.pallas_call(
        _mid_kernel,
        grid=(b, s // tm),
        in_specs=[tile(d), tile(half), tile(half), const((d, d)), const((1, d)), const((d, d)),
                  pl.BlockSpec((None, m, d), lambda i, j: (i, 0, 0)),
                  pl.BlockSpec((None, m, d), lambda i, j: (i, 0, 0)),
                  const((d, d)), const((1, d)), const((d, LANES)), const((d, LANES)),
                  const((1, LANES)), const((rc, rc))],
        out_specs=[tile(d), pl.BlockSpec((tm * TILE_ROWS, LANES), lambda i, j: (i * (s // tm) + j, 0)),
                   tile(LANES), tile(LANES), const((8, LANES))],
        out_shape=[jax.ShapeDtypeStruct((b, s, d), F32), jax.ShapeDtypeStruct((b * s * TILE_ROWS, LANES), F32),
                   jax.ShapeDtypeStruct((b, s, LANES), I32), jax.ShapeDtypeStruct((b, s, LANES), F32),
                   jax.ShapeDtypeStruct((8, LANES), F32)],
        scratch_shapes=[pltpu.VMEM((1, LANES), F32)],
        compiler_params=_params("arbitrary", "arbitrary"),
        name="outproj_cross_router",
    )(x, y_rwkv, y_diff, w_out.astype(BF16), g_cross, w_cq.astype(BF16), k_mem, v_mem,
      w_co.astype(BF16), g_ffn, wr_hi, wr_lo, br, tri)


def _expert_kernel(be_ref, nblk_ref, nused_ref, x_ref, w1_hbm, b1_ref, w2_hbm, b2_ref, o_ref,
                   w1_stage, w2_stage, w1_scr, w2_scr, w2i_scr, sem):
    i = pl.program_id(0)
    n_used = nused_ref[0]
    used = i < n_used
    expert = be_ref[i]
    new_expert = jnp.logical_or(i == 0, expert != be_ref[jnp.maximum(i - 1, 0)])
    d, f = w1_stage.shape[0], w2_stage.shape[0]

    def weight_copies(ex):
        return (pltpu.make_async_copy(w1_hbm.at[ex], w1_stage, sem.at[0]),
                pltpu.make_async_copy(w2_hbm.at[ex], w2_stage, sem.at[1]))

    @pl.when(jnp.logical_and(used, i == 0))
    def _():
        for cp in weight_copies(expert):
            cp.start()

    @pl.when(jnp.logical_and(used, new_expert))
    def _():
        for cp in weight_copies(expert):
            cp.wait()
        step = 128

        def cast_rows(c, carry):
            rows = pl.ds(pl.multiple_of(c * step, step), step)
            w1_scr[rows, :] = w1_stage[rows, :].astype(BF16)
            return carry

        lax.fori_loop(0, d // step, cast_rows, 0)
        for cb in range(w2_stage.shape[1] // LANES):
            cols = slice(cb * LANES, (cb + 1) * LANES)
            w2i_scr[cb, pl.ds(0, f // 2, stride=2), :] = w2_stage[:f // 2, cols]
            w2i_scr[cb, pl.ds(1, f // 2, stride=2), :] = w2_stage[f // 2:, cols]
            w2_scr[:, cols] = w2i_scr[cb].astype(BF16)
        next_run = i + nblk_ref[expert]

        @pl.when(next_run < n_used)
        def _():
            for cp in weight_copies(be_ref[jnp.minimum(next_run, be_ref.shape[0] - 1)]):
                cp.start()

    @pl.when(used)
    def _():
        x = _load_token_tiles(x_ref, MOE_ROWS).astype(BF16)
        even = lax.broadcasted_iota(I32, (MOE_ROWS, LANES), 1) % 2 == 0
        cw = 2 * LANES

        def hidden_pair(j):
            lo, hi = slice(j * cw, (j + 1) * cw), slice(f + j * cw, f + (j + 1) * cw)
            return (jnp.dot(x, w1_scr[:, lo], preferred_element_type=F32) + b1_ref[:, lo],
                    jnp.dot(x, w1_scr[:, hi], preferred_element_type=F32) + b1_ref[:, hi])

        def act_block(h):
            lin = jnp.clip(pltpu.roll(h, LANES - 1, axis=1), -SWIGLU_LIMIT, SWIGLU_LIMIT)
            glu = jnp.minimum(h, SWIGLU_LIMIT)
            return jnp.where(even, glu * jax.nn.sigmoid(SWIGLU_ALPHA * glu) * (lin + 1.0), 0.0)

        y = b2_ref[...]
        pending = hidden_pair(0)
        for j in range(f // cw):
            ha, hb = pending
            if (j + 1) * cw < f:
                pending = hidden_pair(j + 1)
            packed = jnp.concatenate(
                [act_block(ha[:, c * LANES:(c + 1) * LANES])
                 + pltpu.roll(act_block(hb[:, c * LANES:(c + 1) * LANES]), 1, axis=1) for c in range(cw // LANES)],
                axis=1)
            y = y + jnp.dot(packed.astype(BF16), w2_scr[j * cw:(j + 1) * cw, :], preferred_element_type=F32)
        _store_token_tiles(o_ref, y)

    @pl.when(jnp.logical_not(used))
    def _():
        o_ref[...] = jnp.zeros_like(o_ref)


def _expert_ffn(x_sorted, block_e, blocks_per_expert, n_used, w1, b1, w2, b2):
    e, d, f2 = w1.shape
    r = MOE_ROWS
    n_rows = x_sorted.shape[0] // TILE_ROWS
    grid_spec = pltpu.PrefetchScalarGridSpec(
        num_scalar_prefetch=3,
        grid=(n_rows // r,),
        in_specs=[pl.BlockSpec((r * TILE_ROWS, LANES), lambda i, be, nb, nu: (i, 0)),
                  pl.BlockSpec(memory_space=pl.ANY),
                  pl.BlockSpec((None, 1, f2), lambda i, be, nb, nu: (be[i], 0, 0)),
                  pl.BlockSpec(memory_space=pl.ANY),
                  pl.BlockSpec((None, 1, d), lambda i, be, nb, nu: (be[i], 0, 0))],
        out_specs=pl.BlockSpec((r * TILE_ROWS, LANES), lambda i, be, nb, nu: (i, 0)),
        scratch_shapes=[pltpu.VMEM((d, f2), F32), pltpu.VMEM((f2 // 2, d), F32),
                        pltpu.VMEM((d, f2), BF16), pltpu.VMEM((f2 // 2, d), BF16),
                        pltpu.VMEM((d // LANES, f2 // 2, LANES), F32), pltpu.SemaphoreType.DMA((2,))],
    )
    return pl.pallas_call(
        _expert_kernel,
        grid_spec=grid_spec,
        out_shape=jax.ShapeDtypeStruct((n_rows * TILE_ROWS, LANES), F32),
        compiler_params=_params("arbitrary"),
        name="expert_ffn",
    )(block_e, blocks_per_expert, n_used, x_sorted, w1, b1.reshape(e, 1, f2), w2, b2.reshape(e, 1, d))


def _tile_at(ref, first_row):
    return ref.at[pl.ds(pl.multiple_of(first_row, TILE_ROWS), TILE_ROWS)]


def _dispatch_kernel(pad_lo_ref, pad_hi_ref, dest_ref, hn_ref, xs_hbm, zero_scr, sem, pad_sem, *, tm):
    @pl.when(pl.program_id(0) == 0)
    def _():
        zero_scr[...] = jnp.zeros_like(zero_scr)

        def pad_copy(row):
            return pltpu.make_async_copy(zero_scr, _tile_at(xs_hbm, row * TILE_ROWS), pad_sem)

        def per_expert(e, carry):
            lo, hi = pad_lo_ref[e], pad_hi_ref[e]
            lax.fori_loop(lo, hi, lambda r, c: (pad_copy(r).start(), c)[1], 0)
            lax.fori_loop(lo, hi, lambda r, c: (pad_copy(r).wait(), c)[1], 0)
            return carry

        lax.fori_loop(0, pad_lo_ref.shape[0], per_expert, 0)

    def issue(t, carry):
        src = _tile_at(hn_ref, t * TILE_ROWS)
        for j in range(TOP_K):
            pltpu.make_async_copy(src, _tile_at(xs_hbm, dest_ref[0, t * TOP_K + j]), sem).start(priority=j % 2)
        return carry

    lax.fori_loop(0, tm, issue, 0, unroll=8)
    for j in range(TOP_K):
        pltpu.make_async_copy(hn_ref, xs_hbm.at[pl.ds(0, tm * TILE_ROWS)], sem).wait()


def _dispatch(hn_tiles, dest_rows, pad_lo, pad_hi, n_rows):
    t = hn_tiles.shape[0] // TILE_ROWS
    tm = min(DISPATCH_TM, t)
    steps = t // tm
    grid_spec = pltpu.PrefetchScalarGridSpec(
        num_scalar_prefetch=2,
        grid=(steps,),
        in_specs=[pl.BlockSpec((None, 1, tm * TOP_K), lambda i, lo, hi: (i, 0, 0), memory_space=pltpu.SMEM),
                  pl.BlockSpec((tm * TILE_ROWS, LANES), lambda i, lo, hi: (i, 0))],
        out_specs=pl.BlockSpec(memory_space=pl.ANY),
        scratch_shapes=[pltpu.VMEM((TILE_ROWS, LANES), F32), pltpu.SemaphoreType.DMA(()),
                        pltpu.SemaphoreType.DMA(())],
    )
    return pl.pallas_call(
        functools.partial(_dispatch_kernel, tm=tm),
        grid_spec=grid_spec,
        out_shape=jax.ShapeDtypeStruct((n_rows * TILE_ROWS, LANES), F32),
        compiler_params=_params("arbitrary"),
        name="moe_dispatch",
    )(pad_lo, pad_hi, dest_rows.reshape(steps, 1, tm * TOP_K), hn_tiles)


def _combine_kernel(dest_ref, x_ref, gate_ref, g_ref, ys_hbm, o_ref, buf, sem, *, tm, final):
    def issue(t, carry):
        for j in range(TOP_K):
            pltpu.make_async_copy(_tile_at(ys_hbm, dest_ref[0, t * TOP_K + j]),
                                  _tile_at(buf.at[j], t * TILE_ROWS), sem).start(priority=j % 2)
        return carry

    lax.fori_loop(0, tm, issue, 0, unroll=8)
    pltpu.make_async_copy(buf, buf, sem).wait()
    y = x_ref[...]
    gate = gate_ref[...]
    for j in range(TOP_K):
        y = y + gate[:, j:j + 1] * _load_token_tiles(buf, tm, lead=(j,))
    o_ref[...] = _rms(y, g_ref[...]) if final else y


def _combine(x2d, gate2d, dest_rows, y_sorted, g, final):
    t, d = x2d.shape
    tm = min(COMBINE_TM, t)
    steps = t // tm
    return pl.pallas_call(
        functools.partial(_combine_kernel, tm=tm, final=final),
        grid=(steps,),
        in_specs=[pl.BlockSpec((None, 1, tm * TOP_K), lambda i: (i, 0, 0), memory_space=pltpu.SMEM),
                  pl.BlockSpec((tm, d), lambda i: (i, 0)), pl.BlockSpec((tm, LANES), lambda i: (i, 0)),
                  pl.BlockSpec((1, d), lambda i: (0, 0)), pl.BlockSpec(memory_space=pl.ANY)],
        out_specs=pl.BlockSpec((tm, d), lambda i: (i, 0)),
        out_shape=jax.ShapeDtypeStruct((t, d), F32),
        scratch_shapes=[pltpu.VMEM((TOP_K, tm * TILE_ROWS, LANES), F32), pltpu.SemaphoreType.DMA(())],
        compiler_params=_params("arbitrary"),
        name="combine_norm",
    )(dest_rows.reshape(steps, 1, tm * TOP_K), x2d, gate2d, g, y_sorted)


def _layer(x, mem, lyr, final, p):
    b, s, d = x.shape
    t = b * s
    c = RWKV_W
    rwkv_in = 3 * c + LORA_W

    w_in = p["w_in"][lyr]
    w_cat = jnp.concatenate([w_in[:, rwkv_in:], w_in[:, :3 * c], w_in[:, 3 * c:rwkv_in],
                             jnp.zeros((d, LORA_PAD - LORA_W), F32)], axis=1).astype(BF16)
    diff_cols = 3 * DIFF_W
    proj_diff, proj_rwkv = _norm_inproj(x.reshape(t, d), p["norm_mix_g"][lyr][None], w_cat, diff_cols)
    proj_diff, proj_rwkv = proj_diff.reshape(b, s, -1), proj_rwkv.reshape(b, s, -1)

    mu = p["shift_mu"][lyr]
    mu_l = jnp.zeros((1, LORA_PAD), F32).at[0, :LORA_W].set(mu[3 * c:])
    wl = jnp.zeros((LORA_PAD, 3 * c), F32)
    wl = wl.at[:DECAY_LORA, :c].set(p["w_decay_up"][lyr])
    wl = wl.at[DECAY_LORA:DECAY_LORA + ICLR_LORA, c:2 * c].set(p["w_iclr_up"][lyr])
    wl = wl.at[DECAY_LORA + ICLR_LORA:LORA_W, 2 * c:].set(p["w_gate_up"][lyr])
    vec = jnp.stack([p["w0"][lyr], p["a0"][lyr], p["k_k"][lyr], p["k_a"][lyr], p["r_k"][lyr].reshape(c),
                     p["lnx_g"][lyr], p["lnx_b"][lyr], jnp.zeros((c,), F32)])
    y_rwkv = _rwkv_group(proj_rwkv, mu[None, :3 * c], mu_l, wl.astype(BF16), vec,
                         rkv_block=0, lora_block=3 * c // LORA_PAD)

    lambda_init = 0.8 - 0.6 * math.exp(-0.3 * lyr)
    lam_vecs = jnp.stack([p["lambda_q1"][lyr], p["lambda_k1"][lyr], p["lambda_q2"][lyr], p["lambda_k2"][lyr]])
    y_diff = _diff_attention(proj_diff, lam_vecs, p["subln_g"][lyr][None], lambda_init)

    k_mem, v_mem = _mem_kv(mem, p["norm_mem_g"][lyr][None], p["w_ckv"][lyr].astype(BF16))
    x2, hn, route_i, route_g, counts = _mid_stage(
        x, y_rwkv, y_diff, p["w_out"][lyr], p["norm_cross_g"][lyr][None], p["w_cq"][lyr], k_mem, v_mem,
        p["w_co"][lyr], p["norm_ffn_g"][lyr][None], p["w_router"][lyr], p["b_router"][lyr])

    e = N_EXPERTS
    r = MOE_ROWS
    n_blocks = (t * TOP_K) // r + e
    route_i = route_i.reshape(t, LANES)
    idx, rank = route_i[:, :TOP_K], route_i[:, TOP_K:2 * TOP_K]
    cnt = counts[0, :e].astype(I32)
    padded = (cnt + r - 1) // r * r
    pad_end = jnp.cumsum(padded)
    pad_start = pad_end - padded
    dest = jnp.sum(jnp.where(idx[:, :, None] == jnp.arange(e, dtype=I32), pad_start, 0), axis=-1) + rank
    dest = dest.reshape(-1) * TILE_ROWS
    block_start = jnp.arange(n_blocks, dtype=I32) * r
    block_e = jnp.minimum(jnp.sum((block_start[:, None] >= pad_end[None, :]).astype(I32), axis=1), e - 1)
    n_used = pad_end[-1:] // r

    unused_lo = jnp.concatenate([pad_start + cnt, pad_end[-1:]])
    unused_hi = jnp.concatenate([pad_end, jnp.full((1,), n_blocks * r, I32)])
    x_sorted = _dispatch(hn, dest, unused_lo, unused_hi, n_blocks * r)
    y_sorted = _expert_ffn(x_sorted, block_e, padded // r, n_used, p["w1"][lyr], p["b1"][lyr], p["w2"][lyr],
                           p["b2"][lyr])
    return _combine(x2.reshape(t, d), route_g.reshape(t, LANES), dest, y_sorted,
                    p["norm_final_g"][None], final).reshape(b, s, d)


def kernel(x, mem, norm_mix_g, w_in, shift_mu, w0, w_decay_up, a0, w_iclr_up, w_gate_up, k_k, k_a, r_k,
           lnx_g, lnx_b, lambda_q1, lambda_k1, lambda_q2, lambda_k2, subln_g, w_out, norm_cross_g,
           norm_mem_g, w_cq, w_ckv, w_co, norm_ffn_g, w_router, b_router, w1, b1, w2, b2, norm_final_g):
    p = dict(norm_mix_g=norm_mix_g, w_in=w_in, shift_mu=shift_mu, w0=w0, w_decay_up=w_decay_up, a0=a0,
             w_iclr_up=w_iclr_up, w_gate_up=w_gate_up, k_k=k_k, k_a=k_a, r_k=r_k, lnx_g=lnx_g, lnx_b=lnx_b,
             lambda_q1=lambda_q1, lambda_k1=lambda_k1, lambda_q2=lambda_q2, lambda_k2=lambda_k2,
             subln_g=subln_g, w_out=w_out, norm_cross_g=norm_cross_g, norm_mem_g=norm_mem_g, w_cq=w_cq,
             w_ckv=w_ckv, w_co=w_co, norm_ffn_g=norm_ffn_g, w_router=w_router, b_router=b_router,
             w1=w1, b1=b1, w2=w2, b2=b2, norm_final_g=norm_final_g)
    depth = w_in.shape[0]
    for lyr in range(depth):
        x = _layer(x, mem, lyr, lyr == depth - 1, p)
    return x
```

```python
import functools
import math

import jax
import jax.numpy as jnp
from jax import lax
from jax.experimental import pallas as pl
from jax.experimental.pallas import tpu as pltpu

F32 = jnp.float32
BF16 = jnp.bfloat16
I32 = jnp.int32

NORM_EPS = 1e-5
HEAD = 64
RWKV_W = 512
RWKV_HEADS = RWKV_W // HEAD
DECAY_LORA, ICLR_LORA, GATE_LORA = 64, 64, 160
LORA_W = DECAY_LORA + ICLR_LORA + GATE_LORA
LORA_PAD = 384
RWKV_GN_EPS = 64e-5
DIFF_W = 512
DIFF_HEADS = DIFF_W // (2 * HEAD)
CROSS_HEADS = 4
N_EXPERTS = 32
TOP_K = 4
SWIGLU_LIMIT = 7.0
SWIGLU_ALPHA = 1.702
LANES = 128
TILE_ROWS = 8
NEG_BIG = -1e30

RWKV_CHUNK = 64
RWKV_BATCH = 4
PROJ_TM = 512
ATT_BLOCK = 512
ATT_ROW_CHUNK = 128
MID_TM = 512
MID_CHAIN_ROWS = 256
MOE_ROWS = 256
DISPATCH_TM = 1024
COMBINE_TM = 256
COMBINE_PARTS = 4
VMEM_LIMIT = 56 * 1024 * 1024


def _dot(a, b):
    return jnp.dot(a.astype(BF16), b.astype(BF16), preferred_element_type=F32)


def _dot_nt(a, b):
    return lax.dot_general(a.astype(BF16), b.astype(BF16), (((1,), (1,)), ((), ())),
                           preferred_element_type=F32)


def _dot_tn(a, b):
    return lax.dot_general(a.astype(BF16), b.astype(BF16), (((0,), (0,)), ((), ())),
                           preferred_element_type=F32)


def _split2(x):
    hi = x.astype(BF16)
    lo = (x - hi.astype(F32)).astype(BF16)
    return hi, lo


def _dot_exact_rhs(x, ones_bf16):
    hi, lo = _split2(x)
    return jnp.dot(hi, ones_bf16, preferred_element_type=F32) + jnp.dot(lo, ones_bf16, preferred_element_type=F32)


def _dot_exact_lhs(ones_bf16, x):
    hi, lo = _split2(x)
    return jnp.dot(ones_bf16, hi, preferred_element_type=F32) + jnp.dot(ones_bf16, lo, preferred_element_type=F32)


def _store_token_tiles(ref, x, lead=()):
    rows = x.shape[0]
    for c in range(TILE_ROWS):
        ref[lead + (pl.ds(c, rows, stride=TILE_ROWS), slice(None))] = x[:, c * LANES:(c + 1) * LANES]


def _load_token_tiles(ref, rows, lead=(), first=0):
    return jnp.concatenate([ref[lead + (pl.ds(first * TILE_ROWS + c, rows, stride=TILE_ROWS), slice(None))]
                            for c in range(TILE_ROWS)], axis=1)


def _rms(x, g):
    return x * lax.rsqrt(jnp.mean(x * x, axis=-1, keepdims=True) + NORM_EPS) * g


def _params(*sem):
    return pltpu.CompilerParams(dimension_semantics=sem, vmem_limit_bytes=VMEM_LIMIT)


def _norm_inproj_kernel(x_ref, g_ref, w_ref, od_ref, or_ref):
    h = _rms(x_ref[...], g_ref[...]).astype(BF16)
    nd = od_ref.shape[1]
    od_ref[...] = jnp.dot(h, w_ref[:, :nd], preferred_element_type=F32).astype(BF16)
    or_ref[...] = jnp.dot(h, w_ref[:, nd:], preferred_element_type=F32)


def _norm_inproj(x2d, g, w_bf16, n_diff):
    t, d = x2d.shape
    n = w_bf16.shape[1]
    tm = min(PROJ_TM, t)
    return pl.pallas_call(
        _norm_inproj_kernel,
        grid=(t // tm,),
        in_specs=[pl.BlockSpec((tm, d), lambda i: (i, 0)),
                  pl.BlockSpec((1, d), lambda i: (0, 0)),
                  pl.BlockSpec((d, n), lambda i: (0, 0))],
        out_specs=[pl.BlockSpec((tm, n_diff), lambda i: (i, 0)), pl.BlockSpec((tm, n - n_diff), lambda i: (i, 0))],
        out_shape=[jax.ShapeDtypeStruct((t, n_diff), BF16), jax.ShapeDtypeStruct((t, n - n_diff), F32)],
        compiler_params=_params("parallel"),
        name="norm_inproj",
    )(x2d, g, w_bf16)


def _rwkv_kernel(rkv_ref, lora_ref, mu_rkv_ref, mu_l_ref, wl_ref, vec_ref, bd_ref, tri_ref,
                 o_ref, st_ref, prev_rkv_ref, prev_l_ref):
    L = RWKV_CHUNK
    C = RWKV_W

    @pl.when(pl.program_id(1) == 0)
    def _():
        st_ref[...] = jnp.zeros_like(st_ref)
        prev_rkv_ref[...] = jnp.zeros_like(prev_rkv_ref)
        prev_l_ref[...] = jnp.zeros_like(prev_l_ref)

    row = lax.broadcasted_iota(I32, (L, 1), 0)
    w0, a0, k_k, k_a = vec_ref[0:1, :], vec_ref[1:2, :], vec_ref[2:3, :], vec_ref[3:4, :]
    r_k, lnx_g, lnx_b = vec_ref[4:5, :], vec_ref[5:6, :], vec_ref[6:7, :]
    bd = bd_ref[...]
    tri = tri_ref[...]
    lane_l = lax.broadcasted_iota(I32, (L, LORA_PAD), 1)
    nb = rkv_ref.shape[0]

    def token_shift(raw, prev_ref, bi, mu):
        prev = jnp.where(row == 0, prev_ref[bi], pltpu.roll(raw, 1, axis=0))
        prev_ref[bi] = raw[L - 1:L, :]
        return raw + mu * (prev - raw)

    def elementwise(bi):
        u = token_shift(rkv_ref[bi], prev_rkv_ref, bi, mu_rkv_ref[...])
        ul = token_shift(lora_ref[bi], prev_l_ref, bi, mu_l_ref[...])
        r, k, v = u[:, :C], u[:, C:2 * C], u[:, 2 * C:]
        act = jnp.where(lane_l < DECAY_LORA, jnp.tanh(ul),
                        jnp.where(lane_l < DECAY_LORA + ICLR_LORA, ul, jax.nn.sigmoid(ul)))
        lo = _dot(act, wl_ref[...])
        w_log = -jax.nn.softplus(-(w0 + lo[:, :C])) - 0.5
        lw = -jnp.exp(w_log)
        a = jax.nn.sigmoid(a0 + lo[:, C:2 * C])
        kk = k * k_k
        kk = kk / jnp.maximum(jnp.sqrt(_dot_exact_rhs(kk * kk, bd)), 1e-12)
        k2 = k * (1.0 + (a - 1.0) * k_a)
        cum = _dot_exact_lhs(tri, lw)
        total = cum[L - 1:L, :]
        e_neg = jnp.exp(-cum)
        e_rem = jnp.exp(total - cum)
        kka = kk * a
        return dict(r=r, v=v, k2=k2, g=lo[:, 2 * C:], a_t=-kk * jnp.exp(cum - lw), b_t=kka * e_neg,
                    k_t=k2 * e_neg, r_t=r * jnp.exp(cum), b_bar=kka * e_rem, k_bar=k2 * e_rem,
                    p_total=jnp.exp(total))

    ew = [elementwise(bi) for bi in range(nb)]

    m0 = lax.broadcasted_iota(I32, (L, LANES), 1) < HEAD
    r2i = lax.broadcasted_iota(I32, (2 * L, 4 * L), 0)
    c2i = lax.broadcasted_iota(I32, (2 * L, 4 * L), 1) % (2 * L)
    strict, incl = c2i < r2i, c2i <= r2i
    npairs = RWKV_HEADS // 2
    chains = [(bi, p) for bi in range(nb) for p in range(npairs)]
    nc = range(len(chains))

    def stack2(name):
        out = []
        for bi, p in chains:
            xp = ew[bi][name][:, p * LANES:(p + 1) * LANES]
            out.append(jnp.concatenate([jnp.where(m0, xp, 0.0), jnp.where(m0, 0.0, xp)], axis=0))
        return out

    A2, B2, K2, R2 = stack2("a_t"), stack2("b_t"), stack2("k_t"), stack2("r_t")
    V2, Bb2, Kb2 = stack2("v"), stack2("b_bar"), stack2("k_bar")
    mg = [_dot_nt(jnp.concatenate([A2[c], R2[c]], axis=0), jnp.concatenate([B2[c], K2[c]], axis=0)) for c in nc]
    m_cat = [jnp.where(strict, mg[c][:2 * L], 0.0) for c in nc]
    g_cat = [jnp.where(incl, mg[c][2 * L:], 0.0) for c in nc]
    m_ab = [m_cat[c][:, :LANES] for c in nc]
    pw = [_dot(m_ab[c], m_ab[c]) for c in nc]
    tr = list(m_ab)
    k = 2
    while 2 * k < L:
        x = [_dot(pw[c], jnp.concatenate([pw[c], tr[c]], axis=1)) for c in nc]
        tr = [tr[c] + pw[c] + x[c][:, LANES:] for c in nc]
        pw = [x[c][:, :LANES] for c in nc]
        k *= 2
    tr = [tr[c] + pw[c] + _dot(pw[c], tr[c]) for c in nc]
    az = [jnp.concatenate([A2[c], _dot(m_cat[c][:, LANES:], V2[c])], axis=1) for c in nc]
    tz = [az[c] + _dot(tr[c], az[c]) for c in nc]
    st = [st_ref[bi, p] for bi, p in chains]
    u2 = [_dot_nt(tz[c][:, :LANES], st[c]) + tz[c][:, LANES:] for c in nc]
    uv = [jnp.concatenate([u2[c], V2[c]], axis=0) for c in nc]
    y2 = [_dot_nt(R2[c], st[c]) + _dot(g_cat[c], uv[c]) for c in nc]
    for c, (bi, p) in enumerate(chains):
        st_ref[bi, p] = (st[c] * ew[bi]["p_total"][:, p * LANES:(p + 1) * LANES]
                         + _dot_tn(uv[c], jnp.concatenate([Bb2[c], Kb2[c]], axis=0)))

    inv_n = 1.0 / HEAD
    for bi in range(nb):
        y = jnp.concatenate([y2[bi * npairs + p][:L] + y2[bi * npairs + p][L:] for p in range(npairs)], axis=1)
        e = ew[bi]
        mu = _dot_exact_rhs(y, bd) * inv_n
        d = y - mu
        var = _dot_exact_rhs(d * d, bd) * inv_n
        yn = d * lax.rsqrt(var + RWKV_GN_EPS) * lnx_g + lnx_b
        bonus = _dot_exact_rhs(e["r"] * e["k2"] * r_k, bd) * e["v"]
        o_ref[bi] = (yn + bonus) * e["g"]


def _rwkv_group(proj3, mu_rkv, mu_l, wl, vec, rkv_block, lora_block):
    b, s, _ = proj3.shape
    L, C = RWKV_CHUNK, RWKV_W
    head = jnp.arange(C, dtype=I32) // HEAD
    bd = (head[:, None] == head[None, :]).astype(BF16)
    t = jnp.arange(L, dtype=I32)
    tri = (t[None, :] <= t[:, None]).astype(BF16)
    const = lambda shape: pl.BlockSpec(shape, lambda i, j: (0,) * len(shape))
    nb = RWKV_BATCH if b % RWKV_BATCH == 0 else 1
    return pl.pallas_call(
        _rwkv_kernel,
        grid=(b // nb, s // L),
        in_specs=[pl.BlockSpec((nb, L, 3 * C), lambda i, j: (i, j, rkv_block)),
                  pl.BlockSpec((nb, L, LORA_PAD), lambda i, j: (i, j, lora_block)),
                  const((1, 3 * C)), const((1, LORA_PAD)), const((LORA_PAD, 3 * C)),
                  const((8, C)), const((C, C)), const((L, L))],
        out_specs=pl.BlockSpec((nb, L, C), lambda i, j: (i, j, 0)),
        out_shape=jax.ShapeDtypeStruct((b, s, C), F32),
        scratch_shapes=[pltpu.VMEM((nb, RWKV_HEADS // 2, LANES, LANES), F32),
                        pltpu.VMEM((nb, 1, 3 * C), F32), pltpu.VMEM((nb, 1, LORA_PAD), F32)],
        compiler_params=_params("parallel", "arbitrary"),
        name="rwkv_group",
    )(proj3, proj3, mu_rkv, mu_l, wl, vec, bd, tri)


def _diff_attn_kernel(qi_ref, ki_ref, q_ref, k_ref, v_ref, slope_ref, lam_ref, g_ref, o_ref,
                      q2_scr, m_scr, acc_scr, *, lambda_init):
    blk = ATT_BLOCK
    step_id = pl.program_id(2)
    qi, ki = qi_ref[step_id], ki_ref[step_id]
    log2e = math.log2(math.e)

    @pl.when(ki == 0)
    def _():
        q = q_ref[...].astype(F32) * (HEAD ** -0.5 * log2e)
        m0 = lax.broadcasted_iota(I32, (blk, LANES), 1) < HEAD
        q2_scr[0:blk, :] = jnp.where(m0, q, 0.0).astype(BF16)
        q2_scr[blk:, :] = jnp.where(m0, 0.0, q).astype(BF16)
        m_scr[...] = jnp.full_like(m_scr, NEG_BIG)
        acc_scr[...] = jnp.zeros_like(acc_scr)

    def step(on_diagonal):
        k = k_ref[...].astype(BF16)
        v_ext = jnp.concatenate([v_ref[...].astype(BF16), jnp.ones((blk, LANES), BF16)], axis=1)
        koff = lax.broadcasted_iota(I32, (1, blk), 1)
        col_bias = (slope_ref[:, :1] * log2e) * ((ki - qi) * blk + koff).astype(F32)
        rc = ATT_ROW_CHUNK
        n_chunks = 2 * blk // rc

        def scores(c):
            return lax.dot_general(q2_scr[c * rc:(c + 1) * rc, :], k, (((1,), (1,)), ((), ())),
                                   preferred_element_type=F32)

        s_next = scores(0)
        for c in range(n_chunks):
            rows = slice(c * rc, (c + 1) * rc)
            s = s_next + col_bias
            if c + 1 < n_chunks:
                s_next = scores(c + 1)
            if on_diagonal:
                qoff = (c * rc) % blk + lax.broadcasted_iota(I32, (rc, 1), 0)
                s = jnp.where(koff <= qoff, s, NEG_BIG)
            m_prev = m_scr[rows, :]
            m_new = jnp.maximum(m_prev, jnp.max(s, axis=-1, keepdims=True))
            alpha = jnp.exp2(m_prev - m_new)
            p = jnp.exp2(s - jnp.concatenate([m_new] * (blk // LANES), axis=1))
            acc_scr[rows, :] = (jnp.concatenate([alpha, alpha], axis=1) * acc_scr[rows, :]
                                + jnp.dot(p.astype(BF16), v_ext, preferred_element_type=F32))
            m_scr[rows, :] = m_new

    @pl.when(ki < qi)
    def _():
        step(False)

    @pl.when(ki == qi)
    def _():
        step(True)
        lam_v = lam_ref[...]
        lam = (jnp.exp(jnp.sum(lam_v[0:1] * lam_v[1:2], axis=-1, keepdims=True))
               - jnp.exp(jnp.sum(lam_v[2:3] * lam_v[3:4], axis=-1, keepdims=True)) + lambda_init)
        o2 = acc_scr[:, :LANES] / acc_scr[:, LANES:]
        o = o2[:blk] - lam * o2[blk:]
        o_ref[...] = _rms(o, g_ref[...]) * (1.0 - lambda_init)


def _diff_attention(proj3, lam_vecs, subln_g, lambda_init):
    b, s, _ = proj3.shape
    blk = ATT_BLOCK
    nb = s // blk
    nh = DIFF_HEADS
    slopes = jnp.exp2(-8.0 * jnp.arange(1, nh + 1, dtype=F32) / nh)
    slopes = jnp.broadcast_to(slopes[:, None, None], (nh, 1, LANES))
    pairs = [(qi, ki) for qi in range(nb) for ki in range(qi + 1)]
    qi_tab = jnp.asarray([pq for pq, _ in pairs], I32)
    ki_tab = jnp.asarray([pk for _, pk in pairs], I32)
    kernel = functools.partial(_diff_attn_kernel, lambda_init=lambda_init)
    grid_spec = pltpu.PrefetchScalarGridSpec(
        num_scalar_prefetch=2,
        grid=(b, nh, len(pairs)),
        in_specs=[pl.BlockSpec((None, blk, LANES), lambda bi, h, t, qt, kt: (bi, qt[t], h)),
                  pl.BlockSpec((None, blk, LANES), lambda bi, h, t, qt, kt: (bi, kt[t], nh + h)),
                  pl.BlockSpec((None, blk, LANES), lambda bi, h, t, qt, kt: (bi, kt[t], 2 * nh + h)),
                  pl.BlockSpec((None, 1, LANES), lambda bi, h, t, qt, kt: (h, 0, 0)),
                  pl.BlockSpec((4, HEAD), lambda bi, h, t, qt, kt: (0, 0)),
                  pl.BlockSpec((1, 2 * HEAD), lambda bi, h, t, qt, kt: (0, 0))],
        out_specs=pl.BlockSpec((None, blk, LANES), lambda bi, h, t, qt, kt: (bi, qt[t], h)),
        scratch_shapes=[pltpu.VMEM((2 * blk, LANES), BF16), pltpu.VMEM((2 * blk, LANES), F32),
                        pltpu.VMEM((2 * blk, 2 * LANES), F32)],
    )
    return pl.pallas_call(
        kernel,
        grid_spec=grid_spec,
        out_shape=jax.ShapeDtypeStruct((b, s, DIFF_W), F32),
        compiler_params=_params("parallel", "parallel", "arbitrary"),
        name="diff_attention",
    )(qi_tab, ki_tab, proj3, proj3, proj3, slopes, lam_vecs, subln_g)


def _mem_kv_kernel(m_ref, g_ref, w_ref, k_ref, v_ref):
    d = m_ref.shape[-1]
    kv = _dot(_rms(m_ref[...], g_ref[...]), w_ref[...])
    k_ref[...] = kv[:, :d].astype(BF16)
    v_ref[...] = kv[:, d:].astype(BF16)


def _mem_kv(mem, g, w_ckv_bf16):
    b, m, d = mem.shape
    return pl.pallas_call(
        _mem_kv_kernel,
        grid=(b,),
        in_specs=[pl.BlockSpec((None, m, d), lambda i: (i, 0, 0)),
                  pl.BlockSpec((1, d), lambda i: (0, 0)),
                  pl.BlockSpec((d, 2 * d), lambda i: (0, 0))],
        out_specs=[pl.BlockSpec((None, m, d), lambda i: (i, 0, 0))] * 2,
        out_shape=[jax.ShapeDtypeStruct((b, m, d), BF16)] * 2,
        compiler_params=_params("parallel"),
        name="mem_kv",
    )(mem, g, w_ckv_bf16)


def _mid_kernel(x_ref, yr_ref, yd_ref, wo_ref, gc_ref, wcq_ref, km_ref, vm_ref, wco_ref, gf_ref,
                wr_hi_ref, wr_lo_ref, br_ref, tri_ref,
                x2_ref, hn_ref, ri_ref, rg_ref, cnt_ref, carry_scr):
    tm, d = x_ref.shape
    first = jnp.logical_and(pl.program_id(0) == 0, pl.program_id(1) == 0)

    @pl.when(first)
    def _():
        carry_scr[...] = jnp.zeros_like(carry_scr)

    rc = tri_ref.shape[0]
    chains = range(tm // rc)
    rows = [slice(c * rc, (c + 1) * rc) for c in chains]
    half = yr_ref.shape[-1]
    x1 = [x_ref[r, :] + _dot(yr_ref[r, :], wo_ref[:half, :]) + _dot(yd_ref[r, :], wo_ref[half:, :]) for r in rows]

    q = [_dot(_rms(x1[c], gc_ref[...]), wcq_ref[...]) for c in chains]
    hd = d // CROSS_HEADS
    outs = [[] for _ in chains]
    for h in range(CROSS_HEADS):
        sl = slice(h * hd, (h + 1) * hd)
        s = [_dot_nt(q[c][:, sl], km_ref[:, sl]) * (hd ** -0.5) for c in chains]
        p = [jnp.exp(s[c] - jnp.max(s[c], axis=-1, keepdims=True)) for c in chains]
        p = [p[c] / jnp.sum(p[c], axis=-1, keepdims=True) for c in chains]
        for c in chains:
            outs[c].append(_dot(p[c], vm_ref[:, sl]))
    x2 = [x1[c] + _dot(jnp.concatenate(outs[c], axis=1), wco_ref[...]) for c in chains]
    for c in chains:
        x2_ref[rows[c], :] = x2[c]

    hn = [_rms(x2[c], gf_ref[...]) for c in chains]
    _store_token_tiles(hn_ref, jnp.concatenate(hn, axis=0))
    hi = [hn[c].astype(BF16) for c in chains]
    lo = [(hn[c] - hi[c].astype(F32)).astype(BF16) for c in chains]
    vals = [(jnp.dot(hi[c], wr_hi_ref[...], preferred_element_type=F32)
             + jnp.dot(lo[c], wr_hi_ref[...], preferred_element_type=F32)
             + jnp.dot(hi[c], wr_lo_ref[...], preferred_element_type=F32) + br_ref[...]) for c in chains]
    lane = lax.broadcasted_iota(I32, (rc, LANES), 1)
    tops, idxs, hots = [[] for _ in chains], [[] for _ in chains], [[] for _ in chains]
    for _ in range(TOP_K):
        for c in chains:
            mx = jnp.max(vals[c], axis=-1, keepdims=True)
            idx = jnp.min(jnp.where(vals[c] == mx, lane, LANES), axis=-1, keepdims=True)
            hot = lane == idx
            vals[c] = jnp.where(hot, -jnp.inf, vals[c])
            tops[c].append(mx)
            idxs[c].append(idx)
            hots[c].append(hot)
    for c in chains:
        es = [jnp.exp(t - tops[c][0]) for t in tops[c]]
        denom = es[0] + es[1] + es[2] + es[3]
        sel = jnp.zeros((rc, LANES), F32)
        for hot in hots[c]:
            sel = sel + hot.astype(F32)
        before = jnp.dot(tri_ref[...], sel.astype(BF16), preferred_element_type=F32) + carry_scr[...]
        carry_scr[...] = carry_scr[...] + jnp.sum(sel, axis=0, keepdims=True)
        ri = jnp.zeros((rc, LANES), I32)
        rg = jnp.zeros((rc, LANES), F32)
        for j in range(TOP_K):
            rank = jnp.sum(jnp.where(hots[c][j], before, 0.0), axis=-1, keepdims=True).astype(I32)
            ri = jnp.where(lane == j, idxs[c][j], ri)
            ri = jnp.where(lane == TOP_K + j, rank, ri)
            rg = jnp.where(lane == j, es[j] / denom, rg)
        ri_ref[rows[c], :] = ri
        rg_ref[rows[c], :] = rg
    cnt_ref[...] = jnp.broadcast_to(carry_scr[...], cnt_ref.shape)


def _mid_stage(x, y_rwkv, y_diff, w_out, g_cross, w_cq, k_mem, v_mem, w_co, g_ffn, w_router, b_router):
    b, s, d = x.shape
    tm = min(MID_TM, s)
    m = k_mem.shape[1]
    half = y_rwkv.shape[-1]
    e = w_router.shape[1]
    wr = jnp.zeros((d, LANES), F32).at[:, :e].set(w_router)
    wr_hi = wr.astype(BF16)
    wr_lo = (wr - wr_hi.astype(F32)).astype(BF16)
    br = jnp.full((1, LANES), NEG_BIG, F32).at[0, :e].set(b_router)
    rc = min(MID_CHAIN_ROWS, tm)
    t = jnp.arange(rc, dtype=I32)
    tri = (t[None, :] < t[:, None]).astype(BF16)
    tile = lambda w: pl.BlockSpec((None, tm, w), lambda i, j: (i, j, 0))
    const = lambda shape: pl.BlockSpec(shape, lambda i, j: (0,) * len(shape))
    return pl.pallas_call(
        _mid_kernel,
        grid=(b, s // tm),
        in_specs=[tile(d), tile(half), tile(half), const((d, d)), const((1, d)), const((d, d)),
                  pl.BlockSpec((None, m, d), lambda i, j: (i, 0, 0)),
                  pl.BlockSpec((None, m, d), lambda i, j: (i, 0, 0)),
                  const((d, d)), const((1, d)), const((d, LANES)), const((d, LANES)),
                  const((1, LANES)), const((rc, rc))],
        out_specs=[tile(d), pl.BlockSpec((tm * TILE_ROWS, LANES), lambda i, j: (i * (s // tm) + j, 0)),
                   tile(LANES), tile(LANES), const((8, LANES))],
        out_shape=[jax.ShapeDtypeStruct((b, s, d), F32), jax.ShapeDtypeStruct((b * s * TILE_ROWS, LANES), F32),
                   jax.ShapeDtypeStruct((b, s, LANES), I32), jax.ShapeDtypeStruct((b, s, LANES), F32),
                   jax.ShapeDtypeStruct((8, LANES), F32)],
        scratch_shapes=[pltpu.VMEM((1, LANES), F32)],
        compiler_params=_params("arbitrary", "arbitrary"),
        name="outproj_cross_router",
    )(x, y_rwkv, y_diff, w_out.astype(BF16), g_cross, w_cq.astype(BF16), k_mem, v_mem,
      w_co.astype(BF16), g_ffn, wr_hi, wr_lo, br, tri)


def _expert_kernel(be_ref, nblk_ref, nused_ref, x_ref, w1_hbm, b1_ref, w2_hbm, b2_ref, o_ref,
                   w1_stage, w2_stage, w1_scr, w2_scr, w2i_scr, sem):
    i = pl.program_id(0)
    n_used = nused_ref[0]
    used = i < n_used
    expert = be_ref[i]
    new_expert = jnp.logical_or(i == 0, expert != be_ref[jnp.maximum(i - 1, 0)])
    d, f = w1_stage.shape[0], w2_stage.shape[0]

    def weight_copies(ex):
        return (pltpu.make_async_copy(w1_hbm.at[ex], w1_stage, sem.at[0]),
                pltpu.make_async_copy(w2_hbm.at[ex], w2_stage, sem.at[1]))

    @pl.when(jnp.logical_and(used, i == 0))
    def _():
        for cp in weight_copies(expert):
            cp.start()

    @pl.when(jnp.logical_and(used, new_expert))
    def _():
        for cp in weight_copies(expert):
            cp.wait()
        step = 128

        def cast_rows(c, carry):
            rows = pl.ds(pl.multiple_of(c * step, step), step)
            w1_scr[rows, :] = w1_stage[rows, :].astype(BF16)
            return carry

        lax.fori_loop(0, d // step, cast_rows, 0)
        for cb in range(w2_stage.shape[1] // LANES):
            cols = slice(cb * LANES, (cb + 1) * LANES)
            w2i_scr[cb, pl.ds(0, f // 2, stride=2), :] = w2_stage[:f // 2, cols]
            w2i_scr[cb, pl.ds(1, f // 2, stride=2), :] = w2_stage[f // 2:, cols]
            w2_scr[:, cols] = w2i_scr[cb].astype(BF16)
        next_run = i + nblk_ref[expert]

        @pl.when(next_run < n_used)
        def _():
            for cp in weight_copies(be_ref[jnp.minimum(next_run, be_ref.shape[0] - 1)]):
                cp.start()

    @pl.when(used)
    def _():
        x = _load_token_tiles(x_ref, MOE_ROWS).astype(BF16)
        even = lax.broadcasted_iota(I32, (MOE_ROWS, LANES), 1) % 2 == 0
        cw = 2 * LANES

        def hidden_pair(j):
            lo, hi = slice(j * cw, (j + 1) * cw), slice(f + j * cw, f + (j + 1) * cw)
            return (jnp.dot(x, w1_scr[:, lo], preferred_element_type=F32) + b1_ref[:, lo],
                    jnp.dot(x, w1_scr[:, hi], preferred_element_type=F32) + b1_ref[:, hi])

        def act_block(h):
            lin = jnp.clip(pltpu.roll(h, LANES - 1, axis=1), -SWIGLU_LIMIT, SWIGLU_LIMIT)
            glu = jnp.minimum(h, SWIGLU_LIMIT)
            return jnp.where(even, glu * jax.nn.sigmoid(SWIGLU_ALPHA * glu) * (lin + 1.0), 0.0)

        y = b2_ref[...]
        pending = hidden_pair(0)
        for j in range(f // cw):
            ha, hb = pending
            if (j + 1) * cw < f:
                pending = hidden_pair(j + 1)
            packed = jnp.concatenate(
                [act_block(ha[:, c * LANES:(c + 1) * LANES])
                 + pltpu.roll(act_block(hb[:, c * LANES:(c + 1) * LANES]), 1, axis=1) for c in range(cw // LANES)],
                axis=1)
            y = y + jnp.dot(packed.astype(BF16), w2_scr[j * cw:(j + 1) * cw, :], preferred_element_type=F32)
        _store_token_tiles(o_ref, y)

    @pl.when(jnp.logical_not(used))
    def _():
        o_ref[...] = jnp.zeros_like(o_ref)


def _expert_ffn(x_sorted, block_e, blocks_per_expert, n_used, w1, b1, w2, b2):
    e, d, f2 = w1.shape
    r = MOE_ROWS
    n_rows = x_sorted.shape[0] // TILE_ROWS
    grid_spec = pltpu.PrefetchScalarGridSpec(
        num_scalar_prefetch=3,
        grid=(n_rows // r,),
        in_specs=[pl.BlockSpec((r * TILE_ROWS, LANES), lambda i, be, nb, nu: (i, 0)),
                  pl.BlockSpec(memory_space=pl.ANY),
                  pl.BlockSpec((None, 1, f2), lambda i, be, nb, nu: (be[i], 0, 0)),
                  pl.BlockSpec(memory_space=pl.ANY),
                  pl.BlockSpec((None, 1, d), lambda i, be, nb, nu: (be[i], 0, 0))],
        out_specs=pl.BlockSpec((r * TILE_ROWS, LANES), lambda i, be, nb, nu: (i, 0)),
        scratch_shapes=[pltpu.VMEM((d, f2), F32), pltpu.VMEM((f2 // 2, d), F32),
                        pltpu.VMEM((d, f2), BF16), pltpu.VMEM((f2 // 2, d), BF16),
                        pltpu.VMEM((d // LANES, f2 // 2, LANES), F32), pltpu.SemaphoreType.DMA((2,))],
    )
    return pl.pallas_call(
        _expert_kernel,
        grid_spec=grid_spec,
        out_shape=jax.ShapeDtypeStruct((n_rows * TILE_ROWS, LANES), F32),
        compiler_params=_params("arbitrary"),
        name="expert_ffn",
    )(block_e, blocks_per_expert, n_used, x_sorted, w1, b1.reshape(e, 1, f2), w2, b2.reshape(e, 1, d))


def _tile_at(ref, first_row):
    return ref.at[pl.ds(pl.multiple_of(first_row, TILE_ROWS), TILE_ROWS)]


def _dispatch_kernel(pad_lo_ref, pad_hi_ref, dest_ref, hn_ref, xs_hbm, zero_scr, sem, pad_sem, *, tm):
    @pl.when(pl.program_id(0) == 0)
    def _():
        zero_scr[...] = jnp.zeros_like(zero_scr)

        def pad_copy(row):
            return pltpu.make_async_copy(zero_scr, _tile_at(xs_hbm, row * TILE_ROWS), pad_sem)

        def per_expert(e, carry):
            lo, hi = pad_lo_ref[e], pad_hi_ref[e]
            lax.fori_loop(lo, hi, lambda r, c: (pad_copy(r).start(), c)[1], 0)
            lax.fori_loop(lo, hi, lambda r, c: (pad_copy(r).wait(), c)[1], 0)
            return carry

        lax.fori_loop(0, pad_lo_ref.shape[0], per_expert, 0)

    def issue(t, carry):
        src = _tile_at(hn_ref, t * TILE_ROWS)
        for j in range(TOP_K):
            pltpu.make_async_copy(src, _tile_at(xs_hbm, dest_ref[0, t * TOP_K + j]), sem).start(priority=j % 2)
        return carry

    lax.fori_loop(0, tm, issue, 0, unroll=8)
    for j in range(TOP_K):
        pltpu.make_async_copy(hn_ref, xs_hbm.at[pl.ds(0, tm * TILE_ROWS)], sem).wait()


def _dispatch(hn_tiles, dest_rows, pad_lo, pad_hi, n_rows):
    t = hn_tiles.shape[0] // TILE_ROWS
    tm = min(DISPATCH_TM, t)
    steps = t // tm
    grid_spec = pltpu.PrefetchScalarGridSpec(
        num_scalar_prefetch=2,
        grid=(steps,),
        in_specs=[pl.BlockSpec((None, 1, tm * TOP_K), lambda i, lo, hi: (i, 0, 0), memory_space=pltpu.SMEM),
                  pl.BlockSpec((tm * TILE_ROWS, LANES), lambda i, lo, hi: (i, 0))],
        out_specs=pl.BlockSpec(memory_space=pl.ANY),
        scratch_shapes=[pltpu.VMEM((TILE_ROWS, LANES), F32), pltpu.SemaphoreType.DMA(()),
                        pltpu.SemaphoreType.DMA(())],
    )
    return pl.pallas_call(
        functools.partial(_dispatch_kernel, tm=tm),
        grid_spec=grid_spec,
        out_shape=jax.ShapeDtypeStruct((n_rows * TILE_ROWS, LANES), F32),
        compiler_params=_params("arbitrary"),
        name="moe_dispatch",
    )(pad_lo, pad_hi, dest_rows.reshape(steps, 1, tm * TOP_K), hn_tiles)


def _combine_kernel(dest_ref, x_ref, gate_ref, g_ref, ys_hbm, o_ref, buf, sem, *, tm, final):
    parts = COMBINE_PARTS
    pt = tm // parts

    for q in range(parts):
        def issue(t, carry, q=q):
            for j in range(TOP_K):
                pltpu.make_async_copy(_tile_at(ys_hbm, dest_ref[0, t * TOP_K + j]),
                                      _tile_at(buf.at[j], t * TILE_ROWS), sem.at[q]).start(priority=j % 2)
            return carry

        lax.fori_loop(q * pt, (q + 1) * pt, issue, 0, unroll=8)

    for q in range(parts):
        part = buf.at[:, pl.ds(q * pt * TILE_ROWS, pt * TILE_ROWS)]
        pltpu.make_async_copy(part, part, sem.at[q]).wait()
        rows = slice(q * pt, (q + 1) * pt)
        y = x_ref[rows, :]
        gate = gate_ref[rows, :]
        for j in range(TOP_K):
            y = y + gate[:, j:j + 1] * _load_token_tiles(buf, pt, lead=(j,), first=q * pt)
        o_ref[rows, :] = _rms(y, g_ref[...]) if final else y


def _combine(x2d, gate2d, dest_rows, y_sorted, g, final):
    t, d = x2d.shape
    tm = min(COMBINE_TM, t)
    steps = t // tm
    return pl.pallas_call(
        functools.partial(_combine_kernel, tm=tm, final=final),
        grid=(steps,),
        in_specs=[pl.BlockSpec((None, 1, tm * TOP_K), lambda i: (i, 0, 0), memory_space=pltpu.SMEM),
                  pl.BlockSpec((tm, d), lambda i: (i, 0)), pl.BlockSpec((tm, LANES), lambda i: (i, 0)),
                  pl.BlockSpec((1, d), lambda i: (0, 0)), pl.BlockSpec(memory_space=pl.ANY)],
        out_specs=pl.BlockSpec((tm, d), lambda i: (i, 0)),
        out_shape=jax.ShapeDtypeStruct((t, d), F32),
        scratch_shapes=[pltpu.VMEM((TOP_K, tm * TILE_ROWS, LANES), F32), pltpu.SemaphoreType.DMA((COMBINE_PARTS,))],
        compiler_params=_params("arbitrary"),
        name="combine_norm",
    )(dest_rows.reshape(steps, 1, tm * TOP_K), x2d, gate2d, g, y_sorted)


def _layer(x, mem, lyr, final, p):
    b, s, d = x.shape
    t = b * s
    c = RWKV_W
    rwkv_in = 3 * c + LORA_W

    w_in = p["w_in"][lyr]
    w_cat = jnp.concatenate([w_in[:, rwkv_in:], w_in[:, :3 * c], w_in[:, 3 * c:rwkv_in],
                             jnp.zeros((d, LORA_PAD - LORA_W), F32)], axis=1).astype(BF16)
    diff_cols = 3 * DIFF_W
    proj_diff, proj_rwkv = _norm_inproj(x.reshape(t, d), p["norm_mix_g"][lyr][None], w_cat, diff_cols)
    proj_diff, proj_rwkv = proj_diff.reshape(b, s, -1), proj_rwkv.reshape(b, s, -1)

    mu = p["shift_mu"][lyr]
    mu_l = jnp.zeros((1, LORA_PAD), F32).at[0, :LORA_W].set(mu[3 * c:])
    wl = jnp.zeros((LORA_PAD, 3 * c), F32)
    wl = wl.at[:DECAY_LORA, :c].set(p["w_decay_up"][lyr])
    wl = wl.at[DECAY_LORA:DECAY_LORA + ICLR_LORA, c:2 * c].set(p["w_iclr_up"][lyr])
    wl = wl.at[DECAY_LORA + ICLR_LORA:LORA_W, 2 * c:].set(p["w_gate_up"][lyr])
    vec = jnp.stack([p["w0"][lyr], p["a0"][lyr], p["k_k"][lyr], p["k_a"][lyr], p["r_k"][lyr].reshape(c),
                     p["lnx_g"][lyr], p["lnx_b"][lyr], jnp.zeros((c,), F32)])
    y_rwkv = _rwkv_group(proj_rwkv, mu[None, :3 * c], mu_l, wl.astype(BF16), vec,
                         rkv_block=0, lora_block=3 * c // LORA_PAD)

    lambda_init = 0.8 - 0.6 * math.exp(-0.3 * lyr)
    lam_vecs = jnp.stack([p["lambda_q1"][lyr], p["lambda_k1"][lyr], p["lambda_q2"][lyr], p["lambda_k2"][lyr]])
    y_diff = _diff_attention(proj_diff, lam_vecs, p["subln_g"][lyr][None], lambda_init)

    k_mem, v_mem = _mem_kv(mem, p["norm_mem_g"][lyr][None], p["w_ckv"][lyr].astype(BF16))
    x2, hn, route_i, route_g, counts = _mid_stage(
        x, y_rwkv, y_diff, p["w_out"][lyr], p["norm_cross_g"][lyr][None], p["w_cq"][lyr], k_mem, v_mem,
        p["w_co"][lyr], p["norm_ffn_g"][lyr][None], p["w_router"][lyr], p["b_router"][lyr])

    e = N_EXPERTS
    r = MOE_ROWS
    n_blocks = (t * TOP_K) // r + e
    route_i = route_i.reshape(t, LANES)
    idx, rank = route_i[:, :TOP_K], route_i[:, TOP_K:2 * TOP_K]
    cnt = counts[0, :e].astype(I32)
    padded = (cnt + r - 1) // r * r
    pad_end = jnp.cumsum(padded)
    pad_start = pad_end - padded
    dest = jnp.sum(jnp.where(idx[:, :, None] == jnp.arange(e, dtype=I32), pad_start, 0), axis=-1) + rank
    dest = dest.reshape(-1) * TILE_ROWS
    block_start = jnp.arange(n_blocks, dtype=I32) * r
    block_e = jnp.minimum(jnp.sum((block_start[:, None] >= pad_end[None, :]).astype(I32), axis=1), e - 1)
    n_used = pad_end[-1:] // r

    unused_lo = jnp.concatenate([pad_start + cnt, pad_end[-1:]])
    unused_hi = jnp.concatenate([pad_end, jnp.full((1,), n_blocks * r, I32)])
    x_sorted = _dispatch(hn, dest, unused_lo, unused_hi, n_blocks * r)
    y_sorted = _expert_ffn(x_sorted, block_e, padded // r, n_used, p["w1"][lyr], p["b1"][lyr], p["w2"][lyr],
                           p["b2"][lyr])
    return _combine(x2.reshape(t, d), route_g.reshape(t, LANES), dest, y_sorted,
                    p["norm_final_g"][None], final).reshape(b, s, d)


def kernel(x, mem, norm_mix_g, w_in, shift_mu, w0, w_decay_up, a0, w_iclr_up, w_gate_up, k_k, k_a, r_k,
           lnx_g, lnx_b, lambda_q1, lambda_k1, lambda_q2, lambda_k2, subln_g, w_out, norm_cross_g,
           norm_mem_g, w_cq, w_ckv, w_co, norm_ffn_g, w_router, b_router, w1, b1, w2, b2, norm_final_g):
    p = dict(norm_mix_g=norm_mix_g, w_in=w_in, shift_mu=shift_mu, w0=w0, w_decay_up=w_decay_up, a0=a0,
             w_iclr_up=w_iclr_up, w_gate_up=w_gate_up, k_k=k_k, k_a=k_a, r_k=r_k, lnx_g=lnx_g, lnx_b=lnx_b,
             lambda_q1=lambda_q1, lambda_k1=lambda_k1, lambda_q2=lambda_q2, lambda_k2=lambda_k2,
             subln_g=subln_g, w_out=w_out, norm_cross_g=norm_cross_g, norm_mem_g=norm_mem_g, w_cq=w_cq,
             w_ckv=w_ckv, w_co=w_co, norm_ffn_g=norm_ffn_g, w_router=w_router, b_router=b_router,
             w1=w1, b1=b1, w2=w2, b2=b2, norm_final_g=norm_final_g)
    depth = w_in.shape[0]
    for lyr in range(depth):
        x = _layer(x, mem, lyr, lyr == depth - 1, p)
    return x
```

```python
import functools
import math

import jax
import jax.numpy as jnp
from jax import lax
from jax.experimental import pallas as pl
from jax.experimental.pallas import tpu as pltpu

F32 = jnp.float32
BF16 = jnp.bfloat16
I32 = jnp.int32

NORM_EPS = 1e-5
HEAD = 64
RWKV_W = 512
RWKV_HEADS = RWKV_W // HEAD
DECAY_LORA, ICLR_LORA, GATE_LORA = 64, 64, 160
LORA_W = DECAY_LORA + ICLR_LORA + GATE_LORA
LORA_PAD = 384
RWKV_GN_EPS = 64e-5
DIFF_W = 512
DIFF_HEADS = DIFF_W // (2 * HEAD)
CROSS_HEADS = 4
N_EXPERTS = 32
TOP_K = 4
SWIGLU_LIMIT = 7.0
SWIGLU_ALPHA = 1.702
LANES = 128
TILE_ROWS = 8
NEG_BIG = -1e30

RWKV_CHUNK = 64
RWKV_BATCH = 4
PROJ_TM = 512
ATT_BLOCK = 512
ATT_ROW_CHUNK = 128
ATT_KEY_GROUP = 4
MID_TM = 512
MID_CHAIN_ROWS = 256
MOE_ROWS = 256
DISPATCH_TM = 1024
COMBINE_TM = 256
COMBINE_PARTS = 4
VMEM_LIMIT = 56 * 1024 * 1024


def _dot(a, b):
    return jnp.dot(a.astype(BF16), b.astype(BF16), preferred_element_type=F32)


def _dot_nt(a, b):
    return lax.dot_general(a.astype(BF16), b.astype(BF16), (((1,), (1,)), ((), ())),
                           preferred_element_type=F32)


def _dot_tn(a, b):
    return lax.dot_general(a.astype(BF16), b.astype(BF16), (((0,), (0,)), ((), ())),
                           preferred_element_type=F32)


def _split2(x):
    hi = x.astype(BF16)
    lo = (x - hi.astype(F32)).astype(BF16)
    return hi, lo


def _dot_exact_rhs(x, ones_bf16):
    hi, lo = _split2(x)
    return jnp.dot(hi, ones_bf16, preferred_element_type=F32) + jnp.dot(lo, ones_bf16, preferred_element_type=F32)


def _dot_exact_lhs(ones_bf16, x):
    hi, lo = _split2(x)
    return jnp.dot(ones_bf16, hi, preferred_element_type=F32) + jnp.dot(ones_bf16, lo, preferred_element_type=F32)


def _store_token_tiles(ref, x, lead=()):
    rows = x.shape[0]
    for c in range(TILE_ROWS):
        ref[lead + (pl.ds(c, rows, stride=TILE_ROWS), slice(None))] = x[:, c * LANES:(c + 1) * LANES]


def _load_token_tiles(ref, rows, lead=(), first=0):
    return jnp.concatenate([ref[lead + (pl.ds(first * TILE_ROWS + c, rows, stride=TILE_ROWS), slice(None))]
                            for c in range(TILE_ROWS)], axis=1)


def _rms(x, g):
    return x * lax.rsqrt(jnp.mean(x * x, axis=-1, keepdims=True) + NORM_EPS) * g


def _params(*sem):
    return pltpu.CompilerParams(dimension_semantics=sem, vmem_limit_bytes=VMEM_LIMIT)


def _norm_inproj_kernel(x_ref, g_ref, w_ref, od_ref, or_ref):
    h = _rms(x_ref[...], g_ref[...]).astype(BF16)
    nd = od_ref.shape[1]
    od_ref[...] = jnp.dot(h, w_ref[:, :nd], preferred_element_type=F32).astype(BF16)
    or_ref[...] = jnp.dot(h, w_ref[:, nd:], preferred_element_type=F32)


def _norm_inproj(x2d, g, w_bf16, n_diff):
    t, d = x2d.shape
    n = w_bf16.shape[1]
    tm = min(PROJ_TM, t)
    return pl.pallas_call(
        _norm_inproj_kernel,
        grid=(t // tm,),
        in_specs=[pl.BlockSpec((tm, d), lambda i: (i, 0)),
                  pl.BlockSpec((1, d), lambda i: (0, 0)),
                  pl.BlockSpec((d, n), lambda i: (0, 0))],
        out_specs=[pl.BlockSpec((tm, n_diff), lambda i: (i, 0)), pl.BlockSpec((tm, n - n_diff), lambda i: (i, 0))],
        out_shape=[jax.ShapeDtypeStruct((t, n_diff), BF16), jax.ShapeDtypeStruct((t, n - n_diff), F32)],
        compiler_params=_params("parallel"),
        name="norm_inproj",
    )(x2d, g, w_bf16)


def _rwkv_kernel(rkv_ref, lora_ref, mu_rkv_ref, mu_l_ref, wl_ref, vec_ref, bd_ref, tri_ref,
                 o_ref, st_ref, prev_rkv_ref, prev_l_ref):
    L = RWKV_CHUNK
    C = RWKV_W

    @pl.when(pl.program_id(1) == 0)
    def _():
        st_ref[...] = jnp.zeros_like(st_ref)
        prev_rkv_ref[...] = jnp.zeros_like(prev_rkv_ref)
        prev_l_ref[...] = jnp.zeros_like(prev_l_ref)

    row = lax.broadcasted_iota(I32, (L, 1), 0)
    w0, a0, k_k, k_a = vec_ref[0:1, :], vec_ref[1:2, :], vec_ref[2:3, :], vec_ref[3:4, :]
    r_k, lnx_g, lnx_b = vec_ref[4:5, :], vec_ref[5:6, :], vec_ref[6:7, :]
    bd = bd_ref[...]
    tri = tri_ref[...]
    lane_l = lax.broadcasted_iota(I32, (L, LORA_PAD), 1)
    nb = rkv_ref.shape[0]

    def token_shift(raw, prev_ref, bi, mu):
        prev = jnp.where(row == 0, prev_ref[bi], pltpu.roll(raw, 1, axis=0))
        prev_ref[bi] = raw[L - 1:L, :]
        return raw + mu * (prev - raw)

    def elementwise(bi):
        u = token_shift(rkv_ref[bi], prev_rkv_ref, bi, mu_rkv_ref[...])
        ul = token_shift(lora_ref[bi], prev_l_ref, bi, mu_l_ref[...])
        r, k, v = u[:, :C], u[:, C:2 * C], u[:, 2 * C:]
        act = jnp.where(lane_l < DECAY_LORA, jnp.tanh(ul),
                        jnp.where(lane_l < DECAY_LORA + ICLR_LORA, ul, jax.nn.sigmoid(ul)))
        lo = _dot(act, wl_ref[...])
        w_log = -jax.nn.softplus(-(w0 + lo[:, :C])) - 0.5
        lw = -jnp.exp(w_log)
        a = jax.nn.sigmoid(a0 + lo[:, C:2 * C])
        kk = k * k_k
        kk = kk / jnp.maximum(jnp.sqrt(_dot(kk * kk, bd)), 1e-12)
        k2 = k * (1.0 + (a - 1.0) * k_a)
        cum = _dot_exact_lhs(tri, lw)
        total = cum[L - 1:L, :]
        e_neg = jnp.exp(-cum)
        e_rem = jnp.exp(total - cum)
        kka = kk * a
        return dict(r=r, v=v, k2=k2, g=lo[:, 2 * C:], a_t=-kk * jnp.exp(cum - lw), b_t=kka * e_neg,
                    k_t=k2 * e_neg, r_t=r * jnp.exp(cum), b_bar=kka * e_rem, k_bar=k2 * e_rem,
                    p_total=jnp.exp(total))

    ew = [elementwise(bi) for bi in range(nb)]

    m0 = lax.broadcasted_iota(I32, (L, LANES), 1) < HEAD
    r2i = lax.broadcasted_iota(I32, (2 * L, 4 * L), 0)
    c2i = lax.broadcasted_iota(I32, (2 * L, 4 * L), 1) % (2 * L)
    strict, incl = c2i < r2i, c2i <= r2i
    npairs = RWKV_HEADS // 2
    chains = [(bi, p) for bi in range(nb) for p in range(npairs)]
    nc = range(len(chains))

    def stack2(name):
        out = []
        for bi, p in chains:
            xp = ew[bi][name][:, p * LANES:(p + 1) * LANES]
            out.append(jnp.concatenate([jnp.where(m0, xp, 0.0), jnp.where(m0, 0.0, xp)], axis=0))
        return out

    A2, B2, K2, R2 = stack2("a_t"), stack2("b_t"), stack2("k_t"), stack2("r_t")
    V2, Bb2, Kb2 = stack2("v"), stack2("b_bar"), stack2("k_bar")
    mg = [_dot_nt(jnp.concatenate([A2[c], R2[c]], axis=0), jnp.concatenate([B2[c], K2[c]], axis=0)) for c in nc]
    m_cat = [jnp.where(strict, mg[c][:2 * L], 0.0) for c in nc]
    g_cat = [jnp.where(incl, mg[c][2 * L:], 0.0) for c in nc]
    m_ab = [m_cat[c][:, :LANES] for c in nc]
    pw = [_dot(m_ab[c], m_ab[c]) for c in nc]
    tr = list(m_ab)
    k = 2
    while 2 * k < L:
        x = [_dot(pw[c], jnp.concatenate([pw[c], tr[c]], axis=1)) for c in nc]
        tr = [tr[c] + pw[c] + x[c][:, LANES:] for c in nc]
        pw = [x[c][:, :LANES] for c in nc]
        k *= 2
    tr = [tr[c] + pw[c] + _dot(pw[c], tr[c]) for c in nc]
    az = [jnp.concatenate([A2[c], _dot(m_cat[c][:, LANES:], V2[c])], axis=1) for c in nc]
    tz = [az[c] + _dot(tr[c], az[c]) for c in nc]
    st = [st_ref[bi, p] for bi, p in chains]
    u2 = [_dot_nt(tz[c][:, :LANES], st[c]) + tz[c][:, LANES:] for c in nc]
    uv = [jnp.concatenate([u2[c], V2[c]], axis=0) for c in nc]
    y2 = [_dot_nt(R2[c], st[c]) + _dot(g_cat[c], uv[c]) for c in nc]
    for c, (bi, p) in enumerate(chains):
        st_ref[bi, p] = (st[c] * ew[bi]["p_total"][:, p * LANES:(p + 1) * LANES]
                         + _dot_tn(uv[c], jnp.concatenate([Bb2[c], Kb2[c]], axis=0)))

    inv_n = 1.0 / HEAD
    for bi in range(nb):
        y = jnp.concatenate([y2[bi * npairs + p][:L] + y2[bi * npairs + p][L:] for p in range(npairs)], axis=1)
        e = ew[bi]
        mu = _dot_exact_rhs(y, bd) * inv_n
        d = y - mu
        var = _dot_exact_rhs(d * d, bd) * inv_n
        yn = d * lax.rsqrt(var + RWKV_GN_EPS) * lnx_g + lnx_b
        bonus = _dot_exact_rhs(e["r"] * e["k2"] * r_k, bd) * e["v"]
        o_ref[bi] = (yn + bonus) * e["g"]


def _rwkv_group(proj3, mu_rkv, mu_l, wl, vec, rkv_block, lora_block):
    b, s, _ = proj3.shape
    L, C = RWKV_CHUNK, RWKV_W
    head = jnp.arange(C, dtype=I32) // HEAD
    bd = (head[:, None] == head[None, :]).astype(BF16)
    t = jnp.arange(L, dtype=I32)
    tri = (t[None, :] <= t[:, None]).astype(BF16)
    const = lambda shape: pl.BlockSpec(shape, lambda i, j: (0,) * len(shape))
    nb = RWKV_BATCH if b % RWKV_BATCH == 0 else 1
    return pl.pallas_call(
        _rwkv_kernel,
        grid=(b // nb, s // L),
        in_specs=[pl.BlockSpec((nb, L, 3 * C), lambda i, j: (i, j, rkv_block)),
                  pl.BlockSpec((nb, L, LORA_PAD), lambda i, j: (i, j, lora_block)),
                  const((1, 3 * C)), const((1, LORA_PAD)), const((LORA_PAD, 3 * C)),
                  const((8, C)), const((C, C)), const((L, L))],
        out_specs=pl.BlockSpec((nb, L, C), lambda i, j: (i, j, 0)),
        out_shape=jax.ShapeDtypeStruct((b, s, C), F32),
        scratch_shapes=[pltpu.VMEM((nb, RWKV_HEADS // 2, LANES, LANES), F32),
                        pltpu.VMEM((nb, 1, 3 * C), F32), pltpu.VMEM((nb, 1, LORA_PAD), F32)],
        compiler_params=_params("parallel", "arbitrary"),
        name="rwkv_group",
    )(proj3, proj3, mu_rkv, mu_l, wl, vec, bd, tri)


def _diff_attn_kernel(qi_ref, kj_ref, diag_ref, q_ref, k_ref, v_ref, slope_ref, lam_ref, g_ref, o_ref,
                      q2_scr, m_scr, acc_scr, *, lambda_init):
    blk = ATT_BLOCK
    group = k_ref.shape[0] // blk
    step_id = pl.program_id(2)
    qi, kj, diag = qi_ref[step_id], kj_ref[step_id], diag_ref[step_id]
    log2e = math.log2(math.e)

    @pl.when(kj == 0)
    def _():
        q = q_ref[...].astype(F32) * (HEAD ** -0.5 * log2e)
        m0 = lax.broadcasted_iota(I32, (blk, LANES), 1) < HEAD
        q2_scr[0:blk, :] = jnp.where(m0, q, 0.0).astype(BF16)
        q2_scr[blk:, :] = jnp.where(m0, 0.0, q).astype(BF16)
        m_scr[...] = jnp.full_like(m_scr, NEG_BIG)
        acc_scr[...] = jnp.zeros_like(acc_scr)

    def step(kh, on_diagonal):
        k = k_ref[kh * blk:(kh + 1) * blk, :].astype(BF16)
        v_ext = jnp.concatenate([v_ref[kh * blk:(kh + 1) * blk, :].astype(BF16), jnp.ones((blk, LANES), BF16)],
                                axis=1)
        koff = lax.broadcasted_iota(I32, (1, blk), 1)
        col_bias = (slope_ref[:, :1] * log2e) * ((kj * group + kh - qi) * blk + koff).astype(F32)
        rc = ATT_ROW_CHUNK
        n_chunks = 2 * blk // rc

        def scores(c):
            return lax.dot_general(q2_scr[c * rc:(c + 1) * rc, :], k, (((1,), (1,)), ((), ())),
                                   preferred_element_type=F32)

        s_next = scores(0)
        for c in range(n_chunks):
            rows = slice(c * rc, (c + 1) * rc)
            s = s_next + col_bias
            if c + 1 < n_chunks:
                s_next = scores(c + 1)
            if on_diagonal:
                qoff = (c * rc) % blk + lax.broadcasted_iota(I32, (rc, 1), 0)
                s = jnp.where(koff <= qoff, s, NEG_BIG)
            m_prev = m_scr[rows, :]
            m_new = jnp.maximum(m_prev, jnp.max(s, axis=-1, keepdims=True))
            alpha = jnp.exp2(m_prev - m_new)
            p = jnp.exp2(s - jnp.concatenate([m_new] * (blk // LANES), axis=1))
            acc_scr[rows, :] = (jnp.concatenate([alpha, alpha], axis=1) * acc_scr[rows, :]
                                + jnp.dot(p.astype(BF16), v_ext, preferred_element_type=F32))
            m_scr[rows, :] = m_new

    @pl.when(diag < 0)
    def _():
        for kh in range(group):
            step(kh, False)

    def last_step(n_full):
        for kh in range(n_full):
            step(kh, False)
        step(n_full, True)
        lam_v = lam_ref[...]
        lam = (jnp.exp(jnp.sum(lam_v[0:1] * lam_v[1:2], axis=-1, keepdims=True))
               - jnp.exp(jnp.sum(lam_v[2:3] * lam_v[3:4], axis=-1, keepdims=True)) + lambda_init)
        o2 = acc_scr[:, :LANES] / acc_scr[:, LANES:]
        o = o2[:blk] - lam * o2[blk:]
        o_ref[...] = _rms(o, g_ref[...]) * (1.0 - lambda_init)

    for n_full in range(group):
        pl.when(diag == n_full)(functools.partial(last_step, n_full))


def _diff_attention(proj3, lam_vecs, subln_g, lambda_init):
    b, s, _ = proj3.shape
    blk = ATT_BLOCK
    nb = s // blk
    group = math.gcd(ATT_KEY_GROUP, nb)
    nh = DIFF_HEADS
    slopes = jnp.exp2(-8.0 * jnp.arange(1, nh + 1, dtype=F32) / nh)
    slopes = jnp.broadcast_to(slopes[:, None, None], (nh, 1, LANES))
    steps = [(qi, kj, qi % group if kj == qi // group else -1) for qi in range(nb) for kj in range(qi // group + 1)]
    tabs = [jnp.asarray([st[i] for st in steps], I32) for i in range(3)]
    kernel = functools.partial(_diff_attn_kernel, lambda_init=lambda_init)
    grid_spec = pltpu.PrefetchScalarGridSpec(
        num_scalar_prefetch=3,
        grid=(b, nh, len(steps)),
        in_specs=[pl.BlockSpec((None, blk, LANES), lambda bi, h, t, qt, kt, dt: (bi, qt[t], h)),
                  pl.BlockSpec((None, group * blk, LANES), lambda bi, h, t, qt, kt, dt: (bi, kt[t], nh + h)),
                  pl.BlockSpec((None, group * blk, LANES), lambda bi, h, t, qt, kt, dt: (bi, kt[t], 2 * nh + h)),
                  pl.BlockSpec((None, 1, LANES), lambda bi, h, t, qt, kt, dt: (h, 0, 0)),
                  pl.BlockSpec((4, HEAD), lambda bi, h, t, qt, kt, dt: (0, 0)),
                  pl.BlockSpec((1, 2 * HEAD), lambda bi, h, t, qt, kt, dt: (0, 0))],
        out_specs=pl.BlockSpec((None, blk, LANES), lambda bi, h, t, qt, kt, dt: (bi, qt[t], h)),
        scratch_shapes=[pltpu.VMEM((2 * blk, LANES), BF16), pltpu.VMEM((2 * blk, LANES), F32),
                        pltpu.VMEM((2 * blk, 2 * LANES), F32)],
    )
    return pl.pallas_call(
        kernel,
        grid_spec=grid_spec,
        out_shape=jax.ShapeDtypeStruct((b, s, DIFF_W), F32),
        compiler_params=_params("parallel", "parallel", "arbitrary"),
        name="diff_attention",
    )(*tabs, proj3, proj3, proj3, slopes, lam_vecs, subln_g)


def _mem_kv_kernel(m_ref, g_ref, w_ref, k_ref, v_ref):
    d = m_ref.shape[-1]
    kv = _dot(_rms(m_ref[...], g_ref[...]), w_ref[...])
    k_ref[...] = kv[:, :d].astype(BF16)
    v_ref[...] = kv[:, d:].astype(BF16)


def _mem_kv(mem, g, w_ckv_bf16):
    b, m, d = mem.shape
    return pl.pallas_call(
        _mem_kv_kernel,
        grid=(b,),
        in_specs=[pl.BlockSpec((None, m, d), lambda i: (i, 0, 0)),
                  pl.BlockSpec((1, d), lambda i: (0, 0)),
                  pl.BlockSpec((d, 2 * d), lambda i: (0, 0))],
        out_specs=[pl.BlockSpec((None, m, d), lambda i: (i, 0, 0))] * 2,
        out_shape=[jax.ShapeDtypeStruct((b, m, d), BF16)] * 2,
        compiler_params=_params("parallel"),
        name="mem_kv",
    )(mem, g, w_ckv_bf16)


def _mid_kernel(x_ref, yr_ref, yd_ref, wo_ref, gc_ref, wcq_ref, km_ref, vm_ref, wco_ref, gf_ref,
                wr_hi_ref, wr_lo_ref, br_ref, tri_ref,
                x2_ref, hn_ref, ri_ref, rg_ref, cnt_ref, carry_scr):
    tm, d = x_ref.shape
    first = jnp.logical_and(pl.program_id(0) == 0, pl.program_id(1) == 0)

    @pl.when(first)
    def _():
        carry_scr[...] = jnp.zeros_like(carry_scr)

    rc = tri_ref.shape[0]
    chains = range(tm // rc)
    rows = [slice(c * rc, (c + 1) * rc) for c in chains]
    half = yr_ref.shape[-1]
    x1 = [x_ref[r, :] + _dot(yr_ref[r, :], wo_ref[:half, :]) + _dot(yd_ref[r, :], wo_ref[half:, :]) for r in rows]

    q = [_dot(_rms(x1[c], gc_ref[...]), wcq_ref[...]) for c in chains]
    hd = d // CROSS_HEADS
    outs = [[] for _ in chains]
    for h in range(CROSS_HEADS):
        sl = slice(h * hd, (h + 1) * hd)
        s = [_dot_nt(q[c][:, sl], km_ref[:, sl]) * (hd ** -0.5) for c in chains]
        p = [jnp.exp(s[c] - jnp.max(s[c], axis=-1, keepdims=True)) for c in chains]
        p = [p[c] / jnp.sum(p[c], axis=-1, keepdims=True) for c in chains]
        for c in chains:
            outs[c].append(_dot(p[c], vm_ref[:, sl]))
    x2 = [x1[c] + _dot(jnp.concatenate(outs[c], axis=1), wco_ref[...]) for c in chains]
    for c in chains:
        x2_ref[rows[c], :] = x2[c]

    hn = [_rms(x2[c], gf_ref[...]) for c in chains]
    _store_token_tiles(hn_ref, jnp.concatenate(hn, axis=0))
    hi = [hn[c].astype(BF16) for c in chains]
    lo = [(hn[c] - hi[c].astype(F32)).astype(BF16) for c in chains]
    vals = [(jnp.dot(hi[c], wr_hi_ref[...], preferred_element_type=F32)
             + jnp.dot(lo[c], wr_hi_ref[...], preferred_element_type=F32)
             + jnp.dot(hi[c], wr_lo_ref[...], preferred_element_type=F32) + br_ref[...]) for c in chains]
    lane = lax.broadcasted_iota(I32, (rc, LANES), 1)
    tops, idxs, hots = [[] for _ in chains], [[] for _ in chains], [[] for _ in chains]
    for _ in range(TOP_K):
        for c in chains:
            mx = jnp.max(vals[c], axis=-1, keepdims=True)
            idx = jnp.min(jnp.where(vals[c] == mx, lane, LANES), axis=-1, keepdims=True)
            hot = lane == idx
            vals[c] = jnp.where(hot, -jnp.inf, vals[c])
            tops[c].append(mx)
            idxs[c].append(idx)
            hots[c].append(hot)
    for c in chains:
        es = [jnp.exp(t - tops[c][0]) for t in tops[c]]
        denom = es[0] + es[1] + es[2] + es[3]
        sel = jnp.zeros((rc, LANES), F32)
        for hot in hots[c]:
            sel = sel + hot.astype(F32)
        before = jnp.dot(tri_ref[...], sel.astype(BF16), preferred_element_type=F32) + carry_scr[...]
        carry_scr[...] = carry_scr[...] + jnp.sum(sel, axis=0, keepdims=True)
        ri = jnp.zeros((rc, LANES), I32)
        rg = jnp.zeros((rc, LANES), F32)
        for j in range(TOP_K):
            rank = jnp.sum(jnp.where(hots[c][j], before, 0.0), axis=-1, keepdims=True).astype(I32)
            ri = jnp.where(lane == j, idxs[c][j], ri)
            ri = jnp.where(lane == TOP_K + j, rank, ri)
            rg = jnp.where(lane == j, es[j] / denom, rg)
        ri_ref[rows[c], :] = ri
        rg_ref[rows[c], :] = rg
    cnt_ref[...] = jnp.broadcast_to(carry_scr[...], cnt_ref.shape)


def _mid_stage(x, y_rwkv, y_diff, w_out, g_cross, w_cq, k_mem, v_mem, w_co, g_ffn, w_router, b_router):
    b, s, d = x.shape
    tm = min(MID_TM, s)
    m = k_mem.shape[1]
    half = y_rwkv.shape[-1]
    e = w_router.shape[1]
    wr = jnp.zeros((d, LANES), F32).at[:, :e].set(w_router)
    wr_hi = wr.astype(BF16)
    wr_lo = (wr - wr_hi.astype(F32)).astype(BF16)
    br = jnp.full((1, LANES), NEG_BIG, F32).at[0, :e].set(b_router)
    rc = min(MID_CHAIN_ROWS, tm)
    t = jnp.arange(rc, dtype=I32)
    tri = (t[None, :] < t[:, None]).astype(BF16)
    tile = lambda w: pl.BlockSpec((None, tm, w), lambda i, j: (i, j, 0))
    const = lambda shape: pl.BlockSpec(shape, lambda i, j: (0,) * len(shape))
    return pl.pallas_call(
        _mid_kernel,
        grid=(b, s // tm),
        in_specs=[tile(d), tile(half), tile(half), const((d, d)), const((1, d)), const((d, d)),
                  pl.BlockSpec((None, m, d), lambda i, j: (i, 0, 0)),
                  pl.BlockSpec((None, m, d), lambda i, j: (i, 0, 0)),
                  const((d, d)), const((1, d)), const((d, LANES)), const((d, LANES)),
                  const((1, LANES)), const((rc, rc))],
        out_specs=[tile(d), pl.BlockSpec((tm * TILE_ROWS, LANES), lambda i, j: (i * (s // tm) + j, 0)),
                   tile(LANES), tile(LANES), const((8, LANES))],
        out_shape=[jax.ShapeDtypeStruct((b, s, d), F32), jax.ShapeDtypeStruct((b * s * TILE_ROWS, LANES), F32),
                   jax.ShapeDtypeStruct((b, s, LANES), I32), jax.ShapeDtypeStruct((b, s, LANES), F32),
                   jax.ShapeDtypeStruct((8, LANES), F32)],
        scratch_shapes=[pltpu.VMEM((1, LANES), F32)],
        compiler_params=_params("arbitrary", "arbitrary"),
        name="outproj_cross_router",
    )(x, y_rwkv, y_diff, w_out.astype(BF16), g_cross, w_cq.astype(BF16), k_mem, v_mem,
      w_co.astype(BF16), g_ffn, wr_hi, wr_lo, br, tri)


def _expert_kernel(be_ref, nblk_ref, nused_ref, x_ref, w1_hbm, b1_ref, w2_hbm, b2_ref, o_ref,
                   w1_stage, w2_stage, w1_scr, w2_scr, w2i_scr, sem):
    i = pl.program_id(0)
    n_used = nused_ref[0]
    used = i < n_used
    expert = be_ref[i]
    new_expert = jnp.logical_or(i == 0, expert != be_ref[jnp.maximum(i - 1, 0)])
    d, f = w1_stage.shape[0], w2_stage.shape[0]

    def weight_copies(ex):
        return (pltpu.make_async_copy(w1_hbm.at[ex], w1_stage, sem.at[0]),
                pltpu.make_async_copy(w2_hbm.at[ex], w2_stage, sem.at[1]))

    @pl.when(jnp.logical_and(used, i == 0))
    def _():
        for cp in weight_copies(expert):
            cp.start()

    @pl.when(jnp.logical_and(used, new_expert))
    def _():
        for cp in weight_copies(expert):
            cp.wait()
        step = 128

        def cast_rows(c, carry):
            rows = pl.ds(pl.multiple_of(c * step, step), step)
            w1_scr[rows, :] = w1_stage[rows, :].astype(BF16)
            return carry

        lax.fori_loop(0, d // step, cast_rows, 0)
        for cb in range(w2_stage.shape[1] // LANES):
            cols = slice(cb * LANES, (cb + 1) * LANES)
            w2i_scr[cb, pl.ds(0, f // 2, stride=2), :] = w2_stage[:f // 2, cols]
            w2i_scr[cb, pl.ds(1, f // 2, stride=2), :] = w2_stage[f // 2:, cols]
            w2_scr[:, cols] = w2i_scr[cb].astype(BF16)
        next_run = i + nblk_ref[expert]

        @pl.when(next_run < n_used)
        def _():
            for cp in weight_copies(be_ref[jnp.minimum(next_run, be_ref.shape[0] - 1)]):
                cp.start()

    @pl.when(used)
    def _():
        x = _load_token_tiles(x_ref, MOE_ROWS).astype(BF16)
        even = lax.broadcasted_iota(I32, (MOE_ROWS, LANES), 1) % 2 == 0
        cw = 2 * LANES

        def hidden_pair(j):
            lo, hi = slice(j * cw, (j + 1) * cw), slice(f + j * cw, f + (j + 1) * cw)
            return (jnp.dot(x, w1_scr[:, lo], preferred_element_type=F32) + b1_ref[:, lo],
                    jnp.dot(x, w1_scr[:, hi], preferred_element_type=F32) + b1_ref[:, hi])

        def act_block(h):
            lin = jnp.clip(pltpu.roll(h, LANES - 1, axis=1), -SWIGLU_LIMIT, SWIGLU_LIMIT)
            glu = jnp.minimum(h, SWIGLU_LIMIT)
            return jnp.where(even, glu * jax.nn.sigmoid(SWIGLU_ALPHA * glu) * (lin + 1.0), 0.0)

        y = b2_ref[...]
        pending = hidden_pair(0)
        for j in range(f // cw):
            ha, hb = pending
            if (j + 1) * cw < f:
                pending = hidden_pair(j + 1)
            packed = jnp.concatenate(
                [act_block(ha[:, c * LANES:(c + 1) * LANES])
                 + pltpu.roll(act_block(hb[:, c * LANES:(c + 1) * LANES]), 1, axis=1) for c in range(cw // LANES)],
                axis=1)
            y = y + jnp.dot(packed.astype(BF16), w2_scr[j * cw:(j + 1) * cw, :], preferred_element_type=F32)
        _store_token_tiles(o_ref, y)

    @pl.when(jnp.logical_not(used))
    def _():
        o_ref[...] = jnp.zeros_like(o_ref)


def _expert_ffn(x_sorted, block_e, blocks_per_expert, n_used, w1, b1, w2, b2):
    e, d, f2 = w1.shape
    r = MOE_ROWS
    n_rows = x_sorted.shape[0] // TILE_ROWS
    grid_spec = pltpu.PrefetchScalarGridSpec(
        num_scalar_prefetch=3,
        grid=(n_rows // r,),
        in_specs=[pl.BlockSpec((r * TILE_ROWS, LANES), lambda i, be, nb, nu: (i, 0)),
                  pl.BlockSpec(memory_space=pl.ANY),
                  pl.BlockSpec((None, 1, f2), lambda i, be, nb, nu: (be[i], 0, 0)),
                  pl.BlockSpec(memory_space=pl.ANY),
                  pl.BlockSpec((None, 1, d), lambda i, be, nb, nu: (be[i], 0, 0))],
        out_specs=pl.BlockSpec((r * TILE_ROWS, LANES), lambda i, be, nb, nu: (i, 0)),
        scratch_shapes=[pltpu.VMEM((d, f2), F32), pltpu.VMEM((f2 // 2, d), F32),
                        pltpu.VMEM((d, f2), BF16), pltpu.VMEM((f2 // 2, d), BF16),
                        pltpu.VMEM((d // LANES, f2 // 2, LANES), F32), pltpu.SemaphoreType.DMA((2,))],
    )
    return pl.pallas_call(
        _expert_kernel,
        grid_spec=grid_spec,
        out_shape=jax.ShapeDtypeStruct((n_rows * TILE_ROWS, LANES), F32),
        compiler_params=_params("arbitrary"),
        name="expert_ffn",
    )(block_e, blocks_per_expert, n_used, x_sorted, w1, b1.reshape(e, 1, f2), w2, b2.reshape(e, 1, d))


def _tile_at(ref, first_row):
    return ref.at[pl.ds(pl.multiple_of(first_row, TILE_ROWS), TILE_ROWS)]


def _dispatch_kernel(pad_lo_ref, pad_hi_ref, dest_ref, hn_ref, xs_hbm, zero_scr, sem, pad_sem, *, tm):
    @pl.when(pl.program_id(0) == 0)
    def _():
        zero_scr[...] = jnp.zeros_like(zero_scr)

        def pad_copy(row):
            return pltpu.make_async_copy(zero_scr, _tile_at(xs_hbm, row * TILE_ROWS), pad_sem)

        def per_expert(e, carry):
            lo, hi = pad_lo_ref[e], pad_hi_ref[e]
            lax.fori_loop(lo, hi, lambda r, c: (pad_copy(r).start(), c)[1], 0)
            lax.fori_loop(lo, hi, lambda r, c: (pad_copy(r).wait(), c)[1], 0)
            return carry

        lax.fori_loop(0, pad_lo_ref.shape[0], per_expert, 0)

    def issue(t, carry):
        src = _tile_at(hn_ref, t * TILE_ROWS)
        for j in range(TOP_K):
            pltpu.make_async_copy(src, _tile_at(xs_hbm, dest_ref[0, t * TOP_K + j]), sem).start(priority=j % 2)
        return carry

    lax.fori_loop(0, tm, issue, 0, unroll=8)
    for j in range(TOP_K):
        pltpu.make_async_copy(hn_ref, xs_hbm.at[pl.ds(0, tm * TILE_ROWS)], sem).wait()


def _dispatch(hn_tiles, dest_rows, pad_lo, pad_hi, n_rows):
    t = hn_tiles.shape[0] // TILE_ROWS
    tm = min(DISPATCH_TM, t)
    steps = t // tm
    grid_spec = pltpu.PrefetchScalarGridSpec(
        num_scalar_prefetch=2,
        grid=(steps,),
        in_specs=[pl.BlockSpec((None, 1, tm * TOP_K), lambda i, lo, hi: (i, 0, 0), memory_space=pltpu.SMEM),
                  pl.BlockSpec((tm * TILE_ROWS, LANES), lambda i, lo, hi: (i, 0))],
        out_specs=pl.BlockSpec(memory_space=pl.ANY),
        scratch_shapes=[pltpu.VMEM((TILE_ROWS, LANES), F32), pltpu.SemaphoreType.DMA(()),
                        pltpu.SemaphoreType.DMA(())],
    )
    return pl.pallas_call(
        functools.partial(_dispatch_kernel, tm=tm),
        grid_spec=grid_spec,
        out_shape=jax.ShapeDtypeStruct((n_rows * TILE_ROWS, LANES), F32),
        compiler_params=_params("arbitrary"),
        name="moe_dispatch",
    )(pad_lo, pad_hi, dest_rows.reshape(steps, 1, tm * TOP_K), hn_tiles)


def _combine_kernel(dest_ref, x_ref, gate_ref, g_ref, ys_hbm, o_ref, buf, sem, *, tm, final):
    parts = COMBINE_PARTS
    pt = tm // parts

    for q in range(parts):
        def issue(t, carry, q=q):
            for j in range(TOP_K):
                pltpu.make_async_copy(_tile_at(ys_hbm, dest_ref[0, t * TOP_K + j]),
                                      _tile_at(buf.at[j], t * TILE_ROWS), sem.at[q]).start(priority=j % 2)
            return carry

        lax.fori_loop(q * pt, (q + 1) * pt, issue, 0, unroll=8)

    for q in range(parts):
        part = buf.at[:, pl.ds(q * pt * TILE_ROWS, pt * TILE_ROWS)]
        pltpu.make_async_copy(part, part, sem.at[q]).wait()
        rows = slice(q * pt, (q + 1) * pt)
        y = x_ref[rows, :]
        gate = gate_ref[rows, :]
        for j in range(TOP_K):
            y = y + gate[:, j:j + 1] * _load_token_tiles(buf, pt, lead=(j,), first=q * pt)
        o_ref[rows, :] = _rms(y, g_ref[...]) if final else y


def _combine(x2d, gate2d, dest_rows, y_sorted, g, final):
    t, d = x2d.shape
    tm = min(COMBINE_TM, t)
    steps = t // tm
    return pl.pallas_call(
        functools.partial(_combine_kernel, tm=tm, final=final),
        grid=(steps,),
        in_specs=[pl.BlockSpec((None, 1, tm * TOP_K), lambda i: (i, 0, 0), memory_space=pltpu.SMEM),
                  pl.BlockSpec((tm, d), lambda i: (i, 0)), pl.BlockSpec((tm, LANES), lambda i: (i, 0)),
                  pl.BlockSpec((1, d), lambda i: (0, 0)), pl.BlockSpec(memory_space=pl.ANY)],
        out_specs=pl.BlockSpec((tm, d), lambda i: (i, 0)),
        out_shape=jax.ShapeDtypeStruct((t, d), F32),
        scratch_shapes=[pltpu.VMEM((TOP_K, tm * TILE_ROWS, LANES), F32), pltpu.SemaphoreType.DMA((COMBINE_PARTS,))],
        compiler_params=_params("arbitrary"),
        name="combine_norm",
    )(dest_rows.reshape(steps, 1, tm * TOP_K), x2d, gate2d, g, y_sorted)


def _layer(x, mem, lyr, final, p):
    b, s, d = x.shape
    t = b * s
    c = RWKV_W
    rwkv_in = 3 * c + LORA_W

    w_in = p["w_in"][lyr]
    w_cat = jnp.concatenate([w_in[:, rwkv_in:], w_in[:, :3 * c], w_in[:, 3 * c:rwkv_in],
                             jnp.zeros((d, LORA_PAD - LORA_W), F32)], axis=1).astype(BF16)
    diff_cols = 3 * DIFF_W
    proj_diff, proj_rwkv = _norm_inproj(x.reshape(t, d), p["norm_mix_g"][lyr][None], w_cat, diff_cols)
    proj_diff, proj_rwkv = proj_diff.reshape(b, s, -1), proj_rwkv.reshape(b, s, -1)

    mu = p["shift_mu"][lyr]
    mu_l = jnp.zeros((1, LORA_PAD), F32).at[0, :LORA_W].set(mu[3 * c:])
    wl = jnp.zeros((LORA_PAD, 3 * c), F32)
    wl = wl.at[:DECAY_LORA, :c].set(p["w_decay_up"][lyr])
    wl = wl.at[DECAY_LORA:DECAY_LORA + ICLR_LORA, c:2 * c].set(p["w_iclr_up"][lyr])
    wl = wl.at[DECAY_LORA + ICLR_LORA:LORA_W, 2 * c:].set(p["w_gate_up"][lyr])
    vec = jnp.stack([p["w0"][lyr], p["a0"][lyr], p["k_k"][lyr], p["k_a"][lyr], p["r_k"][lyr].reshape(c),
                     p["lnx_g"][lyr], p["lnx_b"][lyr], jnp.zeros((c,), F32)])
    y_rwkv = _rwkv_group(proj_rwkv, mu[None, :3 * c], mu_l, wl.astype(BF16), vec,
                         rkv_block=0, lora_block=3 * c // LORA_PAD)

    lambda_init = 0.8 - 0.6 * math.exp(-0.3 * lyr)
    lam_vecs = jnp.stack([p["lambda_q1"][lyr], p["lambda_k1"][lyr], p["lambda_q2"][lyr], p["lambda_k2"][lyr]])
    y_diff = _diff_attention(proj_diff, lam_vecs, p["subln_g"][lyr][None], lambda_init)

    k_mem, v_mem = _mem_kv(mem, p["norm_mem_g"][lyr][None], p["w_ckv"][lyr].astype(BF16))
    x2, hn, route_i, route_g, counts = _mid_stage(
        x, y_rwkv, y_diff, p["w_out"][lyr], p["norm_cross_g"][lyr][None], p["w_cq"][lyr], k_mem, v_mem,
        p["w_co"][lyr], p["norm_ffn_g"][lyr][None], p["w_router"][lyr], p["b_router"][lyr])

    e = N_EXPERTS
    r = MOE_ROWS
    n_blocks = (t * TOP_K) // r + e
    route_i = route_i.reshape(t, LANES)
    idx, rank = route_i[:, :TOP_K], route_i[:, TOP_K:2 * TOP_K]
    cnt = counts[0, :e].astype(I32)
    padded = (cnt + r - 1) // r * r
    pad_end = jnp.cumsum(padded)
    pad_start = pad_end - padded
    dest = jnp.sum(jnp.where(idx[:, :, None] == jnp.arange(e, dtype=I32), pad_start, 0), axis=-1) + rank
    dest = dest.reshape(-1) * TILE_ROWS
    block_start = jnp.arange(n_blocks, dtype=I32) * r
    block_e = jnp.minimum(jnp.sum((block_start[:, None] >= pad_end[None, :]).astype(I32), axis=1), e - 1)
    n_used = pad_end[-1:] // r

    unused_lo = jnp.concatenate([pad_start + cnt, pad_end[-1:]])
    unused_hi = jnp.concatenate([pad_end, jnp.full((1,), n_blocks * r, I32)])
    x_sorted = _dispatch(hn, dest, unused_lo, unused_hi, n_blocks * r)
    y_sorted = _expert_ffn(x_sorted, block_e, padded // r, n_used, p["w1"][lyr], p["b1"][lyr], p["w2"][lyr],
                           p["b2"][lyr])
    return _combine(x2.reshape(t, d), route_g.reshape(t, LANES), dest, y_sorted,
                    p["norm_final_g"][None], final).reshape(b, s, d)


def kernel(x, mem, norm_mix_g, w_in, shift_mu, w0, w_decay_up, a0, w_iclr_up, w_gate_up, k_k, k_a, r_k,
           lnx_g, lnx_b, lambda_q1, lambda_k1, lambda_q2, lambda_k2, subln_g, w_out, norm_cross_g,
           norm_mem_g, w_cq, w_ckv, w_co, norm_ffn_g, w_router, b_router, w1, b1, w2, b2, norm_final_g):
    p = dict(norm_mix_g=norm_mix_g, w_in=w_in, shift_mu=shift_mu, w0=w0, w_decay_up=w_decay_up, a0=a0,
             w_iclr_up=w_iclr_up, w_gate_up=w_gate_up, k_k=k_k, k_a=k_a, r_k=r_k, lnx_g=lnx_g, lnx_b=lnx_b,
             lambda_q1=lambda_q1, lambda_k1=lambda_k1, lambda_q2=lambda_q2, lambda_k2=lambda_k2,
             subln_g=subln_g, w_out=w_out, norm_cross_g=norm_cross_g, norm_mem_g=norm_mem_g, w_cq=w_cq,
             w_ckv=w_ckv, w_co=w_co, norm_ffn_g=norm_ffn_g, w_router=w_router, b_router=b_router,
             w1=w1, b1=b1, w2=w2, b2=b2, norm_final_g=norm_final_g)
    depth = w_in.shape[0]
    for lyr in range(depth):
        x = _layer(x, mem, lyr, lyr == depth - 1, p)
    return x
```

```python
import functools
import math

import jax
import jax.numpy as jnp
from jax import lax
from jax.experimental import pallas as pl
from jax.experimental.pallas import tpu as pltpu

F32 = jnp.float32
BF16 = jnp.bfloat16
I32 = jnp.int32

NORM_EPS = 1e-5
HEAD = 64
RWKV_W = 512
RWKV_HEADS = RWKV_W // HEAD
DECAY_LORA, ICLR_LORA, GATE_LORA = 64, 64, 160
LORA_W = DECAY_LORA + ICLR_LORA + GATE_LORA
LORA_PAD = 384
RWKV_GN_EPS = 64e-5
DIFF_W = 512
DIFF_HEADS = DIFF_W // (2 * HEAD)
CROSS_HEADS = 4
N_EXPERTS = 32
TOP_K = 4
SWIGLU_LIMIT = 7.0
SWIGLU_ALPHA = 1.702
LANES = 128
TILE_ROWS = 8
NEG_BIG = -1e30

RWKV_CHUNK = 64
RWKV_BATCH = 4
PROJ_TM = 512
ATT_BLOCK = 512
ATT_ROW_CHUNK = 128
ATT_KEY_GROUP = 4
MID_TM = 512
MID_CHAIN_ROWS = 256
MOE_ROWS = 256
DISPATCH_TM = 1024
COMBINE_TM = 256
COMBINE_PARTS = 4
SEG_CHUNK = 8
VMEM_LIMIT = 56 * 1024 * 1024


def _dot(a, b):
    return jnp.dot(a.astype(BF16), b.astype(BF16), preferred_element_type=F32)


def _dot_nt(a, b):
    return lax.dot_general(a.astype(BF16), b.astype(BF16), (((1,), (1,)), ((), ())),
                           preferred_element_type=F32)


def _dot_tn(a, b):
    return lax.dot_general(a.astype(BF16), b.astype(BF16), (((0,), (0,)), ((), ())),
                           preferred_element_type=F32)


def _split2(x):
    hi = x.astype(BF16)
    lo = (x - hi.astype(F32)).astype(BF16)
    return hi, lo


def _dot_exact_rhs(x, ones_bf16):
    hi, lo = _split2(x)
    return jnp.dot(hi, ones_bf16, preferred_element_type=F32) + jnp.dot(lo, ones_bf16, preferred_element_type=F32)


def _dot_exact_lhs(ones_bf16, x):
    hi, lo = _split2(x)
    return jnp.dot(ones_bf16, hi, preferred_element_type=F32) + jnp.dot(ones_bf16, lo, preferred_element_type=F32)


def _store_token_tiles(ref, x, lead=()):
    rows = x.shape[0]
    for c in range(TILE_ROWS):
        ref[lead + (pl.ds(c, rows, stride=TILE_ROWS), slice(None))] = x[:, c * LANES:(c + 1) * LANES]


def _load_token_tiles(ref, rows, lead=(), first=0):
    return jnp.concatenate([ref[lead + (pl.ds(first * TILE_ROWS + c, rows, stride=TILE_ROWS), slice(None))]
                            for c in range(TILE_ROWS)], axis=1)


def _rms(x, g):
    return x * lax.rsqrt(jnp.mean(x * x, axis=-1, keepdims=True) + NORM_EPS) * g


def _params(*sem):
    return pltpu.CompilerParams(dimension_semantics=sem, vmem_limit_bytes=VMEM_LIMIT)


def _norm_inproj_kernel(x_ref, g_ref, w_ref, od_ref, or_ref):
    h = _rms(x_ref[...], g_ref[...]).astype(BF16)
    nd = od_ref.shape[1]
    od_ref[...] = jnp.dot(h, w_ref[:, :nd], preferred_element_type=F32).astype(BF16)
    or_ref[...] = jnp.dot(h, w_ref[:, nd:], preferred_element_type=F32)


def _norm_inproj(x2d, g, w_bf16, n_diff):
    t, d = x2d.shape
    n = w_bf16.shape[1]
    tm = min(PROJ_TM, t)
    return pl.pallas_call(
        _norm_inproj_kernel,
        grid=(t // tm,),
        in_specs=[pl.BlockSpec((tm, d), lambda i: (i, 0)),
                  pl.BlockSpec((1, d), lambda i: (0, 0)),
                  pl.BlockSpec((d, n), lambda i: (0, 0))],
        out_specs=[pl.BlockSpec((tm, n_diff), lambda i: (i, 0)), pl.BlockSpec((tm, n - n_diff), lambda i: (i, 0))],
        out_shape=[jax.ShapeDtypeStruct((t, n_diff), BF16), jax.ShapeDtypeStruct((t, n - n_diff), F32)],
        compiler_params=_params("parallel"),
        name="norm_inproj",
    )(x2d, g, w_bf16)


def _rwkv_kernel(rkv_ref, lora_ref, mu_rkv_ref, mu_l_ref, wl_ref, vec_ref, bd_ref, tri_ref,
                 o_ref, st_ref, prev_rkv_ref, prev_l_ref):
    L = RWKV_CHUNK
    C = RWKV_W

    @pl.when(pl.program_id(1) == 0)
    def _():
        st_ref[...] = jnp.zeros_like(st_ref)
        prev_rkv_ref[...] = jnp.zeros_like(prev_rkv_ref)
        prev_l_ref[...] = jnp.zeros_like(prev_l_ref)

    row = lax.broadcasted_iota(I32, (L, 1), 0)
    w0, a0, k_k, k_a = vec_ref[0:1, :], vec_ref[1:2, :], vec_ref[2:3, :], vec_ref[3:4, :]
    r_k, lnx_g, lnx_b = vec_ref[4:5, :], vec_ref[5:6, :], vec_ref[6:7, :]
    bd = bd_ref[...]
    tri = tri_ref[...]
    lane_l = lax.broadcasted_iota(I32, (L, LORA_PAD), 1)
    nb = rkv_ref.shape[0]

    def token_shift(raw, prev_ref, bi, mu):
        prev = jnp.where(row == 0, prev_ref[bi], pltpu.roll(raw, 1, axis=0))
        prev_ref[bi] = raw[L - 1:L, :]
        return raw + mu * (prev - raw)

    def elementwise(bi):
        u = token_shift(rkv_ref[bi], prev_rkv_ref, bi, mu_rkv_ref[...])
        ul = token_shift(lora_ref[bi], prev_l_ref, bi, mu_l_ref[...])
        r, k, v = u[:, :C], u[:, C:2 * C], u[:, 2 * C:]
        act = jnp.where(lane_l < DECAY_LORA, jnp.tanh(ul),
                        jnp.where(lane_l < DECAY_LORA + ICLR_LORA, ul, jax.nn.sigmoid(ul)))
        lo = _dot(act, wl_ref[...])
        w_log = -jax.nn.softplus(-(w0 + lo[:, :C])) - 0.5
        lw = -jnp.exp(w_log)
        a = jax.nn.sigmoid(a0 + lo[:, C:2 * C])
        kk = k * k_k
        kk = kk / jnp.maximum(jnp.sqrt(_dot(kk * kk, bd)), 1e-12)
        k2 = k * (1.0 + (a - 1.0) * k_a)
        cum = _dot_exact_lhs(tri, lw)
        total = cum[L - 1:L, :]
        e_neg = jnp.exp(-cum)
        e_rem = jnp.exp(total - cum)
        kka = kk * a
        return dict(r=r, v=v, k2=k2, g=lo[:, 2 * C:], a_t=-kk * jnp.exp(cum - lw), b_t=kka * e_neg,
                    k_t=k2 * e_neg, r_t=r * jnp.exp(cum), b_bar=kka * e_rem, k_bar=k2 * e_rem,
                    p_total=jnp.exp(total))

    ew = [elementwise(bi) for bi in range(nb)]

    m0 = lax.broadcasted_iota(I32, (L, LANES), 1) < HEAD
    r2i = lax.broadcasted_iota(I32, (2 * L, 4 * L), 0)
    c2i = lax.broadcasted_iota(I32, (2 * L, 4 * L), 1) % (2 * L)
    strict, incl = c2i < r2i, c2i <= r2i
    npairs = RWKV_HEADS // 2
    chains = [(bi, p) for bi in range(nb) for p in range(npairs)]
    nc = range(len(chains))

    def stack2(name):
        out = []
        for bi, p in chains:
            xp = ew[bi][name][:, p * LANES:(p + 1) * LANES]
            out.append(jnp.concatenate([jnp.where(m0, xp, 0.0), jnp.where(m0, 0.0, xp)], axis=0))
        return out

    A2, B2, K2, R2 = stack2("a_t"), stack2("b_t"), stack2("k_t"), stack2("r_t")
    V2, Bb2, Kb2 = stack2("v"), stack2("b_bar"), stack2("k_bar")
    mg = [_dot_nt(jnp.concatenate([A2[c], R2[c]], axis=0), jnp.concatenate([B2[c], K2[c]], axis=0)) for c in nc]
    m_cat = [jnp.where(strict, mg[c][:2 * L], 0.0) for c in nc]
    g_cat = [jnp.where(incl, mg[c][2 * L:], 0.0) for c in nc]
    m_ab = [m_cat[c][:, :LANES] for c in nc]
    pw = [_dot(m_ab[c], m_ab[c]) for c in nc]
    tr = list(m_ab)
    k = 2
    while 2 * k < L:
        x = [_dot(pw[c], jnp.concatenate([pw[c], tr[c]], axis=1)) for c in nc]
        tr = [tr[c] + pw[c] + x[c][:, LANES:] for c in nc]
        pw = [x[c][:, :LANES] for c in nc]
        k *= 2
    tr = [tr[c] + pw[c] + _dot(pw[c], tr[c]) for c in nc]
    az = [jnp.concatenate([A2[c], _dot(m_cat[c][:, LANES:], V2[c])], axis=1) for c in nc]
    tz = [az[c] + _dot(tr[c], az[c]) for c in nc]
    st = [st_ref[bi, p] for bi, p in chains]
    u2 = [_dot_nt(tz[c][:, :LANES], st[c]) + tz[c][:, LANES:] for c in nc]
    uv = [jnp.concatenate([u2[c], V2[c]], axis=0) for c in nc]
    y2 = [_dot_nt(R2[c], st[c]) + _dot(g_cat[c], uv[c]) for c in nc]
    for c, (bi, p) in enumerate(chains):
        st_ref[bi, p] = (st[c] * ew[bi]["p_total"][:, p * LANES:(p + 1) * LANES]
                         + _dot_tn(uv[c], jnp.concatenate([Bb2[c], Kb2[c]], axis=0)))

    inv_n = 1.0 / HEAD
    for bi in range(nb):
        y = jnp.concatenate([y2[bi * npairs + p][:L] + y2[bi * npairs + p][L:] for p in range(npairs)], axis=1)
        e = ew[bi]
        mu = _dot_exact_rhs(y, bd) * inv_n
        d = y - mu
        var = _dot_exact_rhs(d * d, bd) * inv_n
        yn = d * lax.rsqrt(var + RWKV_GN_EPS) * lnx_g + lnx_b
        bonus = _dot_exact_rhs(e["r"] * e["k2"] * r_k, bd) * e["v"]
        o_ref[bi] = (yn + bonus) * e["g"]


def _rwkv_group(proj3, mu_rkv, mu_l, wl, vec, rkv_block, lora_block):
    b, s, _ = proj3.shape
    L, C = RWKV_CHUNK, RWKV_W
    head = jnp.arange(C, dtype=I32) // HEAD
    bd = (head[:, None] == head[None, :]).astype(BF16)
    t = jnp.arange(L, dtype=I32)
    tri = (t[None, :] <= t[:, None]).astype(BF16)
    const = lambda shape: pl.BlockSpec(shape, lambda i, j: (0,) * len(shape))
    nb = RWKV_BATCH if b % RWKV_BATCH == 0 else 1
    return pl.pallas_call(
        _rwkv_kernel,
        grid=(b // nb, s // L),
        in_specs=[pl.BlockSpec((nb, L, 3 * C), lambda i, j: (i, j, rkv_block)),
                  pl.BlockSpec((nb, L, LORA_PAD), lambda i, j: (i, j, lora_block)),
                  const((1, 3 * C)), const((1, LORA_PAD)), const((LORA_PAD, 3 * C)),
                  const((8, C)), const((C, C)), const((L, L))],
        out_specs=pl.BlockSpec((nb, L, C), lambda i, j: (i, j, 0)),
        out_shape=jax.ShapeDtypeStruct((b, s, C), F32),
        scratch_shapes=[pltpu.VMEM((nb, RWKV_HEADS // 2, LANES, LANES), F32),
                        pltpu.VMEM((nb, 1, 3 * C), F32), pltpu.VMEM((nb, 1, LORA_PAD), F32)],
        compiler_params=_params("parallel", "arbitrary"),
        name="rwkv_group",
    )(proj3, proj3, mu_rkv, mu_l, wl, vec, bd, tri)


def _diff_attn_kernel(qi_ref, kj_ref, diag_ref, q_ref, k_ref, v_ref, slope_ref, lam_ref, g_ref, o_ref,
                      q2_scr, m_scr, acc_scr, *, lambda_init):
    blk = ATT_BLOCK
    group = k_ref.shape[0] // blk
    step_id = pl.program_id(2)
    qi, kj, diag = qi_ref[step_id], kj_ref[step_id], diag_ref[step_id]
    log2e = math.log2(math.e)

    @pl.when(kj == 0)
    def _():
        q = q_ref[...].astype(F32) * (HEAD ** -0.5 * log2e)
        m0 = lax.broadcasted_iota(I32, (blk, LANES), 1) < HEAD
        q2_scr[0:blk, :] = jnp.where(m0, q, 0.0).astype(BF16)
        q2_scr[blk:, :] = jnp.where(m0, 0.0, q).astype(BF16)
        m_scr[...] = jnp.full_like(m_scr, NEG_BIG)
        acc_scr[...] = jnp.zeros_like(acc_scr)

    def step(kh, on_diagonal):
        k = k_ref[kh * blk:(kh + 1) * blk, :].astype(BF16)
        v_ext = jnp.concatenate([v_ref[kh * blk:(kh + 1) * blk, :].astype(BF16), jnp.ones((blk, LANES), BF16)],
                                axis=1)
        koff = lax.broadcasted_iota(I32, (1, blk), 1)
        col_bias = (slope_ref[:, :1] * log2e) * ((kj * group + kh - qi) * blk + koff).astype(F32)
        rc = ATT_ROW_CHUNK
        n_chunks = 2 * blk // rc

        def scores(c):
            return lax.dot_general(q2_scr[c * rc:(c + 1) * rc, :], k, (((1,), (1,)), ((), ())),
                                   preferred_element_type=F32)

        s_next = scores(0)
        for c in range(n_chunks):
            rows = slice(c * rc, (c + 1) * rc)
            s = s_next + col_bias
            if c + 1 < n_chunks:
                s_next = scores(c + 1)
            if on_diagonal:
                qoff = (c * rc) % blk + lax.broadcasted_iota(I32, (rc, 1), 0)
                s = jnp.where(koff <= qoff, s, NEG_BIG)
            m_prev = m_scr[rows, :]
            m_new = jnp.maximum(m_prev, jnp.max(s, axis=-1, keepdims=True))
            alpha = jnp.exp2(m_prev - m_new)
            p = jnp.exp2(s - jnp.concatenate([m_new] * (blk // LANES), axis=1))
            acc_scr[rows, :] = (jnp.concatenate([alpha, alpha], axis=1) * acc_scr[rows, :]
                                + jnp.dot(p.astype(BF16), v_ext, preferred_element_type=F32))
            m_scr[rows, :] = m_new

    @pl.when(diag < 0)
    def _():
        for kh in range(group):
            step(kh, False)

    def last_step(n_full):
        for kh in range(n_full):
            step(kh, False)
        step(n_full, True)
        lam_v = lam_ref[...]
        lam = (jnp.exp(jnp.sum(lam_v[0:1] * lam_v[1:2], axis=-1, keepdims=True))
               - jnp.exp(jnp.sum(lam_v[2:3] * lam_v[3:4], axis=-1, keepdims=True)) + lambda_init)
        o2 = acc_scr[:, :LANES] / acc_scr[:, LANES:]
        o = o2[:blk] - lam * o2[blk:]
        o_ref[...] = _rms(o, g_ref[...]) * (1.0 - lambda_init)

    for n_full in range(group):
        pl.when(diag == n_full)(functools.partial(last_step, n_full))


def _diff_attention(proj3, lam_vecs, subln_g, lambda_init):
    b, s, _ = proj3.shape
    blk = ATT_BLOCK
    nb = s // blk
    group = math.gcd(ATT_KEY_GROUP, nb)
    nh = DIFF_HEADS
    slopes = jnp.exp2(-8.0 * jnp.arange(1, nh + 1, dtype=F32) / nh)
    slopes = jnp.broadcast_to(slopes[:, None, None], (nh, 1, LANES))
    steps = [(qi, kj, qi % group if kj == qi // group else -1) for qi in range(nb) for kj in range(qi // group + 1)]
    tabs = [jnp.asarray([st[i] for st in steps], I32) for i in range(3)]
    kernel = functools.partial(_diff_attn_kernel, lambda_init=lambda_init)
    grid_spec = pltpu.PrefetchScalarGridSpec(
        num_scalar_prefetch=3,
        grid=(b, nh, len(steps)),
        in_specs=[pl.BlockSpec((None, blk, LANES), lambda bi, h, t, qt, kt, dt: (bi, qt[t], h)),
                  pl.BlockSpec((None, group * blk, LANES), lambda bi, h, t, qt, kt, dt: (bi, kt[t], nh + h)),
                  pl.BlockSpec((None, group * blk, LANES), lambda bi, h, t, qt, kt, dt: (bi, kt[t], 2 * nh + h)),
                  pl.BlockSpec((None, 1, LANES), lambda bi, h, t, qt, kt, dt: (h, 0, 0)),
                  pl.BlockSpec((4, HEAD), lambda bi, h, t, qt, kt, dt: (0, 0)),
                  pl.BlockSpec((1, 2 * HEAD), lambda bi, h, t, qt, kt, dt: (0, 0))],
        out_specs=pl.BlockSpec((None, blk, LANES), lambda bi, h, t, qt, kt, dt: (bi, qt[t], h)),
        scratch_shapes=[pltpu.VMEM((2 * blk, LANES), BF16), pltpu.VMEM((2 * blk, LANES), F32),
                        pltpu.VMEM((2 * blk, 2 * LANES), F32)],
    )
    return pl.pallas_call(
        kernel,
        grid_spec=grid_spec,
        out_shape=jax.ShapeDtypeStruct((b, s, DIFF_W), F32),
        compiler_params=_params("parallel", "parallel", "arbitrary"),
        name="diff_attention",
    )(*tabs, proj3, proj3, proj3, slopes, lam_vecs, subln_g)


def _mem_kv_kernel(m_ref, g_ref, w_ref, k_ref, v_ref):
    d = m_ref.shape[-1]
    kv = _dot(_rms(m_ref[...], g_ref[...]), w_ref[...])
    k_ref[...] = kv[:, :d].astype(BF16)
    v_ref[...] = kv[:, d:].astype(BF16)


def _mem_kv(mem, g, w_ckv_bf16):
    b, m, d = mem.shape
    return pl.pallas_call(
        _mem_kv_kernel,
        grid=(b,),
        in_specs=[pl.BlockSpec((None, m, d), lambda i: (i, 0, 0)),
                  pl.BlockSpec((1, d), lambda i: (0, 0)),
                  pl.BlockSpec((d, 2 * d), lambda i: (0, 0))],
        out_specs=[pl.BlockSpec((None, m, d), lambda i: (i, 0, 0))] * 2,
        out_shape=[jax.ShapeDtypeStruct((b, m, d), BF16)] * 2,
        compiler_params=_params("parallel"),
        name="mem_kv",
    )(mem, g, w_ckv_bf16)


def _mid_kernel(x_ref, yr_ref, yd_ref, wo_ref, gc_ref, wcq_ref, km_ref, vm_ref, wco_ref, gf_ref,
                wr_hi_ref, wr_lo_ref, br_ref, tri_ref,
                x2_ref, hn_ref, ri_ref, rg_ref, cnt_ref, carry_scr):
    tm, d = x_ref.shape
    first = jnp.logical_and(pl.program_id(0) == 0, pl.program_id(1) == 0)

    @pl.when(first)
    def _():
        carry_scr[...] = jnp.zeros_like(carry_scr)

    rc = tri_ref.shape[0]
    chains = range(tm // rc)
    rows = [slice(c * rc, (c + 1) * rc) for c in chains]
    half = yr_ref.shape[-1]
    x1 = [x_ref[r, :] + _dot(yr_ref[r, :], wo_ref[:half, :]) + _dot(yd_ref[r, :], wo_ref[half:, :]) for r in rows]

    q = [_dot(_rms(x1[c], gc_ref[...]), wcq_ref[...]) for c in chains]
    hd = d // CROSS_HEADS
    outs = [[] for _ in chains]

    def scores(h):
        sl = slice(h * hd, (h + 1) * hd)
        return [_dot_nt(q[c][:, sl], km_ref[:, sl]) * (hd ** -0.5) for c in chains]

    s_next = scores(0)
    for h in range(CROSS_HEADS):
        sl = slice(h * hd, (h + 1) * hd)
        s = s_next
        if h + 1 < CROSS_HEADS:
            s_next = scores(h + 1)
        p = [jnp.exp(s[c] - jnp.max(s[c], axis=-1, keepdims=True)) for c in chains]
        p = [p[c] / jnp.sum(p[c], axis=-1, keepdims=True) for c in chains]
        for c in chains:
            outs[c].append(_dot(p[c], vm_ref[:, sl]))
    x2 = [x1[c] + _dot(jnp.concatenate(outs[c], axis=1), wco_ref[...]) for c in chains]
    for c in chains:
        x2_ref[rows[c], :] = x2[c]

    hn = [_rms(x2[c], gf_ref[...]) for c in chains]
    _store_token_tiles(hn_ref, jnp.concatenate(hn, axis=0))
    hi = [hn[c].astype(BF16) for c in chains]
    lo = [(hn[c] - hi[c].astype(F32)).astype(BF16) for c in chains]
    vals = [(jnp.dot(hi[c], wr_hi_ref[...], preferred_element_type=F32)
             + jnp.dot(lo[c], wr_hi_ref[...], preferred_element_type=F32)
             + jnp.dot(hi[c], wr_lo_ref[...], preferred_element_type=F32) + br_ref[...]) for c in chains]
    lane = lax.broadcasted_iota(I32, (rc, LANES), 1)
    tops, idxs, hots = [[] for _ in chains], [[] for _ in chains], [[] for _ in chains]
    for _ in range(TOP_K):
        for c in chains:
            mx = jnp.max(vals[c], axis=-1, keepdims=True)
            idx = jnp.min(jnp.where(vals[c] == mx, lane, LANES), axis=-1, keepdims=True)
            hot = lane == idx
            vals[c] = jnp.where(hot, -jnp.inf, vals[c])
            tops[c].append(mx)
            idxs[c].append(idx)
            hots[c].append(hot)
    for c in chains:
        es = [jnp.exp(t - tops[c][0]) for t in tops[c]]
        denom = es[0] + es[1] + es[2] + es[3]
        sel = jnp.zeros((rc, LANES), F32)
        for hot in hots[c]:
            sel = sel + hot.astype(F32)
        before = jnp.dot(tri_ref[...], sel.astype(BF16), preferred_element_type=F32) + carry_scr[...]
        carry_scr[...] = carry_scr[...] + jnp.sum(sel, axis=0, keepdims=True)
        ri = jnp.zeros((rc, LANES), I32)
        rg = jnp.zeros((rc, LANES), F32)
        for j in range(TOP_K):
            rank = jnp.sum(jnp.where(hots[c][j], before, 0.0), axis=-1, keepdims=True).astype(I32)
            ri = jnp.where(lane == j, idxs[c][j], ri)
            ri = jnp.where(lane == TOP_K + j, rank, ri)
            rg = jnp.where(lane == j, es[j] / denom, rg)
        ri_ref[rows[c], :] = ri
        rg_ref[rows[c], :] = rg
    cnt_ref[...] = jnp.broadcast_to(carry_scr[...], cnt_ref.shape)


def _mid_stage(x, y_rwkv, y_diff, w_out, g_cross, w_cq, k_mem, v_mem, w_co, g_ffn, w_router, b_router):
    b, s, d = x.shape
    tm = min(MID_TM, s)
    m = k_mem.shape[1]
    half = y_rwkv.shape[-1]
    e = w_router.shape[1]
    wr = jnp.zeros((d, LANES), F32).at[:, :e].set(w_router)
    wr_hi = wr.astype(BF16)
    wr_lo = (wr - wr_hi.astype(F32)).astype(BF16)
    br = jnp.full((1, LANES), NEG_BIG, F32).at[0, :e].set(b_router)
    rc = min(MID_CHAIN_ROWS, tm)
    t = jnp.arange(rc, dtype=I32)
    tri = (t[None, :] < t[:, None]).astype(BF16)
    tile = lambda w: pl.BlockSpec((None, tm, w), lambda i, j: (i, j, 0))
    const = lambda shape: pl.BlockSpec(shape, lambda i, j: (0,) * len(shape))
    return pl.pallas_call(
        _mid_kernel,
        grid=(b, s // tm),
        in_specs=[tile(d), tile(half), tile(half), const((d, d)), const((1, d)), const((d, d)),
                  pl.BlockSpec((None, m, d), lambda i, j: (i, 0, 0)),
                  pl.BlockSpec((None, m, d), lambda i, j: (i, 0, 0)),
                  const((d, d)), const((1, d)), const((d, LANES)), const((d, LANES)),
                  const((1, LANES)), const((rc, rc))],
        out_specs=[tile(d), pl.BlockSpec((tm * TILE_ROWS, LANES), lambda i, j: (i * (s // tm) + j, 0)),
                   tile(LANES), tile(LANES), const((8, LANES))],
        out_shape=[jax.ShapeDtypeStruct((b, s, d), F32), jax.ShapeDtypeStruct((b * s * TILE_ROWS, LANES), F32),
                   jax.ShapeDtypeStruct((b, s, LANES), I32), jax.ShapeDtypeStruct((b, s, LANES), F32),
                   jax.ShapeDtypeStruct((8, LANES), F32)],
        scratch_shapes=[pltpu.VMEM((1, LANES), F32)],
        compiler_params=_params("arbitrary", "arbitrary"),
        name="outproj_cross_router",
    )(x, y_rwkv, y_diff, w_out.astype(BF16), g_cross, w_cq.astype(BF16), k_mem, v_mem,
      w_co.astype(BF16), g_ffn, wr_hi, wr_lo, br, tri)


def _expert_kernel(be_ref, nblk_ref, nused_ref, x_ref, w1_hbm, b1_ref, w2_hbm, b2_ref, o_ref,
                   w1_stage, w2_stage, w1_scr, w2_scr, w2i_scr, sem):
    i = pl.program_id(0)
    n_used = nused_ref[0]
    used = i < n_used
    expert = be_ref[i]
    new_expert = jnp.logical_or(i == 0, expert != be_ref[jnp.maximum(i - 1, 0)])
    d, f = w1_stage.shape[0], w2_stage.shape[0]

    def weight_copies(ex):
        return (pltpu.make_async_copy(w1_hbm.at[ex], w1_stage, sem.at[0]),
                pltpu.make_async_copy(w2_hbm.at[ex], w2_stage, sem.at[1]))

    @pl.when(jnp.logical_and(used, i == 0))
    def _():
        for cp in weight_copies(expert):
            cp.start()

    @pl.when(jnp.logical_and(used, new_expert))
    def _():
        for cp in weight_copies(expert):
            cp.wait()
        step = 128

        def cast_rows(c, carry):
            rows = pl.ds(pl.multiple_of(c * step, step), step)
            w1_scr[rows, :] = w1_stage[rows, :].astype(BF16)
            return carry

        lax.fori_loop(0, d // step, cast_rows, 0)
        for cb in range(w2_stage.shape[1] // LANES):
            cols = slice(cb * LANES, (cb + 1) * LANES)
            w2i_scr[cb, pl.ds(0, f // 2, stride=2), :] = w2_stage[:f // 2, cols]
            w2i_scr[cb, pl.ds(1, f // 2, stride=2), :] = w2_stage[f // 2:, cols]
            w2_scr[:, cols] = w2i_scr[cb].astype(BF16)
        next_run = i + nblk_ref[expert]

        @pl.when(next_run < n_used)
        def _():
            for cp in weight_copies(be_ref[jnp.minimum(next_run, be_ref.shape[0] - 1)]):
                cp.start()

    @pl.when(used)
    def _():
        x = _load_token_tiles(x_ref, MOE_ROWS).astype(BF16)
        even = lax.broadcasted_iota(I32, (MOE_ROWS, LANES), 1) % 2 == 0
        cw = 2 * LANES

        def hidden_pair(j):
            lo, hi = slice(j * cw, (j + 1) * cw), slice(f + j * cw, f + (j + 1) * cw)
            return (jnp.dot(x, w1_scr[:, lo], preferred_element_type=F32) + b1_ref[:, lo],
                    jnp.dot(x, w1_scr[:, hi], preferred_element_type=F32) + b1_ref[:, hi])

        def act_block(h):
            lin = jnp.clip(pltpu.roll(h, LANES - 1, axis=1), -SWIGLU_LIMIT, SWIGLU_LIMIT)
            glu = jnp.minimum(h, SWIGLU_LIMIT)
            return jnp.where(even, glu * jax.nn.sigmoid(SWIGLU_ALPHA * glu) * (lin + 1.0), 0.0)

        y = b2_ref[...]
        pending = hidden_pair(0)
        for j in range(f // cw):
            ha, hb = pending
            if (j + 1) * cw < f:
                pending = hidden_pair(j + 1)
            packed = jnp.concatenate(
                [act_block(ha[:, c * LANES:(c + 1) * LANES])
                 + pltpu.roll(act_block(hb[:, c * LANES:(c + 1) * LANES]), 1, axis=1) for c in range(cw // LANES)],
                axis=1)
            y = y + jnp.dot(packed.astype(BF16), w2_scr[j * cw:(j + 1) * cw, :], preferred_element_type=F32)
        _store_token_tiles(o_ref, y)

    @pl.when(jnp.logical_not(used))
    def _():
        o_ref[...] = jnp.zeros_like(o_ref)


def _expert_ffn(x_sorted, block_e, blocks_per_expert, n_used, w1, b1, w2, b2):
    e, d, f2 = w1.shape
    r = MOE_ROWS
    n_rows = x_sorted.shape[0] // TILE_ROWS
    grid_spec = pltpu.PrefetchScalarGridSpec(
        num_scalar_prefetch=3,
        grid=(n_rows // r,),
        in_specs=[pl.BlockSpec((r * TILE_ROWS, LANES), lambda i, be, nb, nu: (i, 0)),
                  pl.BlockSpec(memory_space=pl.ANY),
                  pl.BlockSpec((None, 1, f2), lambda i, be, nb, nu: (be[i], 0, 0)),
                  pl.BlockSpec(memory_space=pl.ANY),
                  pl.BlockSpec((None, 1, d), lambda i, be, nb, nu: (be[i], 0, 0))],
        out_specs=pl.BlockSpec((r * TILE_ROWS, LANES), lambda i, be, nb, nu: (i, 0)),
        scratch_shapes=[pltpu.VMEM((d, f2), F32), pltpu.VMEM((f2 // 2, d), F32),
                        pltpu.VMEM((d, f2), BF16), pltpu.VMEM((f2 // 2, d), BF16),
                        pltpu.VMEM((d // LANES, f2 // 2, LANES), F32), pltpu.SemaphoreType.DMA((2,))],
    )
    return pl.pallas_call(
        _expert_kernel,
        grid_spec=grid_spec,
        out_shape=jax.ShapeDtypeStruct((n_rows * TILE_ROWS, LANES), F32),
        compiler_params=_params("arbitrary"),
        name="expert_ffn",
    )(block_e, blocks_per_expert, n_used, x_sorted, w1, b1.reshape(e, 1, f2), w2, b2.reshape(e, 1, d))


def _tile_at(ref, first_row):
    return ref.at[pl.ds(pl.multiple_of(first_row, TILE_ROWS), TILE_ROWS)]


def _dispatch_kernel(pad_lo_ref, pad_hi_ref, dest_ref, hn_ref, xs_hbm, zero_scr, sem, pad_sem, *, tm):
    @pl.when(pl.program_id(0) == 0)
    def _():
        zero_scr[...] = jnp.zeros_like(zero_scr)

        def pad_copy(row):
            return pltpu.make_async_copy(zero_scr, _tile_at(xs_hbm, row * TILE_ROWS), pad_sem)

        def per_expert(e, carry):
            lo, hi = pad_lo_ref[e], pad_hi_ref[e]
            lax.fori_loop(lo, hi, lambda r, c: (pad_copy(r).start(), c)[1], 0)
            lax.fori_loop(lo, hi, lambda r, c: (pad_copy(r).wait(), c)[1], 0)
            return carry

        lax.fori_loop(0, pad_lo_ref.shape[0], per_expert, 0)

    def issue(t, carry):
        src = _tile_at(hn_ref, t * TILE_ROWS)
        for j in range(TOP_K):
            pltpu.make_async_copy(src, _tile_at(xs_hbm, dest_ref[0, t * TOP_K + j]), sem).start(priority=j % 2)
        return carry

    lax.fori_loop(0, tm, issue, 0, unroll=8)
    for j in range(TOP_K):
        pltpu.make_async_copy(hn_ref, xs_hbm.at[pl.ds(0, tm * TILE_ROWS)], sem).wait()


def _dispatch(hn_tiles, dest_rows, pad_lo, pad_hi, n_rows):
    t = hn_tiles.shape[0] // TILE_ROWS
    tm = min(DISPATCH_TM, t)
    steps = t // tm
    grid_spec = pltpu.PrefetchScalarGridSpec(
        num_scalar_prefetch=2,
        grid=(steps,),
        in_specs=[pl.BlockSpec((None, 1, tm * TOP_K), lambda i, lo, hi: (i, 0, 0), memory_space=pltpu.SMEM),
                  pl.BlockSpec((tm * TILE_ROWS, LANES), lambda i, lo, hi: (i, 0))],
        out_specs=pl.BlockSpec(memory_space=pl.ANY),
        scratch_shapes=[pltpu.VMEM((TILE_ROWS, LANES), F32), pltpu.SemaphoreType.DMA(()),
                        pltpu.SemaphoreType.DMA(())],
    )
    return pl.pallas_call(
        functools.partial(_dispatch_kernel, tm=tm),
        grid_spec=grid_spec,
        out_shape=jax.ShapeDtypeStruct((n_rows * TILE_ROWS, LANES), F32),
        compiler_params=_params("arbitrary"),
        name="moe_dispatch",
    )(pad_lo, pad_hi, dest_rows.reshape(steps, 1, tm * TOP_K), hn_tiles)


def _combine_kernel(dest_ref, x_ref, gate_ref, g_ref, ys_hbm, o_ref, buf, sem, *, tm, final):
    parts = COMBINE_PARTS
    pt = tm // parts

    for q in range(parts):
        def issue(t, carry, q=q):
            for j in range(TOP_K):
                pltpu.make_async_copy(_tile_at(ys_hbm, dest_ref[0, t * TOP_K + j]),
                                      _tile_at(buf.at[j], t * TILE_ROWS), sem.at[q]).start(priority=j % 2)
            return carry

        lax.fori_loop(q * pt, (q + 1) * pt, issue, 0, unroll=8)

    for q in range(parts):
        part = buf.at[:, pl.ds(q * pt * TILE_ROWS, pt * TILE_ROWS)]
        pltpu.make_async_copy(part, part, sem.at[q]).wait()
        rows = slice(q * pt, (q + 1) * pt)
        y = x_ref[rows, :]
        gate = gate_ref[rows, :]
        for j in range(TOP_K):
            y = y + gate[:, j:j + 1] * _load_token_tiles(buf, pt, lead=(j,), first=q * pt)
        o_ref[rows, :] = _rms(y, g_ref[...]) if final else y


def _combine(x2d, gate2d, dest_rows, y_sorted, g, final):
    t, d = x2d.shape
    tm = min(COMBINE_TM, t)
    steps = t // tm
    return pl.pallas_call(
        functools.partial(_combine_kernel, tm=tm, final=final),
        grid=(steps,),
        in_specs=[pl.BlockSpec((None, 1, tm * TOP_K), lambda i: (i, 0, 0), memory_space=pltpu.SMEM),
                  pl.BlockSpec((tm, d), lambda i: (i, 0)), pl.BlockSpec((tm, LANES), lambda i: (i, 0)),
                  pl.BlockSpec((1, d), lambda i: (0, 0)), pl.BlockSpec(memory_space=pl.ANY)],
        out_specs=pl.BlockSpec((tm, d), lambda i: (i, 0)),
        out_shape=jax.ShapeDtypeStruct((t, d), F32),
        scratch_shapes=[pltpu.VMEM((TOP_K, tm * TILE_ROWS, LANES), F32), pltpu.SemaphoreType.DMA((COMBINE_PARTS,))],
        compiler_params=_params("arbitrary"),
        name="combine_norm",
    )(dest_rows.reshape(steps, 1, tm * TOP_K), x2d, gate2d, g, y_sorted)


def _combine_seg_kernel(src_cur_ref, src_nxt_ref, li_ref, x_ref, gb_ref, g_ref, ys_hbm, o_ref, buf, acc_scr, sem,
                        *, tm, final):
    i = pl.program_id(0)
    n_chunks = src_cur_ref.shape[1]
    chunk = SEG_CHUNK * TILE_ROWS
    slot_rows = n_chunks * chunk

    def fetch(src_ref, slot):
        def start(q, carry):
            pltpu.make_async_copy(ys_hbm.at[pl.ds(pl.multiple_of(src_ref[0, q], TILE_ROWS), chunk)],
                                  buf.at[pl.ds(pl.multiple_of(slot * slot_rows + q * chunk, TILE_ROWS), chunk)],
                                  sem.at[slot]).start(priority=1)
            return carry
        lax.fori_loop(0, n_chunks, start, 0, unroll=8)

    @pl.when(i == 0)
    def _():
        fetch(src_cur_ref, 0)

    @pl.when(i + 1 < pl.num_programs(0))
    def _():
        fetch(src_nxt_ref, (i + 1) % 2)

    slot = i % 2
    base = slot * slot_rows
    whole = buf.at[pl.ds(pl.multiple_of(base, TILE_ROWS), slot_rows)]
    pltpu.make_async_copy(whole, whole, sem.at[slot]).wait()

    def token(t, carry):
        acc = jnp.zeros((TILE_ROWS, LANES), F32)
        for j in range(TOP_K):
            s = t * TOP_K + j
            row = pl.multiple_of(base + li_ref[0, s], TILE_ROWS)
            acc = acc + gb_ref[pl.ds(s, 1), :] * buf[pl.ds(row, TILE_ROWS), :]
        acc_scr[pl.ds(pl.multiple_of(t * TILE_ROWS, TILE_ROWS), TILE_ROWS), :] = acc
        return carry

    lax.fori_loop(0, tm, token, 0, unroll=8)
    y = x_ref[...] + _load_token_tiles(acc_scr, tm)
    o_ref[...] = _rms(y, g_ref[...]) if final else y


def _combine_segments(x2d, gate_b, chunk_src, li, y_sorted, g, final):
    t, d = x2d.shape
    tm = COMBINE_TM
    steps = t // tm
    n_chunks = chunk_src.shape[-1]
    src3 = chunk_src.reshape(steps, 1, n_chunks)
    smem = lambda w, imap: pl.BlockSpec((None, 1, w), imap, memory_space=pltpu.SMEM)
    return pl.pallas_call(
        functools.partial(_combine_seg_kernel, tm=tm, final=final),
        grid=(steps,),
        in_specs=[smem(n_chunks, lambda i: (i, 0, 0)),
                  smem(n_chunks, lambda i: (jnp.minimum(i + 1, steps - 1), 0, 0)),
                  smem(tm * TOP_K, lambda i: (i, 0, 0)),
                  pl.BlockSpec((tm, d), lambda i: (i, 0)), pl.BlockSpec((tm * TOP_K, LANES), lambda i: (i, 0)),
                  pl.BlockSpec((1, d), lambda i: (0, 0)), pl.BlockSpec(memory_space=pl.ANY)],
        out_specs=pl.BlockSpec((tm, d), lambda i: (i, 0)),
        out_shape=jax.ShapeDtypeStruct((t, d), F32),
        scratch_shapes=[pltpu.VMEM((2 * n_chunks * SEG_CHUNK * TILE_ROWS, LANES), F32),
                        pltpu.VMEM((tm * TILE_ROWS, LANES), F32), pltpu.SemaphoreType.DMA((2,))],
        compiler_params=_params("arbitrary"),
        name="combine_norm",
    )(src3, src3, li.reshape(steps, 1, tm * TOP_K), x2d, gate_b, g, y_sorted)


def _layer(x, mem, lyr, final, p):
    b, s, d = x.shape
    t = b * s
    c = RWKV_W
    rwkv_in = 3 * c + LORA_W

    w_in = p["w_in"][lyr]
    w_cat = jnp.concatenate([w_in[:, rwkv_in:], w_in[:, :3 * c], w_in[:, 3 * c:rwkv_in],
                             jnp.zeros((d, LORA_PAD - LORA_W), F32)], axis=1).astype(BF16)
    diff_cols = 3 * DIFF_W
    proj_diff, proj_rwkv = _norm_inproj(x.reshape(t, d), p["norm_mix_g"][lyr][None], w_cat, diff_cols)
    proj_diff, proj_rwkv = proj_diff.reshape(b, s, -1), proj_rwkv.reshape(b, s, -1)

    mu = p["shift_mu"][lyr]
    mu_l = jnp.zeros((1, LORA_PAD), F32).at[0, :LORA_W].set(mu[3 * c:])
    wl = jnp.zeros((LORA_PAD, 3 * c), F32)
    wl = wl.at[:DECAY_LORA, :c].set(p["w_decay_up"][lyr])
    wl = wl.at[DECAY_LORA:DECAY_LORA + ICLR_LORA, c:2 * c].set(p["w_iclr_up"][lyr])
    wl = wl.at[DECAY_LORA + ICLR_LORA:LORA_W, 2 * c:].set(p["w_gate_up"][lyr])
    vec = jnp.stack([p["w0"][lyr], p["a0"][lyr], p["k_k"][lyr], p["k_a"][lyr], p["r_k"][lyr].reshape(c),
                     p["lnx_g"][lyr], p["lnx_b"][lyr], jnp.zeros((c,), F32)])
    y_rwkv = _rwkv_group(proj_rwkv, mu[None, :3 * c], mu_l, wl.astype(BF16), vec,
                         rkv_block=0, lora_block=3 * c // LORA_PAD)

    lambda_init = 0.8 - 0.6 * math.exp(-0.3 * lyr)
    lam_vecs = jnp.stack([p["lambda_q1"][lyr], p["lambda_k1"][lyr], p["lambda_q2"][lyr], p["lambda_k2"][lyr]])
    y_diff = _diff_attention(proj_diff, lam_vecs, p["subln_g"][lyr][None], lambda_init)

    k_mem, v_mem = _mem_kv(mem, p["norm_mem_g"][lyr][None], p["w_ckv"][lyr].astype(BF16))
    x2, hn, route_i, route_g, counts = _mid_stage(
        x, y_rwkv, y_diff, p["w_out"][lyr], p["norm_cross_g"][lyr][None], p["w_cq"][lyr], k_mem, v_mem,
        p["w_co"][lyr], p["norm_ffn_g"][lyr][None], p["w_router"][lyr], p["b_router"][lyr])

    e = N_EXPERTS
    r = MOE_ROWS
    n_blocks = (t * TOP_K) // r + e
    route_i = route_i.reshape(t, LANES)
    idx, rank = route_i[:, :TOP_K], route_i[:, TOP_K:2 * TOP_K]
    cnt = counts[0, :e].astype(I32)
    padded = (cnt + r - 1) // r * r
    pad_end = jnp.cumsum(padded)
    pad_start = pad_end - padded
    onehot = idx[:, :, None] == jnp.arange(e, dtype=I32)
    dest = jnp.sum(jnp.where(onehot, pad_start, 0), axis=-1) + rank
    dest = dest.reshape(-1) * TILE_ROWS
    block_start = jnp.arange(n_blocks, dtype=I32) * r
    block_e = jnp.minimum(jnp.sum((block_start[:, None] >= pad_end[None, :]).astype(I32), axis=1), e - 1)
    n_used = pad_end[-1:] // r

    unused_lo = jnp.concatenate([pad_start + cnt, pad_end[-1:]])
    unused_hi = jnp.concatenate([pad_end, jnp.full((1,), n_blocks * r, I32)])
    x_sorted = _dispatch(hn, dest, unused_lo, unused_hi, n_blocks * r)
    y_sorted = _expert_ffn(x_sorted, block_e, padded // r, n_used, p["w1"][lyr], p["b1"][lyr], p["w2"][lyr],
                           p["b2"][lyr])
    tmc = COMBINE_TM
    nt = t // tmc
    per_tile = jnp.sum(onehot.reshape(nt, tmc * TOP_K, e), axis=1, dtype=I32)
    before = jnp.cumsum(per_tile, axis=0) - per_tile
    n_chunks = (per_tile + SEG_CHUNK - 1) // SEG_CHUNK
    chunk_end = jnp.cumsum(n_chunks, axis=1)
    chunk_first = chunk_end - n_chunks
    shift = jnp.repeat(chunk_first * SEG_CHUNK - before, tmc, axis=0)
    local_row = (jnp.sum(jnp.where(onehot, shift[:, None, :], 0), axis=-1) + rank) * TILE_ROWS
    slots = jnp.arange(tmc * TOP_K // SEG_CHUNK + e, dtype=I32)
    owner = jnp.sum((slots[None, :, None] >= chunk_end[:, None, :]).astype(I32), axis=-1)
    own = owner[:, :, None] == jnp.arange(e, dtype=I32)
    seg_row = jnp.sum(jnp.where(own, (pad_start[None, :] + before - chunk_first * SEG_CHUNK)[:, None, :], 0), axis=-1)
    chunk_src = jnp.where(owner < e, seg_row + slots[None, :] * SEG_CHUNK, 0) * TILE_ROWS
    gate_b = jnp.broadcast_to(route_g.reshape(t, LANES)[:, :TOP_K].reshape(t * TOP_K, 1), (t * TOP_K, LANES))
    return _combine_segments(x2.reshape(t, d), gate_b, chunk_src, local_row.reshape(-1), y_sorted,
                             p["norm_final_g"][None], final).reshape(b, s, d)


def kernel(x, mem, norm_mix_g, w_in, shift_mu, w0, w_decay_up, a0, w_iclr_up, w_gate_up, k_k, k_a, r_k,
           lnx_g, lnx_b, lambda_q1, lambda_k1, lambda_q2, lambda_k2, subln_g, w_out, norm_cross_g,
           norm_mem_g, w_cq, w_ckv, w_co, norm_ffn_g, w_router, b_router, w1, b1, w2, b2, norm_final_g):
    p = dict(norm_mix_g=norm_mix_g, w_in=w_in, shift_mu=shift_mu, w0=w0, w_decay_up=w_decay_up, a0=a0,
             w_iclr_up=w_iclr_up, w_gate_up=w_gate_up, k_k=k_k, k_a=k_a, r_k=r_k, lnx_g=lnx_g, lnx_b=lnx_b,
             lambda_q1=lambda_q1, lambda_k1=lambda_k1, lambda_q2=lambda_q2, lambda_k2=lambda_k2,
             subln_g=subln_g, w_out=w_out, norm_cross_g=norm_cross_g, norm_mem_g=norm_mem_g, w_cq=w_cq,
             w_ckv=w_ckv, w_co=w_co, norm_ffn_g=norm_ffn_g, w_router=w_router, b_router=b_router,
             w1=w1, b1=b1, w2=w2, b2=b2, norm_final_g=norm_final_g)
    depth = w_in.shape[0]
    for lyr in range(depth):
        x = _layer(x, mem, lyr, lyr == depth - 1, p)
    return x
```

```python
import functools
import math

import jax
import jax.numpy as jnp
from jax import lax
from jax.experimental import pallas as pl
from jax.experimental.pallas import tpu as pltpu

F32 = jnp.float32
BF16 = jnp.bfloat16
I32 = jnp.int32

NORM_EPS = 1e-5
HEAD = 64
RWKV_W = 512
RWKV_HEADS = RWKV_W // HEAD
DECAY_LORA, ICLR_LORA, GATE_LORA = 64, 64, 160
LORA_W = DECAY_LORA + ICLR_LORA + GATE_LORA
LORA_PAD = 384
RWKV_GN_EPS = 64e-5
DIFF_W = 512
DIFF_HEADS = DIFF_W // (2 * HEAD)
CROSS_HEADS = 4
N_EXPERTS = 32
TOP_K = 4
SWIGLU_LIMIT = 7.0
SWIGLU_ALPHA = 1.702
LANES = 128
TILE_ROWS = 8
NEG_BIG = -1e30

RWKV_CHUNK = 64
RWKV_BATCH = 4
PROJ_TM = 512
ATT_BLOCK = 512
ATT_ROW_CHUNK = 128
ATT_KEY_GROUP = 4
MID_TM = 512
MID_CHAIN_ROWS = 256
MOE_ROWS = 256
COMBINE_TM = 256
SEG_CHUNK = 8
VMEM_LIMIT = 56 * 1024 * 1024


def _dot(a, b):
    return jnp.dot(a.astype(BF16), b.astype(BF16), preferred_element_type=F32)


def _dot_nt(a, b):
    return lax.dot_general(a.astype(BF16), b.astype(BF16), (((1,), (1,)), ((), ())),
                           preferred_element_type=F32)


def _dot_tn(a, b):
    return lax.dot_general(a.astype(BF16), b.astype(BF16), (((0,), (0,)), ((), ())),
                           preferred_element_type=F32)


def _split2(x):
    hi = x.astype(BF16)
    lo = (x - hi.astype(F32)).astype(BF16)
    return hi, lo


def _dot_exact_rhs(x, ones_bf16):
    hi, lo = _split2(x)
    return jnp.dot(hi, ones_bf16, preferred_element_type=F32) + jnp.dot(lo, ones_bf16, preferred_element_type=F32)


def _dot_exact_lhs(ones_bf16, x):
    hi, lo = _split2(x)
    return jnp.dot(ones_bf16, hi, preferred_element_type=F32) + jnp.dot(ones_bf16, lo, preferred_element_type=F32)


def _store_token_tiles(ref, x, lead=()):
    rows = x.shape[0]
    for c in range(TILE_ROWS):
        ref[lead + (pl.ds(c, rows, stride=TILE_ROWS), slice(None))] = x[:, c * LANES:(c + 1) * LANES]


def _load_token_tiles(ref, rows, lead=(), first=0):
    return jnp.concatenate([ref[lead + (pl.ds(first * TILE_ROWS + c, rows, stride=TILE_ROWS), slice(None))]
                            for c in range(TILE_ROWS)], axis=1)


def _rms(x, g):
    return x * lax.rsqrt(jnp.mean(x * x, axis=-1, keepdims=True) + NORM_EPS) * g


def _params(*sem):
    return pltpu.CompilerParams(dimension_semantics=sem, vmem_limit_bytes=VMEM_LIMIT)


def _norm_inproj_kernel(x_ref, g_ref, w_ref, od_ref, or_ref):
    h = _rms(x_ref[...], g_ref[...]).astype(BF16)
    nd = od_ref.shape[1]
    od_ref[...] = jnp.dot(h, w_ref[:, :nd], preferred_element_type=F32).astype(BF16)
    or_ref[...] = jnp.dot(h, w_ref[:, nd:], preferred_element_type=F32)


def _norm_inproj(x2d, g, w_bf16, n_diff):
    t, d = x2d.shape
    n = w_bf16.shape[1]
    tm = min(PROJ_TM, t)
    return pl.pallas_call(
        _norm_inproj_kernel,
        grid=(t // tm,),
        in_specs=[pl.BlockSpec((tm, d), lambda i: (i, 0)),
                  pl.BlockSpec((1, d), lambda i: (0, 0)),
                  pl.BlockSpec((d, n), lambda i: (0, 0))],
        out_specs=[pl.BlockSpec((tm, n_diff), lambda i: (i, 0)), pl.BlockSpec((tm, n - n_diff), lambda i: (i, 0))],
        out_shape=[jax.ShapeDtypeStruct((t, n_diff), BF16), jax.ShapeDtypeStruct((t, n - n_diff), F32)],
        compiler_params=_params("parallel"),
        name="norm_inproj",
    )(x2d, g, w_bf16)


def _rwkv_kernel(rkv_ref, lora_ref, mu_rkv_ref, mu_l_ref, wl_ref, vec_ref, bd_ref, tri_ref,
                 o_ref, st_ref, prev_rkv_ref, prev_l_ref):
    L = RWKV_CHUNK
    C = RWKV_W

    @pl.when(pl.program_id(1) == 0)
    def _():
        st_ref[...] = jnp.zeros_like(st_ref)
        prev_rkv_ref[...] = jnp.zeros_like(prev_rkv_ref)
        prev_l_ref[...] = jnp.zeros_like(prev_l_ref)

    row = lax.broadcasted_iota(I32, (L, 1), 0)
    w0, a0, k_k, k_a = vec_ref[0:1, :], vec_ref[1:2, :], vec_ref[2:3, :], vec_ref[3:4, :]
    r_k, lnx_g, lnx_b = vec_ref[4:5, :], vec_ref[5:6, :], vec_ref[6:7, :]
    bd = bd_ref[...]
    tri = tri_ref[...]
    lane_l = lax.broadcasted_iota(I32, (L, LORA_PAD), 1)
    nb = rkv_ref.shape[0]

    def token_shift(raw, prev_ref, bi, mu):
        prev = jnp.where(row == 0, prev_ref[bi], pltpu.roll(raw, 1, axis=0))
        prev_ref[bi] = raw[L - 1:L, :]
        return raw + mu * (prev - raw)

    def elementwise(bi):
        u = token_shift(rkv_ref[bi], prev_rkv_ref, bi, mu_rkv_ref[...])
        ul = token_shift(lora_ref[bi], prev_l_ref, bi, mu_l_ref[...])
        r, k, v = u[:, :C], u[:, C:2 * C], u[:, 2 * C:]
        act = jnp.where(lane_l < DECAY_LORA, jnp.tanh(ul),
                        jnp.where(lane_l < DECAY_LORA + ICLR_LORA, ul, jax.nn.sigmoid(ul)))
        lo = _dot(act, wl_ref[...])
        w_log = -jax.nn.softplus(-(w0 + lo[:, :C])) - 0.5
        lw = -jnp.exp(w_log)
        a = jax.nn.sigmoid(a0 + lo[:, C:2 * C])
        kk = k * k_k
        kk = kk / jnp.maximum(jnp.sqrt(_dot(kk * kk, bd)), 1e-12)
        k2 = k * (1.0 + (a - 1.0) * k_a)
        cum = _dot_exact_lhs(tri, lw)
        total = cum[L - 1:L, :]
        e_neg = jnp.exp(-cum)
        e_rem = jnp.exp(total - cum)
        kka = kk * a
        return dict(r=r, v=v, k2=k2, g=lo[:, 2 * C:], a_t=-kk * jnp.exp(cum - lw), b_t=kka * e_neg,
                    k_t=k2 * e_neg, r_t=r * jnp.exp(cum), b_bar=kka * e_rem, k_bar=k2 * e_rem,
                    p_total=jnp.exp(total))

    ew = [elementwise(bi) for bi in range(nb)]

    m0 = lax.broadcasted_iota(I32, (L, LANES), 1) < HEAD
    r2i = lax.broadcasted_iota(I32, (2 * L, 4 * L), 0)
    c2i = lax.broadcasted_iota(I32, (2 * L, 4 * L), 1) % (2 * L)
    strict, incl = c2i < r2i, c2i <= r2i
    npairs = RWKV_HEADS // 2
    chains = [(bi, p) for bi in range(nb) for p in range(npairs)]
    nc = range(len(chains))

    def stack2(name):
        out = []
        for bi, p in chains:
            xp = ew[bi][name][:, p * LANES:(p + 1) * LANES]
            out.append(jnp.concatenate([jnp.where(m0, xp, 0.0), jnp.where(m0, 0.0, xp)], axis=0))
        return out

    A2, B2, K2, R2 = stack2("a_t"), stack2("b_t"), stack2("k_t"), stack2("r_t")
    V2, Bb2, Kb2 = stack2("v"), stack2("b_bar"), stack2("k_bar")
    mg = [_dot_nt(jnp.concatenate([A2[c], R2[c]], axis=0), jnp.concatenate([B2[c], K2[c]], axis=0)) for c in nc]
    m_cat = [jnp.where(strict, mg[c][:2 * L], 0.0) for c in nc]
    g_cat = [jnp.where(incl, mg[c][2 * L:], 0.0) for c in nc]
    m_ab = [m_cat[c][:, :LANES] for c in nc]
    pw = [_dot(m_ab[c], m_ab[c]) for c in nc]
    tr = list(m_ab)
    k = 2
    while 2 * k < L:
        x = [_dot(pw[c], jnp.concatenate([pw[c], tr[c]], axis=1)) for c in nc]
        tr = [tr[c] + pw[c] + x[c][:, LANES:] for c in nc]
        pw = [x[c][:, :LANES] for c in nc]
        k *= 2
    tr = [tr[c] + pw[c] + _dot(pw[c], tr[c]) for c in nc]
    az = [jnp.concatenate([A2[c], _dot(m_cat[c][:, LANES:], V2[c])], axis=1) for c in nc]
    tz = [az[c] + _dot(tr[c], az[c]) for c in nc]
    st = [st_ref[bi, p] for bi, p in chains]
    u2 = [_dot_nt(tz[c][:, :LANES], st[c]) + tz[c][:, LANES:] for c in nc]
    uv = [jnp.concatenate([u2[c], V2[c]], axis=0) for c in nc]
    y2 = [_dot_nt(R2[c], st[c]) + _dot(g_cat[c], uv[c]) for c in nc]
    for c, (bi, p) in enumerate(chains):
        st_ref[bi, p] = (st[c] * ew[bi]["p_total"][:, p * LANES:(p + 1) * LANES]
                         + _dot_tn(uv[c], jnp.concatenate([Bb2[c], Kb2[c]], axis=0)))

    inv_n = 1.0 / HEAD
    for bi in range(nb):
        y = jnp.concatenate([y2[bi * npairs + p][:L] + y2[bi * npairs + p][L:] for p in range(npairs)], axis=1)
        e = ew[bi]
        mu = _dot_exact_rhs(y, bd) * inv_n
        d = y - mu
        var = _dot_exact_rhs(d * d, bd) * inv_n
        yn = d * lax.rsqrt(var + RWKV_GN_EPS) * lnx_g + lnx_b
        bonus = _dot_exact_rhs(e["r"] * e["k2"] * r_k, bd) * e["v"]
        o_ref[bi] = (yn + bonus) * e["g"]


def _rwkv_group(proj3, mu_rkv, mu_l, wl, vec, rkv_block, lora_block):
    b, s, _ = proj3.shape
    L, C = RWKV_CHUNK, RWKV_W
    head = jnp.arange(C, dtype=I32) // HEAD
    bd = (head[:, None] == head[None, :]).astype(BF16)
    t = jnp.arange(L, dtype=I32)
    tri = (t[None, :] <= t[:, None]).astype(BF16)
    const = lambda shape: pl.BlockSpec(shape, lambda i, j: (0,) * len(shape))
    nb = RWKV_BATCH if b % RWKV_BATCH == 0 else 1
    return pl.pallas_call(
        _rwkv_kernel,
        grid=(b // nb, s // L),
        in_specs=[pl.BlockSpec((nb, L, 3 * C), lambda i, j: (i, j, rkv_block)),
                  pl.BlockSpec((nb, L, LORA_PAD), lambda i, j: (i, j, lora_block)),
                  const((1, 3 * C)), const((1, LORA_PAD)), const((LORA_PAD, 3 * C)),
                  const((8, C)), const((C, C)), const((L, L))],
        out_specs=pl.BlockSpec((nb, L, C), lambda i, j: (i, j, 0)),
        out_shape=jax.ShapeDtypeStruct((b, s, C), F32),
        scratch_shapes=[pltpu.VMEM((nb, RWKV_HEADS // 2, LANES, LANES), F32),
                        pltpu.VMEM((nb, 1, 3 * C), F32), pltpu.VMEM((nb, 1, LORA_PAD), F32)],
        compiler_params=_params("parallel", "arbitrary"),
        name="rwkv_group",
    )(proj3, proj3, mu_rkv, mu_l, wl, vec, bd, tri)


def _diff_attn_kernel(qi_ref, kj_ref, diag_ref, q_ref, k_ref, v_ref, slope_ref, lam_ref, g_ref, o_ref,
                      q2_scr, m_scr, acc_scr, *, lambda_init):
    blk = ATT_BLOCK
    group = k_ref.shape[0] // blk
    step_id = pl.program_id(2)
    qi, kj, diag = qi_ref[step_id], kj_ref[step_id], diag_ref[step_id]
    log2e = math.log2(math.e)

    @pl.when(kj == 0)
    def _():
        q = q_ref[...].astype(F32) * (HEAD ** -0.5 * log2e)
        m0 = lax.broadcasted_iota(I32, (blk, LANES), 1) < HEAD
        q2_scr[0:blk, :] = jnp.where(m0, q, 0.0).astype(BF16)
        q2_scr[blk:, :] = jnp.where(m0, 0.0, q).astype(BF16)
        m_scr[...] = jnp.full_like(m_scr, NEG_BIG)
        acc_scr[...] = jnp.zeros_like(acc_scr)

    def step(kh, on_diagonal):
        k = k_ref[kh * blk:(kh + 1) * blk, :].astype(BF16)
        v_ext = jnp.concatenate([v_ref[kh * blk:(kh + 1) * blk, :].astype(BF16), jnp.ones((blk, LANES), BF16)],
                                axis=1)
        koff = lax.broadcasted_iota(I32, (1, blk), 1)
        col_bias = (slope_ref[:, :1] * log2e) * ((kj * group + kh - qi) * blk + koff).astype(F32)
        rc = ATT_ROW_CHUNK
        n_chunks = 2 * blk // rc

        def scores(c):
            return lax.dot_general(q2_scr[c * rc:(c + 1) * rc, :], k, (((1,), (1,)), ((), ())),
                                   preferred_element_type=F32)

        s_next = scores(0)
        for c in range(n_chunks):
            rows = slice(c * rc, (c + 1) * rc)
            s = s_next + col_bias
            if c + 1 < n_chunks:
                s_next = scores(c + 1)
            if on_diagonal:
                qoff = (c * rc) % blk + lax.broadcasted_iota(I32, (rc, 1), 0)
                s = jnp.where(koff <= qoff, s, NEG_BIG)
            m_prev = m_scr[rows, :]
            m_new = jnp.maximum(m_prev, jnp.max(s, axis=-1, keepdims=True))
            alpha = jnp.exp2(m_prev - m_new)
            p = jnp.exp2(s - jnp.concatenate([m_new] * (blk // LANES), axis=1))
            acc_scr[rows, :] = (jnp.concatenate([alpha, alpha], axis=1) * acc_scr[rows, :]
                                + jnp.dot(p.astype(BF16), v_ext, preferred_element_type=F32))
            m_scr[rows, :] = m_new

    @pl.when(diag < 0)
    def _():
        for kh in range(group):
            step(kh, False)

    def last_step(n_full):
        for kh in range(n_full):
            step(kh, False)
        step(n_full, True)
        lam_v = lam_ref[...]
        lam = (jnp.exp(jnp.sum(lam_v[0:1] * lam_v[1:2], axis=-1, keepdims=True))
               - jnp.exp(jnp.sum(lam_v[2:3] * lam_v[3:4], axis=-1, keepdims=True)) + lambda_init)
        o2 = acc_scr[:, :LANES] / acc_scr[:, LANES:]
        o = o2[:blk] - lam * o2[blk:]
        o_ref[...] = _rms(o, g_ref[...]) * (1.0 - lambda_init)

    for n_full in range(group):
        pl.when(diag == n_full)(functools.partial(last_step, n_full))


def _diff_attention(proj3, lam_vecs, subln_g, lambda_init):
    b, s, _ = proj3.shape
    blk = ATT_BLOCK
    nb = s // blk
    group = math.gcd(ATT_KEY_GROUP, nb)
    nh = DIFF_HEADS
    slopes = jnp.exp2(-8.0 * jnp.arange(1, nh + 1, dtype=F32) / nh)
    slopes = jnp.broadcast_to(slopes[:, None, None], (nh, 1, LANES))
    steps = [(qi, kj, qi % group if kj == qi // group else -1) for qi in range(nb) for kj in range(qi // group + 1)]
    tabs = [jnp.asarray([st[i] for st in steps], I32) for i in range(3)]
    kernel = functools.partial(_diff_attn_kernel, lambda_init=lambda_init)
    grid_spec = pltpu.PrefetchScalarGridSpec(
        num_scalar_prefetch=3,
        grid=(b, nh, len(steps)),
        in_specs=[pl.BlockSpec((None, blk, LANES), lambda bi, h, t, qt, kt, dt: (bi, qt[t], h)),
                  pl.BlockSpec((None, group * blk, LANES), lambda bi, h, t, qt, kt, dt: (bi, kt[t], nh + h)),
                  pl.BlockSpec((None, group * blk, LANES), lambda bi, h, t, qt, kt, dt: (bi, kt[t], 2 * nh + h)),
                  pl.BlockSpec((None, 1, LANES), lambda bi, h, t, qt, kt, dt: (h, 0, 0)),
                  pl.BlockSpec((4, HEAD), lambda bi, h, t, qt, kt, dt: (0, 0)),
                  pl.BlockSpec((1, 2 * HEAD), lambda bi, h, t, qt, kt, dt: (0, 0))],
        out_specs=pl.BlockSpec((None, blk, LANES), lambda bi, h, t, qt, kt, dt: (bi, qt[t], h)),
        scratch_shapes=[pltpu.VMEM((2 * blk, LANES), BF16), pltpu.VMEM((2 * blk, LANES), F32),
                        pltpu.VMEM((2 * blk, 2 * LANES), F32)],
    )
    return pl.pallas_call(
        kernel,
        grid_spec=grid_spec,
        out_shape=jax.ShapeDtypeStruct((b, s, DIFF_W), F32),
        compiler_params=_params("parallel", "parallel", "arbitrary"),
        name="diff_attention",
    )(*tabs, proj3, proj3, proj3, slopes, lam_vecs, subln_g)


def _mem_kv_kernel(m_ref, g_ref, w_ref, k_ref, v_ref):
    d = m_ref.shape[-1]
    kv = _dot(_rms(m_ref[...], g_ref[...]), w_ref[...])
    k_ref[...] = kv[:, :d].astype(BF16)
    v_ref[...] = kv[:, d:].astype(BF16)


def _mem_kv(mem, g, w_ckv_bf16):
    b, m, d = mem.shape
    return pl.pallas_call(
        _mem_kv_kernel,
        grid=(b,),
        in_specs=[pl.BlockSpec((None, m, d), lambda i: (i, 0, 0)),
                  pl.BlockSpec((1, d), lambda i: (0, 0)),
                  pl.BlockSpec((d, 2 * d), lambda i: (0, 0))],
        out_specs=[pl.BlockSpec((None, m, d), lambda i: (i, 0, 0))] * 2,
        out_shape=[jax.ShapeDtypeStruct((b, m, d), BF16)] * 2,
        compiler_params=_params("parallel"),
        name="mem_kv",
    )(mem, g, w_ckv_bf16)


def _mid_kernel(x_ref, yr_ref, yd_ref, wo_ref, gc_ref, wcq_ref, km_ref, vm_ref, wco_ref, gf_ref,
                wr_hi_ref, wr_lo_ref, br_ref, tri_ref,
                x2_ref, hn_ref, ri_ref, rg_ref, cnt_ref, carry_scr):
    tm, d = x_ref.shape
    first = jnp.logical_and(pl.program_id(0) == 0, pl.program_id(1) == 0)

    @pl.when(first)
    def _():
        carry_scr[...] = jnp.zeros_like(carry_scr)

    rc = tri_ref.shape[0]
    chains = range(tm // rc)
    rows = [slice(c * rc, (c + 1) * rc) for c in chains]
    half = yr_ref.shape[-1]
    x1 = [x_ref[r, :] + _dot(yr_ref[r, :], wo_ref[:half, :]) + _dot(yd_ref[r, :], wo_ref[half:, :]) for r in rows]

    q = [_dot(_rms(x1[c], gc_ref[...]), wcq_ref[...]) for c in chains]
    hd = d // CROSS_HEADS
    outs = [[] for _ in chains]

    def scores(h):
        sl = slice(h * hd, (h + 1) * hd)
        return [_dot_nt(q[c][:, sl], km_ref[:, sl]) * (hd ** -0.5) for c in chains]

    s_next = scores(0)
    for h in range(CROSS_HEADS):
        sl = slice(h * hd, (h + 1) * hd)
        s = s_next
        if h + 1 < CROSS_HEADS:
            s_next = scores(h + 1)
        p = [jnp.exp(s[c] - jnp.max(s[c], axis=-1, keepdims=True)) for c in chains]
        p = [p[c] / jnp.sum(p[c], axis=-1, keepdims=True) for c in chains]
        for c in chains:
            outs[c].append(_dot(p[c], vm_ref[:, sl]))
    x2 = [x1[c] + _dot(jnp.concatenate(outs[c], axis=1), wco_ref[...]) for c in chains]
    for c in chains:
        x2_ref[rows[c], :] = x2[c]

    hn = [_rms(x2[c], gf_ref[...]) for c in chains]
    _store_token_tiles(hn_ref, jnp.concatenate(hn, axis=0))
    hi = [hn[c].astype(BF16) for c in chains]
    lo = [(hn[c] - hi[c].astype(F32)).astype(BF16) for c in chains]
    vals = [(jnp.dot(hi[c], wr_hi_ref[...], preferred_element_type=F32)
             + jnp.dot(lo[c], wr_hi_ref[...], preferred_element_type=F32)
             + jnp.dot(hi[c], wr_lo_ref[...], preferred_element_type=F32) + br_ref[...]) for c in chains]
    lane = lax.broadcasted_iota(I32, (rc, LANES), 1)
    tops, idxs, hots = [[] for _ in chains], [[] for _ in chains], [[] for _ in chains]
    for _ in range(TOP_K):
        for c in chains:
            mx = jnp.max(vals[c], axis=-1, keepdims=True)
            idx = jnp.min(jnp.where(vals[c] == mx, lane, LANES), axis=-1, keepdims=True)
            hot = lane == idx
            vals[c] = jnp.where(hot, -jnp.inf, vals[c])
            tops[c].append(mx)
            idxs[c].append(idx)
            hots[c].append(hot)
    for c in chains:
        es = [jnp.exp(t - tops[c][0]) for t in tops[c]]
        denom = es[0] + es[1] + es[2] + es[3]
        sel = jnp.zeros((rc, LANES), F32)
        for hot in hots[c]:
            sel = sel + hot.astype(F32)
        before = jnp.dot(tri_ref[...], sel.astype(BF16), preferred_element_type=F32) + carry_scr[...]
        carry_scr[...] = carry_scr[...] + jnp.sum(sel, axis=0, keepdims=True)
        ri = jnp.zeros((rc, LANES), I32)
        rg = jnp.zeros((rc, LANES), F32)
        for j in range(TOP_K):
            rank = jnp.sum(jnp.where(hots[c][j], before, 0.0), axis=-1, keepdims=True).astype(I32)
            ri = jnp.where(lane == j, idxs[c][j], ri)
            ri = jnp.where(lane == TOP_K + j, rank, ri)
            rg = jnp.where(lane == j, es[j] / denom, rg)
        ri_ref[rows[c], :] = ri
        rg_ref[rows[c], :] = rg
    cnt_ref[...] = jnp.broadcast_to(carry_scr[...], cnt_ref.shape)


def _mid_stage(x, y_rwkv, y_diff, w_out, g_cross, w_cq, k_mem, v_mem, w_co, g_ffn, w_router, b_router):
    b, s, d = x.shape
    tm = min(MID_TM, s)
    m = k_mem.shape[1]
    half = y_rwkv.shape[-1]
    e = w_router.shape[1]
    wr = jnp.zeros((d, LANES), F32).at[:, :e].set(w_router)
    wr_hi = wr.astype(BF16)
    wr_lo = (wr - wr_hi.astype(F32)).astype(BF16)
    br = jnp.full((1, LANES), NEG_BIG, F32).at[0, :e].set(b_router)
    rc = min(MID_CHAIN_ROWS, tm)
    t = jnp.arange(rc, dtype=I32)
    tri = (t[None, :] < t[:, None]).astype(BF16)
    tile = lambda w: pl.BlockSpec((None, tm, w), lambda i, j: (i, j, 0))
    const = lambda shape: pl.BlockSpec(shape, lambda i, j: (0,) * len(shape))
    return pl.pallas_call(
        _mid_kernel,
        grid=(b, s // tm),
        in_specs=[tile(d), tile(half), tile(half), const((d, d)), const((1, d)), const((d, d)),
                  pl.BlockSpec((None, m, d), lambda i, j: (i, 0, 0)),
                  pl.BlockSpec((None, m, d), lambda i, j: (i, 0, 0)),
                  const((d, d)), const((1, d)), const((d, LANES)), const((d, LANES)),
                  const((1, LANES)), const((rc, rc))],
        out_specs=[tile(d), pl.BlockSpec((tm * TILE_ROWS, LANES), lambda i, j: (i * (s // tm) + j, 0)),
                   tile(LANES), tile(LANES), const((8, LANES))],
        out_shape=[jax.ShapeDtypeStruct((b, s, d), F32), jax.ShapeDtypeStruct((b * s * TILE_ROWS, LANES), F32),
                   jax.ShapeDtypeStruct((b, s, LANES), I32), jax.ShapeDtypeStruct((b, s, LANES), F32),
                   jax.ShapeDtypeStruct((8, LANES), F32)],
        scratch_shapes=[pltpu.VMEM((1, LANES), F32)],
        compiler_params=_params("arbitrary", "arbitrary"),
        name="outproj_cross_router",
    )(x, y_rwkv, y_diff, w_out.astype(BF16), g_cross, w_cq.astype(BF16), k_mem, v_mem,
      w_co.astype(BF16), g_ffn, wr_hi, wr_lo, br, tri)


def _expert_kernel(be_ref, nblk_ref, nused_ref, x_ref, w1_hbm, b1_ref, w2_hbm, b2_ref, o_ref,
                   w1_stage, w2_stage, w1_scr, w2_scr, w2i_scr, sem):
    i = pl.program_id(0)
    n_used = nused_ref[0]
    used = i < n_used
    expert = be_ref[i]
    new_expert = jnp.logical_or(i == 0, expert != be_ref[jnp.maximum(i - 1, 0)])
    d, f = w1_stage.shape[0], w2_stage.shape[0]

    def weight_copies(ex):
        return (pltpu.make_async_copy(w1_hbm.at[ex], w1_stage, sem.at[0]),
                pltpu.make_async_copy(w2_hbm.at[ex], w2_stage, sem.at[1]))

    @pl.when(jnp.logical_and(used, i == 0))
    def _():
        for cp in weight_copies(expert):
            cp.start()

    @pl.when(jnp.logical_and(used, new_expert))
    def _():
        for cp in weight_copies(expert):
            cp.wait()
        step = 128

        def cast_rows(c, carry):
            rows = pl.ds(pl.multiple_of(c * step, step), step)
            w1_scr[rows, :] = w1_stage[rows, :].astype(BF16)
            return carry

        lax.fori_loop(0, d // step, cast_rows, 0)
        for cb in range(w2_stage.shape[1] // LANES):
            cols = slice(cb * LANES, (cb + 1) * LANES)
            w2i_scr[cb, pl.ds(0, f // 2, stride=2), :] = w2_stage[:f // 2, cols]
            w2i_scr[cb, pl.ds(1, f // 2, stride=2), :] = w2_stage[f // 2:, cols]
            w2_scr[:, cols] = w2i_scr[cb].astype(BF16)
        next_run = i + nblk_ref[expert]

        @pl.when(next_run < n_used)
        def _():
            for cp in weight_copies(be_ref[jnp.minimum(next_run, be_ref.shape[0] - 1)]):
                cp.start()

    @pl.when(used)
    def _():
        x = _load_token_tiles(x_ref, MOE_ROWS).astype(BF16)
        even = lax.broadcasted_iota(I32, (MOE_ROWS, LANES), 1) % 2 == 0
        cw = 2 * LANES

        def hidden_pair(j):
            lo, hi = slice(j * cw, (j + 1) * cw), slice(f + j * cw, f + (j + 1) * cw)
            return (jnp.dot(x, w1_scr[:, lo], preferred_element_type=F32) + b1_ref[:, lo],
                    jnp.dot(x, w1_scr[:, hi], preferred_element_type=F32) + b1_ref[:, hi])

        def act_block(h):
            lin = jnp.clip(pltpu.roll(h, LANES - 1, axis=1), -SWIGLU_LIMIT, SWIGLU_LIMIT)
            glu = jnp.minimum(h, SWIGLU_LIMIT)
            return jnp.where(even, glu * jax.nn.sigmoid(SWIGLU_ALPHA * glu) * (lin + 1.0), 0.0)

        y = b2_ref[...]
        pending = hidden_pair(0)
        for j in range(f // cw):
            ha, hb = pending
            if (j + 1) * cw < f:
                pending = hidden_pair(j + 1)
            packed = jnp.concatenate(
                [act_block(ha[:, c * LANES:(c + 1) * LANES])
                 + pltpu.roll(act_block(hb[:, c * LANES:(c + 1) * LANES]), 1, axis=1) for c in range(cw // LANES)],
                axis=1)
            y = y + jnp.dot(packed.astype(BF16), w2_scr[j * cw:(j + 1) * cw, :], preferred_element_type=F32)
        _store_token_tiles(o_ref, y)

    @pl.when(jnp.logical_not(used))
    def _():
        o_ref[...] = jnp.zeros_like(o_ref)


def _expert_ffn(x_sorted, block_e, blocks_per_expert, n_used, w1, b1, w2, b2):
    e, d, f2 = w1.shape
    r = MOE_ROWS
    n_rows = block_e.shape[0] * r
    grid_spec = pltpu.PrefetchScalarGridSpec(
        num_scalar_prefetch=3,
        grid=(n_rows // r,),
        in_specs=[pl.BlockSpec((r * TILE_ROWS, LANES), lambda i, be, nb, nu: (i, 0)),
                  pl.BlockSpec(memory_space=pl.ANY),
                  pl.BlockSpec((None, 1, f2), lambda i, be, nb, nu: (be[i], 0, 0)),
                  pl.BlockSpec(memory_space=pl.ANY),
                  pl.BlockSpec((None, 1, d), lambda i, be, nb, nu: (be[i], 0, 0))],
        out_specs=pl.BlockSpec((r * TILE_ROWS, LANES), lambda i, be, nb, nu: (i, 0)),
        scratch_shapes=[pltpu.VMEM((d, f2), F32), pltpu.VMEM((f2 // 2, d), F32),
                        pltpu.VMEM((d, f2), BF16), pltpu.VMEM((f2 // 2, d), BF16),
                        pltpu.VMEM((d // LANES, f2 // 2, LANES), F32), pltpu.SemaphoreType.DMA((2,))],
    )
    return pl.pallas_call(
        _expert_kernel,
        grid_spec=grid_spec,
        out_shape=jax.ShapeDtypeStruct((n_rows * TILE_ROWS, LANES), F32),
        compiler_params=_params("arbitrary"),
        name="expert_ffn",
    )(block_e, blocks_per_expert, n_used, x_sorted, w1, b1.reshape(e, 1, f2), w2, b2.reshape(e, 1, d))


def _tile_at(ref, first_row):
    return ref.at[pl.ds(pl.multiple_of(first_row, TILE_ROWS), TILE_ROWS)]


def _dispatch_kernel(pad_lo_ref, pad_hi_ref, dst_ref, li_ref, hn_ref, xs_hbm, buf, zero_scr, sem, pad_sem, *, tm):
    @pl.when(pl.program_id(0) == 0)
    def _():
        zero_scr[...] = jnp.zeros_like(zero_scr)

        def pad_copy(row):
            return pltpu.make_async_copy(zero_scr, _tile_at(xs_hbm, row * TILE_ROWS), pad_sem)

        def per_expert(e, carry):
            lo, hi = pad_lo_ref[e], pad_hi_ref[e]
            lax.fori_loop(lo, hi, lambda r, c: (pad_copy(r).start(), c)[1], 0)
            lax.fori_loop(lo, hi, lambda r, c: (pad_copy(r).wait(), c)[1], 0)
            return carry

        lax.fori_loop(0, pad_lo_ref.shape[0], per_expert, 0)

        buf[...] = jnp.zeros_like(buf)

    i = pl.program_id(0)
    n_chunks = dst_ref.shape[1]
    chunk = SEG_CHUNK * TILE_ROWS
    slot_rows = n_chunks * chunk
    slot = i % 2
    base = slot * slot_rows

    def place(t, carry):
        tile = hn_ref[pl.ds(pl.multiple_of(t * TILE_ROWS, TILE_ROWS), TILE_ROWS), :]
        for j in range(TOP_K):
            buf[pl.ds(pl.multiple_of(base + li_ref[0, t * TOP_K + j], TILE_ROWS), TILE_ROWS), :] = tile
        return carry

    lax.fori_loop(0, tm, place, 0, unroll=8)

    def slot_wait(sl):
        whole = buf.at[pl.ds(pl.multiple_of(sl * slot_rows, TILE_ROWS), slot_rows)]
        pltpu.make_async_copy(whole, whole, sem.at[sl]).wait()

    @pl.when(i > 0)
    def _():
        slot_wait(1 - slot)

    def start(q, carry):
        pltpu.make_async_copy(buf.at[pl.ds(pl.multiple_of(base + q * chunk, TILE_ROWS), chunk)],
                              xs_hbm.at[pl.ds(pl.multiple_of(dst_ref[0, q], TILE_ROWS), chunk)],
                              sem.at[slot]).start(priority=1)
        return carry

    lax.fori_loop(0, n_chunks, start, 0, unroll=8)

    @pl.when(i == pl.num_programs(0) - 1)
    def _():
        slot_wait(slot)


def _dispatch(hn_tiles, chunk_dst, local_rows, pad_lo, pad_hi, n_rows):
    t = hn_tiles.shape[0] // TILE_ROWS
    tm = COMBINE_TM
    steps = t // tm
    n_chunks = chunk_dst.shape[-1]
    smem = lambda w: pl.BlockSpec((None, 1, w), lambda i, lo, hi: (i, 0, 0), memory_space=pltpu.SMEM)
    grid_spec = pltpu.PrefetchScalarGridSpec(
        num_scalar_prefetch=2,
        grid=(steps,),
        in_specs=[smem(n_chunks), smem(tm * TOP_K), pl.BlockSpec((tm * TILE_ROWS, LANES), lambda i, lo, hi: (i, 0))],
        out_specs=pl.BlockSpec(memory_space=pl.ANY),
        scratch_shapes=[pltpu.VMEM((2 * n_chunks * SEG_CHUNK * TILE_ROWS, LANES), F32),
                        pltpu.VMEM((TILE_ROWS, LANES), F32), pltpu.SemaphoreType.DMA((2,)),
                        pltpu.SemaphoreType.DMA(())],
    )
    return pl.pallas_call(
        functools.partial(_dispatch_kernel, tm=tm),
        grid_spec=grid_spec,
        out_shape=jax.ShapeDtypeStruct((n_rows * TILE_ROWS, LANES), F32),
        compiler_params=_params("arbitrary"),
        name="moe_dispatch",
    )(pad_lo, pad_hi, chunk_dst.reshape(steps, 1, n_chunks), local_rows.reshape(steps, 1, tm * TOP_K), hn_tiles)


def _combine_seg_kernel(src_cur_ref, src_nxt_ref, li_ref, gate_ref, x_ref, g_ref, ys_hbm, o_ref, buf, acc_scr, sem,
                        *, tm, final):
    i = pl.program_id(0)
    n_chunks = src_cur_ref.shape[1]
    chunk = SEG_CHUNK * TILE_ROWS
    slot_rows = n_chunks * chunk

    def fetch(src_ref, slot):
        def start(q, carry):
            pltpu.make_async_copy(ys_hbm.at[pl.ds(pl.multiple_of(src_ref[0, q], TILE_ROWS), chunk)],
                                  buf.at[pl.ds(pl.multiple_of(slot * slot_rows + q * chunk, TILE_ROWS), chunk)],
                                  sem.at[slot]).start(priority=1)
            return carry
        lax.fori_loop(0, n_chunks, start, 0, unroll=8)

    @pl.when(i == 0)
    def _():
        fetch(src_cur_ref, 0)

    @pl.when(i + 1 < pl.num_programs(0))
    def _():
        fetch(src_nxt_ref, (i + 1) % 2)

    slot = i % 2
    base = slot * slot_rows
    whole = buf.at[pl.ds(pl.multiple_of(base, TILE_ROWS), slot_rows)]
    pltpu.make_async_copy(whole, whole, sem.at[slot]).wait()

    def token(t, carry):
        acc = jnp.zeros((TILE_ROWS, LANES), F32)
        for j in range(TOP_K):
            s = t * TOP_K + j
            row = pl.multiple_of(base + li_ref[0, s], TILE_ROWS)
            acc = acc + gate_ref[0, s] * buf[pl.ds(row, TILE_ROWS), :]
        acc_scr[pl.ds(pl.multiple_of(t * TILE_ROWS, TILE_ROWS), TILE_ROWS), :] = acc
        return carry

    lax.fori_loop(0, tm, token, 0, unroll=8)
    y = x_ref[...] + _load_token_tiles(acc_scr, tm)
    o_ref[...] = _rms(y, g_ref[...]) if final else y


def _combine_segments(x2d, gates, chunk_src, li, y_sorted, g, final):
    t, d = x2d.shape
    tm = COMBINE_TM
    steps = t // tm
    n_chunks = chunk_src.shape[-1]
    src3 = chunk_src.reshape(steps, 1, n_chunks)
    smem = lambda w, imap: pl.BlockSpec((None, 1, w), imap, memory_space=pltpu.SMEM)
    return pl.pallas_call(
        functools.partial(_combine_seg_kernel, tm=tm, final=final),
        grid=(steps,),
        in_specs=[smem(n_chunks, lambda i: (i, 0, 0)),
                  smem(n_chunks, lambda i: (jnp.minimum(i + 1, steps - 1), 0, 0)),
                  smem(tm * TOP_K, lambda i: (i, 0, 0)), smem(tm * TOP_K, lambda i: (i, 0, 0)),
                  pl.BlockSpec((tm, d), lambda i: (i, 0)),
                  pl.BlockSpec((1, d), lambda i: (0, 0)), pl.BlockSpec(memory_space=pl.ANY)],
        out_specs=pl.BlockSpec((tm, d), lambda i: (i, 0)),
        out_shape=jax.ShapeDtypeStruct((t, d), F32),
        scratch_shapes=[pltpu.VMEM((2 * n_chunks * SEG_CHUNK * TILE_ROWS, LANES), F32),
                        pltpu.VMEM((tm * TILE_ROWS, LANES), F32), pltpu.SemaphoreType.DMA((2,))],
        compiler_params=_params("arbitrary"),
        name="combine_norm",
    )(src3, src3, li.reshape(steps, 1, tm * TOP_K), gates.reshape(steps, 1, tm * TOP_K), x2d, g, y_sorted)


def _layer(x, mem, lyr, final, p):
    b, s, d = x.shape
    t = b * s
    c = RWKV_W
    rwkv_in = 3 * c + LORA_W

    w_in = p["w_in"][lyr]
    w_cat = jnp.concatenate([w_in[:, rwkv_in:], w_in[:, :3 * c], w_in[:, 3 * c:rwkv_in],
                             jnp.zeros((d, LORA_PAD - LORA_W), F32)], axis=1).astype(BF16)
    diff_cols = 3 * DIFF_W
    proj_diff, proj_rwkv = _norm_inproj(x.reshape(t, d), p["norm_mix_g"][lyr][None], w_cat, diff_cols)
    proj_diff, proj_rwkv = proj_diff.reshape(b, s, -1), proj_rwkv.reshape(b, s, -1)

    mu = p["shift_mu"][lyr]
    mu_l = jnp.zeros((1, LORA_PAD), F32).at[0, :LORA_W].set(mu[3 * c:])
    wl = jnp.zeros((LORA_PAD, 3 * c), F32)
    wl = wl.at[:DECAY_LORA, :c].set(p["w_decay_up"][lyr])
    wl = wl.at[DECAY_LORA:DECAY_LORA + ICLR_LORA, c:2 * c].set(p["w_iclr_up"][lyr])
    wl = wl.at[DECAY_LORA + ICLR_LORA:LORA_W, 2 * c:].set(p["w_gate_up"][lyr])
    vec = jnp.stack([p["w0"][lyr], p["a0"][lyr], p["k_k"][lyr], p["k_a"][lyr], p["r_k"][lyr].reshape(c),
                     p["lnx_g"][lyr], p["lnx_b"][lyr], jnp.zeros((c,), F32)])
    y_rwkv = _rwkv_group(proj_rwkv, mu[None, :3 * c], mu_l, wl.astype(BF16), vec,
                         rkv_block=0, lora_block=3 * c // LORA_PAD)

    lambda_init = 0.8 - 0.6 * math.exp(-0.3 * lyr)
    lam_vecs = jnp.stack([p["lambda_q1"][lyr], p["lambda_k1"][lyr], p["lambda_q2"][lyr], p["lambda_k2"][lyr]])
    y_diff = _diff_attention(proj_diff, lam_vecs, p["subln_g"][lyr][None], lambda_init)

    k_mem, v_mem = _mem_kv(mem, p["norm_mem_g"][lyr][None], p["w_ckv"][lyr].astype(BF16))
    x2, hn, route_i, route_g, counts = _mid_stage(
        x, y_rwkv, y_diff, p["w_out"][lyr], p["norm_cross_g"][lyr][None], p["w_cq"][lyr], k_mem, v_mem,
        p["w_co"][lyr], p["norm_ffn_g"][lyr][None], p["w_router"][lyr], p["b_router"][lyr])

    e = N_EXPERTS
    r = MOE_ROWS
    n_blocks = (t * TOP_K) // r + e + 1
    n_slots = COMBINE_TM * TOP_K // SEG_CHUNK + e
    n_rows = n_blocks * r + n_slots * SEG_CHUNK
    route_i = route_i.reshape(t, LANES)
    idx, rank = route_i[:, :TOP_K], route_i[:, TOP_K:2 * TOP_K]
    cnt = counts[0, :e].astype(I32)
    padded = (cnt + SEG_CHUNK - 1 + r - 1) // r * r
    pad_end = jnp.cumsum(padded)
    pad_start = pad_end - padded
    block_start = jnp.arange(n_blocks, dtype=I32) * r
    block_e = jnp.minimum(jnp.sum((block_start[:, None] >= pad_end[None, :]).astype(I32), axis=1), e - 1)
    n_used = pad_end[-1:] // r

    onehot = idx[:, :, None] == jnp.arange(e, dtype=I32)
    tmc = COMBINE_TM
    nt = t // tmc
    per_tile = jnp.sum(onehot.reshape(nt, tmc * TOP_K, e), axis=1, dtype=I32)
    before = jnp.cumsum(per_tile, axis=0) - per_tile
    n_chunks = (per_tile + SEG_CHUNK - 1) // SEG_CHUNK
    chunk_end = jnp.cumsum(n_chunks, axis=1)
    chunk_first = chunk_end - n_chunks
    shift = jnp.repeat(chunk_first * SEG_CHUNK - before, tmc, axis=0)
    local_row = ((jnp.sum(jnp.where(onehot, shift[:, None, :], 0), axis=-1) + rank) * TILE_ROWS).reshape(-1)
    slots = jnp.arange(n_slots, dtype=I32)
    owner = jnp.sum((slots[None, :, None] >= chunk_end[:, None, :]).astype(I32), axis=-1)
    own = owner[:, :, None] == jnp.arange(e, dtype=I32)
    seg_row = jnp.sum(jnp.where(own, (pad_start[None, :] + before - chunk_first * SEG_CHUNK)[:, None, :], 0), axis=-1)
    chunk_row = seg_row + slots[None, :] * SEG_CHUNK
    chunk_dst = jnp.where(owner < e, chunk_row, n_blocks * r + slots[None, :] * SEG_CHUNK) * TILE_ROWS
    chunk_src = jnp.where(owner < e, chunk_row, 0) * TILE_ROWS

    unused_lo = jnp.concatenate([pad_start + cnt, pad_end[-1:]])
    unused_hi = jnp.concatenate([pad_end, jnp.full((1,), n_rows, I32)])
    x_sorted = _dispatch(hn, chunk_dst, local_row, unused_lo, unused_hi, n_rows)
    y_sorted = _expert_ffn(x_sorted, block_e, padded // r, n_used, p["w1"][lyr], p["b1"][lyr], p["w2"][lyr],
                           p["b2"][lyr])
    gates = route_g.reshape(t, LANES)[:, :TOP_K]
    return _combine_segments(x2.reshape(t, d), gates, chunk_src, local_row, y_sorted,
                             p["norm_final_g"][None], final).reshape(b, s, d)


def kernel(x, mem, norm_mix_g, w_in, shift_mu, w0, w_decay_up, a0, w_iclr_up, w_gate_up, k_k, k_a, r_k,
           lnx_g, lnx_b, lambda_q1, lambda_k1, lambda_q2, lambda_k2, subln_g, w_out, norm_cross_g,
           norm_mem_g, w_cq, w_ckv, w_co, norm_ffn_g, w_router, b_router, w1, b1, w2, b2, norm_final_g):
    p = dict(norm_mix_g=norm_mix_g, w_in=w_in, shift_mu=shift_mu, w0=w0, w_decay_up=w_decay_up, a0=a0,
             w_iclr_up=w_iclr_up, w_gate_up=w_gate_up, k_k=k_k, k_a=k_a, r_k=r_k, lnx_g=lnx_g, lnx_b=lnx_b,
             lambda_q1=lambda_q1, lambda_k1=lambda_k1, lambda_q2=lambda_q2, lambda_k2=lambda_k2,
             subln_g=subln_g, w_out=w_out, norm_cross_g=norm_cross_g, norm_mem_g=norm_mem_g, w_cq=w_cq,
             w_ckv=w_ckv, w_co=w_co, norm_ffn_g=norm_ffn_g, w_router=w_router, b_router=b_router,
             w1=w1, b1=b1, w2=w2, b2=b2, norm_final_g=norm_final_g)
    depth = w_in.shape[0]
    for lyr in range(depth):
        x = _layer(x, mem, lyr, lyr == depth - 1, p)
    return x
```

```python
import functools
import math

import jax
import jax.numpy as jnp
from jax import lax
from jax.experimental import pallas as pl
from jax.experimental.pallas import tpu as pltpu

F32 = jnp.float32
BF16 = jnp.bfloat16
I32 = jnp.int32

NORM_EPS = 1e-5
HEAD = 64
RWKV_W = 512
RWKV_HEADS = RWKV_W // HEAD
DECAY_LORA, ICLR_LORA, GATE_LORA = 64, 64, 160
LORA_W = DECAY_LORA + ICLR_LORA + GATE_LORA
LORA_PAD = 384
RWKV_GN_EPS = 64e-5
DIFF_W = 512
DIFF_HEADS = DIFF_W // (2 * HEAD)
CROSS_HEADS = 4
N_EXPERTS = 32
TOP_K = 4
SWIGLU_LIMIT = 7.0
SWIGLU_ALPHA = 1.702
LANES = 128
TILE_ROWS = 8
NEG_BIG = -1e30

RWKV_CHUNK = 64
RWKV_BATCH = 4
PROJ_TM = 512
ATT_BLOCK = 512
ATT_ROW_CHUNK = 128
ATT_KEY_GROUP = 4
MID_TM = 512
MID_CHAIN_ROWS = 256
MOE_ROWS = 256
COMBINE_TM = 256
SEG_CHUNK = 8
ZERO_FILL_ROWS = (64, 8, 1)
VMEM_LIMIT = 56 * 1024 * 1024


def _dot(a, b):
    return jnp.dot(a.astype(BF16), b.astype(BF16), preferred_element_type=F32)


def _dot_nt(a, b):
    return lax.dot_general(a.astype(BF16), b.astype(BF16), (((1,), (1,)), ((), ())),
                           preferred_element_type=F32)


def _dot_tn(a, b):
    return lax.dot_general(a.astype(BF16), b.astype(BF16), (((0,), (0,)), ((), ())),
                           preferred_element_type=F32)


def _split2(x):
    hi = x.astype(BF16)
    lo = (x - hi.astype(F32)).astype(BF16)
    return hi, lo


def _dot_exact_rhs(x, ones_bf16):
    hi, lo = _split2(x)
    return jnp.dot(hi, ones_bf16, preferred_element_type=F32) + jnp.dot(lo, ones_bf16, preferred_element_type=F32)


def _dot_exact_lhs(ones_bf16, x):
    hi, lo = _split2(x)
    return jnp.dot(ones_bf16, hi, preferred_element_type=F32) + jnp.dot(ones_bf16, lo, preferred_element_type=F32)


def _store_token_tiles(ref, x, lead=()):
    rows = x.shape[0]
    for c in range(TILE_ROWS):
        ref[lead + (pl.ds(c, rows, stride=TILE_ROWS), slice(None))] = x[:, c * LANES:(c + 1) * LANES]


def _load_token_tiles(ref, rows, lead=(), first=0):
    return jnp.concatenate([ref[lead + (pl.ds(first * TILE_ROWS + c, rows, stride=TILE_ROWS), slice(None))]
                            for c in range(TILE_ROWS)], axis=1)


def _rms(x, g):
    return x * lax.rsqrt(jnp.mean(x * x, axis=-1, keepdims=True) + NORM_EPS) * g


def _params(*sem):
    return pltpu.CompilerParams(dimension_semantics=sem, vmem_limit_bytes=VMEM_LIMIT)


def _norm_inproj_kernel(x_ref, g_ref, w_ref, od_ref, or_ref):
    h = _rms(x_ref[...], g_ref[...]).astype(BF16)
    nd = od_ref.shape[1]
    od_ref[...] = jnp.dot(h, w_ref[:, :nd], preferred_element_type=F32).astype(BF16)
    or_ref[...] = jnp.dot(h, w_ref[:, nd:], preferred_element_type=F32)


def _norm_inproj(x2d, g, w_bf16, n_diff):
    t, d = x2d.shape
    n = w_bf16.shape[1]
    tm = min(PROJ_TM, t)
    return pl.pallas_call(
        _norm_inproj_kernel,
        grid=(t // tm,),
        in_specs=[pl.BlockSpec((tm, d), lambda i: (i, 0)),
                  pl.BlockSpec((1, d), lambda i: (0, 0)),
                  pl.BlockSpec((d, n), lambda i: (0, 0))],
        out_specs=[pl.BlockSpec((tm, n_diff), lambda i: (i, 0)), pl.BlockSpec((tm, n - n_diff), lambda i: (i, 0))],
        out_shape=[jax.ShapeDtypeStruct((t, n_diff), BF16), jax.ShapeDtypeStruct((t, n - n_diff), F32)],
        compiler_params=_params("parallel"),
        name="norm_inproj",
    )(x2d, g, w_bf16)


def _rwkv_kernel(rkv_ref, lora_ref, mu_rkv_ref, mu_l_ref, wl_ref, vec_ref, bd_ref, tri_ref,
                 o_ref, st_ref, prev_rkv_ref, prev_l_ref):
    L = RWKV_CHUNK
    C = RWKV_W

    @pl.when(pl.program_id(1) == 0)
    def _():
        st_ref[...] = jnp.zeros_like(st_ref)
        prev_rkv_ref[...] = jnp.zeros_like(prev_rkv_ref)
        prev_l_ref[...] = jnp.zeros_like(prev_l_ref)

    row = lax.broadcasted_iota(I32, (L, 1), 0)
    w0, a0, k_k, k_a = vec_ref[0:1, :], vec_ref[1:2, :], vec_ref[2:3, :], vec_ref[3:4, :]
    r_k, lnx_g, lnx_b = vec_ref[4:5, :], vec_ref[5:6, :], vec_ref[6:7, :]
    bd = bd_ref[...]
    tri = tri_ref[...]
    lane_l = lax.broadcasted_iota(I32, (L, LORA_PAD), 1)
    nb = rkv_ref.shape[0]

    def token_shift(raw, prev_ref, bi, mu):
        prev = jnp.where(row == 0, prev_ref[bi], pltpu.roll(raw, 1, axis=0))
        prev_ref[bi] = raw[L - 1:L, :]
        return raw + mu * (prev - raw)

    def elementwise(bi):
        u = token_shift(rkv_ref[bi], prev_rkv_ref, bi, mu_rkv_ref[...])
        ul = token_shift(lora_ref[bi], prev_l_ref, bi, mu_l_ref[...])
        r, k, v = u[:, :C], u[:, C:2 * C], u[:, 2 * C:]
        act = jnp.where(lane_l < DECAY_LORA, jnp.tanh(ul),
                        jnp.where(lane_l < DECAY_LORA + ICLR_LORA, ul, jax.nn.sigmoid(ul)))
        lo = _dot(act, wl_ref[...])
        w_log = -jax.nn.softplus(-(w0 + lo[:, :C])) - 0.5
        lw = -jnp.exp(w_log)
        a = jax.nn.sigmoid(a0 + lo[:, C:2 * C])
        kk = k * k_k
        kk = kk / jnp.maximum(jnp.sqrt(_dot(kk * kk, bd)), 1e-12)
        k2 = k * (1.0 + (a - 1.0) * k_a)
        cum = _dot_exact_lhs(tri, lw)
        total = cum[L - 1:L, :]
        e_neg = jnp.exp(-cum)
        e_rem = jnp.exp(total - cum)
        kka = kk * a
        return dict(r=r, v=v, k2=k2, g=lo[:, 2 * C:], a_t=-kk * jnp.exp(cum - lw), b_t=kka * e_neg,
                    k_t=k2 * e_neg, r_t=r * jnp.exp(cum), b_bar=kka * e_rem, k_bar=k2 * e_rem,
                    p_total=jnp.exp(total))

    ew = [elementwise(bi) for bi in range(nb)]

    m0 = lax.broadcasted_iota(I32, (L, LANES), 1) < HEAD
    r2i = lax.broadcasted_iota(I32, (2 * L, 4 * L), 0)
    c2i = lax.broadcasted_iota(I32, (2 * L, 4 * L), 1) % (2 * L)
    strict, incl = c2i < r2i, c2i <= r2i
    npairs = RWKV_HEADS // 2
    chains = [(bi, p) for bi in range(nb) for p in range(npairs)]
    nc = range(len(chains))

    def stack2(name):
        out = []
        for bi, p in chains:
            xp = ew[bi][name][:, p * LANES:(p + 1) * LANES]
            out.append(jnp.concatenate([jnp.where(m0, xp, 0.0), jnp.where(m0, 0.0, xp)], axis=0))
        return out

    A2, B2, K2, R2 = stack2("a_t"), stack2("b_t"), stack2("k_t"), stack2("r_t")
    V2, Bb2, Kb2 = stack2("v"), stack2("b_bar"), stack2("k_bar")
    mg = [_dot_nt(jnp.concatenate([A2[c], R2[c]], axis=0), jnp.concatenate([B2[c], K2[c]], axis=0)) for c in nc]
    m_cat = [jnp.where(strict, mg[c][:2 * L], 0.0) for c in nc]
    g_cat = [jnp.where(incl, mg[c][2 * L:], 0.0) for c in nc]
    m_ab = [m_cat[c][:, :LANES] for c in nc]
    pw = [_dot(m_ab[c], m_ab[c]) for c in nc]
    tr = list(m_ab)
    k = 2
    while 2 * k < L:
        x = [_dot(pw[c], jnp.concatenate([pw[c], tr[c]], axis=1)) for c in nc]
        tr = [tr[c] + pw[c] + x[c][:, LANES:] for c in nc]
        pw = [x[c][:, :LANES] for c in nc]
        k *= 2
    tr = [tr[c] + pw[c] + _dot(pw[c], tr[c]) for c in nc]
    az = [jnp.concatenate([A2[c], _dot(m_cat[c][:, LANES:], V2[c])], axis=1) for c in nc]
    tz = [az[c] + _dot(tr[c], az[c]) for c in nc]
    st = [st_ref[bi, p] for bi, p in chains]
    u2 = [_dot_nt(tz[c][:, :LANES], st[c]) + tz[c][:, LANES:] for c in nc]
    uv = [jnp.concatenate([u2[c], V2[c]], axis=0) for c in nc]
    y2 = [_dot_nt(R2[c], st[c]) + _dot(g_cat[c], uv[c]) for c in nc]
    for c, (bi, p) in enumerate(chains):
        st_ref[bi, p] = (st[c] * ew[bi]["p_total"][:, p * LANES:(p + 1) * LANES]
                         + _dot_tn(uv[c], jnp.concatenate([Bb2[c], Kb2[c]], axis=0)))

    inv_n = 1.0 / HEAD
    for bi in range(nb):
        y = jnp.concatenate([y2[bi * npairs + p][:L] + y2[bi * npairs + p][L:] for p in range(npairs)], axis=1)
        e = ew[bi]
        mu = _dot_exact_rhs(y, bd) * inv_n
        d = y - mu
        var = _dot_exact_rhs(d * d, bd) * inv_n
        yn = d * lax.rsqrt(var + RWKV_GN_EPS) * lnx_g + lnx_b
        bonus = _dot_exact_rhs(e["r"] * e["k2"] * r_k, bd) * e["v"]
        o_ref[bi] = (yn + bonus) * e["g"]


def _rwkv_group(proj3, mu_rkv, mu_l, wl, vec, rkv_block, lora_block):
    b, s, _ = proj3.shape
    L, C = RWKV_CHUNK, RWKV_W
    head = jnp.arange(C, dtype=I32) // HEAD
    bd = (head[:, None] == head[None, :]).astype(BF16)
    t = jnp.arange(L, dtype=I32)
    tri = (t[None, :] <= t[:, None]).astype(BF16)
    const = lambda shape: pl.BlockSpec(shape, lambda i, j: (0,) * len(shape))
    nb = RWKV_BATCH if b % RWKV_BATCH == 0 else 1
    return pl.pallas_call(
        _rwkv_kernel,
        grid=(b // nb, s // L),
        in_specs=[pl.BlockSpec((nb, L, 3 * C), lambda i, j: (i, j, rkv_block)),
                  pl.BlockSpec((nb, L, LORA_PAD), lambda i, j: (i, j, lora_block)),
                  const((1, 3 * C)), const((1, LORA_PAD)), const((LORA_PAD, 3 * C)),
                  const((8, C)), const((C, C)), const((L, L))],
        out_specs=pl.BlockSpec((nb, L, C), lambda i, j: (i, j, 0)),
        out_shape=jax.ShapeDtypeStruct((b, s, C), F32),
        scratch_shapes=[pltpu.VMEM((nb, RWKV_HEADS // 2, LANES, LANES), F32),
                        pltpu.VMEM((nb, 1, 3 * C), F32), pltpu.VMEM((nb, 1, LORA_PAD), F32)],
        compiler_params=_params("parallel", "arbitrary"),
        name="rwkv_group",
    )(proj3, proj3, mu_rkv, mu_l, wl, vec, bd, tri)


def _diff_attn_kernel(qi_ref, kj_ref, diag_ref, q_ref, k_ref, v_ref, slope_ref, lam_ref, g_ref, o_ref,
                      q2_scr, m_scr, acc_scr, *, lambda_init):
    blk = ATT_BLOCK
    group = k_ref.shape[0] // blk
    step_id = pl.program_id(2)
    qi, kj, diag = qi_ref[step_id], kj_ref[step_id], diag_ref[step_id]
    log2e = math.log2(math.e)

    @pl.when(kj == 0)
    def _():
        q = q_ref[...].astype(F32) * (HEAD ** -0.5 * log2e)
        m0 = lax.broadcasted_iota(I32, (blk, LANES), 1) < HEAD
        q2_scr[0:blk, :] = jnp.where(m0, q, 0.0).astype(BF16)
        q2_scr[blk:, :] = jnp.where(m0, 0.0, q).astype(BF16)
        m_scr[...] = jnp.full_like(m_scr, NEG_BIG)
        acc_scr[...] = jnp.zeros_like(acc_scr)

    def step(kh, on_diagonal):
        k = k_ref[kh * blk:(kh + 1) * blk, :].astype(BF16)
        v_ext = jnp.concatenate([v_ref[kh * blk:(kh + 1) * blk, :].astype(BF16), jnp.ones((blk, LANES), BF16)],
                                axis=1)
        koff = lax.broadcasted_iota(I32, (1, blk), 1)
        col_bias = (slope_ref[:, :1] * log2e) * ((kj * group + kh - qi) * blk + koff).astype(F32)
        rc = ATT_ROW_CHUNK
        n_chunks = 2 * blk // rc

        def scores(c):
            return lax.dot_general(q2_scr[c * rc:(c + 1) * rc, :], k, (((1,), (1,)), ((), ())),
                                   preferred_element_type=F32)

        s_next = scores(0)
        for c in range(n_chunks):
            rows = slice(c * rc, (c + 1) * rc)
            s = s_next + col_bias
            if c + 1 < n_chunks:
                s_next = scores(c + 1)
            if on_diagonal:
                qoff = (c * rc) % blk + lax.broadcasted_iota(I32, (rc, 1), 0)
                s = jnp.where(koff <= qoff, s, NEG_BIG)
            m_prev = m_scr[rows, :]
            m_new = jnp.maximum(m_prev, jnp.max(s, axis=-1, keepdims=True))
            alpha = jnp.exp2(m_prev - m_new)
            p = jnp.exp2(s - jnp.concatenate([m_new] * (blk // LANES), axis=1))
            acc_scr[rows, :] = (jnp.concatenate([alpha, alpha], axis=1) * acc_scr[rows, :]
                                + jnp.dot(p.astype(BF16), v_ext, preferred_element_type=F32))
            m_scr[rows, :] = m_new

    @pl.when(diag < 0)
    def _():
        for kh in range(group):
            step(kh, False)

    def last_step(n_full):
        for kh in range(n_full):
            step(kh, False)
        step(n_full, True)
        lam_v = lam_ref[...]
        lam = (jnp.exp(jnp.sum(lam_v[0:1] * lam_v[1:2], axis=-1, keepdims=True))
               - jnp.exp(jnp.sum(lam_v[2:3] * lam_v[3:4], axis=-1, keepdims=True)) + lambda_init)
        o2 = acc_scr[:, :LANES] / acc_scr[:, LANES:]
        o = o2[:blk] - lam * o2[blk:]
        o_ref[...] = _rms(o, g_ref[...]) * (1.0 - lambda_init)

    for n_full in range(group):
        pl.when(diag == n_full)(functools.partial(last_step, n_full))


def _diff_attention(proj3, lam_vecs, subln_g, lambda_init):
    b, s, _ = proj3.shape
    blk = ATT_BLOCK
    nb = s // blk
    group = math.gcd(ATT_KEY_GROUP, nb)
    nh = DIFF_HEADS
    slopes = jnp.exp2(-8.0 * jnp.arange(1, nh + 1, dtype=F32) / nh)
    slopes = jnp.broadcast_to(slopes[:, None, None], (nh, 1, LANES))
    steps = [(qi, kj, qi % group if kj == qi // group else -1) for qi in range(nb) for kj in range(qi // group + 1)]
    tabs = [jnp.asarray([st[i] for st in steps], I32) for i in range(3)]
    kernel = functools.partial(_diff_attn_kernel, lambda_init=lambda_init)
    grid_spec = pltpu.PrefetchScalarGridSpec(
        num_scalar_prefetch=3,
        grid=(b, nh, len(steps)),
        in_specs=[pl.BlockSpec((None, blk, LANES), lambda bi, h, t, qt, kt, dt: (bi, qt[t], h)),
                  pl.BlockSpec((None, group * blk, LANES), lambda bi, h, t, qt, kt, dt: (bi, kt[t], nh + h)),
                  pl.BlockSpec((None, group * blk, LANES), lambda bi, h, t, qt, kt, dt: (bi, kt[t], 2 * nh + h)),
                  pl.BlockSpec((None, 1, LANES), lambda bi, h, t, qt, kt, dt: (h, 0, 0)),
                  pl.BlockSpec((4, HEAD), lambda bi, h, t, qt, kt, dt: (0, 0)),
                  pl.BlockSpec((1, 2 * HEAD), lambda bi, h, t, qt, kt, dt: (0, 0))],
        out_specs=pl.BlockSpec((None, blk, LANES), lambda bi, h, t, qt, kt, dt: (bi, qt[t], h)),
        scratch_shapes=[pltpu.VMEM((2 * blk, LANES), BF16), pltpu.VMEM((2 * blk, LANES), F32),
                        pltpu.VMEM((2 * blk, 2 * LANES), F32)],
    )
    return pl.pallas_call(
        kernel,
        grid_spec=grid_spec,
        out_shape=jax.ShapeDtypeStruct((b, s, DIFF_W), F32),
        compiler_params=_params("parallel", "parallel", "arbitrary"),
        name="diff_attention",
    )(*tabs, proj3, proj3, proj3, slopes, lam_vecs, subln_g)


def _mem_kv_kernel(m_ref, g_ref, w_ref, k_ref, v_ref):
    d = m_ref.shape[-1]
    kv = _dot(_rms(m_ref[...], g_ref[...]), w_ref[...])
    k_ref[...] = kv[:, :d].astype(BF16)
    v_ref[...] = kv[:, d:].astype(BF16)


def _mem_kv(mem, g, w_ckv_bf16):
    b, m, d = mem.shape
    return pl.pallas_call(
        _mem_kv_kernel,
        grid=(b,),
        in_specs=[pl.BlockSpec((None, m, d), lambda i: (i, 0, 0)),
                  pl.BlockSpec((1, d), lambda i: (0, 0)),
                  pl.BlockSpec((d, 2 * d), lambda i: (0, 0))],
        out_specs=[pl.BlockSpec((None, m, d), lambda i: (i, 0, 0))] * 2,
        out_shape=[jax.ShapeDtypeStruct((b, m, d), BF16)] * 2,
        compiler_params=_params("parallel"),
        name="mem_kv",
    )(mem, g, w_ckv_bf16)


def _mid_kernel(x_ref, yr_ref, yd_ref, wo_ref, gc_ref, wcq_ref, km_ref, vm_ref, wco_ref, gf_ref,
                wr_hi_ref, wr_lo_ref, br_ref, tri_ref,
                x2_ref, hn_ref, ri_ref, rg_ref, cnt_ref, carry_scr):
    tm, d = x_ref.shape
    first = jnp.logical_and(pl.program_id(0) == 0, pl.program_id(1) == 0)

    @pl.when(first)
    def _():
        carry_scr[...] = jnp.zeros_like(carry_scr)

    rc = tri_ref.shape[0]
    chains = range(tm // rc)
    rows = [slice(c * rc, (c + 1) * rc) for c in chains]
    half = yr_ref.shape[-1]
    x1 = [x_ref[r, :] + _dot(yr_ref[r, :], wo_ref[:half, :]) + _dot(yd_ref[r, :], wo_ref[half:, :]) for r in rows]

    q = [_dot(_rms(x1[c], gc_ref[...]), wcq_ref[...]) for c in chains]
    hd = d // CROSS_HEADS
    outs = [[] for _ in chains]

    def scores(h):
        sl = slice(h * hd, (h + 1) * hd)
        return [_dot_nt(q[c][:, sl], km_ref[:, sl]) * (hd ** -0.5) for c in chains]

    s_next = scores(0)
    for h in range(CROSS_HEADS):
        sl = slice(h * hd, (h + 1) * hd)
        s = s_next
        if h + 1 < CROSS_HEADS:
            s_next = scores(h + 1)
        p = [jnp.exp(s[c] - jnp.max(s[c], axis=-1, keepdims=True)) for c in chains]
        p = [p[c] / jnp.sum(p[c], axis=-1, keepdims=True) for c in chains]
        for c in chains:
            outs[c].append(_dot(p[c], vm_ref[:, sl]))
    x2 = [x1[c] + _dot(jnp.concatenate(outs[c], axis=1), wco_ref[...]) for c in chains]
    for c in chains:
        x2_ref[rows[c], :] = x2[c]

    hn = [_rms(x2[c], gf_ref[...]) for c in chains]
    _store_token_tiles(hn_ref, jnp.concatenate(hn, axis=0))
    hi = [hn[c].astype(BF16) for c in chains]
    lo = [(hn[c] - hi[c].astype(F32)).astype(BF16) for c in chains]
    vals = [(jnp.dot(hi[c], wr_hi_ref[...], preferred_element_type=F32)
             + jnp.dot(lo[c], wr_hi_ref[...], preferred_element_type=F32)
             + jnp.dot(hi[c], wr_lo_ref[...], preferred_element_type=F32) + br_ref[...]) for c in chains]
    lane = lax.broadcasted_iota(I32, (rc, LANES), 1)
    tops, idxs, hots = [[] for _ in chains], [[] for _ in chains], [[] for _ in chains]
    for _ in range(TOP_K):
        for c in chains:
            mx = jnp.max(vals[c], axis=-1, keepdims=True)
            idx = jnp.min(jnp.where(vals[c] == mx, lane, LANES), axis=-1, keepdims=True)
            hot = lane == idx
            vals[c] = jnp.where(hot, -jnp.inf, vals[c])
            tops[c].append(mx)
            idxs[c].append(idx)
            hots[c].append(hot)
    for c in chains:
        es = [jnp.exp(t - tops[c][0]) for t in tops[c]]
        denom = es[0] + es[1] + es[2] + es[3]
        sel = jnp.zeros((rc, LANES), F32)
        for hot in hots[c]:
            sel = sel + hot.astype(F32)
        before = jnp.dot(tri_ref[...], sel.astype(BF16), preferred_element_type=F32) + carry_scr[...]
        carry_scr[...] = carry_scr[...] + jnp.sum(sel, axis=0, keepdims=True)
        ri = jnp.zeros((rc, LANES), I32)
        rg = jnp.zeros((rc, LANES), F32)
        for j in range(TOP_K):
            rank = jnp.sum(jnp.where(hots[c][j], before, 0.0), axis=-1, keepdims=True).astype(I32)
            ri = jnp.where(lane == j, idxs[c][j], ri)
            ri = jnp.where(lane == TOP_K + j, rank, ri)
            rg = jnp.where(lane == j, es[j] / denom, rg)
        ri_ref[rows[c], :] = ri
        rg_ref[rows[c], :] = rg
    cnt_ref[...] = jnp.broadcast_to(carry_scr[...], cnt_ref.shape)


def _mid_stage(x, y_rwkv, y_diff, w_out, g_cross, w_cq, k_mem, v_mem, w_co, g_ffn, w_router, b_router):
    b, s, d = x.shape
    tm = min(MID_TM, s)
    m = k_mem.shape[1]
    half = y_rwkv.shape[-1]
    e = w_router.shape[1]
    wr = jnp.zeros((d, LANES), F32).at[:, :e].set(w_router)
    wr_hi = wr.astype(BF16)
    wr_lo = (wr - wr_hi.astype(F32)).astype(BF16)
    br = jnp.full((1, LANES), NEG_BIG, F32).at[0, :e].set(b_router)
    rc = min(MID_CHAIN_ROWS, tm)
    t = jnp.arange(rc, dtype=I32)
    tri = (t[None, :] < t[:, None]).astype(BF16)
    tile = lambda w: pl.BlockSpec((None, tm, w), lambda i, j: (i, j, 0))
    const = lambda shape: pl.BlockSpec(shape, lambda i, j: (0,) * len(shape))
    return pl.pallas_call(
        _mid_kernel,
        grid=(b, s // tm),
        in_specs=[tile(d), tile(half), tile(half), const((d, d)), const((1, d)), const((d, d)),
                  pl.BlockSpec((None, m, d), lambda i, j: (i, 0, 0)),
                  pl.BlockSpec((None, m, d), lambda i, j: (i, 0, 0)),
                  const((d, d)), const((1, d)), const((d, LANES)), const((d, LANES)),
                  const((1, LANES)), const((rc, rc))],
        out_specs=[tile(d), pl.BlockSpec((tm * TILE_ROWS, LANES), lambda i, j: (i * (s // tm) + j, 0)),
                   tile(LANES), tile(LANES), const((8, LANES))],
        out_shape=[jax.ShapeDtypeStruct((b, s, d), F32), jax.ShapeDtypeStruct((b * s * TILE_ROWS, LANES), F32),
                   jax.ShapeDtypeStruct((b, s, LANES), I32), jax.ShapeDtypeStruct((b, s, LANES), F32),
                   jax.ShapeDtypeStruct((8, LANES), F32)],
        scratch_shapes=[pltpu.VMEM((1, LANES), F32)],
        compiler_params=_params("arbitrary", "arbitrary"),
        name="outproj_cross_router",
    )(x, y_rwkv, y_diff, w_out.astype(BF16), g_cross, w_cq.astype(BF16), k_mem, v_mem,
      w_co.astype(BF16), g_ffn, wr_hi, wr_lo, br, tri)


def _expert_kernel(be_ref, nblk_ref, nused_ref, x_ref, w1_hbm, b1_ref, w2_hbm, b2_ref, o_ref,
                   w1_stage, w2_stage, w1_scr, w2_scr, w2i_scr, sem):
    i = pl.program_id(0)
    n_used = nused_ref[0]
    used = i < n_used
    expert = be_ref[i]
    new_expert = jnp.logical_or(i == 0, expert != be_ref[jnp.maximum(i - 1, 0)])
    d, f = w1_stage.shape[0], w2_stage.shape[0]

    def weight_copies(ex):
        return (pltpu.make_async_copy(w1_hbm.at[ex], w1_stage, sem.at[0]),
                pltpu.make_async_copy(w2_hbm.at[ex], w2_stage, sem.at[1]))

    @pl.when(jnp.logical_and(used, i == 0))
    def _():
        for cp in weight_copies(expert):
            cp.start()

    @pl.when(jnp.logical_and(used, new_expert))
    def _():
        for cp in weight_copies(expert):
            cp.wait()
        step = 128

        def cast_rows(c, carry):
            rows = pl.ds(pl.multiple_of(c * step, step), step)
            w1_scr[rows, :] = w1_stage[rows, :].astype(BF16)
            return carry

        lax.fori_loop(0, d // step, cast_rows, 0)
        for cb in range(w2_stage.shape[1] // LANES):
            cols = slice(cb * LANES, (cb + 1) * LANES)
            w2i_scr[cb, pl.ds(0, f // 2, stride=2), :] = w2_stage[:f // 2, cols]
            w2i_scr[cb, pl.ds(1, f // 2, stride=2), :] = w2_stage[f // 2:, cols]
            w2_scr[:, cols] = w2i_scr[cb].astype(BF16)
        next_run = i + nblk_ref[expert]

        @pl.when(next_run < n_used)
        def _():
            for cp in weight_copies(be_ref[jnp.minimum(next_run, be_ref.shape[0] - 1)]):
                cp.start()

    @pl.when(used)
    def _():
        x = _load_token_tiles(x_ref, MOE_ROWS).astype(BF16)
        even = lax.broadcasted_iota(I32, (MOE_ROWS, LANES), 1) % 2 == 0
        cw = 2 * LANES

        def hidden_pair(j):
            lo, hi = slice(j * cw, (j + 1) * cw), slice(f + j * cw, f + (j + 1) * cw)
            return (jnp.dot(x, w1_scr[:, lo], preferred_element_type=F32) + b1_ref[:, lo],
                    jnp.dot(x, w1_scr[:, hi], preferred_element_type=F32) + b1_ref[:, hi])

        def act_block(h):
            lin = jnp.clip(pltpu.roll(h, LANES - 1, axis=1), -SWIGLU_LIMIT, SWIGLU_LIMIT)
            glu = jnp.minimum(h, SWIGLU_LIMIT)
            return jnp.where(even, glu * jax.nn.sigmoid(SWIGLU_ALPHA * glu) * (lin + 1.0), 0.0)

        y = b2_ref[...]
        pending = hidden_pair(0)
        for j in range(f // cw):
            ha, hb = pending
            if (j + 1) * cw < f:
                pending = hidden_pair(j + 1)
            packed = jnp.concatenate(
                [act_block(ha[:, c * LANES:(c + 1) * LANES])
                 + pltpu.roll(act_block(hb[:, c * LANES:(c + 1) * LANES]), 1, axis=1) for c in range(cw // LANES)],
                axis=1)
            y = y + jnp.dot(packed.astype(BF16), w2_scr[j * cw:(j + 1) * cw, :], preferred_element_type=F32)
        _store_token_tiles(o_ref, y)

    @pl.when(jnp.logical_not(used))
    def _():
        o_ref[...] = jnp.zeros_like(o_ref)


def _expert_ffn(x_sorted, block_e, blocks_per_expert, n_used, w1, b1, w2, b2):
    e, d, f2 = w1.shape
    r = MOE_ROWS
    n_rows = block_e.shape[0] * r
    grid_spec = pltpu.PrefetchScalarGridSpec(
        num_scalar_prefetch=3,
        grid=(n_rows // r,),
        in_specs=[pl.BlockSpec((r * TILE_ROWS, LANES), lambda i, be, nb, nu: (i, 0)),
                  pl.BlockSpec(memory_space=pl.ANY),
                  pl.BlockSpec((None, 1, f2), lambda i, be, nb, nu: (be[i], 0, 0)),
                  pl.BlockSpec(memory_space=pl.ANY),
                  pl.BlockSpec((None, 1, d), lambda i, be, nb, nu: (be[i], 0, 0))],
        out_specs=pl.BlockSpec((r * TILE_ROWS, LANES), lambda i, be, nb, nu: (i, 0)),
        scratch_shapes=[pltpu.VMEM((d, f2), F32), pltpu.VMEM((f2 // 2, d), F32),
                        pltpu.VMEM((d, f2), BF16), pltpu.VMEM((f2 // 2, d), BF16),
                        pltpu.VMEM((d // LANES, f2 // 2, LANES), F32), pltpu.SemaphoreType.DMA((2,))],
    )
    return pl.pallas_call(
        _expert_kernel,
        grid_spec=grid_spec,
        out_shape=jax.ShapeDtypeStruct((n_rows * TILE_ROWS, LANES), F32),
        compiler_params=_params("arbitrary"),
        name="expert_ffn",
    )(block_e, blocks_per_expert, n_used, x_sorted, w1, b1.reshape(e, 1, f2), w2, b2.reshape(e, 1, d))


def _dispatch_kernel(pad_lo_ref, pad_hi_ref, dst_ref, li_ref, hn_ref, xs_hbm, buf, zero_scr, sem, pad_sem, *, tm):
    @pl.when(pl.program_id(0) == 0)
    def _():
        zero_scr[...] = jnp.zeros_like(zero_scr)

        def per_range(e, carry):
            pos, hi = pad_lo_ref[e], pad_hi_ref[e]
            for size in ZERO_FILL_ROWS:
                n = lax.div(hi - pos, size)

                def pad_copy(k, pos=pos, size=size):
                    first = pl.multiple_of((pos + k * size) * TILE_ROWS, TILE_ROWS)
                    return pltpu.make_async_copy(zero_scr.at[pl.ds(0, size * TILE_ROWS)],
                                                 xs_hbm.at[pl.ds(first, size * TILE_ROWS)], pad_sem)

                lax.fori_loop(0, n, lambda k, c, cp=pad_copy: (cp(k).start(), c)[1], 0)
                lax.fori_loop(0, n, lambda k, c, cp=pad_copy: (cp(k).wait(), c)[1], 0)
                pos = pos + n * size
            return carry

        lax.fori_loop(0, pad_lo_ref.shape[0], per_range, 0)

        buf[...] = jnp.zeros_like(buf)

    i = pl.program_id(0)
    n_chunks = dst_ref.shape[1]
    chunk = SEG_CHUNK * TILE_ROWS
    slot_rows = n_chunks * chunk
    slot = i % 2
    base = slot * slot_rows

    def place(t, carry):
        tile = hn_ref[pl.ds(pl.multiple_of(t * TILE_ROWS, TILE_ROWS), TILE_ROWS), :]
        for j in range(TOP_K):
            buf[pl.ds(pl.multiple_of(base + li_ref[0, t * TOP_K + j], TILE_ROWS), TILE_ROWS), :] = tile
        return carry

    lax.fori_loop(0, tm, place, 0, unroll=8)

    def slot_wait(sl):
        whole = buf.at[pl.ds(pl.multiple_of(sl * slot_rows, TILE_ROWS), slot_rows)]
        pltpu.make_async_copy(whole, whole, sem.at[sl]).wait()

    @pl.when(i > 0)
    def _():
        slot_wait(1 - slot)

    def start(q, carry):
        pltpu.make_async_copy(buf.at[pl.ds(pl.multiple_of(base + q * chunk, TILE_ROWS), chunk)],
                              xs_hbm.at[pl.ds(pl.multiple_of(dst_ref[0, q], TILE_ROWS), chunk)],
                              sem.at[slot]).start(priority=1)
        return carry

    lax.fori_loop(0, n_chunks, start, 0, unroll=8)

    @pl.when(i == pl.num_programs(0) - 1)
    def _():
        slot_wait(slot)


def _dispatch(hn_tiles, chunk_dst, local_rows, pad_lo, pad_hi, n_rows):
    t = hn_tiles.shape[0] // TILE_ROWS
    tm = COMBINE_TM
    steps = t // tm
    n_chunks = chunk_dst.shape[-1]
    smem = lambda w: pl.BlockSpec((None, 1, w), lambda i, lo, hi: (i, 0, 0), memory_space=pltpu.SMEM)
    grid_spec = pltpu.PrefetchScalarGridSpec(
        num_scalar_prefetch=2,
        grid=(steps,),
        in_specs=[smem(n_chunks), smem(tm * TOP_K), pl.BlockSpec((tm * TILE_ROWS, LANES), lambda i, lo, hi: (i, 0))],
        out_specs=pl.BlockSpec(memory_space=pl.ANY),
        scratch_shapes=[pltpu.VMEM((2 * n_chunks * SEG_CHUNK * TILE_ROWS, LANES), F32),
                        pltpu.VMEM((ZERO_FILL_ROWS[0] * TILE_ROWS, LANES), F32), pltpu.SemaphoreType.DMA((2,)),
                        pltpu.SemaphoreType.DMA(())],
    )
    return pl.pallas_call(
        functools.partial(_dispatch_kernel, tm=tm),
        grid_spec=grid_spec,
        out_shape=jax.ShapeDtypeStruct((n_rows * TILE_ROWS, LANES), F32),
        compiler_params=_params("arbitrary"),
        name="moe_dispatch",
    )(pad_lo, pad_hi, chunk_dst.reshape(steps, 1, n_chunks), local_rows.reshape(steps, 1, tm * TOP_K), hn_tiles)


def _combine_seg_kernel(src_cur_ref, src_nxt_ref, li_ref, gate_ref, x_ref, g_ref, ys_hbm, o_ref, buf, acc_scr, sem,
                        *, tm, final):
    i = pl.program_id(0)
    n_chunks = src_cur_ref.shape[1]
    chunk = SEG_CHUNK * TILE_ROWS
    slot_rows = n_chunks * chunk

    def fetch(src_ref, slot):
        def start(q, carry):
            pltpu.make_async_copy(ys_hbm.at[pl.ds(pl.multiple_of(src_ref[0, q], TILE_ROWS), chunk)],
                                  buf.at[pl.ds(pl.multiple_of(slot * slot_rows + q * chunk, TILE_ROWS), chunk)],
                                  sem.at[slot]).start(priority=1)
            return carry
        lax.fori_loop(0, n_chunks, start, 0, unroll=8)

    @pl.when(i == 0)
    def _():
        fetch(src_cur_ref, 0)

    @pl.when(i + 1 < pl.num_programs(0))
    def _():
        fetch(src_nxt_ref, (i + 1) % 2)

    slot = i % 2
    base = slot * slot_rows
    whole = buf.at[pl.ds(pl.multiple_of(base, TILE_ROWS), slot_rows)]
    pltpu.make_async_copy(whole, whole, sem.at[slot]).wait()

    def token(t, carry):
        acc = jnp.zeros((TILE_ROWS, LANES), F32)
        for j in range(TOP_K):
            s = t * TOP_K + j
            row = pl.multiple_of(base + li_ref[0, s], TILE_ROWS)
            acc = acc + gate_ref[0, s] * buf[pl.ds(row, TILE_ROWS), :]
        acc_scr[pl.ds(pl.multiple_of(t * TILE_ROWS, TILE_ROWS), TILE_ROWS), :] = acc
        return carry

    lax.fori_loop(0, tm, token, 0, unroll=8)
    y = x_ref[...] + _load_token_tiles(acc_scr, tm)
    o_ref[...] = _rms(y, g_ref[...]) if final else y


def _combine_segments(x2d, gates, chunk_src, li, y_sorted, g, final):
    t, d = x2d.shape
    tm = COMBINE_TM
    steps = t // tm
    n_chunks = chunk_src.shape[-1]
    src3 = chunk_src.reshape(steps, 1, n_chunks)
    smem = lambda w, imap: pl.BlockSpec((None, 1, w), imap, memory_space=pltpu.SMEM)
    return pl.pallas_call(
        functools.partial(_combine_seg_kernel, tm=tm, final=final),
        grid=(steps,),
        in_specs=[smem(n_chunks, lambda i: (i, 0, 0)),
                  smem(n_chunks, lambda i: (jnp.minimum(i + 1, steps - 1), 0, 0)),
                  smem(tm * TOP_K, lambda i: (i, 0, 0)), smem(tm * TOP_K, lambda i: (i, 0, 0)),
                  pl.BlockSpec((tm, d), lambda i: (i, 0)),
                  pl.BlockSpec((1, d), lambda i: (0, 0)), pl.BlockSpec(memory_space=pl.ANY)],
        out_specs=pl.BlockSpec((tm, d), lambda i: (i, 0)),
        out_shape=jax.ShapeDtypeStruct((t, d), F32),
        scratch_shapes=[pltpu.VMEM((2 * n_chunks * SEG_CHUNK * TILE_ROWS, LANES), F32),
                        pltpu.VMEM((tm * TILE_ROWS, LANES), F32), pltpu.SemaphoreType.DMA((2,))],
        compiler_params=_params("arbitrary"),
        name="combine_norm",
    )(src3, src3, li.reshape(steps, 1, tm * TOP_K), gates.reshape(steps, 1, tm * TOP_K), x2d, g, y_sorted)


def _layer(x, mem, lyr, final, p):
    b, s, d = x.shape
    t = b * s
    c = RWKV_W
    rwkv_in = 3 * c + LORA_W

    w_in = p["w_in"][lyr]
    w_cat = jnp.concatenate([w_in[:, rwkv_in:], w_in[:, :3 * c], w_in[:, 3 * c:rwkv_in],
                             jnp.zeros((d, LORA_PAD - LORA_W), F32)], axis=1).astype(BF16)
    diff_cols = 3 * DIFF_W
    proj_diff, proj_rwkv = _norm_inproj(x.reshape(t, d), p["norm_mix_g"][lyr][None], w_cat, diff_cols)
    proj_diff, proj_rwkv = proj_diff.reshape(b, s, -1), proj_rwkv.reshape(b, s, -1)

    mu = p["shift_mu"][lyr]
    mu_l = jnp.zeros((1, LORA_PAD), F32).at[0, :LORA_W].set(mu[3 * c:])
    wl = jnp.zeros((LORA_PAD, 3 * c), F32)
    wl = wl.at[:DECAY_LORA, :c].set(p["w_decay_up"][lyr])
    wl = wl.at[DECAY_LORA:DECAY_LORA + ICLR_LORA, c:2 * c].set(p["w_iclr_up"][lyr])
    wl = wl.at[DECAY_LORA + ICLR_LORA:LORA_W, 2 * c:].set(p["w_gate_up"][lyr])
    vec = jnp.stack([p["w0"][lyr], p["a0"][lyr], p["k_k"][lyr], p["k_a"][lyr], p["r_k"][lyr].reshape(c),
                     p["lnx_g"][lyr], p["lnx_b"][lyr], jnp.zeros((c,), F32)])
    y_rwkv = _rwkv_group(proj_rwkv, mu[None, :3 * c], mu_l, wl.astype(BF16), vec,
                         rkv_block=0, lora_block=3 * c // LORA_PAD)

    lambda_init = 0.8 - 0.6 * math.exp(-0.3 * lyr)
    lam_vecs = jnp.stack([p["lambda_q1"][lyr], p["lambda_k1"][lyr], p["lambda_q2"][lyr], p["lambda_k2"][lyr]])
    y_diff = _diff_attention(proj_diff, lam_vecs, p["subln_g"][lyr][None], lambda_init)

    k_mem, v_mem = _mem_kv(mem, p["norm_mem_g"][lyr][None], p["w_ckv"][lyr].astype(BF16))
    x2, hn, route_i, route_g, counts = _mid_stage(
        x, y_rwkv, y_diff, p["w_out"][lyr], p["norm_cross_g"][lyr][None], p["w_cq"][lyr], k_mem, v_mem,
        p["w_co"][lyr], p["norm_ffn_g"][lyr][None], p["w_router"][lyr], p["b_router"][lyr])

    e = N_EXPERTS
    r = MOE_ROWS
    n_blocks = (t * TOP_K) // r + e + 1
    n_slots = COMBINE_TM * TOP_K // SEG_CHUNK + e
    n_rows = n_blocks * r + n_slots * SEG_CHUNK
    route_i = route_i.reshape(t, LANES)
    idx, rank = route_i[:, :TOP_K], route_i[:, TOP_K:2 * TOP_K]
    cnt = counts[0, :e].astype(I32)
    padded = (cnt + SEG_CHUNK - 1 + r - 1) // r * r
    pad_end = jnp.cumsum(padded)
    pad_start = pad_end - padded
    block_start = jnp.arange(n_blocks, dtype=I32) * r
    block_e = jnp.minimum(jnp.sum((block_start[:, None] >= pad_end[None, :]).astype(I32), axis=1), e - 1)
    n_used = pad_end[-1:] // r

    onehot = idx[:, :, None] == jnp.arange(e, dtype=I32)
    tmc = COMBINE_TM
    nt = t // tmc
    per_tile = jnp.sum(onehot.reshape(nt, tmc * TOP_K, e), axis=1, dtype=I32)
    before = jnp.cumsum(per_tile, axis=0) - per_tile
    n_chunks = (per_tile + SEG_CHUNK - 1) // SEG_CHUNK
    chunk_end = jnp.cumsum(n_chunks, axis=1)
    chunk_first = chunk_end - n_chunks
    shift = jnp.repeat(chunk_first * SEG_CHUNK - before, tmc, axis=0)
    local_row = ((jnp.sum(jnp.where(onehot, shift[:, None, :], 0), axis=-1) + rank) * TILE_ROWS).reshape(-1)
    slots = jnp.arange(n_slots, dtype=I32)
    owner = jnp.sum((slots[None, :, None] >= chunk_end[:, None, :]).astype(I32), axis=-1)
    own = owner[:, :, None] == jnp.arange(e, dtype=I32)
    seg_row = jnp.sum(jnp.where(own, (pad_start[None, :] + before - chunk_first * SEG_CHUNK)[:, None, :], 0), axis=-1)
    chunk_row = seg_row + slots[None, :] * SEG_CHUNK
    chunk_dst = jnp.where(owner < e, chunk_row, n_blocks * r + slots[None, :] * SEG_CHUNK) * TILE_ROWS
    chunk_src = jnp.where(owner < e, chunk_row, 0) * TILE_ROWS

    unused_lo = jnp.concatenate([pad_start + cnt, pad_end[-1:]])
    unused_hi = jnp.concatenate([pad_end, jnp.full((1,), n_rows, I32)])
    x_sorted = _dispatch(hn, chunk_dst, local_row, unused_lo, unused_hi, n_rows)
    y_sorted = _expert_ffn(x_sorted, block_e, padded // r, n_used, p["w1"][lyr], p["b1"][lyr], p["w2"][lyr],
                           p["b2"][lyr])
    gates = route_g.reshape(t, LANES)[:, :TOP_K]
    return _combine_segments(x2.reshape(t, d), gates, chunk_src, local_row, y_sorted,
                             p["norm_final_g"][None], final).reshape(b, s, d)


def kernel(x, mem, norm_mix_g, w_in, shift_mu, w0, w_decay_up, a0, w_iclr_up, w_gate_up, k_k, k_a, r_k,
           lnx_g, lnx_b, lambda_q1, lambda_k1, lambda_q2, lambda_k2, subln_g, w_out, norm_cross_g,
           norm_mem_g, w_cq, w_ckv, w_co, norm_ffn_g, w_router, b_router, w1, b1, w2, b2, norm_final_g):
    p = dict(norm_mix_g=norm_mix_g, w_in=w_in, shift_mu=shift_mu, w0=w0, w_decay_up=w_decay_up, a0=a0,
             w_iclr_up=w_iclr_up, w_gate_up=w_gate_up, k_k=k_k, k_a=k_a, r_k=r_k, lnx_g=lnx_g, lnx_b=lnx_b,
             lambda_q1=lambda_q1, lambda_k1=lambda_k1, lambda_q2=lambda_q2, lambda_k2=lambda_k2,
             subln_g=subln_g, w_out=w_out, norm_cross_g=norm_cross_g, norm_mem_g=norm_mem_g, w_cq=w_cq,
             w_ckv=w_ckv, w_co=w_co, norm_ffn_g=norm_ffn_g, w_router=w_router, b_router=b_router,
             w1=w1, b1=b1, w2=w2, b2=b2, norm_final_g=norm_final_g)
    depth = w_in.shape[0]
    for lyr in range(depth):
        x = _layer(x, mem, lyr, lyr == depth - 1, p)
    return x
```

```python
import functools
import math

import jax
import jax.numpy as jnp
from jax import lax
from jax.experimental import pallas as pl
from jax.experimental.pallas import tpu as pltpu

F32 = jnp.float32
BF16 = jnp.bfloat16
I32 = jnp.int32

NORM_EPS = 1e-5
HEAD = 64
RWKV_W = 512
RWKV_HEADS = RWKV_W // HEAD
DECAY_LORA, ICLR_LORA, GATE_LORA = 64, 64, 160
LORA_W = DECAY_LORA + ICLR_LORA + GATE_LORA
LORA_PAD = 384
RWKV_GN_EPS = 64e-5
DIFF_W = 512
DIFF_HEADS = DIFF_W // (2 * HEAD)
CROSS_HEADS = 4
N_EXPERTS = 32
TOP_K = 4
SWIGLU_LIMIT = 7.0
SWIGLU_ALPHA = 1.702
LANES = 128
TILE_ROWS = 8
NEG_BIG = -1e30

RWKV_CHUNK = 64
RWKV_BATCH = 4
PROJ_TM = 512
ATT_BLOCK = 512
ATT_ROW_CHUNK = 128
ATT_KEY_GROUP = 4
MID_TM = 512
MID_CHAIN_ROWS = 256
MOE_ROWS = 512
MOE_CHAIN_ROWS = 256
COMBINE_TM = 256
SEG_CHUNK = 8
ZERO_FILL_ROWS = (64, 8, 1)
VMEM_LIMIT = 56 * 1024 * 1024


def _dot(a, b):
    return jnp.dot(a.astype(BF16), b.astype(BF16), preferred_element_type=F32)


def _dot_nt(a, b):
    return lax.dot_general(a.astype(BF16), b.astype(BF16), (((1,), (1,)), ((), ())),
                           preferred_element_type=F32)


def _dot_tn(a, b):
    return lax.dot_general(a.astype(BF16), b.astype(BF16), (((0,), (0,)), ((), ())),
                           preferred_element_type=F32)


def _split2(x):
    hi = x.astype(BF16)
    lo = (x - hi.astype(F32)).astype(BF16)
    return hi, lo


def _dot_exact_rhs(x, ones_bf16):
    hi, lo = _split2(x)
    return jnp.dot(hi, ones_bf16, preferred_element_type=F32) + jnp.dot(lo, ones_bf16, preferred_element_type=F32)


def _dot_exact_lhs(ones_bf16, x):
    hi, lo = _split2(x)
    return jnp.dot(ones_bf16, hi, preferred_element_type=F32) + jnp.dot(ones_bf16, lo, preferred_element_type=F32)


def _store_token_tiles(ref, x, first=0):
    rows = x.shape[0]
    for c in range(TILE_ROWS):
        ref[pl.ds(first * TILE_ROWS + c, rows, stride=TILE_ROWS), :] = x[:, c * LANES:(c + 1) * LANES]


def _load_token_tiles(ref, rows, lead=(), first=0):
    return jnp.concatenate([ref[lead + (pl.ds(first * TILE_ROWS + c, rows, stride=TILE_ROWS), slice(None))]
                            for c in range(TILE_ROWS)], axis=1)


def _rms(x, g):
    return x * lax.rsqrt(jnp.mean(x * x, axis=-1, keepdims=True) + NORM_EPS) * g


def _params(*sem):
    return pltpu.CompilerParams(dimension_semantics=sem, vmem_limit_bytes=VMEM_LIMIT)


def _norm_inproj_kernel(x_ref, g_ref, w_ref, od_ref, or_ref):
    h = _rms(x_ref[...], g_ref[...]).astype(BF16)
    nd = od_ref.shape[1]
    od_ref[...] = jnp.dot(h, w_ref[:, :nd], preferred_element_type=F32).astype(BF16)
    or_ref[...] = jnp.dot(h, w_ref[:, nd:], preferred_element_type=F32)


def _norm_inproj(x2d, g, w_bf16, n_diff):
    t, d = x2d.shape
    n = w_bf16.shape[1]
    tm = min(PROJ_TM, t)
    return pl.pallas_call(
        _norm_inproj_kernel,
        grid=(t // tm,),
        in_specs=[pl.BlockSpec((tm, d), lambda i: (i, 0)),
                  pl.BlockSpec((1, d), lambda i: (0, 0)),
                  pl.BlockSpec((d, n), lambda i: (0, 0))],
        out_specs=[pl.BlockSpec((tm, n_diff), lambda i: (i, 0)), pl.BlockSpec((tm, n - n_diff), lambda i: (i, 0))],
        out_shape=[jax.ShapeDtypeStruct((t, n_diff), BF16), jax.ShapeDtypeStruct((t, n - n_diff), F32)],
        compiler_params=_params("parallel"),
        name="norm_inproj",
    )(x2d, g, w_bf16)


def _rwkv_kernel(rkv_ref, lora_ref, mu_rkv_ref, mu_l_ref, wl_ref, vec_ref, bd_ref, tri_ref,
                 o_ref, st_ref, prev_rkv_ref, prev_l_ref):
    L = RWKV_CHUNK
    C = RWKV_W

    @pl.when(pl.program_id(1) == 0)
    def _():
        st_ref[...] = jnp.zeros_like(st_ref)
        prev_rkv_ref[...] = jnp.zeros_like(prev_rkv_ref)
        prev_l_ref[...] = jnp.zeros_like(prev_l_ref)

    row = lax.broadcasted_iota(I32, (L, 1), 0)
    w0, a0, k_k, k_a = vec_ref[0:1, :], vec_ref[1:2, :], vec_ref[2:3, :], vec_ref[3:4, :]
    r_k, lnx_g, lnx_b = vec_ref[4:5, :], vec_ref[5:6, :], vec_ref[6:7, :]
    bd = bd_ref[...]
    tri = tri_ref[...]
    lane_l = lax.broadcasted_iota(I32, (L, LORA_PAD), 1)
    nb = rkv_ref.shape[0]

    def token_shift(raw, prev_ref, bi, mu):
        prev = jnp.where(row == 0, prev_ref[bi], pltpu.roll(raw, 1, axis=0))
        prev_ref[bi] = raw[L - 1:L, :]
        return raw + mu * (prev - raw)

    def elementwise(bi):
        u = token_shift(rkv_ref[bi], prev_rkv_ref, bi, mu_rkv_ref[...])
        ul = token_shift(lora_ref[bi], prev_l_ref, bi, mu_l_ref[...])
        r, k, v = u[:, :C], u[:, C:2 * C], u[:, 2 * C:]
        act = jnp.where(lane_l < DECAY_LORA, jnp.tanh(ul),
                        jnp.where(lane_l < DECAY_LORA + ICLR_LORA, ul, jax.nn.sigmoid(ul)))
        lo = _dot(act, wl_ref[...])
        w_log = -jax.nn.softplus(-(w0 + lo[:, :C])) - 0.5
        lw = -jnp.exp(w_log)
        a = jax.nn.sigmoid(a0 + lo[:, C:2 * C])
        kk = k * k_k
        kk = kk / jnp.maximum(jnp.sqrt(_dot(kk * kk, bd)), 1e-12)
        k2 = k * (1.0 + (a - 1.0) * k_a)
        cum = _dot_exact_lhs(tri, lw)
        total = cum[L - 1:L, :]
        e_neg = jnp.exp(-cum)
        e_rem = jnp.exp(total - cum)
        kka = kk * a
        return dict(r=r, v=v, k2=k2, g=lo[:, 2 * C:], a_t=-kk * jnp.exp(cum - lw), b_t=kka * e_neg,
                    k_t=k2 * e_neg, r_t=r * jnp.exp(cum), b_bar=kka * e_rem, k_bar=k2 * e_rem,
                    p_total=jnp.exp(total))

    ew = [elementwise(bi) for bi in range(nb)]

    m0 = lax.broadcasted_iota(I32, (L, LANES), 1) < HEAD
    r2i = lax.broadcasted_iota(I32, (2 * L, 4 * L), 0)
    c2i = lax.broadcasted_iota(I32, (2 * L, 4 * L), 1) % (2 * L)
    strict, incl = c2i < r2i, c2i <= r2i
    npairs = RWKV_HEADS // 2
    chains = [(bi, p) for bi in range(nb) for p in range(npairs)]
    nc = range(len(chains))

    def stack2(name):
        out = []
        for bi, p in chains:
            xp = ew[bi][name][:, p * LANES:(p + 1) * LANES]
            out.append(jnp.concatenate([jnp.where(m0, xp, 0.0), jnp.where(m0, 0.0, xp)], axis=0))
        return out

    A2, B2, K2, R2 = stack2("a_t"), stack2("b_t"), stack2("k_t"), stack2("r_t")
    V2, Bb2, Kb2 = stack2("v"), stack2("b_bar"), stack2("k_bar")
    mg = [_dot_nt(jnp.concatenate([A2[c], R2[c]], axis=0), jnp.concatenate([B2[c], K2[c]], axis=0)) for c in nc]
    m_cat = [jnp.where(strict, mg[c][:2 * L], 0.0) for c in nc]
    g_cat = [jnp.where(incl, mg[c][2 * L:], 0.0) for c in nc]
    m_ab = [m_cat[c][:, :LANES] for c in nc]
    pw = [_dot(m_ab[c], m_ab[c]) for c in nc]
    tr = list(m_ab)
    k = 2
    while 2 * k < L:
        x = [_dot(pw[c], jnp.concatenate([pw[c], tr[c]], axis=1)) for c in nc]
        tr = [tr[c] + pw[c] + x[c][:, LANES:] for c in nc]
        pw = [x[c][:, :LANES] for c in nc]
        k *= 2
    tr = [tr[c] + pw[c] + _dot(pw[c], tr[c]) for c in nc]
    az = [jnp.concatenate([A2[c], _dot(m_cat[c][:, LANES:], V2[c])], axis=1) for c in nc]
    tz = [az[c] + _dot(tr[c], az[c]) for c in nc]
    st = [st_ref[bi, p] for bi, p in chains]
    u2 = [_dot_nt(tz[c][:, :LANES], st[c]) + tz[c][:, LANES:] for c in nc]
    uv = [jnp.concatenate([u2[c], V2[c]], axis=0) for c in nc]
    y2 = [_dot_nt(R2[c], st[c]) + _dot(g_cat[c], uv[c]) for c in nc]
    for c, (bi, p) in enumerate(chains):
        st_ref[bi, p] = (st[c] * ew[bi]["p_total"][:, p * LANES:(p + 1) * LANES]
                         + _dot_tn(uv[c], jnp.concatenate([Bb2[c], Kb2[c]], axis=0)))

    inv_n = 1.0 / HEAD
    for bi in range(nb):
        y = jnp.concatenate([y2[bi * npairs + p][:L] + y2[bi * npairs + p][L:] for p in range(npairs)], axis=1)
        e = ew[bi]
        mu = _dot_exact_rhs(y, bd) * inv_n
        d = y - mu
        var = _dot_exact_rhs(d * d, bd) * inv_n
        yn = d * lax.rsqrt(var + RWKV_GN_EPS) * lnx_g + lnx_b
        bonus = _dot_exact_rhs(e["r"] * e["k2"] * r_k, bd) * e["v"]
        o_ref[bi] = (yn + bonus) * e["g"]


def _rwkv_group(proj3, mu_rkv, mu_l, wl, vec, rkv_block, lora_block):
    b, s, _ = proj3.shape
    L, C = RWKV_CHUNK, RWKV_W
    head = jnp.arange(C, dtype=I32) // HEAD
    bd = (head[:, None] == head[None, :]).astype(BF16)
    t = jnp.arange(L, dtype=I32)
    tri = (t[None, :] <= t[:, None]).astype(BF16)
    const = lambda shape: pl.BlockSpec(shape, lambda i, j: (0,) * len(shape))
    nb = RWKV_BATCH if b % RWKV_BATCH == 0 else 1
    return pl.pallas_call(
        _rwkv_kernel,
        grid=(b // nb, s // L),
        in_specs=[pl.BlockSpec((nb, L, 3 * C), lambda i, j: (i, j, rkv_block)),
                  pl.BlockSpec((nb, L, LORA_PAD), lambda i, j: (i, j, lora_block)),
                  const((1, 3 * C)), const((1, LORA_PAD)), const((LORA_PAD, 3 * C)),
                  const((8, C)), const((C, C)), const((L, L))],
        out_specs=pl.BlockSpec((nb, L, C), lambda i, j: (i, j, 0)),
        out_shape=jax.ShapeDtypeStruct((b, s, C), F32),
        scratch_shapes=[pltpu.VMEM((nb, RWKV_HEADS // 2, LANES, LANES), F32),
                        pltpu.VMEM((nb, 1, 3 * C), F32), pltpu.VMEM((nb, 1, LORA_PAD), F32)],
        compiler_params=_params("parallel", "arbitrary"),
        name="rwkv_group",
    )(proj3, proj3, mu_rkv, mu_l, wl, vec, bd, tri)


def _diff_attn_kernel(qi_ref, kj_ref, diag_ref, q_ref, k_ref, v_ref, slope_ref, lam_ref, g_ref, o_ref,
                      q2_scr, m_scr, acc_scr, *, lambda_init):
    blk = ATT_BLOCK
    group = k_ref.shape[0] // blk
    step_id = pl.program_id(2)
    qi, kj, diag = qi_ref[step_id], kj_ref[step_id], diag_ref[step_id]
    log2e = math.log2(math.e)

    @pl.when(kj == 0)
    def _():
        q = q_ref[...].astype(F32) * (HEAD ** -0.5 * log2e)
        m0 = lax.broadcasted_iota(I32, (blk, LANES), 1) < HEAD
        q2_scr[0:blk, :] = jnp.where(m0, q, 0.0).astype(BF16)
        q2_scr[blk:, :] = jnp.where(m0, 0.0, q).astype(BF16)
        m_scr[...] = jnp.full_like(m_scr, NEG_BIG)
        acc_scr[...] = jnp.zeros_like(acc_scr)

    def step(kh, on_diagonal):
        k = k_ref[kh * blk:(kh + 1) * blk, :].astype(BF16)
        v_ext = jnp.concatenate([v_ref[kh * blk:(kh + 1) * blk, :].astype(BF16), jnp.ones((blk, LANES), BF16)],
                                axis=1)
        koff = lax.broadcasted_iota(I32, (1, blk), 1)
        col_bias = (slope_ref[:, :1] * log2e) * ((kj * group + kh - qi) * blk + koff).astype(F32)
        rc = ATT_ROW_CHUNK
        n_chunks = 2 * blk // rc

        def scores(c):
            return lax.dot_general(q2_scr[c * rc:(c + 1) * rc, :], k, (((1,), (1,)), ((), ())),
                                   preferred_element_type=F32)

        s_next = scores(0)
        for c in range(n_chunks):
            rows = slice(c * rc, (c + 1) * rc)
            s = s_next + col_bias
            if c + 1 < n_chunks:
                s_next = scores(c + 1)
            if on_diagonal:
                qoff = (c * rc) % blk + lax.broadcasted_iota(I32, (rc, 1), 0)
                s = jnp.where(koff <= qoff, s, NEG_BIG)
            m_prev = m_scr[rows, :]
            m_new = jnp.maximum(m_prev, jnp.max(s, axis=-1, keepdims=True))
            alpha = jnp.exp2(m_prev - m_new)
            p = jnp.exp2(s - jnp.concatenate([m_new] * (blk // LANES), axis=1))
            acc_scr[rows, :] = (jnp.concatenate([alpha, alpha], axis=1) * acc_scr[rows, :]
                                + jnp.dot(p.astype(BF16), v_ext, preferred_element_type=F32))
            m_scr[rows, :] = m_new

    @pl.when(diag < 0)
    def _():
        for kh in range(group):
            step(kh, False)

    def last_step(n_full):
        for kh in range(n_full):
            step(kh, False)
        step(n_full, True)
        lam_v = lam_ref[...]
        lam = (jnp.exp(jnp.sum(lam_v[0:1] * lam_v[1:2], axis=-1, keepdims=True))
               - jnp.exp(jnp.sum(lam_v[2:3] * lam_v[3:4], axis=-1, keepdims=True)) + lambda_init)
        o2 = acc_scr[:, :LANES] / acc_scr[:, LANES:]
        o = o2[:blk] - lam * o2[blk:]
        o_ref[...] = _rms(o, g_ref[...]) * (1.0 - lambda_init)

    for n_full in range(group):
        pl.when(diag == n_full)(functools.partial(last_step, n_full))


def _diff_attention(proj3, lam_vecs, subln_g, lambda_init):
    b, s, _ = proj3.shape
    blk = ATT_BLOCK
    nb = s // blk
    group = math.gcd(ATT_KEY_GROUP, nb)
    nh = DIFF_HEADS
    slopes = jnp.exp2(-8.0 * jnp.arange(1, nh + 1, dtype=F32) / nh)
    slopes = jnp.broadcast_to(slopes[:, None, None], (nh, 1, LANES))
    steps = [(qi, kj, qi % group if kj == qi // group else -1) for qi in range(nb) for kj in range(qi // group + 1)]
    tabs = [jnp.asarray([st[i] for st in steps], I32) for i in range(3)]
    kernel = functools.partial(_diff_attn_kernel, lambda_init=lambda_init)
    grid_spec = pltpu.PrefetchScalarGridSpec(
        num_scalar_prefetch=3,
        grid=(b, nh, len(steps)),
        in_specs=[pl.BlockSpec((None, blk, LANES), lambda bi, h, t, qt, kt, dt: (bi, qt[t], h)),
                  pl.BlockSpec((None, group * blk, LANES), lambda bi, h, t, qt, kt, dt: (bi, kt[t], nh + h)),
                  pl.BlockSpec((None, group * blk, LANES), lambda bi, h, t, qt, kt, dt: (bi, kt[t], 2 * nh + h)),
                  pl.BlockSpec((None, 1, LANES), lambda bi, h, t, qt, kt, dt: (h, 0, 0)),
                  pl.BlockSpec((4, HEAD), lambda bi, h, t, qt, kt, dt: (0, 0)),
                  pl.BlockSpec((1, 2 * HEAD), lambda bi, h, t, qt, kt, dt: (0, 0))],
        out_specs=pl.BlockSpec((None, blk, LANES), lambda bi, h, t, qt, kt, dt: (bi, qt[t], h)),
        scratch_shapes=[pltpu.VMEM((2 * blk, LANES), BF16), pltpu.VMEM((2 * blk, LANES), F32),
                        pltpu.VMEM((2 * blk, 2 * LANES), F32)],
    )
    return pl.pallas_call(
        kernel,
        grid_spec=grid_spec,
        out_shape=jax.ShapeDtypeStruct((b, s, DIFF_W), F32),
        compiler_params=_params("parallel", "parallel", "arbitrary"),
        name="diff_attention",
    )(*tabs, proj3, proj3, proj3, slopes, lam_vecs, subln_g)


def _mem_kv_kernel(m_ref, g_ref, w_ref, k_ref, v_ref):
    d = m_ref.shape[-1]
    kv = _dot(_rms(m_ref[...], g_ref[...]), w_ref[...])
    k_ref[...] = kv[:, :d].astype(BF16)
    v_ref[...] = kv[:, d:].astype(BF16)


def _mem_kv(mem, g, w_ckv_bf16):
    b, m, d = mem.shape
    return pl.pallas_call(
        _mem_kv_kernel,
        grid=(b,),
        in_specs=[pl.BlockSpec((None, m, d), lambda i: (i, 0, 0)),
                  pl.BlockSpec((1, d), lambda i: (0, 0)),
                  pl.BlockSpec((d, 2 * d), lambda i: (0, 0))],
        out_specs=[pl.BlockSpec((None, m, d), lambda i: (i, 0, 0))] * 2,
        out_shape=[jax.ShapeDtypeStruct((b, m, d), BF16)] * 2,
        compiler_params=_params("parallel"),
        name="mem_kv",
    )(mem, g, w_ckv_bf16)


def _mid_kernel(x_ref, yr_ref, yd_ref, wo_ref, gc_ref, wcq_ref, km_ref, vm_ref, wco_ref, gf_ref,
                wr_hi_ref, wr_lo_ref, br_ref, tri_ref,
                x2_ref, hn_ref, ri_ref, rg_ref, cnt_ref, carry_scr):
    tm, d = x_ref.shape
    first = jnp.logical_and(pl.program_id(0) == 0, pl.program_id(1) == 0)

    @pl.when(first)
    def _():
        carry_scr[...] = jnp.zeros_like(carry_scr)

    rc = tri_ref.shape[0]
    chains = range(tm // rc)
    rows = [slice(c * rc, (c + 1) * rc) for c in chains]
    half = yr_ref.shape[-1]
    x1 = [x_ref[r, :] + _dot(yr_ref[r, :], wo_ref[:half, :]) + _dot(yd_ref[r, :], wo_ref[half:, :]) for r in rows]

    q = [_dot(_rms(x1[c], gc_ref[...]), wcq_ref[...]) for c in chains]
    hd = d // CROSS_HEADS
    outs = [[] for _ in chains]

    def scores(h):
        sl = slice(h * hd, (h + 1) * hd)
        return [_dot_nt(q[c][:, sl], km_ref[:, sl]) * (hd ** -0.5) for c in chains]

    s_next = scores(0)
    for h in range(CROSS_HEADS):
        sl = slice(h * hd, (h + 1) * hd)
        s = s_next
        if h + 1 < CROSS_HEADS:
            s_next = scores(h + 1)
        p = [jnp.exp(s[c] - jnp.max(s[c], axis=-1, keepdims=True)) for c in chains]
        p = [p[c] / jnp.sum(p[c], axis=-1, keepdims=True) for c in chains]
        for c in chains:
            outs[c].append(_dot(p[c], vm_ref[:, sl]))
    x2 = [x1[c] + _dot(jnp.concatenate(outs[c], axis=1), wco_ref[...]) for c in chains]
    for c in chains:
        x2_ref[rows[c], :] = x2[c]

    hn = [_rms(x2[c], gf_ref[...]) for c in chains]
    _store_token_tiles(hn_ref, jnp.concatenate(hn, axis=0))
    hi = [hn[c].astype(BF16) for c in chains]
    lo = [(hn[c] - hi[c].astype(F32)).astype(BF16) for c in chains]
    vals = [(jnp.dot(hi[c], wr_hi_ref[...], preferred_element_type=F32)
             + jnp.dot(lo[c], wr_hi_ref[...], preferred_element_type=F32)
             + jnp.dot(hi[c], wr_lo_ref[...], preferred_element_type=F32) + br_ref[...]) for c in chains]
    lane = lax.broadcasted_iota(I32, (rc, LANES), 1)
    tops, idxs, hots = [[] for _ in chains], [[] for _ in chains], [[] for _ in chains]
    for _ in range(TOP_K):
        for c in chains:
            mx = jnp.max(vals[c], axis=-1, keepdims=True)
            idx = jnp.min(jnp.where(vals[c] == mx, lane, LANES), axis=-1, keepdims=True)
            hot = lane == idx
            vals[c] = jnp.where(hot, -jnp.inf, vals[c])
            tops[c].append(mx)
            idxs[c].append(idx)
            hots[c].append(hot)
    for c in chains:
        es = [jnp.exp(t - tops[c][0]) for t in tops[c]]
        denom = es[0] + es[1] + es[2] + es[3]
        sel = jnp.zeros((rc, LANES), F32)
        for hot in hots[c]:
            sel = sel + hot.astype(F32)
        before = jnp.dot(tri_ref[...], sel.astype(BF16), preferred_element_type=F32) + carry_scr[...]
        carry_scr[...] = carry_scr[...] + jnp.sum(sel, axis=0, keepdims=True)
        ri = jnp.zeros((rc, LANES), I32)
        rg = jnp.zeros((rc, LANES), F32)
        for j in range(TOP_K):
            rank = jnp.sum(jnp.where(hots[c][j], before, 0.0), axis=-1, keepdims=True).astype(I32)
            ri = jnp.where(lane == j, idxs[c][j], ri)
            ri = jnp.where(lane == TOP_K + j, rank, ri)
            rg = jnp.where(lane == j, es[j] / denom, rg)
        ri_ref[rows[c], :] = ri
        rg_ref[rows[c], :] = rg
    cnt_ref[...] = jnp.broadcast_to(carry_scr[...], cnt_ref.shape)


def _mid_stage(x, y_rwkv, y_diff, w_out, g_cross, w_cq, k_mem, v_mem, w_co, g_ffn, w_router, b_router):
    b, s, d = x.shape
    tm = min(MID_TM, s)
    m = k_mem.shape[1]
    half = y_rwkv.shape[-1]
    e = w_router.shape[1]
    wr = jnp.zeros((d, LANES), F32).at[:, :e].set(w_router)
    wr_hi = wr.astype(BF16)
    wr_lo = (wr - wr_hi.astype(F32)).astype(BF16)
    br = jnp.full((1, LANES), NEG_BIG, F32).at[0, :e].set(b_router)
    rc = min(MID_CHAIN_ROWS, tm)
    t = jnp.arange(rc, dtype=I32)
    tri = (t[None, :] < t[:, None]).astype(BF16)
    tile = lambda w: pl.BlockSpec((None, tm, w), lambda i, j: (i, j, 0))
    const = lambda shape: pl.BlockSpec(shape, lambda i, j: (0,) * len(shape))
    return pl.pallas_call(
        _mid_kernel,
        grid=(b, s // tm),
        in_specs=[tile(d), tile(half), tile(half), const((d, d)), const((1, d)), const((d, d)),
                  pl.BlockSpec((None, m, d), lambda i, j: (i, 0, 0)),
                  pl.BlockSpec((None, m, d), lambda i, j: (i, 0, 0)),
                  const((d, d)), const((1, d)), const((d, LANES)), const((d, LANES)),
                  const((1, LANES)), const((rc, rc))],
        out_specs=[tile(d), pl.BlockSpec((tm * TILE_ROWS, LANES), lambda i, j: (i * (s // tm) + j, 0)),
                   tile(LANES), tile(LANES), const((8, LANES))],
        out_shape=[jax.ShapeDtypeStruct((b, s, d), F32), jax.ShapeDtypeStruct((b * s * TILE_ROWS, LANES), F32),
                   jax.ShapeDtypeStruct((b, s, LANES), I32), jax.ShapeDtypeStruct((b, s, LANES), F32),
                   jax.ShapeDtypeStruct((8, LANES), F32)],
        scratch_shapes=[pltpu.VMEM((1, LANES), F32)],
        compiler_params=_params("arbitrary", "arbitrary"),
        name="outproj_cross_router",
    )(x, y_rwkv, y_diff, w_out.astype(BF16), g_cross, w_cq.astype(BF16), k_mem, v_mem,
      w_co.astype(BF16), g_ffn, wr_hi, wr_lo, br, tri)


def _expert_kernel(be_ref, nblk_ref, nused_ref, x_ref, w1_hbm, b1_ref, w2_hbm, b2_ref, o_ref,
                   w1_stage, w2_stage, w1_scr, w2_scr, w2i_scr, sem):
    i = pl.program_id(0)
    n_used = nused_ref[0]
    used = i < n_used
    expert = be_ref[i]
    new_expert = jnp.logical_or(i == 0, expert != be_ref[jnp.maximum(i - 1, 0)])
    d, f = w1_stage.shape[0], w2_stage.shape[0]

    def weight_copies(ex):
        return (pltpu.make_async_copy(w1_hbm.at[ex], w1_stage, sem.at[0]),
                pltpu.make_async_copy(w2_hbm.at[ex], w2_stage, sem.at[1]))

    @pl.when(jnp.logical_and(used, i == 0))
    def _():
        for cp in weight_copies(expert):
            cp.start()

    @pl.when(jnp.logical_and(used, new_expert))
    def _():
        for cp in weight_copies(expert):
            cp.wait()
        step = 128

        def cast_rows(c, carry):
            rows = pl.ds(pl.multiple_of(c * step, step), step)
            w1_scr[rows, :] = w1_stage[rows, :].astype(BF16)
            return carry

        lax.fori_loop(0, d // step, cast_rows, 0)
        for cb in range(w2_stage.shape[1] // LANES):
            cols = slice(cb * LANES, (cb + 1) * LANES)
            w2i_scr[cb, pl.ds(0, f // 2, stride=2), :] = w2_stage[:f // 2, cols]
            w2i_scr[cb, pl.ds(1, f // 2, stride=2), :] = w2_stage[f // 2:, cols]
            w2_scr[:, cols] = w2i_scr[cb].astype(BF16)
        next_run = i + nblk_ref[expert]

        @pl.when(next_run < n_used)
        def _():
            for cp in weight_copies(be_ref[jnp.minimum(next_run, be_ref.shape[0] - 1)]):
                cp.start()

    @pl.when(used)
    def _():
        rc = MOE_CHAIN_ROWS
        even = lax.broadcasted_iota(I32, (rc, LANES), 1) % 2 == 0
        cw = 2 * LANES
        n_stage = f // cw
        stages = [(h, j) for h in range(MOE_ROWS // rc) for j in range(n_stage)]
        xs = {}

        def hidden_pair(stage):
            h, j = stage
            if h not in xs:
                xs[h] = _load_token_tiles(x_ref, rc, first=h * rc).astype(BF16)
            lo, hi = slice(j * cw, (j + 1) * cw), slice(f + j * cw, f + (j + 1) * cw)
            return (jnp.dot(xs[h], w1_scr[:, lo], preferred_element_type=F32) + b1_ref[:, lo],
                    jnp.dot(xs[h], w1_scr[:, hi], preferred_element_type=F32) + b1_ref[:, hi])

        def act_block(h):
            lin = jnp.clip(pltpu.roll(h, LANES - 1, axis=1), -SWIGLU_LIMIT, SWIGLU_LIMIT)
            glu = jnp.minimum(h, SWIGLU_LIMIT)
            return jnp.where(even, glu * jax.nn.sigmoid(SWIGLU_ALPHA * glu) * (lin + 1.0), 0.0)

        y = None
        pending = hidden_pair(stages[0])
        for n, (h, j) in enumerate(stages):
            ha, hb = pending
            if n + 1 < len(stages):
                pending = hidden_pair(stages[n + 1])
            packed = jnp.concatenate(
                [act_block(ha[:, c * LANES:(c + 1) * LANES])
                 + pltpu.roll(act_block(hb[:, c * LANES:(c + 1) * LANES]), 1, axis=1) for c in range(cw // LANES)],
                axis=1)
            part = jnp.dot(packed.astype(BF16), w2_scr[j * cw:(j + 1) * cw, :], preferred_element_type=F32)
            y = b2_ref[...] + part if j == 0 else y + part
            if j == n_stage - 1:
                _store_token_tiles(o_ref, y, first=h * rc)

    @pl.when(jnp.logical_not(used))
    def _():
        o_ref[...] = jnp.zeros_like(o_ref)


def _expert_ffn(x_sorted, block_e, blocks_per_expert, n_used, w1, b1, w2, b2):
    e, d, f2 = w1.shape
    r = MOE_ROWS
    n_rows = block_e.shape[0] * r
    grid_spec = pltpu.PrefetchScalarGridSpec(
        num_scalar_prefetch=3,
        grid=(n_rows // r,),
        in_specs=[pl.BlockSpec((r * TILE_ROWS, LANES), lambda i, be, nb, nu: (i, 0)),
                  pl.BlockSpec(memory_space=pl.ANY),
                  pl.BlockSpec((None, 1, f2), lambda i, be, nb, nu: (be[i], 0, 0)),
                  pl.BlockSpec(memory_space=pl.ANY),
                  pl.BlockSpec((None, 1, d), lambda i, be, nb, nu: (be[i], 0, 0))],
        out_specs=pl.BlockSpec((r * TILE_ROWS, LANES), lambda i, be, nb, nu: (i, 0)),
        scratch_shapes=[pltpu.VMEM((d, f2), F32), pltpu.VMEM((f2 // 2, d), F32),
                        pltpu.VMEM((d, f2), BF16), pltpu.VMEM((f2 // 2, d), BF16),
                        pltpu.VMEM((d // LANES, f2 // 2, LANES), F32), pltpu.SemaphoreType.DMA((2,))],
    )
    return pl.pallas_call(
        _expert_kernel,
        grid_spec=grid_spec,
        out_shape=jax.ShapeDtypeStruct((n_rows * TILE_ROWS, LANES), F32),
        compiler_params=_params("arbitrary"),
        name="expert_ffn",
    )(block_e, blocks_per_expert, n_used, x_sorted, w1, b1.reshape(e, 1, f2), w2, b2.reshape(e, 1, d))


def _dispatch_kernel(pad_lo_ref, pad_hi_ref, dst_ref, li_ref, hn_ref, xs_hbm, buf, zero_scr, sem, pad_sem, *, tm):
    @pl.when(pl.program_id(0) == 0)
    def _():
        zero_scr[...] = jnp.zeros_like(zero_scr)

        def per_range(e, carry):
            pos, hi = pad_lo_ref[e], pad_hi_ref[e]
            for size in ZERO_FILL_ROWS:
                n = lax.div(hi - pos, size)

                def pad_copy(k, pos=pos, size=size):
                    first = pl.multiple_of((pos + k * size) * TILE_ROWS, TILE_ROWS)
                    return pltpu.make_async_copy(zero_scr.at[pl.ds(0, size * TILE_ROWS)],
                                                 xs_hbm.at[pl.ds(first, size * TILE_ROWS)], pad_sem)

                lax.fori_loop(0, n, lambda k, c, cp=pad_copy: (cp(k).start(), c)[1], 0)
                lax.fori_loop(0, n, lambda k, c, cp=pad_copy: (cp(k).wait(), c)[1], 0)
                pos = pos + n * size
            return carry

        lax.fori_loop(0, pad_lo_ref.shape[0], per_range, 0)

        buf[...] = jnp.zeros_like(buf)

    i = pl.program_id(0)
    n_chunks = dst_ref.shape[1]
    chunk = SEG_CHUNK * TILE_ROWS
    slot_rows = n_chunks * chunk
    slot = i % 2
    base = slot * slot_rows

    def place(t, carry):
        tile = hn_ref[pl.ds(pl.multiple_of(t * TILE_ROWS, TILE_ROWS), TILE_ROWS), :]
        for j in range(TOP_K):
            buf[pl.ds(pl.multiple_of(base + li_ref[0, t * TOP_K + j], TILE_ROWS), TILE_ROWS), :] = tile
        return carry

    lax.fori_loop(0, tm, place, 0, unroll=8)

    def slot_wait(sl):
        whole = buf.at[pl.ds(pl.multiple_of(sl * slot_rows, TILE_ROWS), slot_rows)]
        pltpu.make_async_copy(whole, whole, sem.at[sl]).wait()

    @pl.when(i > 0)
    def _():
        slot_wait(1 - slot)

    def start(q, carry):
        pltpu.make_async_copy(buf.at[pl.ds(pl.multiple_of(base + q * chunk, TILE_ROWS), chunk)],
                              xs_hbm.at[pl.ds(pl.multiple_of(dst_ref[0, q], TILE_ROWS), chunk)],
                              sem.at[slot]).start(priority=1)
        return carry

    lax.fori_loop(0, n_chunks, start, 0, unroll=8)

    @pl.when(i == pl.num_programs(0) - 1)
    def _():
        slot_wait(slot)


def _dispatch(hn_tiles, chunk_dst, local_rows, pad_lo, pad_hi, n_rows):
    t = hn_tiles.shape[0] // TILE_ROWS
    tm = COMBINE_TM
    steps = t // tm
    n_chunks = chunk_dst.shape[-1]
    smem = lambda w: pl.BlockSpec((None, 1, w), lambda i, lo, hi: (i, 0, 0), memory_space=pltpu.SMEM)
    grid_spec = pltpu.PrefetchScalarGridSpec(
        num_scalar_prefetch=2,
        grid=(steps,),
        in_specs=[smem(n_chunks), smem(tm * TOP_K), pl.BlockSpec((tm * TILE_ROWS, LANES), lambda i, lo, hi: (i, 0))],
        out_specs=pl.BlockSpec(memory_space=pl.ANY),
        scratch_shapes=[pltpu.VMEM((2 * n_chunks * SEG_CHUNK * TILE_ROWS, LANES), F32),
                        pltpu.VMEM((ZERO_FILL_ROWS[0] * TILE_ROWS, LANES), F32), pltpu.SemaphoreType.DMA((2,)),
                        pltpu.SemaphoreType.DMA(())],
    )
    return pl.pallas_call(
        functools.partial(_dispatch_kernel, tm=tm),
        grid_spec=grid_spec,
        out_shape=jax.ShapeDtypeStruct((n_rows * TILE_ROWS, LANES), F32),
        compiler_params=_params("arbitrary"),
        name="moe_dispatch",
    )(pad_lo, pad_hi, chunk_dst.reshape(steps, 1, n_chunks), local_rows.reshape(steps, 1, tm * TOP_K), hn_tiles)


def _combine_seg_kernel(src_cur_ref, src_nxt_ref, li_ref, gate_ref, x_ref, g_ref, ys_hbm, o_ref, buf, acc_scr, sem,
                        *, tm, final):
    i = pl.program_id(0)
    n_chunks = src_cur_ref.shape[1]
    chunk = SEG_CHUNK * TILE_ROWS
    slot_rows = n_chunks * chunk

    def fetch(src_ref, slot):
        def start(q, carry):
            pltpu.make_async_copy(ys_hbm.at[pl.ds(pl.multiple_of(src_ref[0, q], TILE_ROWS), chunk)],
                                  buf.at[pl.ds(pl.multiple_of(slot * slot_rows + q * chunk, TILE_ROWS), chunk)],
                                  sem.at[slot]).start(priority=1)
            return carry
        lax.fori_loop(0, n_chunks, start, 0, unroll=8)

    @pl.when(i == 0)
    def _():
        fetch(src_cur_ref, 0)

    @pl.when(i + 1 < pl.num_programs(0))
    def _():
        fetch(src_nxt_ref, (i + 1) % 2)

    slot = i % 2
    base = slot * slot_rows
    whole = buf.at[pl.ds(pl.multiple_of(base, TILE_ROWS), slot_rows)]
    pltpu.make_async_copy(whole, whole, sem.at[slot]).wait()

    def token(t, carry):
        acc = jnp.zeros((TILE_ROWS, LANES), F32)
        for j in range(TOP_K):
            s = t * TOP_K + j
            row = pl.multiple_of(base + li_ref[0, s], TILE_ROWS)
            acc = acc + gate_ref[0, s] * buf[pl.ds(row, TILE_ROWS), :]
        acc_scr[pl.ds(pl.multiple_of(t * TILE_ROWS, TILE_ROWS), TILE_ROWS), :] = acc
        return carry

    lax.fori_loop(0, tm, token, 0, unroll=8)
    y = x_ref[...] + _load_token_tiles(acc_scr, tm)
    o_ref[...] = _rms(y, g_ref[...]) if final else y


def _combine_segments(x2d, gates, chunk_src, li, y_sorted, g, final):
    t, d = x2d.shape
    tm = COMBINE_TM
    steps = t // tm
    n_chunks = chunk_src.shape[-1]
    src3 = chunk_src.reshape(steps, 1, n_chunks)
    smem = lambda w, imap: pl.BlockSpec((None, 1, w), imap, memory_space=pltpu.SMEM)
    return pl.pallas_call(
        functools.partial(_combine_seg_kernel, tm=tm, final=final),
        grid=(steps,),
        in_specs=[smem(n_chunks, lambda i: (i, 0, 0)),
                  smem(n_chunks, lambda i: (jnp.minimum(i + 1, steps - 1), 0, 0)),
                  smem(tm * TOP_K, lambda i: (i, 0, 0)), smem(tm * TOP_K, lambda i: (i, 0, 0)),
                  pl.BlockSpec((tm, d), lambda i: (i, 0)),
                  pl.BlockSpec((1, d), lambda i: (0, 0)), pl.BlockSpec(memory_space=pl.ANY)],
        out_specs=pl.BlockSpec((tm, d), lambda i: (i, 0)),
        out_shape=jax.ShapeDtypeStruct((t, d), F32),
        scratch_shapes=[pltpu.VMEM((2 * n_chunks * SEG_CHUNK * TILE_ROWS, LANES), F32),
                        pltpu.VMEM((tm * TILE_ROWS, LANES), F32), pltpu.SemaphoreType.DMA((2,))],
        compiler_params=_params("arbitrary"),
        name="combine_norm",
    )(src3, src3, li.reshape(steps, 1, tm * TOP_K), gates.reshape(steps, 1, tm * TOP_K), x2d, g, y_sorted)


def _layer(x, mem, lyr, final, p):
    b, s, d = x.shape
    t = b * s
    c = RWKV_W
    rwkv_in = 3 * c + LORA_W

    w_in = p["w_in"][lyr]
    w_cat = jnp.concatenate([w_in[:, rwkv_in:], w_in[:, :3 * c], w_in[:, 3 * c:rwkv_in],
                             jnp.zeros((d, LORA_PAD - LORA_W), F32)], axis=1).astype(BF16)
    diff_cols = 3 * DIFF_W
    proj_diff, proj_rwkv = _norm_inproj(x.reshape(t, d), p["norm_mix_g"][lyr][None], w_cat, diff_cols)
    proj_diff, proj_rwkv = proj_diff.reshape(b, s, -1), proj_rwkv.reshape(b, s, -1)

    mu = p["shift_mu"][lyr]
    mu_l = jnp.zeros((1, LORA_PAD), F32).at[0, :LORA_W].set(mu[3 * c:])
    wl = jnp.zeros((LORA_PAD, 3 * c), F32)
    wl = wl.at[:DECAY_LORA, :c].set(p["w_decay_up"][lyr])
    wl = wl.at[DECAY_LORA:DECAY_LORA + ICLR_LORA, c:2 * c].set(p["w_iclr_up"][lyr])
    wl = wl.at[DECAY_LORA + ICLR_LORA:LORA_W, 2 * c:].set(p["w_gate_up"][lyr])
    vec = jnp.stack([p["w0"][lyr], p["a0"][lyr], p["k_k"][lyr], p["k_a"][lyr], p["r_k"][lyr].reshape(c),
                     p["lnx_g"][lyr], p["lnx_b"][lyr], jnp.zeros((c,), F32)])
    y_rwkv = _rwkv_group(proj_rwkv, mu[None, :3 * c], mu_l, wl.astype(BF16), vec,
                         rkv_block=0, lora_block=3 * c // LORA_PAD)

    lambda_init = 0.8 - 0.6 * math.exp(-0.3 * lyr)
    lam_vecs = jnp.stack([p["lambda_q1"][lyr], p["lambda_k1"][lyr], p["lambda_q2"][lyr], p["lambda_k2"][lyr]])
    y_diff = _diff_attention(proj_diff, lam_vecs, p["subln_g"][lyr][None], lambda_init)

    k_mem, v_mem = _mem_kv(mem, p["norm_mem_g"][lyr][None], p["w_ckv"][lyr].astype(BF16))
    x2, hn, route_i, route_g, counts = _mid_stage(
        x, y_rwkv, y_diff, p["w_out"][lyr], p["norm_cross_g"][lyr][None], p["w_cq"][lyr], k_mem, v_mem,
        p["w_co"][lyr], p["norm_ffn_g"][lyr][None], p["w_router"][lyr], p["b_router"][lyr])

    e = N_EXPERTS
    r = MOE_ROWS
    n_blocks = (t * TOP_K) // r + e + 1
    n_slots = COMBINE_TM * TOP_K // SEG_CHUNK + e
    n_rows = n_blocks * r + n_slots * SEG_CHUNK
    route_i = route_i.reshape(t, LANES)
    idx, rank = route_i[:, :TOP_K], route_i[:, TOP_K:2 * TOP_K]
    cnt = counts[0, :e].astype(I32)
    padded = (cnt + SEG_CHUNK - 1 + r - 1) // r * r
    pad_end = jnp.cumsum(padded)
    pad_start = pad_end - padded
    block_start = jnp.arange(n_blocks, dtype=I32) * r
    block_e = jnp.minimum(jnp.sum((block_start[:, None] >= pad_end[None, :]).astype(I32), axis=1), e - 1)
    n_used = pad_end[-1:] // r

    onehot = idx[:, :, None] == jnp.arange(e, dtype=I32)
    tmc = COMBINE_TM
    nt = t // tmc
    per_tile = jnp.sum(onehot.reshape(nt, tmc * TOP_K, e), axis=1, dtype=I32)
    before = jnp.cumsum(per_tile, axis=0) - per_tile
    n_chunks = (per_tile + SEG_CHUNK - 1) // SEG_CHUNK
    chunk_end = jnp.cumsum(n_chunks, axis=1)
    chunk_first = chunk_end - n_chunks
    shift = jnp.repeat(chunk_first * SEG_CHUNK - before, tmc, axis=0)
    local_row = ((jnp.sum(jnp.where(onehot, shift[:, None, :], 0), axis=-1) + rank) * TILE_ROWS).reshape(-1)
    slots = jnp.arange(n_slots, dtype=I32)
    owner = jnp.sum((slots[None, :, None] >= chunk_end[:, None, :]).astype(I32), axis=-1)
    own = owner[:, :, None] == jnp.arange(e, dtype=I32)
    seg_row = jnp.sum(jnp.where(own, (pad_start[None, :] + before - chunk_first * SEG_CHUNK)[:, None, :], 0), axis=-1)
    chunk_row = seg_row + slots[None, :] * SEG_CHUNK
    chunk_dst = jnp.where(owner < e, chunk_row, n_blocks * r + slots[None, :] * SEG_CHUNK) * TILE_ROWS
    chunk_src = jnp.where(owner < e, chunk_row, 0) * TILE_ROWS

    unused_lo = jnp.concatenate([pad_start + cnt, pad_end[-1:]])
    unused_hi = jnp.concatenate([pad_end, jnp.full((1,), n_rows, I32)])
    x_sorted = _dispatch(hn, chunk_dst, local_row, unused_lo, unused_hi, n_rows)
    y_sorted = _expert_ffn(x_sorted, block_e, padded // r, n_used, p["w1"][lyr], p["b1"][lyr], p["w2"][lyr],
                           p["b2"][lyr])
    gates = route_g.reshape(t, LANES)[:, :TOP_K]
    return _combine_segments(x2.reshape(t, d), gates, chunk_src, local_row, y_sorted,
                             p["norm_final_g"][None], final).reshape(b, s, d)


def kernel(x, mem, norm_mix_g, w_in, shift_mu, w0, w_decay_up, a0, w_iclr_up, w_gate_up, k_k, k_a, r_k,
           lnx_g, lnx_b, lambda_q1, lambda_k1, lambda_q2, lambda_k2, subln_g, w_out, norm_cross_g,
           norm_mem_g, w_cq, w_ckv, w_co, norm_ffn_g, w_router, b_router, w1, b1, w2, b2, norm_final_g):
    p = dict(norm_mix_g=norm_mix_g, w_in=w_in, shift_mu=shift_mu, w0=w0, w_decay_up=w_decay_up, a0=a0,
             w_iclr_up=w_iclr_up, w_gate_up=w_gate_up, k_k=k_k, k_a=k_a, r_k=r_k, lnx_g=lnx_g, lnx_b=lnx_b,
             lambda_q1=lambda_q1, lambda_k1=lambda_k1, lambda_q2=lambda_q2, lambda_k2=lambda_k2,
             subln_g=subln_g, w_out=w_out, norm_cross_g=norm_cross_g, norm_mem_g=norm_mem_g, w_cq=w_cq,
             w_ckv=w_ckv, w_co=w_co, norm_ffn_g=norm_ffn_g, w_router=w_router, b_router=b_router,
             w1=w1, b1=b1, w2=w2, b2=b2, norm_final_g=norm_final_g)
    depth = w_in.shape[0]
    for lyr in range(depth):
        x = _layer(x, mem, lyr, lyr == depth - 1, p)
    return x
```

```python
import functools
import math

import jax
import jax.numpy as jnp
from jax import lax
from jax.experimental import pallas as pl
from jax.experimental.pallas import tpu as pltpu

F32 = jnp.float32
BF16 = jnp.bfloat16
I32 = jnp.int32

NORM_EPS = 1e-5
HEAD = 64
RWKV_W = 512
RWKV_HEADS = RWKV_W // HEAD
DECAY_LORA, ICLR_LORA, GATE_LORA = 64, 64, 160
LORA_W = DECAY_LORA + ICLR_LORA + GATE_LORA
LORA_PAD = 384
RWKV_GN_EPS = 64e-5
DIFF_W = 512
DIFF_HEADS = DIFF_W // (2 * HEAD)
CROSS_HEADS = 4
N_EXPERTS = 32
TOP_K = 4
SWIGLU_LIMIT = 7.0
SWIGLU_ALPHA = 1.702
LANES = 128
TILE_ROWS = 8
NEG_BIG = -1e30

RWKV_CHUNK = 64
RWKV_BATCH = 4
PROJ_TM = 512
ATT_BLOCK = 512
ATT_ROW_CHUNK = 128
ATT_KEY_GROUP = 4
MID_TM = 512
MID_CHAIN_ROWS = 256
MOE_ROWS = 256
MOE_CHAIN_ROWS = 256
COMBINE_TM = 256
SEG_CHUNK = 8
ZERO_FILL_ROWS = (64, 8, 1)
VMEM_LIMIT = 56 * 1024 * 1024


def _dot(a, b):
    return jnp.dot(a.astype(BF16), b.astype(BF16), preferred_element_type=F32)


def _dot_nt(a, b):
    return lax.dot_general(a.astype(BF16), b.astype(BF16), (((1,), (1,)), ((), ())),
                           preferred_element_type=F32)


def _dot_tn(a, b):
    return lax.dot_general(a.astype(BF16), b.astype(BF16), (((0,), (0,)), ((), ())),
                           preferred_element_type=F32)


def _split2(x):
    hi = x.astype(BF16)
    lo = (x - hi.astype(F32)).astype(BF16)
    return hi, lo


def _dot_exact_rhs(x, ones_bf16):
    hi, lo = _split2(x)
    return jnp.dot(hi, ones_bf16, preferred_element_type=F32) + jnp.dot(lo, ones_bf16, preferred_element_type=F32)


def _dot_exact_lhs(ones_bf16, x):
    hi, lo = _split2(x)
    return jnp.dot(ones_bf16, hi, preferred_element_type=F32) + jnp.dot(ones_bf16, lo, preferred_element_type=F32)


def _store_token_tiles(ref, x, first=0):
    rows = x.shape[0]
    for c in range(TILE_ROWS):
        ref[pl.ds(first * TILE_ROWS + c, rows, stride=TILE_ROWS), :] = x[:, c * LANES:(c + 1) * LANES]


def _load_token_tiles(ref, rows, lead=(), first=0):
    return jnp.concatenate([ref[lead + (pl.ds(first * TILE_ROWS + c, rows, stride=TILE_ROWS), slice(None))]
                            for c in range(TILE_ROWS)], axis=1)


def _rms(x, g):
    return x * lax.rsqrt(jnp.mean(x * x, axis=-1, keepdims=True) + NORM_EPS) * g


def _params(*sem):
    return pltpu.CompilerParams(dimension_semantics=sem, vmem_limit_bytes=VMEM_LIMIT)


def _norm_inproj_kernel(x_ref, g_ref, w_ref, od_ref, or_ref):
    h = _rms(x_ref[...], g_ref[...]).astype(BF16)
    nd = od_ref.shape[1]
    od_ref[...] = jnp.dot(h, w_ref[:, :nd], preferred_element_type=F32).astype(BF16)
    or_ref[...] = jnp.dot(h, w_ref[:, nd:], preferred_element_type=F32)


def _norm_inproj(x2d, g, w_bf16, n_diff):
    t, d = x2d.shape
    n = w_bf16.shape[1]
    tm = min(PROJ_TM, t)
    return pl.pallas_call(
        _norm_inproj_kernel,
        grid=(t // tm,),
        in_specs=[pl.BlockSpec((tm, d), lambda i: (i, 0)),
                  pl.BlockSpec((1, d), lambda i: (0, 0)),
                  pl.BlockSpec((d, n), lambda i: (0, 0))],
        out_specs=[pl.BlockSpec((tm, n_diff), lambda i: (i, 0)), pl.BlockSpec((tm, n - n_diff), lambda i: (i, 0))],
        out_shape=[jax.ShapeDtypeStruct((t, n_diff), BF16), jax.ShapeDtypeStruct((t, n - n_diff), F32)],
        compiler_params=_params("parallel"),
        name="norm_inproj",
    )(x2d, g, w_bf16)


def _rwkv_kernel(rkv_ref, lora_ref, mu_rkv_ref, mu_l_ref, wl_ref, vec_ref, bd_ref, tri_ref,
                 o_ref, st_ref, prev_rkv_ref, prev_l_ref):
    L = RWKV_CHUNK
    C = RWKV_W

    @pl.when(pl.program_id(1) == 0)
    def _():
        st_ref[...] = jnp.zeros_like(st_ref)
        prev_rkv_ref[...] = jnp.zeros_like(prev_rkv_ref)
        prev_l_ref[...] = jnp.zeros_like(prev_l_ref)

    row = lax.broadcasted_iota(I32, (L, 1), 0)
    w0, a0, k_k, k_a = vec_ref[0:1, :], vec_ref[1:2, :], vec_ref[2:3, :], vec_ref[3:4, :]
    r_k, lnx_g, lnx_b = vec_ref[4:5, :], vec_ref[5:6, :], vec_ref[6:7, :]
    bd = bd_ref[...]
    tri = tri_ref[...]
    lane_l = lax.broadcasted_iota(I32, (L, LORA_PAD), 1)
    nb = rkv_ref.shape[0]

    def token_shift(raw, prev_ref, bi, mu):
        prev = jnp.where(row == 0, prev_ref[bi], pltpu.roll(raw, 1, axis=0))
        prev_ref[bi] = raw[L - 1:L, :]
        return raw + mu * (prev - raw)

    def front(bi):
        u = token_shift(rkv_ref[bi], prev_rkv_ref, bi, mu_rkv_ref[...])
        ul = token_shift(lora_ref[bi], prev_l_ref, bi, mu_l_ref[...])
        r, k, v = u[:, :C], u[:, C:2 * C], u[:, 2 * C:]
        act = jnp.where(lane_l < DECAY_LORA, jnp.tanh(ul),
                        jnp.where(lane_l < DECAY_LORA + ICLR_LORA, ul, jax.nn.sigmoid(ul)))
        lo = _dot(act, wl_ref[...])
        w_log = -jax.nn.softplus(-(w0 + lo[:, :C])) - 0.5
        lw = -jnp.exp(w_log)
        a = jax.nn.sigmoid(a0 + lo[:, C:2 * C])
        kk = k * k_k
        kk = kk / jnp.maximum(jnp.sqrt(_dot(kk * kk, bd)), 1e-12)
        k2 = k * (1.0 + (a - 1.0) * k_a)
        cum = _dot_exact_lhs(tri, lw)
        return dict(r=r, v=v, k2=k2, g=lo[:, 2 * C:], kk=kk, a=a, lw=lw, cum=cum)

    def rescaled(fr):
        cum, lw, kk, k2 = fr["cum"], fr["lw"], fr["kk"], fr["k2"]
        total = cum[L - 1:L, :]
        e_neg = jnp.exp(-cum)
        e_rem = jnp.exp(total - cum)
        kka = kk * fr["a"]
        return dict(fr, a_t=-kk * jnp.exp(cum - lw), b_t=kka * e_neg, k_t=k2 * e_neg, r_t=fr["r"] * jnp.exp(cum),
                    b_bar=kka * e_rem, k_bar=k2 * e_rem, p_total=jnp.exp(total))

    fronts = [front(bi) for bi in range(nb)]

    m0 = lax.broadcasted_iota(I32, (L, LANES), 1) < HEAD
    r2i = lax.broadcasted_iota(I32, (2 * L, 4 * L), 0)
    c2i = lax.broadcasted_iota(I32, (2 * L, 4 * L), 1) % (2 * L)
    strict, incl = c2i < r2i, c2i <= r2i
    npairs = RWKV_HEADS // 2
    _rwkv_chains({bi: rescaled(fronts[bi]) for bi in range(nb)}, range(nb), npairs, m0, strict, incl,
                 st_ref, o_ref, bd, r_k, lnx_g, lnx_b)


def _rwkv_chains(ew, group_rows, npairs, m0, strict, incl, st_ref, o_ref, bd, r_k, lnx_g, lnx_b):
    L = RWKV_CHUNK
    chains = [(bi, p) for bi in group_rows for p in range(npairs)]
    nc = range(len(chains))

    def stack2(name):
        out = []
        for bi, p in chains:
            xp = ew[bi][name][:, p * LANES:(p + 1) * LANES]
            out.append(jnp.concatenate([jnp.where(m0, xp, 0.0), jnp.where(m0, 0.0, xp)], axis=0))
        return out

    A2, B2, K2, R2 = stack2("a_t"), stack2("b_t"), stack2("k_t"), stack2("r_t")
    V2, Bb2, Kb2 = stack2("v"), stack2("b_bar"), stack2("k_bar")
    mg = [_dot_nt(jnp.concatenate([A2[c], R2[c]], axis=0), jnp.concatenate([B2[c], K2[c]], axis=0)) for c in nc]
    m_cat = [jnp.where(strict, mg[c][:2 * L], 0.0) for c in nc]
    g_cat = [jnp.where(incl, mg[c][2 * L:], 0.0) for c in nc]
    m_ab = [m_cat[c][:, :LANES] for c in nc]
    pw = [_dot(m_ab[c], m_ab[c]) for c in nc]
    tr = list(m_ab)
    k = 2
    while 2 * k < L:
        x = [_dot(pw[c], jnp.concatenate([pw[c], tr[c]], axis=1)) for c in nc]
        tr = [tr[c] + pw[c] + x[c][:, LANES:] for c in nc]
        pw = [x[c][:, :LANES] for c in nc]
        k *= 2
    tr = [tr[c] + pw[c] + _dot(pw[c], tr[c]) for c in nc]
    az = [jnp.concatenate([A2[c], _dot(m_cat[c][:, LANES:], V2[c])], axis=1) for c in nc]
    tz = [az[c] + _dot(tr[c], az[c]) for c in nc]
    st = [st_ref[bi, p] for bi, p in chains]
    u2 = [_dot_nt(tz[c][:, :LANES], st[c]) + tz[c][:, LANES:] for c in nc]
    uv = [jnp.concatenate([u2[c], V2[c]], axis=0) for c in nc]
    y2 = [_dot_nt(R2[c], st[c]) + _dot(g_cat[c], uv[c]) for c in nc]
    for c, (bi, p) in enumerate(chains):
        st_ref[bi, p] = (st[c] * ew[bi]["p_total"][:, p * LANES:(p + 1) * LANES]
                         + _dot_tn(uv[c], jnp.concatenate([Bb2[c], Kb2[c]], axis=0)))

    inv_n = 1.0 / HEAD
    for gi, bi in enumerate(group_rows):
        y = jnp.concatenate([y2[gi * npairs + p][:L] + y2[gi * npairs + p][L:] for p in range(npairs)], axis=1)
        e = ew[bi]
        mu = _dot_exact_rhs(y, bd) * inv_n
        d = y - mu
        var = _dot_exact_rhs(d * d, bd) * inv_n
        yn = d * lax.rsqrt(var + RWKV_GN_EPS) * lnx_g + lnx_b
        bonus = _dot_exact_rhs(e["r"] * e["k2"] * r_k, bd) * e["v"]
        o_ref[bi] = (yn + bonus) * e["g"]


def _rwkv_group(proj3, mu_rkv, mu_l, wl, vec, rkv_block, lora_block):
    b, s, _ = proj3.shape
    L, C = RWKV_CHUNK, RWKV_W
    head = jnp.arange(C, dtype=I32) // HEAD
    bd = (head[:, None] == head[None, :]).astype(BF16)
    t = jnp.arange(L, dtype=I32)
    tri = (t[None, :] <= t[:, None]).astype(BF16)
    const = lambda shape: pl.BlockSpec(shape, lambda i, j: (0,) * len(shape))
    nb = RWKV_BATCH if b % RWKV_BATCH == 0 else 1
    return pl.pallas_call(
        _rwkv_kernel,
        grid=(b // nb, s // L),
        in_specs=[pl.BlockSpec((nb, L, 3 * C), lambda i, j: (i, j, rkv_block)),
                  pl.BlockSpec((nb, L, LORA_PAD), lambda i, j: (i, j, lora_block)),
                  const((1, 3 * C)), const((1, LORA_PAD)), const((LORA_PAD, 3 * C)),
                  const((8, C)), const((C, C)), const((L, L))],
        out_specs=pl.BlockSpec((nb, L, C), lambda i, j: (i, j, 0)),
        out_shape=jax.ShapeDtypeStruct((b, s, C), F32),
        scratch_shapes=[pltpu.VMEM((nb, RWKV_HEADS // 2, LANES, LANES), F32),
                        pltpu.VMEM((nb, 1, 3 * C), F32), pltpu.VMEM((nb, 1, LORA_PAD), F32)],
        compiler_params=_params("parallel", "arbitrary"),
        name="rwkv_group",
    )(proj3, proj3, mu_rkv, mu_l, wl, vec, bd, tri)


def _diff_attn_kernel(qi_ref, kj_ref, diag_ref, q_ref, k_ref, v_ref, slope_ref, lam_ref, g_ref, o_ref,
                      q2_scr, m_scr, acc_scr, *, lambda_init):
    blk = ATT_BLOCK
    group = k_ref.shape[0] // blk
    step_id = pl.program_id(2)
    qi, kj, diag = qi_ref[step_id], kj_ref[step_id], diag_ref[step_id]
    log2e = math.log2(math.e)

    @pl.when(kj == 0)
    def _():
        q = q_ref[...].astype(F32) * (HEAD ** -0.5 * log2e)
        m0 = lax.broadcasted_iota(I32, (blk, LANES), 1) < HEAD
        q2_scr[0:blk, :] = jnp.where(m0, q, 0.0).astype(BF16)
        q2_scr[blk:, :] = jnp.where(m0, 0.0, q).astype(BF16)
        m_scr[...] = jnp.full_like(m_scr, NEG_BIG)
        acc_scr[...] = jnp.zeros_like(acc_scr)

    def step(kh, on_diagonal):
        k = k_ref[kh * blk:(kh + 1) * blk, :].astype(BF16)
        v_ext = jnp.concatenate([v_ref[kh * blk:(kh + 1) * blk, :].astype(BF16), jnp.ones((blk, LANES), BF16)],
                                axis=1)
        koff = lax.broadcasted_iota(I32, (1, blk), 1)
        col_bias = (slope_ref[:, :1] * log2e) * ((kj * group + kh - qi) * blk + koff).astype(F32)
        rc = ATT_ROW_CHUNK
        n_chunks = 2 * blk // rc

        def scores(c):
            return lax.dot_general(q2_scr[c * rc:(c + 1) * rc, :], k, (((1,), (1,)), ((), ())),
                                   preferred_element_type=F32)

        s_next = scores(0)
        for c in range(n_chunks):
            rows = slice(c * rc, (c + 1) * rc)
            s = s_next + col_bias
            if c + 1 < n_chunks:
                s_next = scores(c + 1)
            if on_diagonal:
                qoff = (c * rc) % blk + lax.broadcasted_iota(I32, (rc, 1), 0)
                s = jnp.where(koff <= qoff, s, NEG_BIG)
            m_prev = m_scr[rows, :]
            m_new = jnp.maximum(m_prev, jnp.max(s, axis=-1, keepdims=True))
            alpha = jnp.exp2(m_prev - m_new)
            p = jnp.exp2(s - jnp.concatenate([m_new] * (blk // LANES), axis=1))
            acc_scr[rows, :] = (jnp.concatenate([alpha, alpha], axis=1) * acc_scr[rows, :]
                                + jnp.dot(p.astype(BF16), v_ext, preferred_element_type=F32))
            m_scr[rows, :] = m_new

    @pl.when(diag < 0)
    def _():
        for kh in range(group):
            step(kh, False)

    def last_step(n_full):
        for kh in range(n_full):
            step(kh, False)
        step(n_full, True)
        lam_v = lam_ref[...]
        lam = (jnp.exp(jnp.sum(lam_v[0:1] * lam_v[1:2], axis=-1, keepdims=True))
               - jnp.exp(jnp.sum(lam_v[2:3] * lam_v[3:4], axis=-1, keepdims=True)) + lambda_init)
        o2 = acc_scr[:, :LANES] / acc_scr[:, LANES:]
        o = o2[:blk] - lam * o2[blk:]
        o_ref[...] = _rms(o, g_ref[...]) * (1.0 - lambda_init)

    for n_full in range(group):
        pl.when(diag == n_full)(functools.partial(last_step, n_full))


def _diff_attention(proj3, lam_vecs, subln_g, lambda_init):
    b, s, _ = proj3.shape
    blk = ATT_BLOCK
    nb = s // blk
    group = math.gcd(ATT_KEY_GROUP, nb)
    nh = DIFF_HEADS
    slopes = jnp.exp2(-8.0 * jnp.arange(1, nh + 1, dtype=F32) / nh)
    slopes = jnp.broadcast_to(slopes[:, None, None], (nh, 1, LANES))
    steps = [(qi, kj, qi % group if kj == qi // group else -1) for qi in range(nb) for kj in range(qi // group + 1)]
    tabs = [jnp.asarray([st[i] for st in steps], I32) for i in range(3)]
    kernel = functools.partial(_diff_attn_kernel, lambda_init=lambda_init)
    grid_spec = pltpu.PrefetchScalarGridSpec(
        num_scalar_prefetch=3,
        grid=(b, nh, len(steps)),
        in_specs=[pl.BlockSpec((None, blk, LANES), lambda bi, h, t, qt, kt, dt: (bi, qt[t], h)),
                  pl.BlockSpec((None, group * blk, LANES), lambda bi, h, t, qt, kt, dt: (bi, kt[t], nh + h)),
                  pl.BlockSpec((None, group * blk, LANES), lambda bi, h, t, qt, kt, dt: (bi, kt[t], 2 * nh + h)),
                  pl.BlockSpec((None, 1, LANES), lambda bi, h, t, qt, kt, dt: (h, 0, 0)),
                  pl.BlockSpec((4, HEAD), lambda bi, h, t, qt, kt, dt: (0, 0)),
                  pl.BlockSpec((1, 2 * HEAD), lambda bi, h, t, qt, kt, dt: (0, 0))],
        out_specs=pl.BlockSpec((None, blk, LANES), lambda bi, h, t, qt, kt, dt: (bi, qt[t], h)),
        scratch_shapes=[pltpu.VMEM((2 * blk, LANES), BF16), pltpu.VMEM((2 * blk, LANES), F32),
                        pltpu.VMEM((2 * blk, 2 * LANES), F32)],
    )
    return pl.pallas_call(
        kernel,
        grid_spec=grid_spec,
        out_shape=jax.ShapeDtypeStruct((b, s, DIFF_W), F32),
        compiler_params=_params("parallel", "parallel", "arbitrary"),
        name="diff_attention",
    )(*tabs, proj3, proj3, proj3, slopes, lam_vecs, subln_g)


def _mem_kv_kernel(m_ref, g_ref, w_ref, k_ref, v_ref):
    d = m_ref.shape[-1]
    kv = _dot(_rms(m_ref[...], g_ref[...]), w_ref[...])
    k_ref[...] = kv[:, :d].astype(BF16)
    v_ref[...] = kv[:, d:].astype(BF16)


def _mem_kv(mem, g, w_ckv_bf16):
    b, m, d = mem.shape
    return pl.pallas_call(
        _mem_kv_kernel,
        grid=(b,),
        in_specs=[pl.BlockSpec((None, m, d), lambda i: (i, 0, 0)),
                  pl.BlockSpec((1, d), lambda i: (0, 0)),
                  pl.BlockSpec((d, 2 * d), lambda i: (0, 0))],
        out_specs=[pl.BlockSpec((None, m, d), lambda i: (i, 0, 0))] * 2,
        out_shape=[jax.ShapeDtypeStruct((b, m, d), BF16)] * 2,
        compiler_params=_params("parallel"),
        name="mem_kv",
    )(mem, g, w_ckv_bf16)


def _mid_kernel(x_ref, yr_ref, yd_ref, wo_ref, gc_ref, wcq_ref, km_ref, vm_ref, wco_ref, gf_ref,
                wr_hi_ref, wr_lo_ref, br_ref, tri_ref,
                x2_ref, hn_ref, ri_ref, rg_ref, cnt_ref, carry_scr):
    tm, d = x_ref.shape
    first = jnp.logical_and(pl.program_id(0) == 0, pl.program_id(1) == 0)

    @pl.when(first)
    def _():
        carry_scr[...] = jnp.zeros_like(carry_scr)

    rc = tri_ref.shape[0]
    chains = range(tm // rc)
    rows = [slice(c * rc, (c + 1) * rc) for c in chains]
    half = yr_ref.shape[-1]
    x1 = [x_ref[r, :] + _dot(yr_ref[r, :], wo_ref[:half, :]) + _dot(yd_ref[r, :], wo_ref[half:, :]) for r in rows]

    q = [_dot(_rms(x1[c], gc_ref[...]), wcq_ref[...]) for c in chains]
    hd = d // CROSS_HEADS
    outs = [[] for _ in chains]

    def scores(h):
        sl = slice(h * hd, (h + 1) * hd)
        return [_dot_nt(q[c][:, sl], km_ref[:, sl]) * (hd ** -0.5) for c in chains]

    s_next = scores(0)
    for h in range(CROSS_HEADS):
        sl = slice(h * hd, (h + 1) * hd)
        s = s_next
        if h + 1 < CROSS_HEADS:
            s_next = scores(h + 1)
        p = [jnp.exp(s[c] - jnp.max(s[c], axis=-1, keepdims=True)) for c in chains]
        p = [p[c] / jnp.sum(p[c], axis=-1, keepdims=True) for c in chains]
        for c in chains:
            outs[c].append(_dot(p[c], vm_ref[:, sl]))
    x2 = [x1[c] + _dot(jnp.concatenate(outs[c], axis=1), wco_ref[...]) for c in chains]
    for c in chains:
        x2_ref[rows[c], :] = x2[c]

    hn = [_rms(x2[c], gf_ref[...]) for c in chains]
    _store_token_tiles(hn_ref, jnp.concatenate(hn, axis=0))
    hi = [hn[c].astype(BF16) for c in chains]
    lo = [(hn[c] - hi[c].astype(F32)).astype(BF16) for c in chains]
    vals = [(jnp.dot(hi[c], wr_hi_ref[...], preferred_element_type=F32)
             + jnp.dot(lo[c], wr_hi_ref[...], preferred_element_type=F32)
             + jnp.dot(hi[c], wr_lo_ref[...], preferred_element_type=F32) + br_ref[...]) for c in chains]
    lane = lax.broadcasted_iota(I32, (rc, LANES), 1)
    tops, idxs, hots = [[] for _ in chains], [[] for _ in chains], [[] for _ in chains]
    for _ in range(TOP_K):
        for c in chains:
            mx = jnp.max(vals[c], axis=-1, keepdims=True)
            idx = jnp.min(jnp.where(vals[c] == mx, lane, LANES), axis=-1, keepdims=True)
            hot = lane == idx
            vals[c] = jnp.where(hot, -jnp.inf, vals[c])
            tops[c].append(mx)
            idxs[c].append(idx)
            hots[c].append(hot)
    for c in chains:
        es = [jnp.exp(t - tops[c][0]) for t in tops[c]]
        denom = es[0] + es[1] + es[2] + es[3]
        sel = jnp.zeros((rc, LANES), F32)
        for hot in hots[c]:
            sel = sel + hot.astype(F32)
        before = jnp.dot(tri_ref[...], sel.astype(BF16), preferred_element_type=F32) + carry_scr[...]
        carry_scr[...] = carry_scr[...] + jnp.sum(sel, axis=0, keepdims=True)
        ri = jnp.zeros((rc, LANES), I32)
        rg = jnp.zeros((rc, LANES), F32)
        for j in range(TOP_K):
            rank = jnp.sum(jnp.where(hots[c][j], before, 0.0), axis=-1, keepdims=True).astype(I32)
            ri = jnp.where(lane == j, idxs[c][j], ri)
            ri = jnp.where(lane == TOP_K + j, rank, ri)
            rg = jnp.where(lane == j, es[j] / denom, rg)
        ri_ref[rows[c], :] = ri
        rg_ref[rows[c], :] = rg
    cnt_ref[...] = jnp.broadcast_to(carry_scr[...], cnt_ref.shape)


def _mid_stage(x, y_rwkv, y_diff, w_out, g_cross, w_cq, k_mem, v_mem, w_co, g_ffn, w_router, b_router):
    b, s, d = x.shape
    tm = min(MID_TM, s)
    m = k_mem.shape[1]
    half = y_rwkv.shape[-1]
    e = w_router.shape[1]
    wr = jnp.zeros((d, LANES), F32).at[:, :e].set(w_router)
    wr_hi = wr.astype(BF16)
    wr_lo = (wr - wr_hi.astype(F32)).astype(BF16)
    br = jnp.full((1, LANES), NEG_BIG, F32).at[0, :e].set(b_router)
    rc = min(MID_CHAIN_ROWS, tm)
    t = jnp.arange(rc, dtype=I32)
    tri = (t[None, :] < t[:, None]).astype(BF16)
    tile = lambda w: pl.BlockSpec((None, tm, w), lambda i, j: (i, j, 0))
    const = lambda shape: pl.BlockSpec(shape, lambda i, j: (0,) * len(shape))
    return pl.pallas_call(
        _mid_kernel,
        grid=(b, s // tm),
        in_specs=[tile(d), tile(half), tile(half), const((d, d)), const((1, d)), const((d, d)),
                  pl.BlockSpec((None, m, d), lambda i, j: (i, 0, 0)),
                  pl.BlockSpec((None, m, d), lambda i, j: (i, 0, 0)),
                  const((d, d)), const((1, d)), const((d, LANES)), const((d, LANES)),
                  const((1, LANES)), const((rc, rc))],
        out_specs=[tile(d), pl.BlockSpec((tm * TILE_ROWS, LANES), lambda i, j: (i * (s // tm) + j, 0)),
                   tile(LANES), tile(LANES), const((8, LANES))],
        out_shape=[jax.ShapeDtypeStruct((b, s, d), F32), jax.ShapeDtypeStruct((b * s * TILE_ROWS, LANES), F32),
                   jax.ShapeDtypeStruct((b, s, LANES), I32), jax.ShapeDtypeStruct((b, s, LANES), F32),
                   jax.ShapeDtypeStruct((8, LANES), F32)],
        scratch_shapes=[pltpu.VMEM((1, LANES), F32)],
        compiler_params=_params("arbitrary", "arbitrary"),
        name="outproj_cross_router",
    )(x, y_rwkv, y_diff, w_out.astype(BF16), g_cross, w_cq.astype(BF16), k_mem, v_mem,
      w_co.astype(BF16), g_ffn, wr_hi, wr_lo, br, tri)


def _expert_kernel(be_ref, nblk_ref, nused_ref, x_ref, w1_hbm, b1_ref, w2_hbm, b2_ref, o_ref,
                   w1_stage, w2_stage, w1_scr, w2_scr, w2i_scr, sem):
    i = pl.program_id(0)
    n_used = nused_ref[0]
    used = i < n_used
    expert = be_ref[i]
    new_expert = jnp.logical_or(i == 0, expert != be_ref[jnp.maximum(i - 1, 0)])
    d, f = w1_stage.shape[0], w2_stage.shape[0]

    def weight_copies(ex):
        return (pltpu.make_async_copy(w1_hbm.at[ex], w1_stage, sem.at[0]),
                pltpu.make_async_copy(w2_hbm.at[ex], w2_stage, sem.at[1]))

    @pl.when(jnp.logical_and(used, i == 0))
    def _():
        for cp in weight_copies(expert):
            cp.start()

    @pl.when(jnp.logical_and(used, new_expert))
    def _():
        for cp in weight_copies(expert):
            cp.wait()
        step = 128

        def cast_rows(c, carry):
            rows = pl.ds(pl.multiple_of(c * step, step), step)
            w1_scr[rows, :] = w1_stage[rows, :].astype(BF16)
            return carry

        lax.fori_loop(0, d // step, cast_rows, 0)
        for cb in range(w2_stage.shape[1] // LANES):
            cols = slice(cb * LANES, (cb + 1) * LANES)
            w2i_scr[cb, pl.ds(0, f // 2, stride=2), :] = w2_stage[:f // 2, cols]
            w2i_scr[cb, pl.ds(1, f // 2, stride=2), :] = w2_stage[f // 2:, cols]
            w2_scr[:, cols] = w2i_scr[cb].astype(BF16)
        next_run = i + nblk_ref[expert]

        @pl.when(next_run < n_used)
        def _():
            for cp in weight_copies(be_ref[jnp.minimum(next_run, be_ref.shape[0] - 1)]):
                cp.start()

    @pl.when(used)
    def _():
        rc = MOE_CHAIN_ROWS
        even = lax.broadcasted_iota(I32, (rc, LANES), 1) % 2 == 0
        cw = 2 * LANES
        n_stage = f // cw
        stages = [(h, j) for h in range(MOE_ROWS // rc) for j in range(n_stage)]
        xs = {}

        def hidden_pair(stage):
            h, j = stage
            if h not in xs:
                xs[h] = _load_token_tiles(x_ref, rc, first=h * rc).astype(BF16)
            lo, hi = slice(j * cw, (j + 1) * cw), slice(f + j * cw, f + (j + 1) * cw)
            return (jnp.dot(xs[h], w1_scr[:, lo], preferred_element_type=F32) + b1_ref[:, lo],
                    jnp.dot(xs[h], w1_scr[:, hi], preferred_element_type=F32) + b1_ref[:, hi])

        def act_block(h):
            lin = jnp.clip(pltpu.roll(h, LANES - 1, axis=1), -SWIGLU_LIMIT, SWIGLU_LIMIT)
            glu = jnp.minimum(h, SWIGLU_LIMIT)
            return jnp.where(even, glu * jax.nn.sigmoid(SWIGLU_ALPHA * glu) * (lin + 1.0), 0.0)

        y = None
        pending = hidden_pair(stages[0])
        for n, (h, j) in enumerate(stages):
            ha, hb = pending
            if n + 1 < len(stages):
                pending = hidden_pair(stages[n + 1])
            packed = jnp.concatenate(
                [act_block(ha[:, c * LANES:(c + 1) * LANES])
                 + pltpu.roll(act_block(hb[:, c * LANES:(c + 1) * LANES]), 1, axis=1) for c in range(cw // LANES)],
                axis=1)
            part = jnp.dot(packed.astype(BF16), w2_scr[j * cw:(j + 1) * cw, :], preferred_element_type=F32)
            y = b2_ref[...] + part if j == 0 else y + part
            if j == n_stage - 1:
                _store_token_tiles(o_ref, y, first=h * rc)

    @pl.when(jnp.logical_not(used))
    def _():
        o_ref[...] = jnp.zeros_like(o_ref)


def _expert_ffn(x_sorted, block_e, blocks_per_expert, n_used, w1, b1, w2, b2):
    e, d, f2 = w1.shape
    r = MOE_ROWS
    n_rows = block_e.shape[0] * r
    grid_spec = pltpu.PrefetchScalarGridSpec(
        num_scalar_prefetch=3,
        grid=(n_rows // r,),
        in_specs=[pl.BlockSpec((r * TILE_ROWS, LANES), lambda i, be, nb, nu: (i, 0)),
                  pl.BlockSpec(memory_space=pl.ANY),
                  pl.BlockSpec((None, 1, f2), lambda i, be, nb, nu: (be[i], 0, 0)),
                  pl.BlockSpec(memory_space=pl.ANY),
                  pl.BlockSpec((None, 1, d), lambda i, be, nb, nu: (be[i], 0, 0))],
        out_specs=pl.BlockSpec((r * TILE_ROWS, LANES), lambda i, be, nb, nu: (i, 0)),
        scratch_shapes=[pltpu.VMEM((d, f2), F32), pltpu.VMEM((f2 // 2, d), F32),
                        pltpu.VMEM((d, f2), BF16), pltpu.VMEM((f2 // 2, d), BF16),
                        pltpu.VMEM((d // LANES, f2 // 2, LANES), F32), pltpu.SemaphoreType.DMA((2,))],
    )
    return pl.pallas_call(
        _expert_kernel,
        grid_spec=grid_spec,
        out_shape=jax.ShapeDtypeStruct((n_rows * TILE_ROWS, LANES), F32),
        compiler_params=_params("arbitrary"),
        name="expert_ffn",
    )(block_e, blocks_per_expert, n_used, x_sorted, w1, b1.reshape(e, 1, f2), w2, b2.reshape(e, 1, d))


def _dispatch_kernel(pad_lo_ref, pad_hi_ref, dst_ref, li_ref, hn_ref, xs_hbm, buf, zero_scr, sem, pad_sem, *, tm):
    @pl.when(pl.program_id(0) == 0)
    def _():
        zero_scr[...] = jnp.zeros_like(zero_scr)

        def per_range(e, carry):
            pos, hi = pad_lo_ref[e], pad_hi_ref[e]
            for size in ZERO_FILL_ROWS:
                n = lax.div(hi - pos, size)

                def pad_copy(k, pos=pos, size=size):
                    first = pl.multiple_of((pos + k * size) * TILE_ROWS, TILE_ROWS)
                    return pltpu.make_async_copy(zero_scr.at[pl.ds(0, size * TILE_ROWS)],
                                                 xs_hbm.at[pl.ds(first, size * TILE_ROWS)], pad_sem)

                lax.fori_loop(0, n, lambda k, c, cp=pad_copy: (cp(k).start(), c)[1], 0)
                lax.fori_loop(0, n, lambda k, c, cp=pad_copy: (cp(k).wait(), c)[1], 0)
                pos = pos + n * size
            return carry

        lax.fori_loop(0, pad_lo_ref.shape[0], per_range, 0)

        buf[...] = jnp.zeros_like(buf)

    i = pl.program_id(0)
    n_chunks = dst_ref.shape[1]
    chunk = SEG_CHUNK * TILE_ROWS
    slot_rows = n_chunks * chunk
    slot = i % 2
    base = slot * slot_rows

    def place(t, carry):
        tile = hn_ref[pl.ds(pl.multiple_of(t * TILE_ROWS, TILE_ROWS), TILE_ROWS), :]
        for j in range(TOP_K):
            buf[pl.ds(pl.multiple_of(li_ref[0, t * TOP_K + j], TILE_ROWS), TILE_ROWS), :] = tile
        return carry

    lax.fori_loop(0, tm, place, 0, unroll=8)

    def slot_wait(sl):
        whole = buf.at[pl.ds(pl.multiple_of(sl * slot_rows, TILE_ROWS), slot_rows)]
        pltpu.make_async_copy(whole, whole, sem.at[sl]).wait()

    @pl.when(i > 0)
    def _():
        slot_wait(1 - slot)

    def start(q, carry):
        pltpu.make_async_copy(buf.at[pl.ds(pl.multiple_of(base + q * chunk, TILE_ROWS), chunk)],
                              xs_hbm.at[pl.ds(pl.multiple_of(dst_ref[0, q], TILE_ROWS), chunk)],
                              sem.at[slot]).start(priority=1)
        return carry

    lax.fori_loop(0, n_chunks, start, 0, unroll=8)

    @pl.when(i == pl.num_programs(0) - 1)
    def _():
        slot_wait(slot)


def _dispatch(hn_tiles, chunk_dst, local_rows, pad_lo, pad_hi, n_rows):
    t = hn_tiles.shape[0] // TILE_ROWS
    tm = COMBINE_TM
    steps = t // tm
    n_chunks = chunk_dst.shape[-1]
    smem = lambda w: pl.BlockSpec((None, 1, w), lambda i, lo, hi: (i, 0, 0), memory_space=pltpu.SMEM)
    grid_spec = pltpu.PrefetchScalarGridSpec(
        num_scalar_prefetch=2,
        grid=(steps,),
        in_specs=[smem(n_chunks), smem(tm * TOP_K), pl.BlockSpec((tm * TILE_ROWS, LANES), lambda i, lo, hi: (i, 0))],
        out_specs=pl.BlockSpec(memory_space=pl.ANY),
        scratch_shapes=[pltpu.VMEM((2 * n_chunks * SEG_CHUNK * TILE_ROWS, LANES), F32),
                        pltpu.VMEM((ZERO_FILL_ROWS[0] * TILE_ROWS, LANES), F32), pltpu.SemaphoreType.DMA((2,)),
                        pltpu.SemaphoreType.DMA(())],
    )
    return pl.pallas_call(
        functools.partial(_dispatch_kernel, tm=tm),
        grid_spec=grid_spec,
        out_shape=jax.ShapeDtypeStruct((n_rows * TILE_ROWS, LANES), F32),
        compiler_params=_params("arbitrary"),
        name="moe_dispatch",
    )(pad_lo, pad_hi, chunk_dst.reshape(steps, 1, n_chunks), local_rows.reshape(steps, 1, tm * TOP_K), hn_tiles)


def _combine_seg_kernel(src_cur_ref, src_nxt_ref, li_ref, gate_ref, x_ref, g_ref, ys_hbm, o_ref, buf, acc_scr, sem,
                        *, tm, final):
    i = pl.program_id(0)
    n_chunks = src_cur_ref.shape[1]
    chunk = SEG_CHUNK * TILE_ROWS
    slot_rows = n_chunks * chunk

    def fetch(src_ref, slot):
        def start(q, carry):
            pltpu.make_async_copy(ys_hbm.at[pl.ds(pl.multiple_of(src_ref[0, q], TILE_ROWS), chunk)],
                                  buf.at[pl.ds(pl.multiple_of(slot * slot_rows + q * chunk, TILE_ROWS), chunk)],
                                  sem.at[slot]).start(priority=1)
            return carry
        lax.fori_loop(0, n_chunks, start, 0, unroll=8)

    @pl.when(i == 0)
    def _():
        fetch(src_cur_ref, 0)

    @pl.when(i + 1 < pl.num_programs(0))
    def _():
        fetch(src_nxt_ref, (i + 1) % 2)

    slot = i % 2
    base = slot * slot_rows
    whole = buf.at[pl.ds(pl.multiple_of(base, TILE_ROWS), slot_rows)]
    pltpu.make_async_copy(whole, whole, sem.at[slot]).wait()

    def token(t, carry):
        acc = jnp.zeros((TILE_ROWS, LANES), F32)
        for j in range(TOP_K):
            s = t * TOP_K + j
            row = pl.multiple_of(li_ref[0, s], TILE_ROWS)
            acc = acc + gate_ref[0, s] * buf[pl.ds(row, TILE_ROWS), :]
        acc_scr[pl.ds(pl.multiple_of(t * TILE_ROWS, TILE_ROWS), TILE_ROWS), :] = acc
        return carry

    lax.fori_loop(0, tm, token, 0, unroll=8)
    y = x_ref[...] + _load_token_tiles(acc_scr, tm)
    o_ref[...] = _rms(y, g_ref[...]) if final else y


def _combine_segments(x2d, gates, chunk_src, li, y_sorted, g, final):
    t, d = x2d.shape
    tm = COMBINE_TM
    steps = t // tm
    n_chunks = chunk_src.shape[-1]
    src3 = chunk_src.reshape(steps, 1, n_chunks)
    smem = lambda w, imap: pl.BlockSpec((None, 1, w), imap, memory_space=pltpu.SMEM)
    return pl.pallas_call(
        functools.partial(_combine_seg_kernel, tm=tm, final=final),
        grid=(steps,),
        in_specs=[smem(n_chunks, lambda i: (i, 0, 0)),
                  smem(n_chunks, lambda i: (jnp.minimum(i + 1, steps - 1), 0, 0)),
                  smem(tm * TOP_K, lambda i: (i, 0, 0)), smem(tm * TOP_K, lambda i: (i, 0, 0)),
                  pl.BlockSpec((tm, d), lambda i: (i, 0)),
                  pl.BlockSpec((1, d), lambda i: (0, 0)), pl.BlockSpec(memory_space=pl.ANY)],
        out_specs=pl.BlockSpec((tm, d), lambda i: (i, 0)),
        out_shape=jax.ShapeDtypeStruct((t, d), F32),
        scratch_shapes=[pltpu.VMEM((2 * n_chunks * SEG_CHUNK * TILE_ROWS, LANES), F32),
                        pltpu.VMEM((tm * TILE_ROWS, LANES), F32), pltpu.SemaphoreType.DMA((2,))],
        compiler_params=_params("arbitrary"),
        name="combine_norm",
    )(src3, src3, li.reshape(steps, 1, tm * TOP_K), gates.reshape(steps, 1, tm * TOP_K), x2d, g, y_sorted)


def _layer(x, mem, lyr, final, p):
    b, s, d = x.shape
    t = b * s
    c = RWKV_W
    rwkv_in = 3 * c + LORA_W

    w_in = p["w_in"][lyr]
    w_cat = jnp.concatenate([w_in[:, rwkv_in:], w_in[:, :3 * c], w_in[:, 3 * c:rwkv_in],
                             jnp.zeros((d, LORA_PAD - LORA_W), F32)], axis=1).astype(BF16)
    diff_cols = 3 * DIFF_W
    proj_diff, proj_rwkv = _norm_inproj(x.reshape(t, d), p["norm_mix_g"][lyr][None], w_cat, diff_cols)
    proj_diff, proj_rwkv = proj_diff.reshape(b, s, -1), proj_rwkv.reshape(b, s, -1)

    mu = p["shift_mu"][lyr]
    mu_l = jnp.zeros((1, LORA_PAD), F32).at[0, :LORA_W].set(mu[3 * c:])
    wl = jnp.zeros((LORA_PAD, 3 * c), F32)
    wl = wl.at[:DECAY_LORA, :c].set(p["w_decay_up"][lyr])
    wl = wl.at[DECAY_LORA:DECAY_LORA + ICLR_LORA, c:2 * c].set(p["w_iclr_up"][lyr])
    wl = wl.at[DECAY_LORA + ICLR_LORA:LORA_W, 2 * c:].set(p["w_gate_up"][lyr])
    vec = jnp.stack([p["w0"][lyr], p["a0"][lyr], p["k_k"][lyr], p["k_a"][lyr], p["r_k"][lyr].reshape(c),
                     p["lnx_g"][lyr], p["lnx_b"][lyr], jnp.zeros((c,), F32)])
    y_rwkv = _rwkv_group(proj_rwkv, mu[None, :3 * c], mu_l, wl.astype(BF16), vec,
                         rkv_block=0, lora_block=3 * c // LORA_PAD)

    lambda_init = 0.8 - 0.6 * math.exp(-0.3 * lyr)
    lam_vecs = jnp.stack([p["lambda_q1"][lyr], p["lambda_k1"][lyr], p["lambda_q2"][lyr], p["lambda_k2"][lyr]])
    y_diff = _diff_attention(proj_diff, lam_vecs, p["subln_g"][lyr][None], lambda_init)

    k_mem, v_mem = _mem_kv(mem, p["norm_mem_g"][lyr][None], p["w_ckv"][lyr].astype(BF16))
    x2, hn, route_i, route_g, counts = _mid_stage(
        x, y_rwkv, y_diff, p["w_out"][lyr], p["norm_cross_g"][lyr][None], p["w_cq"][lyr], k_mem, v_mem,
        p["w_co"][lyr], p["norm_ffn_g"][lyr][None], p["w_router"][lyr], p["b_router"][lyr])

    e = N_EXPERTS
    r = MOE_ROWS
    n_blocks = (t * TOP_K) // r + e + 1
    n_slots = COMBINE_TM * TOP_K // SEG_CHUNK + e
    n_rows = n_blocks * r + n_slots * SEG_CHUNK
    route_i = route_i.reshape(t, LANES)
    idx, rank = route_i[:, :TOP_K], route_i[:, TOP_K:2 * TOP_K]
    cnt = counts[0, :e].astype(I32)
    padded = (cnt + SEG_CHUNK - 1 + r - 1) // r * r
    pad_end = jnp.cumsum(padded)
    pad_start = pad_end - padded
    block_start = jnp.arange(n_blocks, dtype=I32) * r
    block_e = jnp.minimum(jnp.sum((block_start[:, None] >= pad_end[None, :]).astype(I32), axis=1), e - 1)
    n_used = pad_end[-1:] // r

    onehot = idx[:, :, None] == jnp.arange(e, dtype=I32)
    tmc = COMBINE_TM
    nt = t // tmc
    per_tile = jnp.sum(onehot.reshape(nt, tmc * TOP_K, e), axis=1, dtype=I32)
    before = jnp.cumsum(per_tile, axis=0) - per_tile
    n_chunks = (per_tile + SEG_CHUNK - 1) // SEG_CHUNK
    chunk_end = jnp.cumsum(n_chunks, axis=1)
    chunk_first = chunk_end - n_chunks
    shift = jnp.repeat(chunk_first * SEG_CHUNK - before, tmc, axis=0)
    local_row = (jnp.sum(jnp.where(onehot, shift[:, None, :], 0), axis=-1) + rank) * TILE_ROWS
    slot_rows = n_slots * SEG_CHUNK * TILE_ROWS
    local_row = (local_row.reshape(nt, -1) + (jnp.arange(nt, dtype=I32) % 2 * slot_rows)[:, None]).reshape(-1)
    slots = jnp.arange(n_slots, dtype=I32)
    owner = jnp.sum((slots[None, :, None] >= chunk_end[:, None, :]).astype(I32), axis=-1)
    own = owner[:, :, None] == jnp.arange(e, dtype=I32)
    seg_row = jnp.sum(jnp.where(own, (pad_start[None, :] + before - chunk_first * SEG_CHUNK)[:, None, :], 0), axis=-1)
    chunk_row = seg_row + slots[None, :] * SEG_CHUNK
    chunk_dst = jnp.where(owner < e, chunk_row, n_blocks * r + slots[None, :] * SEG_CHUNK) * TILE_ROWS
    chunk_src = jnp.where(owner < e, chunk_row, 0) * TILE_ROWS

    unused_lo = jnp.concatenate([pad_start + cnt, pad_end[-1:]])
    unused_hi = jnp.concatenate([pad_end, jnp.full((1,), n_rows, I32)])
    x_sorted = _dispatch(hn, chunk_dst, local_row, unused_lo, unused_hi, n_rows)
    y_sorted = _expert_ffn(x_sorted, block_e, padded // r, n_used, p["w1"][lyr], p["b1"][lyr], p["w2"][lyr],
                           p["b2"][lyr])
    gates = route_g.reshape(t, LANES)[:, :TOP_K]
    return _combine_segments(x2.reshape(t, d), gates, chunk_src, local_row, y_sorted,
                             p["norm_final_g"][None], final).reshape(b, s, d)


def kernel(x, mem, norm_mix_g, w_in, shift_mu, w0, w_decay_up, a0, w_iclr_up, w_gate_up, k_k, k_a, r_k,
           lnx_g, lnx_b, lambda_q1, lambda_k1, lambda_q2, lambda_k2, subln_g, w_out, norm_cross_g,
           norm_mem_g, w_cq, w_ckv, w_co, norm_ffn_g, w_router, b_router, w1, b1, w2, b2, norm_final_g):
    p = dict(norm_mix_g=norm_mix_g, w_in=w_in, shift_mu=shift_mu, w0=w0, w_decay_up=w_decay_up, a0=a0,
             w_iclr_up=w_iclr_up, w_gate_up=w_gate_up, k_k=k_k, k_a=k_a, r_k=r_k, lnx_g=lnx_g, lnx_b=lnx_b,
             lambda_q1=lambda_q1, lambda_k1=lambda_k1, lambda_q2=lambda_q2, lambda_k2=lambda_k2,
             subln_g=subln_g, w_out=w_out, norm_cross_g=norm_cross_g, norm_mem_g=norm_mem_g, w_cq=w_cq,
             w_ckv=w_ckv, w_co=w_co, norm_ffn_g=norm_ffn_g, w_router=w_router, b_router=b_router,
             w1=w1, b1=b1, w2=w2, b2=b2, norm_final_g=norm_final_g)
    depth = w_in.shape[0]
    for lyr in range(depth):
        x = _layer(x, mem, lyr, lyr == depth - 1, p)
    return x
```

```python
import functools
import math

import jax
import jax.numpy as jnp
from jax import lax
from jax.experimental import pallas as pl
from jax.experimental.pallas import tpu as pltpu

F32 = jnp.float32
BF16 = jnp.bfloat16
I32 = jnp.int32

NORM_EPS = 1e-5
HEAD = 64
RWKV_W = 512
RWKV_HEADS = RWKV_W // HEAD
DECAY_LORA, ICLR_LORA, GATE_LORA = 64, 64, 160
LORA_W = DECAY_LORA + ICLR_LORA + GATE_LORA
LORA_PAD = 384
RWKV_GN_EPS = 64e-5
DIFF_W = 512
DIFF_HEADS = DIFF_W // (2 * HEAD)
CROSS_HEADS = 4
N_EXPERTS = 32
TOP_K = 4
SWIGLU_LIMIT = 7.0
SWIGLU_ALPHA = 1.702
LANES = 128
TILE_ROWS = 8
NEG_BIG = -1e30

RWKV_CHUNK = 64
RWKV_BATCH = 4
PROJ_TM = 512
ATT_BLOCK = 512
ATT_ROW_CHUNK = 128
ATT_KEY_GROUP = 4
MID_TM = 512
MID_CHAIN_ROWS = 256
MOE_ROWS = 256
MOE_CHAIN_ROWS = 256
COMBINE_TM = 256
SEG_CHUNK = 8
ZERO_FILL_ROWS = (64, 8, 1)
VMEM_LIMIT = 56 * 1024 * 1024


def _dot(a, b):
    return jnp.dot(a.astype(BF16), b.astype(BF16), preferred_element_type=F32)


def _dot_nt(a, b):
    return lax.dot_general(a.astype(BF16), b.astype(BF16), (((1,), (1,)), ((), ())),
                           preferred_element_type=F32)


def _dot_tn(a, b):
    return lax.dot_general(a.astype(BF16), b.astype(BF16), (((0,), (0,)), ((), ())),
                           preferred_element_type=F32)


def _split2(x):
    hi = x.astype(BF16)
    lo = (x - hi.astype(F32)).astype(BF16)
    return hi, lo


def _dot_exact_rhs(x, ones_bf16):
    hi, lo = _split2(x)
    return jnp.dot(hi, ones_bf16, preferred_element_type=F32) + jnp.dot(lo, ones_bf16, preferred_element_type=F32)


def _dot_exact_lhs(ones_bf16, x):
    hi, lo = _split2(x)
    return jnp.dot(ones_bf16, hi, preferred_element_type=F32) + jnp.dot(ones_bf16, lo, preferred_element_type=F32)


def _store_token_tiles(ref, x, first=0):
    rows = x.shape[0]
    for c in range(TILE_ROWS):
        ref[pl.ds(first * TILE_ROWS + c, rows, stride=TILE_ROWS), :] = x[:, c * LANES:(c + 1) * LANES]


def _load_token_tiles(ref, rows, first=0):
    return jnp.concatenate([ref[pl.ds(first * TILE_ROWS + c, rows, stride=TILE_ROWS), :]
                            for c in range(TILE_ROWS)], axis=1)


def _rms(x, g):
    return x * lax.rsqrt(jnp.mean(x * x, axis=-1, keepdims=True) + NORM_EPS) * g


def _params(*sem):
    return pltpu.CompilerParams(dimension_semantics=sem, vmem_limit_bytes=VMEM_LIMIT)


def _norm_inproj_kernel(x_ref, g_ref, w_ref, od_ref, or_ref):
    h = _rms(x_ref[...], g_ref[...]).astype(BF16)
    nd = od_ref.shape[1]
    od_ref[...] = jnp.dot(h, w_ref[:, :nd], preferred_element_type=F32).astype(BF16)
    or_ref[...] = jnp.dot(h, w_ref[:, nd:], preferred_element_type=F32)


def _norm_inproj(x2d, g, w_bf16, n_diff):
    t, d = x2d.shape
    n = w_bf16.shape[1]
    tm = min(PROJ_TM, t)
    return pl.pallas_call(
        _norm_inproj_kernel,
        grid=(t // tm,),
        in_specs=[pl.BlockSpec((tm, d), lambda i: (i, 0)),
                  pl.BlockSpec((1, d), lambda i: (0, 0)),
                  pl.BlockSpec((d, n), lambda i: (0, 0))],
        out_specs=[pl.BlockSpec((tm, n_diff), lambda i: (i, 0)), pl.BlockSpec((tm, n - n_diff), lambda i: (i, 0))],
        out_shape=[jax.ShapeDtypeStruct((t, n_diff), BF16), jax.ShapeDtypeStruct((t, n - n_diff), F32)],
        compiler_params=_params("parallel"),
        name="norm_inproj",
    )(x2d, g, w_bf16)


def _rwkv_kernel(rkv_ref, lora_ref, mu_rkv_ref, mu_l_ref, wl_ref, vec_ref, bd_ref, tri_ref,
                 o_ref, st_ref, prev_rkv_ref, prev_l_ref):
    L = RWKV_CHUNK
    C = RWKV_W

    @pl.when(pl.program_id(1) == 0)
    def _():
        st_ref[...] = jnp.zeros_like(st_ref)
        prev_rkv_ref[...] = jnp.zeros_like(prev_rkv_ref)
        prev_l_ref[...] = jnp.zeros_like(prev_l_ref)

    row = lax.broadcasted_iota(I32, (L, 1), 0)
    w0, a0, k_k, k_a = vec_ref[0:1, :], vec_ref[1:2, :], vec_ref[2:3, :], vec_ref[3:4, :]
    r_k, lnx_g, lnx_b = vec_ref[4:5, :], vec_ref[5:6, :], vec_ref[6:7, :]
    bd = bd_ref[...]
    tri = tri_ref[...]
    lane_l = lax.broadcasted_iota(I32, (L, LORA_PAD), 1)
    nb = rkv_ref.shape[0]

    def token_shift(raw, prev_ref, bi, mu):
        prev = jnp.where(row == 0, prev_ref[bi], pltpu.roll(raw, 1, axis=0))
        prev_ref[bi] = raw[L - 1:L, :]
        return raw + mu * (prev - raw)

    def front(bi):
        u = token_shift(rkv_ref[bi], prev_rkv_ref, bi, mu_rkv_ref[...])
        ul = token_shift(lora_ref[bi], prev_l_ref, bi, mu_l_ref[...])
        r, k, v = u[:, :C], u[:, C:2 * C], u[:, 2 * C:]
        act = jnp.where(lane_l < DECAY_LORA, jnp.tanh(ul),
                        jnp.where(lane_l < DECAY_LORA + ICLR_LORA, ul, jax.nn.sigmoid(ul)))
        lo = _dot(act, wl_ref[...])
        w_log = -jax.nn.softplus(-(w0 + lo[:, :C])) - 0.5
        lw = -jnp.exp(w_log)
        a = jax.nn.sigmoid(a0 + lo[:, C:2 * C])
        kk = k * k_k
        kk = kk / jnp.maximum(jnp.sqrt(_dot(kk * kk, bd)), 1e-12)
        k2 = k * (1.0 + (a - 1.0) * k_a)
        cum = _dot_exact_lhs(tri, lw)
        return dict(r=r, v=v, k2=k2, g=lo[:, 2 * C:], kk=kk, a=a, lw=lw, cum=cum)

    def rescaled(fr):
        cum, lw, kk, k2 = fr["cum"], fr["lw"], fr["kk"], fr["k2"]
        total = cum[L - 1:L, :]
        e_neg = jnp.exp(-cum)
        e_rem = jnp.exp(total - cum)
        kka = kk * fr["a"]
        return dict(fr, a_t=-kk * jnp.exp(cum - lw), b_t=kka * e_neg, k_t=k2 * e_neg, r_t=fr["r"] * jnp.exp(cum),
                    b_bar=kka * e_rem, k_bar=k2 * e_rem, p_total=jnp.exp(total))

    fronts = [front(bi) for bi in range(nb)]

    m0 = lax.broadcasted_iota(I32, (L, LANES), 1) < HEAD
    r2i = lax.broadcasted_iota(I32, (2 * L, 4 * L), 0)
    c2i = lax.broadcasted_iota(I32, (2 * L, 4 * L), 1) % (2 * L)
    strict, incl = c2i < r2i, c2i <= r2i
    npairs = RWKV_HEADS // 2
    _rwkv_chains({bi: rescaled(fronts[bi]) for bi in range(nb)}, range(nb), npairs, m0, strict, incl,
                 st_ref, o_ref, bd, r_k, lnx_g, lnx_b)


def _rwkv_chains(ew, group_rows, npairs, m0, strict, incl, st_ref, o_ref, bd, r_k, lnx_g, lnx_b):
    L = RWKV_CHUNK
    chains = [(bi, p) for bi in group_rows for p in range(npairs)]
    nc = range(len(chains))

    def stack2(name):
        out = []
        for bi, p in chains:
            xp = ew[bi][name][:, p * LANES:(p + 1) * LANES]
            out.append(jnp.concatenate([jnp.where(m0, xp, 0.0), jnp.where(m0, 0.0, xp)], axis=0))
        return out

    A2, B2, K2, R2 = stack2("a_t"), stack2("b_t"), stack2("k_t"), stack2("r_t")
    V2, Bb2, Kb2 = stack2("v"), stack2("b_bar"), stack2("k_bar")
    mg = [_dot_nt(jnp.concatenate([A2[c], R2[c]], axis=0), jnp.concatenate([B2[c], K2[c]], axis=0)) for c in nc]
    m_cat = [jnp.where(strict, mg[c][:2 * L], 0.0) for c in nc]
    g_cat = [jnp.where(incl, mg[c][2 * L:], 0.0) for c in nc]
    m_ab = [m_cat[c][:, :LANES] for c in nc]
    pw = [_dot(m_ab[c], m_ab[c]) for c in nc]
    tr = list(m_ab)
    k = 2
    while 2 * k < L:
        x = [_dot(pw[c], jnp.concatenate([pw[c], tr[c]], axis=1)) for c in nc]
        tr = [tr[c] + pw[c] + x[c][:, LANES:] for c in nc]
        pw = [x[c][:, :LANES] for c in nc]
        k *= 2
    tr = [tr[c] + pw[c] + _dot(pw[c], tr[c]) for c in nc]
    az = [jnp.concatenate([A2[c], _dot(m_cat[c][:, LANES:], V2[c])], axis=1) for c in nc]
    tz = [az[c] + _dot(tr[c], az[c]) for c in nc]
    st = [st_ref[bi, p] for bi, p in chains]
    u2 = [_dot_nt(tz[c][:, :LANES], st[c]) + tz[c][:, LANES:] for c in nc]
    uv = [jnp.concatenate([u2[c], V2[c]], axis=0) for c in nc]
    y2 = [_dot_nt(R2[c], st[c]) + _dot(g_cat[c], uv[c]) for c in nc]
    for c, (bi, p) in enumerate(chains):
        st_ref[bi, p] = (st[c] * ew[bi]["p_total"][:, p * LANES:(p + 1) * LANES]
                         + _dot_tn(uv[c], jnp.concatenate([Bb2[c], Kb2[c]], axis=0)))

    inv_n = 1.0 / HEAD
    for gi, bi in enumerate(group_rows):
        y = jnp.concatenate([y2[gi * npairs + p][:L] + y2[gi * npairs + p][L:] for p in range(npairs)], axis=1)
        e = ew[bi]
        mu = _dot_exact_rhs(y, bd) * inv_n
        d = y - mu
        var = _dot_exact_rhs(d * d, bd) * inv_n
        yn = d * lax.rsqrt(var + RWKV_GN_EPS) * lnx_g + lnx_b
        bonus = _dot_exact_rhs(e["r"] * e["k2"] * r_k, bd) * e["v"]
        o_ref[bi] = (yn + bonus) * e["g"]


def _rwkv_group(proj3, mu_rkv, mu_l, wl, vec, rkv_block, lora_block):
    b, s, _ = proj3.shape
    L, C = RWKV_CHUNK, RWKV_W
    head = jnp.arange(C, dtype=I32) // HEAD
    bd = (head[:, None] == head[None, :]).astype(BF16)
    t = jnp.arange(L, dtype=I32)
    tri = (t[None, :] <= t[:, None]).astype(BF16)
    const = lambda shape: pl.BlockSpec(shape, lambda i, j: (0,) * len(shape))
    nb = RWKV_BATCH if b % RWKV_BATCH == 0 else 1
    return pl.pallas_call(
        _rwkv_kernel,
        grid=(b // nb, s // L),
        in_specs=[pl.BlockSpec((nb, L, 3 * C), lambda i, j: (i, j, rkv_block)),
                  pl.BlockSpec((nb, L, LORA_PAD), lambda i, j: (i, j, lora_block)),
                  const((1, 3 * C)), const((1, LORA_PAD)), const((LORA_PAD, 3 * C)),
                  const((8, C)), const((C, C)), const((L, L))],
        out_specs=pl.BlockSpec((nb, L, C), lambda i, j: (i, j, 0)),
        out_shape=jax.ShapeDtypeStruct((b, s, C), F32),
        scratch_shapes=[pltpu.VMEM((nb, RWKV_HEADS // 2, LANES, LANES), F32),
                        pltpu.VMEM((nb, 1, 3 * C), F32), pltpu.VMEM((nb, 1, LORA_PAD), F32)],
        compiler_params=_params("parallel", "arbitrary"),
        name="rwkv_group",
    )(proj3, proj3, mu_rkv, mu_l, wl, vec, bd, tri)


def _diff_attn_kernel(qi_ref, kj_ref, diag_ref, q_ref, k_ref, v_ref, slope_ref, lam_ref, g_ref, o_ref,
                      q2_scr, m_scr, acc_scr, *, lambda_init):
    blk = ATT_BLOCK
    group = k_ref.shape[0] // blk
    step_id = pl.program_id(2)
    qi, kj, diag = qi_ref[step_id], kj_ref[step_id], diag_ref[step_id]
    log2e = math.log2(math.e)

    @pl.when(kj == 0)
    def _():
        q = q_ref[...].astype(F32) * (HEAD ** -0.5 * log2e)
        m0 = lax.broadcasted_iota(I32, (blk, LANES), 1) < HEAD
        q2_scr[0:blk, :] = jnp.where(m0, q, 0.0).astype(BF16)
        q2_scr[blk:, :] = jnp.where(m0, 0.0, q).astype(BF16)
        m_scr[...] = jnp.full_like(m_scr, NEG_BIG)
        acc_scr[...] = jnp.zeros_like(acc_scr)

    def step(kh, on_diagonal):
        k = k_ref[kh * blk:(kh + 1) * blk, :].astype(BF16)
        v_ext = jnp.concatenate([v_ref[kh * blk:(kh + 1) * blk, :].astype(BF16), jnp.ones((blk, LANES), BF16)],
                                axis=1)
        koff = lax.broadcasted_iota(I32, (1, blk), 1)
        col_bias = (slope_ref[:, :1] * log2e) * ((kj * group + kh - qi) * blk + koff).astype(F32)
        rc = ATT_ROW_CHUNK
        n_chunks = 2 * blk // rc

        def scores(c):
            return lax.dot_general(q2_scr[c * rc:(c + 1) * rc, :], k, (((1,), (1,)), ((), ())),
                                   preferred_element_type=F32)

        s_next = scores(0)
        for c in range(n_chunks):
            rows = slice(c * rc, (c + 1) * rc)
            s = s_next + col_bias
            if c + 1 < n_chunks:
                s_next = scores(c + 1)
            if on_diagonal:
                qoff = (c * rc) % blk + lax.broadcasted_iota(I32, (rc, 1), 0)
                s = jnp.where(koff <= qoff, s, NEG_BIG)
            m_prev = m_scr[rows, :]
            m_new = jnp.maximum(m_prev, jnp.max(s, axis=-1, keepdims=True))
            alpha = jnp.exp2(m_prev - m_new)
            p = jnp.exp2(s - jnp.concatenate([m_new] * (blk // LANES), axis=1))
            acc_scr[rows, :] = (jnp.concatenate([alpha, alpha], axis=1) * acc_scr[rows, :]
                                + jnp.dot(p.astype(BF16), v_ext, preferred_element_type=F32))
            m_scr[rows, :] = m_new

    @pl.when(diag < 0)
    def _():
        for kh in range(group):
            step(kh, False)

    def last_step(n_full):
        for kh in range(n_full):
            step(kh, False)
        step(n_full, True)
        lam_v = lam_ref[...]
        lam = (jnp.exp(jnp.sum(lam_v[0:1] * lam_v[1:2], axis=-1, keepdims=True))
               - jnp.exp(jnp.sum(lam_v[2:3] * lam_v[3:4], axis=-1, keepdims=True)) + lambda_init)
        o2 = acc_scr[:, :LANES] / acc_scr[:, LANES:]
        o = o2[:blk] - lam * o2[blk:]
        o_ref[...] = _rms(o, g_ref[...]) * (1.0 - lambda_init)

    for n_full in range(group):
        pl.when(diag == n_full)(functools.partial(last_step, n_full))


def _diff_attention(proj3, lam_vecs, subln_g, lambda_init):
    b, s, _ = proj3.shape
    blk = ATT_BLOCK
    nb = s // blk
    group = math.gcd(ATT_KEY_GROUP, nb)
    nh = DIFF_HEADS
    slopes = jnp.exp2(-8.0 * jnp.arange(1, nh + 1, dtype=F32) / nh)
    slopes = jnp.broadcast_to(slopes[:, None, None], (nh, 1, LANES))
    steps = [(qi, kj, qi % group if kj == qi // group else -1) for qi in range(nb) for kj in range(qi // group + 1)]
    tabs = [jnp.asarray([st[i] for st in steps], I32) for i in range(3)]
    kernel = functools.partial(_diff_attn_kernel, lambda_init=lambda_init)
    grid_spec = pltpu.PrefetchScalarGridSpec(
        num_scalar_prefetch=3,
        grid=(b, nh, len(steps)),
        in_specs=[pl.BlockSpec((None, blk, LANES), lambda bi, h, t, qt, kt, dt: (bi, qt[t], h)),
                  pl.BlockSpec((None, group * blk, LANES), lambda bi, h, t, qt, kt, dt: (bi, kt[t], nh + h)),
                  pl.BlockSpec((None, group * blk, LANES), lambda bi, h, t, qt, kt, dt: (bi, kt[t], 2 * nh + h)),
                  pl.BlockSpec((None, 1, LANES), lambda bi, h, t, qt, kt, dt: (h, 0, 0)),
                  pl.BlockSpec((4, HEAD), lambda bi, h, t, qt, kt, dt: (0, 0)),
                  pl.BlockSpec((1, 2 * HEAD), lambda bi, h, t, qt, kt, dt: (0, 0))],
        out_specs=pl.BlockSpec((None, blk, LANES), lambda bi, h, t, qt, kt, dt: (bi, qt[t], h)),
        scratch_shapes=[pltpu.VMEM((2 * blk, LANES), BF16), pltpu.VMEM((2 * blk, LANES), F32),
                        pltpu.VMEM((2 * blk, 2 * LANES), F32)],
    )
    return pl.pallas_call(
        kernel,
        grid_spec=grid_spec,
        out_shape=jax.ShapeDtypeStruct((b, s, DIFF_W), F32),
        compiler_params=_params("parallel", "parallel", "arbitrary"),
        name="diff_attention",
    )(*tabs, proj3, proj3, proj3, slopes, lam_vecs, subln_g)


def _mem_kv_kernel(m_ref, g_ref, w_ref, k_ref, v_ref):
    d = m_ref.shape[-1]
    kv = _dot(_rms(m_ref[...], g_ref[...]), w_ref[...])
    k_ref[...] = kv[:, :d].astype(BF16)
    v_ref[...] = kv[:, d:].astype(BF16)


def _mem_kv(mem, g, w_ckv_bf16):
    b, m, d = mem.shape
    return pl.pallas_call(
        _mem_kv_kernel,
        grid=(b,),
        in_specs=[pl.BlockSpec((None, m, d), lambda i: (i, 0, 0)),
                  pl.BlockSpec((1, d), lambda i: (0, 0)),
                  pl.BlockSpec((d, 2 * d), lambda i: (0, 0))],
        out_specs=[pl.BlockSpec((None, m, d), lambda i: (i, 0, 0))] * 2,
        out_shape=[jax.ShapeDtypeStruct((b, m, d), BF16)] * 2,
        compiler_params=_params("parallel"),
        name="mem_kv",
    )(mem, g, w_ckv_bf16)


def _mid_kernel(x_ref, yr_ref, yd_ref, wo_ref, gc_ref, wcq_ref, km_ref, vm_ref, wco_ref, gf_ref,
                wr_hi_ref, wr_lo_ref, br_ref, tri_ref,
                x2_ref, hn_ref, ri_ref, rg_ref, cnt_ref, carry_scr):
    tm, d = x_ref.shape
    first = jnp.logical_and(pl.program_id(0) == 0, pl.program_id(1) == 0)

    @pl.when(first)
    def _():
        carry_scr[...] = jnp.zeros_like(carry_scr)

    rc = tri_ref.shape[0]
    chains = range(tm // rc)
    rows = [slice(c * rc, (c + 1) * rc) for c in chains]
    half = yr_ref.shape[-1]
    x1 = [x_ref[r, :] + _dot(yr_ref[r, :], wo_ref[:half, :]) + _dot(yd_ref[r, :], wo_ref[half:, :]) for r in rows]

    q = [_dot(_rms(x1[c], gc_ref[...]), wcq_ref[...]) for c in chains]
    hd = d // CROSS_HEADS
    outs = [[] for _ in chains]

    def scores(h):
        sl = slice(h * hd, (h + 1) * hd)
        return [_dot_nt(q[c][:, sl], km_ref[:, sl]) * (hd ** -0.5) for c in chains]

    s_next = scores(0)
    for h in range(CROSS_HEADS):
        sl = slice(h * hd, (h + 1) * hd)
        s = s_next
        if h + 1 < CROSS_HEADS:
            s_next = scores(h + 1)
        p = [jnp.exp(s[c] - jnp.max(s[c], axis=-1, keepdims=True)) for c in chains]
        p = [p[c] / jnp.sum(p[c], axis=-1, keepdims=True) for c in chains]
        for c in chains:
            outs[c].append(_dot(p[c], vm_ref[:, sl]))
    x2 = [x1[c] + _dot(jnp.concatenate(outs[c], axis=1), wco_ref[...]) for c in chains]
    for c in chains:
        x2_ref[rows[c], :] = x2[c]

    hn = [_rms(x2[c], gf_ref[...]) for c in chains]
    _store_token_tiles(hn_ref, jnp.concatenate(hn, axis=0))
    hi = [hn[c].astype(BF16) for c in chains]
    lo = [(hn[c] - hi[c].astype(F32)).astype(BF16) for c in chains]
    vals = [(jnp.dot(hi[c], wr_hi_ref[...], preferred_element_type=F32)
             + jnp.dot(lo[c], wr_hi_ref[...], preferred_element_type=F32)
             + jnp.dot(hi[c], wr_lo_ref[...], preferred_element_type=F32) + br_ref[...]) for c in chains]
    lane = lax.broadcasted_iota(I32, (rc, LANES), 1)
    tops, idxs, hots = [[] for _ in chains], [[] for _ in chains], [[] for _ in chains]
    for _ in range(TOP_K):
        for c in chains:
            mx = jnp.max(vals[c], axis=-1, keepdims=True)
            idx = jnp.min(jnp.where(vals[c] == mx, lane, LANES), axis=-1, keepdims=True)
            hot = lane == idx
            vals[c] = jnp.where(hot, -jnp.inf, vals[c])
            tops[c].append(mx)
            idxs[c].append(idx)
            hots[c].append(hot)
    for c in chains:
        es = [jnp.exp(t - tops[c][0]) for t in tops[c]]
        denom = es[0] + es[1] + es[2] + es[3]
        sel = jnp.zeros((rc, LANES), F32)
        for hot in hots[c]:
            sel = sel + hot.astype(F32)
        before = jnp.dot(tri_ref[...], sel.astype(BF16), preferred_element_type=F32) + carry_scr[...]
        carry_scr[...] = carry_scr[...] + jnp.sum(sel, axis=0, keepdims=True)
        ri = jnp.zeros((rc, LANES), I32)
        rg = jnp.zeros((rc, LANES), F32)
        for j in range(TOP_K):
            rank = jnp.sum(jnp.where(hots[c][j], before, 0.0), axis=-1, keepdims=True).astype(I32)
            ri = jnp.where(lane == j, idxs[c][j], ri)
            ri = jnp.where(lane == TOP_K + j, rank, ri)
            rg = jnp.where(lane == j, es[j] / denom, rg)
        ri_ref[rows[c], :] = ri
        rg_ref[rows[c], :] = rg
    cnt_ref[...] = jnp.broadcast_to(carry_scr[...], cnt_ref.shape)


def _mid_stage(x, y_rwkv, y_diff, w_out, g_cross, w_cq, k_mem, v_mem, w_co, g_ffn, w_router, b_router):
    b, s, d = x.shape
    tm = min(MID_TM, s)
    m = k_mem.shape[1]
    half = y_rwkv.shape[-1]
    e = w_router.shape[1]
    wr = jnp.zeros((d, LANES), F32).at[:, :e].set(w_router)
    wr_hi = wr.astype(BF16)
    wr_lo = (wr - wr_hi.astype(F32)).astype(BF16)
    br = jnp.full((1, LANES), NEG_BIG, F32).at[0, :e].set(b_router)
    rc = min(MID_CHAIN_ROWS, tm)
    t = jnp.arange(rc, dtype=I32)
    tri = (t[None, :] < t[:, None]).astype(BF16)
    tile = lambda w: pl.BlockSpec((None, tm, w), lambda i, j: (i, j, 0))
    const = lambda shape: pl.BlockSpec(shape, lambda i, j: (0,) * len(shape))
    return pl.pallas_call(
        _mid_kernel,
        grid=(b, s // tm),
        in_specs=[tile(d), tile(half), tile(half), const((d, d)), const((1, d)), const((d, d)),
                  pl.BlockSpec((None, m, d), lambda i, j: (i, 0, 0)),
                  pl.BlockSpec((None, m, d), lambda i, j: (i, 0, 0)),
                  const((d, d)), const((1, d)), const((d, LANES)), const((d, LANES)),
                  const((1, LANES)), const((rc, rc))],
        out_specs=[tile(d), pl.BlockSpec((tm * TILE_ROWS, LANES), lambda i, j: (i * (s // tm) + j, 0)),
                   tile(LANES), tile(LANES), const((8, LANES))],
        out_shape=[jax.ShapeDtypeStruct((b, s, d), F32), jax.ShapeDtypeStruct((b * s * TILE_ROWS, LANES), F32),
                   jax.ShapeDtypeStruct((b, s, LANES), I32), jax.ShapeDtypeStruct((b, s, LANES), F32),
                   jax.ShapeDtypeStruct((8, LANES), F32)],
        scratch_shapes=[pltpu.VMEM((1, LANES), F32)],
        compiler_params=_params("arbitrary", "arbitrary"),
        name="outproj_cross_router",
    )(x, y_rwkv, y_diff, w_out.astype(BF16), g_cross, w_cq.astype(BF16), k_mem, v_mem,
      w_co.astype(BF16), g_ffn, wr_hi, wr_lo, br, tri)


def _expert_kernel(be_ref, nblk_ref, nused_ref, x_ref, w1_hbm, b1_ref, w2_hbm, b2_ref, o_ref,
                   w1_stage, w2_stage, w1_scr, w2_scr, w2i_scr, sem):
    i = pl.program_id(0)
    n_used = nused_ref[0]
    used = i < n_used
    expert = be_ref[i]
    new_expert = jnp.logical_or(i == 0, expert != be_ref[jnp.maximum(i - 1, 0)])
    d, f = w1_stage.shape[0], w2_stage.shape[0]

    def weight_copies(ex):
        return (pltpu.make_async_copy(w1_hbm.at[ex], w1_stage, sem.at[0]),
                pltpu.make_async_copy(w2_hbm.at[ex], w2_stage, sem.at[1]))

    @pl.when(jnp.logical_and(used, i == 0))
    def _():
        for cp in weight_copies(expert):
            cp.start()

    @pl.when(jnp.logical_and(used, new_expert))
    def _():
        for cp in weight_copies(expert):
            cp.wait()
        step = 128

        def cast_rows(c, carry):
            rows = pl.ds(pl.multiple_of(c * step, step), step)
            w1_scr[rows, :] = w1_stage[rows, :].astype(BF16)
            return carry

        lax.fori_loop(0, d // step, cast_rows, 0)
        for cb in range(w2_stage.shape[1] // LANES):
            cols = slice(cb * LANES, (cb + 1) * LANES)
            w2i_scr[cb, pl.ds(0, f // 2, stride=2), :] = w2_stage[:f // 2, cols]
            w2i_scr[cb, pl.ds(1, f // 2, stride=2), :] = w2_stage[f // 2:, cols]
            w2_scr[:, cols] = w2i_scr[cb].astype(BF16)
        next_run = i + nblk_ref[expert]

        @pl.when(next_run < n_used)
        def _():
            for cp in weight_copies(be_ref[jnp.minimum(next_run, be_ref.shape[0] - 1)]):
                cp.start()

    @pl.when(used)
    def _():
        rc = MOE_CHAIN_ROWS
        even = lax.broadcasted_iota(I32, (rc, LANES), 1) % 2 == 0
        cw = 2 * LANES
        n_stage = f // cw
        stages = [(h, j) for h in range(MOE_ROWS // rc) for j in range(n_stage)]
        xs = {}

        def hidden_pair(stage):
            h, j = stage
            if h not in xs:
                xs[h] = _load_token_tiles(x_ref, rc, first=h * rc).astype(BF16)
            lo, hi = slice(j * cw, (j + 1) * cw), slice(f + j * cw, f + (j + 1) * cw)
            return (jnp.dot(xs[h], w1_scr[:, lo], preferred_element_type=F32) + b1_ref[:, lo],
                    jnp.dot(xs[h], w1_scr[:, hi], preferred_element_type=F32) + b1_ref[:, hi])

        def act_block(h):
            lin = jnp.clip(pltpu.roll(h, LANES - 1, axis=1), -SWIGLU_LIMIT, SWIGLU_LIMIT)
            glu = jnp.minimum(h, SWIGLU_LIMIT)
            return jnp.where(even, glu * jax.nn.sigmoid(SWIGLU_ALPHA * glu) * (lin + 1.0), 0.0)

        y = None
        pending = hidden_pair(stages[0])
        for n, (h, j) in enumerate(stages):
            ha, hb = pending
            if n + 1 < len(stages):
                pending = hidden_pair(stages[n + 1])
            packed = jnp.concatenate(
                [act_block(ha[:, c * LANES:(c + 1) * LANES])
                 + pltpu.roll(act_block(hb[:, c * LANES:(c + 1) * LANES]), 1, axis=1) for c in range(cw // LANES)],
                axis=1)
            part = jnp.dot(packed.astype(BF16), w2_scr[j * cw:(j + 1) * cw, :], preferred_element_type=F32)
            y = b2_ref[...] + part if j == 0 else y + part
            if j == n_stage - 1:
                _store_token_tiles(o_ref, y, first=h * rc)

    @pl.when(jnp.logical_not(used))
    def _():
        o_ref[...] = jnp.zeros_like(o_ref)


def _expert_ffn(x_sorted, block_e, blocks_per_expert, n_used, w1, b1, w2, b2):
    e, d, f2 = w1.shape
    r = MOE_ROWS
    n_rows = block_e.shape[0] * r
    grid_spec = pltpu.PrefetchScalarGridSpec(
        num_scalar_prefetch=3,
        grid=(n_rows // r,),
        in_specs=[pl.BlockSpec((r * TILE_ROWS, LANES), lambda i, be, nb, nu: (i, 0)),
                  pl.BlockSpec(memory_space=pl.ANY),
                  pl.BlockSpec((None, 1, f2), lambda i, be, nb, nu: (be[i], 0, 0)),
                  pl.BlockSpec(memory_space=pl.ANY),
                  pl.BlockSpec((None, 1, d), lambda i, be, nb, nu: (be[i], 0, 0))],
        out_specs=pl.BlockSpec((r * TILE_ROWS, LANES), lambda i, be, nb, nu: (i, 0)),
        scratch_shapes=[pltpu.VMEM((d, f2), F32), pltpu.VMEM((f2 // 2, d), F32),
                        pltpu.VMEM((d, f2), BF16), pltpu.VMEM((f2 // 2, d), BF16),
                        pltpu.VMEM((d // LANES, f2 // 2, LANES), F32), pltpu.SemaphoreType.DMA((2,))],
    )
    return pl.pallas_call(
        _expert_kernel,
        grid_spec=grid_spec,
        out_shape=jax.ShapeDtypeStruct((n_rows * TILE_ROWS, LANES), F32),
        compiler_params=_params("arbitrary"),
        name="expert_ffn",
    )(block_e, blocks_per_expert, n_used, x_sorted, w1, b1.reshape(e, 1, f2), w2, b2.reshape(e, 1, d))


def _dispatch_kernel(pad_lo_ref, pad_hi_ref, dst_ref, li_ref, hn_ref, xs_hbm, buf, zero_scr, sem, pad_sem, *, tm):
    @pl.when(pl.program_id(0) == 0)
    def _():
        zero_scr[...] = jnp.zeros_like(zero_scr)

        def per_range(e, carry):
            pos, hi = pad_lo_ref[e], pad_hi_ref[e]
            for size in ZERO_FILL_ROWS:
                n = lax.div(hi - pos, size)

                def pad_copy(k, pos=pos, size=size):
                    first = pl.multiple_of((pos + k * size) * TILE_ROWS, TILE_ROWS)
                    return pltpu.make_async_copy(zero_scr.at[pl.ds(0, size * TILE_ROWS)],
                                                 xs_hbm.at[pl.ds(first, size * TILE_ROWS)], pad_sem)

                lax.fori_loop(0, n, lambda k, c, cp=pad_copy: (cp(k).start(), c)[1], 0)
                lax.fori_loop(0, n, lambda k, c, cp=pad_copy: (cp(k).wait(), c)[1], 0)
                pos = pos + n * size
            return carry

        lax.fori_loop(0, pad_lo_ref.shape[0], per_range, 0)

        buf[...] = jnp.zeros_like(buf)

    i = pl.program_id(0)
    n_chunks = dst_ref.shape[1]
    chunk = SEG_CHUNK * TILE_ROWS
    slot_rows = n_chunks * chunk
    slot = i % 2
    base = slot * slot_rows

    def place(t, carry):
        tile = hn_ref[pl.ds(pl.multiple_of(t * TILE_ROWS, TILE_ROWS), TILE_ROWS), :]
        for j in range(TOP_K):
            buf[pl.ds(pl.multiple_of(li_ref[0, t * TOP_K + j], TILE_ROWS), TILE_ROWS), :] = tile
        return carry

    lax.fori_loop(0, tm, place, 0, unroll=8)

    def slot_wait(sl):
        whole = buf.at[pl.ds(pl.multiple_of(sl * slot_rows, TILE_ROWS), slot_rows)]
        pltpu.make_async_copy(whole, whole, sem.at[sl]).wait()

    @pl.when(i > 0)
    def _():
        slot_wait(1 - slot)

    def start(q, carry):
        pltpu.make_async_copy(buf.at[pl.ds(pl.multiple_of(base + q * chunk, TILE_ROWS), chunk)],
                              xs_hbm.at[pl.ds(pl.multiple_of(dst_ref[0, q], TILE_ROWS), chunk)],
                              sem.at[slot]).start(priority=1)
        return carry

    lax.fori_loop(0, n_chunks, start, 0, unroll=8)

    @pl.when(i == pl.num_programs(0) - 1)
    def _():
        slot_wait(slot)


def _dispatch(hn_tiles, chunk_dst, local_rows, pad_lo, pad_hi, n_rows):
    t = hn_tiles.shape[0] // TILE_ROWS
    tm = COMBINE_TM
    steps = t // tm
    n_chunks = chunk_dst.shape[-1]
    smem = lambda w: pl.BlockSpec((None, 1, w), lambda i, lo, hi: (i, 0, 0), memory_space=pltpu.SMEM)
    grid_spec = pltpu.PrefetchScalarGridSpec(
        num_scalar_prefetch=2,
        grid=(steps,),
        in_specs=[smem(n_chunks), smem(tm * TOP_K), pl.BlockSpec((tm * TILE_ROWS, LANES), lambda i, lo, hi: (i, 0))],
        out_specs=pl.BlockSpec(memory_space=pl.ANY),
        scratch_shapes=[pltpu.VMEM((2 * n_chunks * SEG_CHUNK * TILE_ROWS, LANES), F32),
                        pltpu.VMEM((ZERO_FILL_ROWS[0] * TILE_ROWS, LANES), F32), pltpu.SemaphoreType.DMA((2,)),
                        pltpu.SemaphoreType.DMA(())],
    )
    return pl.pallas_call(
        functools.partial(_dispatch_kernel, tm=tm),
        grid_spec=grid_spec,
        out_shape=jax.ShapeDtypeStruct((n_rows * TILE_ROWS, LANES), F32),
        compiler_params=_params("arbitrary"),
        name="moe_dispatch",
    )(pad_lo, pad_hi, chunk_dst.reshape(steps, 1, n_chunks), local_rows.reshape(steps, 1, tm * TOP_K), hn_tiles)


def _combine_seg_kernel(src_cur_ref, src_nxt_ref, li_ref, gate_ref, x_ref, g_ref, ys_hbm, o_ref, buf, acc_scr, sem,
                        *, tm, final):
    i = pl.program_id(0)
    n_chunks = src_cur_ref.shape[1]
    chunk = SEG_CHUNK * TILE_ROWS
    slot_rows = n_chunks * chunk

    def fetch(src_ref, slot):
        def start(q, carry):
            pltpu.make_async_copy(ys_hbm.at[pl.ds(pl.multiple_of(src_ref[0, q], TILE_ROWS), chunk)],
                                  buf.at[pl.ds(pl.multiple_of(slot * slot_rows + q * chunk, TILE_ROWS), chunk)],
                                  sem.at[slot]).start(priority=1)
            return carry
        lax.fori_loop(0, n_chunks, start, 0, unroll=8)

    @pl.when(i == 0)
    def _():
        fetch(src_cur_ref, 0)

    @pl.when(i + 1 < pl.num_programs(0))
    def _():
        fetch(src_nxt_ref, (i + 1) % 2)

    slot = i % 2
    base = slot * slot_rows
    whole = buf.at[pl.ds(pl.multiple_of(base, TILE_ROWS), slot_rows)]
    pltpu.make_async_copy(whole, whole, sem.at[slot]).wait()

    def token(t, carry):
        acc = jnp.zeros((TILE_ROWS, LANES), F32)
        for j in range(TOP_K):
            s = t * TOP_K + j
            row = pl.multiple_of(li_ref[0, s], TILE_ROWS)
            acc = acc + gate_ref[0, s] * buf[pl.ds(row, TILE_ROWS), :]
        acc_scr[pl.ds(pl.multiple_of(t * TILE_ROWS, TILE_ROWS), TILE_ROWS), :] = acc
        return carry

    lax.fori_loop(0, tm, token, 0, unroll=8)
    y = x_ref[...] + _load_token_tiles(acc_scr, tm)
    o_ref[...] = _rms(y, g_ref[...]) if final else y


def _combine_segments(x2d, gates, chunk_src, li, y_sorted, g, final):
    t, d = x2d.shape
    tm = COMBINE_TM
    steps = t // tm
    n_chunks = chunk_src.shape[-1]
    src3 = chunk_src.reshape(steps, 1, n_chunks)
    smem = lambda w, imap: pl.BlockSpec((None, 1, w), imap, memory_space=pltpu.SMEM)
    return pl.pallas_call(
        functools.partial(_combine_seg_kernel, tm=tm, final=final),
        grid=(steps,),
        in_specs=[smem(n_chunks, lambda i: (i, 0, 0)),
                  smem(n_chunks, lambda i: (jnp.minimum(i + 1, steps - 1), 0, 0)),
                  smem(tm * TOP_K, lambda i: (i, 0, 0)), smem(tm * TOP_K, lambda i: (i, 0, 0)),
                  pl.BlockSpec((tm, d), lambda i: (i, 0)),
                  pl.BlockSpec((1, d), lambda i: (0, 0)), pl.BlockSpec(memory_space=pl.ANY)],
        out_specs=pl.BlockSpec((tm, d), lambda i: (i, 0)),
        out_shape=jax.ShapeDtypeStruct((t, d), F32),
        scratch_shapes=[pltpu.VMEM((2 * n_chunks * SEG_CHUNK * TILE_ROWS, LANES), F32),
                        pltpu.VMEM((tm * TILE_ROWS, LANES), F32), pltpu.SemaphoreType.DMA((2,))],
        compiler_params=_params("arbitrary"),
        name="combine_norm",
    )(src3, src3, li.reshape(steps, 1, tm * TOP_K), gates.reshape(steps, 1, tm * TOP_K), x2d, g, y_sorted)


def _layer(x, mem, lyr, final, p):
    b, s, d = x.shape
    t = b * s
    c = RWKV_W
    rwkv_in = 3 * c + LORA_W

    w_in = p["w_in"][lyr]
    w_cat = jnp.concatenate([w_in[:, rwkv_in:], w_in[:, :3 * c], w_in[:, 3 * c:rwkv_in],
                             jnp.zeros((d, LORA_PAD - LORA_W), F32)], axis=1).astype(BF16)
    diff_cols = 3 * DIFF_W
    proj_diff, proj_rwkv = _norm_inproj(x.reshape(t, d), p["norm_mix_g"][lyr][None], w_cat, diff_cols)
    proj_diff, proj_rwkv = proj_diff.reshape(b, s, -1), proj_rwkv.reshape(b, s, -1)

    mu = p["shift_mu"][lyr]
    mu_l = jnp.zeros((1, LORA_PAD), F32).at[0, :LORA_W].set(mu[3 * c:])
    wl = jnp.zeros((LORA_PAD, 3 * c), F32)
    wl = wl.at[:DECAY_LORA, :c].set(p["w_decay_up"][lyr])
    wl = wl.at[DECAY_LORA:DECAY_LORA + ICLR_LORA, c:2 * c].set(p["w_iclr_up"][lyr])
    wl = wl.at[DECAY_LORA + ICLR_LORA:LORA_W, 2 * c:].set(p["w_gate_up"][lyr])
    vec = jnp.stack([p["w0"][lyr], p["a0"][lyr], p["k_k"][lyr], p["k_a"][lyr], p["r_k"][lyr].reshape(c),
                     p["lnx_g"][lyr], p["lnx_b"][lyr], jnp.zeros((c,), F32)])
    y_rwkv = _rwkv_group(proj_rwkv, mu[None, :3 * c], mu_l, wl.astype(BF16), vec,
                         rkv_block=0, lora_block=3 * c // LORA_PAD)

    lambda_init = 0.8 - 0.6 * math.exp(-0.3 * lyr)
    lam_vecs = jnp.stack([p["lambda_q1"][lyr], p["lambda_k1"][lyr], p["lambda_q2"][lyr], p["lambda_k2"][lyr]])
    y_diff = _diff_attention(proj_diff, lam_vecs, p["subln_g"][lyr][None], lambda_init)

    k_mem, v_mem = _mem_kv(mem, p["norm_mem_g"][lyr][None], p["w_ckv"][lyr].astype(BF16))
    x2, hn, route_i, route_g, counts = _mid_stage(
        x, y_rwkv, y_diff, p["w_out"][lyr], p["norm_cross_g"][lyr][None], p["w_cq"][lyr], k_mem, v_mem,
        p["w_co"][lyr], p["norm_ffn_g"][lyr][None], p["w_router"][lyr], p["b_router"][lyr])

    e = N_EXPERTS
    r = MOE_ROWS
    n_blocks = (t * TOP_K) // r + e + 1
    n_slots = COMBINE_TM * TOP_K // SEG_CHUNK + e
    n_rows = n_blocks * r + n_slots * SEG_CHUNK
    route_i = route_i.reshape(t, LANES)
    idx, rank = route_i[:, :TOP_K], route_i[:, TOP_K:2 * TOP_K]
    cnt = counts[0, :e].astype(I32)
    padded = (cnt + SEG_CHUNK - 1 + r - 1) // r * r
    pad_end = jnp.cumsum(padded)
    pad_start = pad_end - padded
    block_start = jnp.arange(n_blocks, dtype=I32) * r
    block_e = jnp.minimum(jnp.sum((block_start[:, None] >= pad_end[None, :]).astype(I32), axis=1), e - 1)
    n_used = pad_end[-1:] // r

    onehot = idx[:, :, None] == jnp.arange(e, dtype=I32)
    tmc = COMBINE_TM
    nt = t // tmc
    per_tile = jnp.sum(onehot.reshape(nt, tmc * TOP_K, e), axis=1, dtype=I32)
    before = jnp.cumsum(per_tile, axis=0) - per_tile
    n_chunks = (per_tile + SEG_CHUNK - 1) // SEG_CHUNK
    chunk_end = jnp.cumsum(n_chunks, axis=1)
    chunk_first = chunk_end - n_chunks
    shift = jnp.repeat(chunk_first * SEG_CHUNK - before, tmc, axis=0)
    local_row = (jnp.sum(jnp.where(onehot, shift[:, None, :], 0), axis=-1) + rank) * TILE_ROWS
    slot_rows = n_slots * SEG_CHUNK * TILE_ROWS
    local_row = (local_row.reshape(nt, -1) + (jnp.arange(nt, dtype=I32) % 2 * slot_rows)[:, None]).reshape(-1)
    slots = jnp.arange(n_slots, dtype=I32)
    owner = jnp.sum((slots[None, :, None] >= chunk_end[:, None, :]).astype(I32), axis=-1)
    own = owner[:, :, None] == jnp.arange(e, dtype=I32)
    seg_row = jnp.sum(jnp.where(own, (pad_start[None, :] + before - chunk_first * SEG_CHUNK)[:, None, :], 0), axis=-1)
    chunk_row = seg_row + slots[None, :] * SEG_CHUNK
    chunk_dst = jnp.where(owner < e, chunk_row, n_blocks * r + slots[None, :] * SEG_CHUNK) * TILE_ROWS
    chunk_src = jnp.where(owner < e, chunk_row, 0) * TILE_ROWS

    unused_lo = jnp.concatenate([pad_start + cnt, pad_end[-1:]])
    unused_hi = jnp.concatenate([pad_end, jnp.full((1,), n_rows, I32)])
    x_sorted = _dispatch(hn, chunk_dst, local_row, unused_lo, unused_hi, n_rows)
    y_sorted = _expert_ffn(x_sorted, block_e, padded // r, n_used, p["w1"][lyr], p["b1"][lyr], p["w2"][lyr],
                           p["b2"][lyr])
    gates = route_g.reshape(t, LANES)[:, :TOP_K]
    return _combine_segments(x2.reshape(t, d), gates, chunk_src, local_row, y_sorted,
                             p["norm_final_g"][None], final).reshape(b, s, d)


def kernel(x, mem, norm_mix_g, w_in, shift_mu, w0, w_decay_up, a0, w_iclr_up, w_gate_up, k_k, k_a, r_k,
           lnx_g, lnx_b, lambda_q1, lambda_k1, lambda_q2, lambda_k2, subln_g, w_out, norm_cross_g,
           norm_mem_g, w_cq, w_ckv, w_co, norm_ffn_g, w_router, b_router, w1, b1, w2, b2, norm_final_g):
    p = dict(norm_mix_g=norm_mix_g, w_in=w_in, shift_mu=shift_mu, w0=w0, w_decay_up=w_decay_up, a0=a0,
             w_iclr_up=w_iclr_up, w_gate_up=w_gate_up, k_k=k_k, k_a=k_a, r_k=r_k, lnx_g=lnx_g, lnx_b=lnx_b,
             lambda_q1=lambda_q1, lambda_k1=lambda_k1, lambda_q2=lambda_q2, lambda_k2=lambda_k2,
             subln_g=subln_g, w_out=w_out, norm_cross_g=norm_cross_g, norm_mem_g=norm_mem_g, w_cq=w_cq,
             w_ckv=w_ckv, w_co=w_co, norm_ffn_g=norm_ffn_g, w_router=w_router, b_router=b_router,
             w1=w1, b1=b1, w2=w2, b2=b2, norm_final_g=norm_final_g)
    depth = w_in.shape[0]
    b, s, d = x.shape
    assert d == TILE_ROWS * LANES and w_in.shape[2] == 3 * RWKV_W + LORA_W + 3 * DIFF_W
    assert w_router.shape[2] == N_EXPERTS and w1.shape[2:] == (d, 2 * w2.shape[2])
    assert s % max(ATT_BLOCK, MID_TM, RWKV_CHUNK) == 0 and (b * s) % PROJ_TM == 0
    for lyr in range(depth):
        x = _layer(x, mem, lyr, lyr == depth - 1, p)
    return x
```

```python
import functools
import math

import jax
import jax.numpy as jnp
from jax import lax
from jax.experimental import pallas as pl
from jax.experimental.pallas import tpu as pltpu

F32 = jnp.float32
BF16 = jnp.bfloat16
I32 = jnp.int32

NORM_EPS = 1e-5
HEAD = 64
RWKV_W = 512
RWKV_HEADS = RWKV_W // HEAD
DECAY_LORA, ICLR_LORA, GATE_LORA = 64, 64, 160
LORA_W = DECAY_LORA + ICLR_LORA + GATE_LORA
LORA_PAD = 384
RWKV_GN_EPS = 64e-5
DIFF_W = 512
DIFF_HEADS = DIFF_W // (2 * HEAD)
CROSS_HEADS = 4
N_EXPERTS = 32
TOP_K = 4
SWIGLU_LIMIT = 7.0
SWIGLU_ALPHA = 1.702
LANES = 128
TILE_ROWS = 8
NEG_BIG = -1e30

RWKV_CHUNK = 64
RWKV_BATCH = 4
PROJ_TM = 512
ATT_BLOCK = 512
ATT_ROW_CHUNK = 128
ATT_KEY_GROUP = 4
MID_TM = 512
MID_CHAIN_ROWS = 256
MOE_ROWS = 256
MOE_CHAIN_ROWS = 256
COMBINE_TM = 256
SEG_CHUNK = 8
ZERO_FILL_ROWS = (64, 8, 1)
VMEM_LIMIT = 56 * 1024 * 1024


def _dot(a, b):
    return jnp.dot(a.astype(BF16), b.astype(BF16), preferred_element_type=F32)


def _dot_nt(a, b):
    return lax.dot_general(a.astype(BF16), b.astype(BF16), (((1,), (1,)), ((), ())),
                           preferred_element_type=F32)


def _dot_tn(a, b):
    return lax.dot_general(a.astype(BF16), b.astype(BF16), (((0,), (0,)), ((), ())),
                           preferred_element_type=F32)


def _split2(x):
    hi = x.astype(BF16)
    lo = (x - hi.astype(F32)).astype(BF16)
    return hi, lo


def _dot_exact_rhs(x, ones_bf16):
    hi, lo = _split2(x)
    return jnp.dot(hi, ones_bf16, preferred_element_type=F32) + jnp.dot(lo, ones_bf16, preferred_element_type=F32)


def _dot_exact_lhs(ones_bf16, x):
    hi, lo = _split2(x)
    return jnp.dot(ones_bf16, hi, preferred_element_type=F32) + jnp.dot(ones_bf16, lo, preferred_element_type=F32)


def _store_token_tiles(ref, x, first=0):
    rows = x.shape[0]
    for c in range(TILE_ROWS):
        ref[pl.ds(first * TILE_ROWS + c, rows, stride=TILE_ROWS), :] = x[:, c * LANES:(c + 1) * LANES]


def _load_token_tiles(ref, rows, first=0):
    return jnp.concatenate([ref[pl.ds(first * TILE_ROWS + c, rows, stride=TILE_ROWS), :]
                            for c in range(TILE_ROWS)], axis=1)


def _rms(x, g):
    return x * lax.rsqrt(jnp.mean(x * x, axis=-1, keepdims=True) + NORM_EPS) * g


def _params(*sem):
    return pltpu.CompilerParams(dimension_semantics=sem, vmem_limit_bytes=VMEM_LIMIT)


def _norm_inproj_kernel(x_ref, g_ref, w_ref, od_ref, or_ref):
    h = _rms(x_ref[...], g_ref[...]).astype(BF16)
    nd = od_ref.shape[1]
    od_ref[...] = jnp.dot(h, w_ref[:, :nd], preferred_element_type=F32).astype(BF16)
    or_ref[...] = jnp.dot(h, w_ref[:, nd:], preferred_element_type=F32)


def _norm_inproj(x2d, g, w_bf16, n_diff):
    t, d = x2d.shape
    n = w_bf16.shape[1]
    tm = min(PROJ_TM, t)
    return pl.pallas_call(
        _norm_inproj_kernel,
        grid=(t // tm,),
        in_specs=[pl.BlockSpec((tm, d), lambda i: (i, 0)),
                  pl.BlockSpec((1, d), lambda i: (0, 0)),
                  pl.BlockSpec((d, n), lambda i: (0, 0))],
        out_specs=[pl.BlockSpec((tm, n_diff), lambda i: (i, 0)), pl.BlockSpec((tm, n - n_diff), lambda i: (i, 0))],
        out_shape=[jax.ShapeDtypeStruct((t, n_diff), BF16), jax.ShapeDtypeStruct((t, n - n_diff), F32)],
        compiler_params=_params("parallel"),
        name="norm_inproj",
    )(x2d, g, w_bf16)


def _rwkv_kernel(rkv_ref, lora_ref, mu_rkv_ref, mu_l_ref, wl_ref, vec_ref, bd_ref, tri_ref,
                 o_ref, st_ref, prev_rkv_ref, prev_l_ref):
    L = RWKV_CHUNK
    C = RWKV_W

    @pl.when(pl.program_id(1) == 0)
    def _():
        st_ref[...] = jnp.zeros_like(st_ref)
        prev_rkv_ref[...] = jnp.zeros_like(prev_rkv_ref)
        prev_l_ref[...] = jnp.zeros_like(prev_l_ref)

    row = lax.broadcasted_iota(I32, (L, 1), 0)
    w0, a0, k_k, k_a = vec_ref[0:1, :], vec_ref[1:2, :], vec_ref[2:3, :], vec_ref[3:4, :]
    r_k, lnx_g, lnx_b = vec_ref[4:5, :], vec_ref[5:6, :], vec_ref[6:7, :]
    bd = bd_ref[...]
    tri = tri_ref[...]
    lane_l = lax.broadcasted_iota(I32, (L, LORA_PAD), 1)
    nb = rkv_ref.shape[0]

    def token_shift(raw, prev_ref, bi, mu):
        prev = jnp.where(row == 0, prev_ref[bi], pltpu.roll(raw, 1, axis=0))
        prev_ref[bi] = raw[L - 1:L, :]
        return raw + mu * (prev - raw)

    def front(bi):
        u = token_shift(rkv_ref[bi], prev_rkv_ref, bi, mu_rkv_ref[...])
        ul = token_shift(lora_ref[bi], prev_l_ref, bi, mu_l_ref[...])
        r, k, v = u[:, :C], u[:, C:2 * C], u[:, 2 * C:]
        act = jnp.where(lane_l < DECAY_LORA, jnp.tanh(ul),
                        jnp.where(lane_l < DECAY_LORA + ICLR_LORA, ul, jax.nn.sigmoid(ul)))
        lo = _dot(act, wl_ref[...])
        w_log = -jax.nn.softplus(-(w0 + lo[:, :C])) - 0.5
        lw = -jnp.exp(w_log)
        a = jax.nn.sigmoid(a0 + lo[:, C:2 * C])
        kk = k * k_k
        kk = kk / jnp.maximum(jnp.sqrt(_dot(kk * kk, bd)), 1e-12)
        k2 = k * (1.0 + (a - 1.0) * k_a)
        cum = _dot_exact_lhs(tri, lw)
        return dict(r=r, v=v, k2=k2, g=lo[:, 2 * C:], kk=kk, a=a, lw=lw, cum=cum)

    def rescaled(fr):
        cum, lw, kk, k2 = fr["cum"], fr["lw"], fr["kk"], fr["k2"]
        total = cum[L - 1:L, :]
        e_neg = jnp.exp(-cum)
        e_rem = jnp.exp(total - cum)
        kka = kk * fr["a"]
        return dict(fr, a_t=-kk * jnp.exp(cum - lw), b_t=kka * e_neg, k_t=k2 * e_neg, r_t=fr["r"] * jnp.exp(cum),
                    b_bar=kka * e_rem, k_bar=k2 * e_rem, p_total=jnp.exp(total))

    fronts = [front(bi) for bi in range(nb)]

    m0 = lax.broadcasted_iota(I32, (L, LANES), 1) < HEAD
    r2i = lax.broadcasted_iota(I32, (2 * L, 4 * L), 0)
    c2i = lax.broadcasted_iota(I32, (2 * L, 4 * L), 1) % (2 * L)
    strict, incl = c2i < r2i, c2i <= r2i
    npairs = RWKV_HEADS // 2
    _rwkv_chains({bi: rescaled(fronts[bi]) for bi in range(nb)}, range(nb), npairs, m0, strict, incl,
                 st_ref, o_ref, bd, r_k, lnx_g, lnx_b)


def _rwkv_chains(ew, group_rows, npairs, m0, strict, incl, st_ref, o_ref, bd, r_k, lnx_g, lnx_b):
    L = RWKV_CHUNK
    chains = [(bi, p) for bi in group_rows for p in range(npairs)]
    nc = range(len(chains))

    def stack2(name):
        out = []
        for bi, p in chains:
            xp = ew[bi][name][:, p * LANES:(p + 1) * LANES]
            out.append(jnp.concatenate([jnp.where(m0, xp, 0.0), jnp.where(m0, 0.0, xp)], axis=0))
        return out

    A2, B2, K2, R2 = stack2("a_t"), stack2("b_t"), stack2("k_t"), stack2("r_t")
    V2, Bb2, Kb2 = stack2("v"), stack2("b_bar"), stack2("k_bar")
    mg = [_dot_nt(jnp.concatenate([A2[c], R2[c]], axis=0), jnp.concatenate([B2[c], K2[c]], axis=0)) for c in nc]
    m_cat = [jnp.where(strict, mg[c][:2 * L], 0.0) for c in nc]
    g_cat = [jnp.where(incl, mg[c][2 * L:], 0.0) for c in nc]
    m_ab = [m_cat[c][:, :LANES] for c in nc]
    pw = [_dot(m_ab[c], m_ab[c]) for c in nc]
    tr = list(m_ab)
    k = 2
    while 2 * k < L:
        x = [_dot(pw[c], jnp.concatenate([pw[c], tr[c]], axis=1)) for c in nc]
        tr = [tr[c] + pw[c] + x[c][:, LANES:] for c in nc]
        pw = [x[c][:, :LANES] for c in nc]
        k *= 2
    tr = [tr[c] + pw[c] + _dot(pw[c], tr[c]) for c in nc]
    az = [jnp.concatenate([A2[c], _dot(m_cat[c][:, LANES:], V2[c])], axis=1) for c in nc]
    tz = [az[c] + _dot(tr[c], az[c]) for c in nc]
    st = [st_ref[bi, p] for bi, p in chains]
    u2 = [_dot_nt(tz[c][:, :LANES], st[c]) + tz[c][:, LANES:] for c in nc]
    uv = [jnp.concatenate([u2[c], V2[c]], axis=0) for c in nc]
    y2 = [_dot_nt(R2[c], st[c]) + _dot(g_cat[c], uv[c]) for c in nc]
    for c, (bi, p) in enumerate(chains):
        st_ref[bi, p] = (st[c] * ew[bi]["p_total"][:, p * LANES:(p + 1) * LANES]
                         + _dot_tn(uv[c], jnp.concatenate([Bb2[c], Kb2[c]], axis=0)))

    inv_n = 1.0 / HEAD
    for gi, bi in enumerate(group_rows):
        y = jnp.concatenate([y2[gi * npairs + p][:L] + y2[gi * npairs + p][L:] for p in range(npairs)], axis=1)
        e = ew[bi]
        mu = _dot_exact_rhs(y, bd) * inv_n
        d = y - mu
        var = _dot_exact_rhs(d * d, bd) * inv_n
        yn = d * lax.rsqrt(var + RWKV_GN_EPS) * lnx_g + lnx_b
        bonus = _dot_exact_rhs(e["r"] * e["k2"] * r_k, bd) * e["v"]
        o_ref[bi] = (yn + bonus) * e["g"]


def _rwkv_group(proj3, mu_rkv, mu_l, wl, vec, rkv_block, lora_block):
    b, s, _ = proj3.shape
    L, C = RWKV_CHUNK, RWKV_W
    head = jnp.arange(C, dtype=I32) // HEAD
    bd = (head[:, None] == head[None, :]).astype(BF16)
    t = jnp.arange(L, dtype=I32)
    tri = (t[None, :] <= t[:, None]).astype(BF16)
    const = lambda shape: pl.BlockSpec(shape, lambda i, j: (0,) * len(shape))
    nb = RWKV_BATCH if b % RWKV_BATCH == 0 else 1
    return pl.pallas_call(
        _rwkv_kernel,
        grid=(b // nb, s // L),
        in_specs=[pl.BlockSpec((nb, L, 3 * C), lambda i, j: (i, j, rkv_block)),
                  pl.BlockSpec((nb, L, LORA_PAD), lambda i, j: (i, j, lora_block)),
                  const((1, 3 * C)), const((1, LORA_PAD)), const((LORA_PAD, 3 * C)),
                  const((8, C)), const((C, C)), const((L, L))],
        out_specs=pl.BlockSpec((nb, L, C), lambda i, j: (i, j, 0)),
        out_shape=jax.ShapeDtypeStruct((b, s, C), F32),
        scratch_shapes=[pltpu.VMEM((nb, RWKV_HEADS // 2, LANES, LANES), F32),
                        pltpu.VMEM((nb, 1, 3 * C), F32), pltpu.VMEM((nb, 1, LORA_PAD), F32)],
        compiler_params=_params("parallel", "arbitrary"),
        name="rwkv_group",
    )(proj3, proj3, mu_rkv, mu_l, wl, vec, bd, tri)


def _diff_attn_kernel(qi_ref, kj_ref, diag_ref, q_ref, k_ref, v_ref, slope_ref, lam_ref, g_ref, o_ref,
                      q2_scr, m_scr, acc_scr, *, lambda_init):
    blk = ATT_BLOCK
    group = k_ref.shape[0] // blk
    step_id = pl.program_id(2)
    qi, kj, diag = qi_ref[step_id], kj_ref[step_id], diag_ref[step_id]
    log2e = math.log2(math.e)

    @pl.when(kj == 0)
    def _():
        q = q_ref[...].astype(F32) * (HEAD ** -0.5 * log2e)
        m0 = lax.broadcasted_iota(I32, (blk, LANES), 1) < HEAD
        q2_scr[0:blk, :] = jnp.where(m0, q, 0.0).astype(BF16)
        q2_scr[blk:, :] = jnp.where(m0, 0.0, q).astype(BF16)
        m_scr[...] = jnp.full_like(m_scr, NEG_BIG)
        acc_scr[...] = jnp.zeros_like(acc_scr)

    def step(kh, on_diagonal):
        k = k_ref[kh * blk:(kh + 1) * blk, :].astype(BF16)
        v_ext = jnp.concatenate([v_ref[kh * blk:(kh + 1) * blk, :].astype(BF16), jnp.ones((blk, LANES), BF16)],
                                axis=1)
        koff = lax.broadcasted_iota(I32, (1, blk), 1)
        col_bias = (slope_ref[:, :1] * log2e) * ((kj * group + kh - qi) * blk + koff).astype(F32)
        rc = ATT_ROW_CHUNK
        n_chunks = 2 * blk // rc

        def scores(c):
            return lax.dot_general(q2_scr[c * rc:(c + 1) * rc, :], k, (((1,), (1,)), ((), ())),
                                   preferred_element_type=F32)

        s_next = scores(0)
        for c in range(n_chunks):
            rows = slice(c * rc, (c + 1) * rc)
            s = s_next + col_bias
            if c + 1 < n_chunks:
                s_next = scores(c + 1)
            if on_diagonal:
                qoff = (c * rc) % blk + lax.broadcasted_iota(I32, (rc, 1), 0)
                s = jnp.where(koff <= qoff, s, NEG_BIG)
            m_prev = m_scr[rows, :]
            m_new = jnp.maximum(m_prev, jnp.max(s, axis=-1, keepdims=True))
            alpha = jnp.exp2(m_prev - m_new)
            p = jnp.exp2(s - jnp.concatenate([m_new] * (blk // LANES), axis=1))
            acc_scr[rows, :] = (jnp.concatenate([alpha, alpha], axis=1) * acc_scr[rows, :]
                                + jnp.dot(p.astype(BF16), v_ext, preferred_element_type=F32))
            m_scr[rows, :] = m_new

    @pl.when(diag < 0)
    def _():
        for kh in range(group):
            step(kh, False)

    def last_step(n_full):
        for kh in range(n_full):
            step(kh, False)
        step(n_full, True)
        lam_v = lam_ref[...]
        lam = (jnp.exp(jnp.sum(lam_v[0:1] * lam_v[1:2], axis=-1, keepdims=True))
               - jnp.exp(jnp.sum(lam_v[2:3] * lam_v[3:4], axis=-1, keepdims=True)) + lambda_init)
        o2 = acc_scr[:, :LANES] / acc_scr[:, LANES:]
        o = o2[:blk] - lam * o2[blk:]
        o_ref[...] = _rms(o, g_ref[...]) * (1.0 - lambda_init)

    for n_full in range(group):
        pl.when(diag == n_full)(functools.partial(last_step, n_full))


def _diff_attention(proj3, lam_vecs, subln_g, lambda_init):
    b, s, _ = proj3.shape
    blk = ATT_BLOCK
    nb = s // blk
    group = math.gcd(ATT_KEY_GROUP, nb)
    nh = DIFF_HEADS
    slopes = jnp.exp2(-8.0 * jnp.arange(1, nh + 1, dtype=F32) / nh)
    slopes = jnp.broadcast_to(slopes[:, None, None], (nh, 1, LANES))
    steps = [(qi, kj, qi % group if kj == qi // group else -1) for qi in range(nb) for kj in range(qi // group + 1)]
    tabs = [jnp.asarray([st[i] for st in steps], I32) for i in range(3)]
    kernel = functools.partial(_diff_attn_kernel, lambda_init=lambda_init)
    grid_spec = pltpu.PrefetchScalarGridSpec(
        num_scalar_prefetch=3,
        grid=(b, nh, len(steps)),
        in_specs=[pl.BlockSpec((None, blk, LANES), lambda bi, h, t, qt, kt, dt: (bi, qt[t], h)),
                  pl.BlockSpec((None, group * blk, LANES), lambda bi, h, t, qt, kt, dt: (bi, kt[t], nh + h)),
                  pl.BlockSpec((None, group * blk, LANES), lambda bi, h, t, qt, kt, dt: (bi, kt[t], 2 * nh + h)),
                  pl.BlockSpec((None, 1, LANES), lambda bi, h, t, qt, kt, dt: (h, 0, 0)),
                  pl.BlockSpec((4, HEAD), lambda bi, h, t, qt, kt, dt: (0, 0)),
                  pl.BlockSpec((1, 2 * HEAD), lambda bi, h, t, qt, kt, dt: (0, 0))],
        out_specs=pl.BlockSpec((None, blk, LANES), lambda bi, h, t, qt, kt, dt: (bi, qt[t], h)),
        scratch_shapes=[pltpu.VMEM((2 * blk, LANES), BF16), pltpu.VMEM((2 * blk, LANES), F32),
                        pltpu.VMEM((2 * blk, 2 * LANES), F32)],
    )
    return pl.pallas_call(
        kernel,
        grid_spec=grid_spec,
        out_shape=jax.ShapeDtypeStruct((b, s, DIFF_W), F32),
        compiler_params=_params("parallel", "parallel", "arbitrary"),
        name="diff_attention",
    )(*tabs, proj3, proj3, proj3, slopes, lam_vecs, subln_g)


def _mem_kv_kernel(m_ref, g_ref, w_ref, k_ref, v_ref):
    d = m_ref.shape[-1]
    kv = _dot(_rms(m_ref[...], g_ref[...]), w_ref[...])
    k_ref[...] = kv[:, :d].astype(BF16)
    v_ref[...] = kv[:, d:].astype(BF16)


def _mem_kv(mem, g, w_ckv_bf16):
    b, m, d = mem.shape
    return pl.pallas_call(
        _mem_kv_kernel,
        grid=(b,),
        in_specs=[pl.BlockSpec((None, m, d), lambda i: (i, 0, 0)),
                  pl.BlockSpec((1, d), lambda i: (0, 0)),
                  pl.BlockSpec((d, 2 * d), lambda i: (0, 0))],
        out_specs=[pl.BlockSpec((None, m, d), lambda i: (i, 0, 0))] * 2,
        out_shape=[jax.ShapeDtypeStruct((b, m, d), BF16)] * 2,
        compiler_params=_params("parallel"),
        name="mem_kv",
    )(mem, g, w_ckv_bf16)


def _mid_kernel(x_ref, yr_ref, yd_ref, wo_ref, gc_ref, wcq_ref, km_ref, vm_ref, wco_ref, gf_ref,
                wr_hi_ref, wr_lo_ref, br_ref, tri_ref,
                x2_ref, hn_ref, ri_ref, rg_ref, cnt_ref, carry_scr):
    tm, d = x_ref.shape
    first = jnp.logical_and(pl.program_id(0) == 0, pl.program_id(1) == 0)

    @pl.when(first)
    def _():
        carry_scr[...] = jnp.zeros_like(carry_scr)

    rc = tri_ref.shape[0]
    chains = range(tm // rc)
    rows = [slice(c * rc, (c + 1) * rc) for c in chains]
    half = yr_ref.shape[-1]
    x1 = [x_ref[r, :] + _dot(yr_ref[r, :], wo_ref[:half, :]) + _dot(yd_ref[r, :], wo_ref[half:, :]) for r in rows]

    q = [_dot(_rms(x1[c], gc_ref[...]), wcq_ref[...]) for c in chains]
    hd = d // CROSS_HEADS
    outs = [[] for _ in chains]

    def scores(h):
        sl = slice(h * hd, (h + 1) * hd)
        return [_dot_nt(q[c][:, sl], km_ref[:, sl]) * (hd ** -0.5) for c in chains]

    s_next = scores(0)
    for h in range(CROSS_HEADS):
        sl = slice(h * hd, (h + 1) * hd)
        s = s_next
        if h + 1 < CROSS_HEADS:
            s_next = scores(h + 1)
        p = [jnp.exp(s[c] - jnp.max(s[c], axis=-1, keepdims=True)) for c in chains]
        p = [p[c] / jnp.sum(p[c], axis=-1, keepdims=True) for c in chains]
        for c in chains:
            outs[c].append(_dot(p[c], vm_ref[:, sl]))
    x2 = [x1[c] + _dot(jnp.concatenate(outs[c], axis=1), wco_ref[...]) for c in chains]
    for c in chains:
        x2_ref[rows[c], :] = x2[c]

    hn = [_rms(x2[c], gf_ref[...]) for c in chains]
    _store_token_tiles(hn_ref, jnp.concatenate(hn, axis=0))
    hi = [hn[c].astype(BF16) for c in chains]
    lo = [(hn[c] - hi[c].astype(F32)).astype(BF16) for c in chains]
    vals = [(jnp.dot(hi[c], wr_hi_ref[...], preferred_element_type=F32)
             + jnp.dot(lo[c], wr_hi_ref[...], preferred_element_type=F32)
             + jnp.dot(hi[c], wr_lo_ref[...], preferred_element_type=F32) + br_ref[...]) for c in chains]
    lane = lax.broadcasted_iota(I32, (rc, LANES), 1)
    tops, idxs, hots = [[] for _ in chains], [[] for _ in chains], [[] for _ in chains]
    for _ in range(TOP_K):
        for c in chains:
            mx = jnp.max(vals[c], axis=-1, keepdims=True)
            idx = jnp.min(jnp.where(vals[c] == mx, lane, LANES), axis=-1, keepdims=True)
            hot = lane == idx
            vals[c] = jnp.where(hot, -jnp.inf, vals[c])
            tops[c].append(mx)
            idxs[c].append(idx)
            hots[c].append(hot)
    for c in chains:
        es = [jnp.exp(t - tops[c][0]) for t in tops[c]]
        denom = es[0] + es[1] + es[2] + es[3]
        sel = jnp.zeros((rc, LANES), F32)
        for hot in hots[c]:
            sel = sel + hot.astype(F32)
        before = jnp.dot(tri_ref[...], sel.astype(BF16), preferred_element_type=F32) + carry_scr[...]
        carry_scr[...] = carry_scr[...] + jnp.sum(sel, axis=0, keepdims=True)
        ri = jnp.zeros((rc, LANES), I32)
        rg = jnp.zeros((rc, LANES), F32)
        for j in range(TOP_K):
            rank = jnp.sum(jnp.where(hots[c][j], before, 0.0), axis=-1, keepdims=True).astype(I32)
            ri = jnp.where(lane == j, idxs[c][j], ri)
            ri = jnp.where(lane == TOP_K + j, rank, ri)
            rg = jnp.where(lane == j, es[j] / denom, rg)
        ri_ref[rows[c], :] = ri
        rg_ref[rows[c], :] = rg
    cnt_ref[...] = jnp.broadcast_to(carry_scr[...], cnt_ref.shape)


def _mid_stage(x, y_rwkv, y_diff, w_out, g_cross, w_cq, k_mem, v_mem, w_co, g_ffn, w_router, b_router):
    b, s, d = x.shape
    tm = min(MID_TM, s)
    m = k_mem.shape[1]
    half = y_rwkv.shape[-1]
    e = w_router.shape[1]
    wr = jnp.zeros((d, LANES), F32).at[:, :e].set(w_router)
    wr_hi = wr.astype(BF16)
    wr_lo = (wr - wr_hi.astype(F32)).astype(BF16)
    br = jnp.full((1, LANES), NEG_BIG, F32).at[0, :e].set(b_router)
    rc = min(MID_CHAIN_ROWS, tm)
    t = jnp.arange(rc, dtype=I32)
    tri = (t[None, :] < t[:, None]).astype(BF16)
    tile = lambda w: pl.BlockSpec((None, tm, w), lambda i, j: (i, j, 0))
    const = lambda shape: pl.BlockSpec(shape, lambda i, j: (0,) * len(shape))
    return pl.pallas_call(
        _mid_kernel,
        grid=(b, s // tm),
        in_specs=[tile(d), tile(half), tile(half), const((d, d)), const((1, d)), const((d, d)),
                  pl.BlockSpec((None, m, d), lambda i, j: (i, 0, 0)),
                  pl.BlockSpec((None, m, d), lambda i, j: (i, 0, 0)),
                  const((d, d)), const((1, d)), const((d, LANES)), const((d, LANES)),
                  const((1, LANES)), const((rc, rc))],
        out_specs=[tile(d), pl.BlockSpec((tm * TILE_ROWS, LANES), lambda i, j: (i * (s // tm) + j, 0)),
                   tile(LANES), tile(LANES), const((8, LANES))],
        out_shape=[jax.ShapeDtypeStruct((b, s, d), F32), jax.ShapeDtypeStruct((b * s * TILE_ROWS, LANES), F32),
                   jax.ShapeDtypeStruct((b, s, LANES), I32), jax.ShapeDtypeStruct((b, s, LANES), F32),
                   jax.ShapeDtypeStruct((8, LANES), F32)],
        scratch_shapes=[pltpu.VMEM((1, LANES), F32)],
        compiler_params=_params("arbitrary", "arbitrary"),
        name="outproj_cross_router",
    )(x, y_rwkv, y_diff, w_out.astype(BF16), g_cross, w_cq.astype(BF16), k_mem, v_mem,
      w_co.astype(BF16), g_ffn, wr_hi, wr_lo, br, tri)


def _expert_kernel(be_ref, nblk_ref, nused_ref, x_ref, w1_hbm, b1_ref, w2_hbm, b2_ref, o_ref,
                   w1_stage, w2_stage, w1_scr, w2_scr, w2i_scr, sem):
    i = pl.program_id(0)
    n_used = nused_ref[0]
    used = i < n_used
    expert = be_ref[i]
    new_expert = jnp.logical_or(i == 0, expert != be_ref[jnp.maximum(i - 1, 0)])
    d, f = w1_stage.shape[0], w2_stage.shape[0]

    def weight_copies(ex):
        return (pltpu.make_async_copy(w1_hbm.at[ex], w1_stage, sem.at[0]),
                pltpu.make_async_copy(w2_hbm.at[ex], w2_stage, sem.at[1]))

    @pl.when(jnp.logical_and(used, i == 0))
    def _():
        for cp in weight_copies(expert):
            cp.start()

    @pl.when(jnp.logical_and(used, new_expert))
    def _():
        for cp in weight_copies(expert):
            cp.wait()
        step = 128

        def cast_rows(c, carry):
            rows = pl.ds(pl.multiple_of(c * step, step), step)
            w1_scr[rows, :] = w1_stage[rows, :].astype(BF16)
            return carry

        lax.fori_loop(0, d // step, cast_rows, 0)
        for cb in range(w2_stage.shape[1] // LANES):
            cols = slice(cb * LANES, (cb + 1) * LANES)
            w2i_scr[cb, pl.ds(0, f // 2, stride=2), :] = w2_stage[:f // 2, cols]
            w2i_scr[cb, pl.ds(1, f // 2, stride=2), :] = w2_stage[f // 2:, cols]
            w2_scr[:, cols] = w2i_scr[cb].astype(BF16)
        next_run = i + nblk_ref[expert]

        @pl.when(next_run < n_used)
        def _():
            for cp in weight_copies(be_ref[jnp.minimum(next_run, be_ref.shape[0] - 1)]):
                cp.start()

    @pl.when(used)
    def _():
        rc = MOE_CHAIN_ROWS
        even = lax.broadcasted_iota(I32, (rc, LANES), 1) % 2 == 0
        cw = 4 * LANES
        n_stage = f // cw
        stages = [(h, j) for h in range(MOE_ROWS // rc) for j in range(n_stage)]
        xs = {}

        def hidden_pair(stage):
            h, j = stage
            if h not in xs:
                xs[h] = _load_token_tiles(x_ref, rc, first=h * rc).astype(BF16)
            lo, hi = slice(j * cw, (j + 1) * cw), slice(f + j * cw, f + (j + 1) * cw)
            return (jnp.dot(xs[h], w1_scr[:, lo], preferred_element_type=F32) + b1_ref[:, lo],
                    jnp.dot(xs[h], w1_scr[:, hi], preferred_element_type=F32) + b1_ref[:, hi])

        def act_block(h):
            lin = jnp.clip(pltpu.roll(h, LANES - 1, axis=1), -SWIGLU_LIMIT, SWIGLU_LIMIT)
            glu = jnp.minimum(h, SWIGLU_LIMIT)
            return jnp.where(even, glu * jax.nn.sigmoid(SWIGLU_ALPHA * glu) * (lin + 1.0), 0.0)

        y = None
        pending = hidden_pair(stages[0])
        for n, (h, j) in enumerate(stages):
            ha, hb = pending
            if n + 1 < len(stages):
                pending = hidden_pair(stages[n + 1])
            packed = jnp.concatenate(
                [act_block(ha[:, c * LANES:(c + 1) * LANES])
                 + pltpu.roll(act_block(hb[:, c * LANES:(c + 1) * LANES]), 1, axis=1) for c in range(cw // LANES)],
                axis=1)
            part = jnp.dot(packed.astype(BF16), w2_scr[j * cw:(j + 1) * cw, :], preferred_element_type=F32)
            y = b2_ref[...] + part if j == 0 else y + part
            if j == n_stage - 1:
                _store_token_tiles(o_ref, y, first=h * rc)

    @pl.when(jnp.logical_not(used))
    def _():
        o_ref[...] = jnp.zeros_like(o_ref)


def _expert_ffn(x_sorted, block_e, blocks_per_expert, n_used, w1, b1, w2, b2):
    e, d, f2 = w1.shape
    r = MOE_ROWS
    n_rows = block_e.shape[0] * r
    grid_spec = pltpu.PrefetchScalarGridSpec(
        num_scalar_prefetch=3,
        grid=(n_rows // r,),
        in_specs=[pl.BlockSpec((r * TILE_ROWS, LANES), lambda i, be, nb, nu: (i, 0)),
                  pl.BlockSpec(memory_space=pl.ANY),
                  pl.BlockSpec((None, 1, f2), lambda i, be, nb, nu: (be[i], 0, 0)),
                  pl.BlockSpec(memory_space=pl.ANY),
                  pl.BlockSpec((None, 1, d), lambda i, be, nb, nu: (be[i], 0, 0))],
        out_specs=pl.BlockSpec((r * TILE_ROWS, LANES), lambda i, be, nb, nu: (i, 0)),
        scratch_shapes=[pltpu.VMEM((d, f2), F32), pltpu.VMEM((f2 // 2, d), F32),
                        pltpu.VMEM((d, f2), BF16), pltpu.VMEM((f2 // 2, d), BF16),
                        pltpu.VMEM((d // LANES, f2 // 2, LANES), F32), pltpu.SemaphoreType.DMA((2,))],
    )
    return pl.pallas_call(
        _expert_kernel,
        grid_spec=grid_spec,
        out_shape=jax.ShapeDtypeStruct((n_rows * TILE_ROWS, LANES), F32),
        compiler_params=_params("arbitrary"),
        name="expert_ffn",
    )(block_e, blocks_per_expert, n_used, x_sorted, w1, b1.reshape(e, 1, f2), w2, b2.reshape(e, 1, d))


def _dispatch_kernel(pad_lo_ref, pad_hi_ref, dst_ref, li_ref, hn_ref, xs_hbm, buf, zero_scr, sem, pad_sem, *, tm):
    @pl.when(pl.program_id(0) == 0)
    def _():
        zero_scr[...] = jnp.zeros_like(zero_scr)

        def per_range(e, carry):
            pos, hi = pad_lo_ref[e], pad_hi_ref[e]
            for size in ZERO_FILL_ROWS:
                n = lax.div(hi - pos, size)

                def pad_copy(k, pos=pos, size=size):
                    first = pl.multiple_of((pos + k * size) * TILE_ROWS, TILE_ROWS)
                    return pltpu.make_async_copy(zero_scr.at[pl.ds(0, size * TILE_ROWS)],
                                                 xs_hbm.at[pl.ds(first, size * TILE_ROWS)], pad_sem)

                lax.fori_loop(0, n, lambda k, c, cp=pad_copy: (cp(k).start(), c)[1], 0)
                lax.fori_loop(0, n, lambda k, c, cp=pad_copy: (cp(k).wait(), c)[1], 0)
                pos = pos + n * size
            return carry

        lax.fori_loop(0, pad_lo_ref.shape[0], per_range, 0)

        buf[...] = jnp.zeros_like(buf)

    i = pl.program_id(0)
    n_chunks = dst_ref.shape[1]
    chunk = SEG_CHUNK * TILE_ROWS
    slot_rows = n_chunks * chunk
    slot = i % 2
    base = slot * slot_rows

    def place(t, carry):
        tile = hn_ref[pl.ds(pl.multiple_of(t * TILE_ROWS, TILE_ROWS), TILE_ROWS), :]
        for j in range(TOP_K):
            buf[pl.ds(pl.multiple_of(li_ref[0, t * TOP_K + j], TILE_ROWS), TILE_ROWS), :] = tile
        return carry

    lax.fori_loop(0, tm, place, 0, unroll=8)

    def slot_wait(sl):
        whole = buf.at[pl.ds(pl.multiple_of(sl * slot_rows, TILE_ROWS), slot_rows)]
        pltpu.make_async_copy(whole, whole, sem.at[sl]).wait()

    @pl.when(i > 0)
    def _():
        slot_wait(1 - slot)

    def start(q, carry):
        pltpu.make_async_copy(buf.at[pl.ds(pl.multiple_of(base + q * chunk, TILE_ROWS), chunk)],
                              xs_hbm.at[pl.ds(pl.multiple_of(dst_ref[0, q], TILE_ROWS), chunk)],
                              sem.at[slot]).start(priority=1)
        return carry

    lax.fori_loop(0, n_chunks, start, 0, unroll=8)

    @pl.when(i == pl.num_programs(0) - 1)
    def _():
        slot_wait(slot)


def _dispatch(hn_tiles, chunk_dst, local_rows, pad_lo, pad_hi, n_rows):
    t = hn_tiles.shape[0] // TILE_ROWS
    tm = COMBINE_TM
    steps = t // tm
    n_chunks = chunk_dst.shape[-1]
    smem = lambda w: pl.BlockSpec((None, 1, w), lambda i, lo, hi: (i, 0, 0), memory_space=pltpu.SMEM)
    grid_spec = pltpu.PrefetchScalarGridSpec(
        num_scalar_prefetch=2,
        grid=(steps,),
        in_specs=[smem(n_chunks), smem(tm * TOP_K), pl.BlockSpec((tm * TILE_ROWS, LANES), lambda i, lo, hi: (i, 0))],
        out_specs=pl.BlockSpec(memory_space=pl.ANY),
        scratch_shapes=[pltpu.VMEM((2 * n_chunks * SEG_CHUNK * TILE_ROWS, LANES), F32),
                        pltpu.VMEM((ZERO_FILL_ROWS[0] * TILE_ROWS, LANES), F32), pltpu.SemaphoreType.DMA((2,)),
                        pltpu.SemaphoreType.DMA(())],
    )
    return pl.pallas_call(
        functools.partial(_dispatch_kernel, tm=tm),
        grid_spec=grid_spec,
        out_shape=jax.ShapeDtypeStruct((n_rows * TILE_ROWS, LANES), F32),
        compiler_params=_params("arbitrary"),
        name="moe_dispatch",
    )(pad_lo, pad_hi, chunk_dst.reshape(steps, 1, n_chunks), local_rows.reshape(steps, 1, tm * TOP_K), hn_tiles)


def _combine_seg_kernel(src_cur_ref, src_nxt_ref, li_ref, gate_ref, x_ref, g_ref, ys_hbm, o_ref, buf, acc_scr, sem,
                        *, tm, final):
    i = pl.program_id(0)
    n_chunks = src_cur_ref.shape[1]
    chunk = SEG_CHUNK * TILE_ROWS
    slot_rows = n_chunks * chunk

    def fetch(src_ref, slot):
        def start(q, carry):
            pltpu.make_async_copy(ys_hbm.at[pl.ds(pl.multiple_of(src_ref[0, q], TILE_ROWS), chunk)],
                                  buf.at[pl.ds(pl.multiple_of(slot * slot_rows + q * chunk, TILE_ROWS), chunk)],
                                  sem.at[slot]).start(priority=1)
            return carry
        lax.fori_loop(0, n_chunks, start, 0, unroll=8)

    @pl.when(i == 0)
    def _():
        fetch(src_cur_ref, 0)

    @pl.when(i + 1 < pl.num_programs(0))
    def _():
        fetch(src_nxt_ref, (i + 1) % 2)

    slot = i % 2
    base = slot * slot_rows
    whole = buf.at[pl.ds(pl.multiple_of(base, TILE_ROWS), slot_rows)]
    pltpu.make_async_copy(whole, whole, sem.at[slot]).wait()

    def token(t, carry):
        acc = jnp.zeros((TILE_ROWS, LANES), F32)
        for j in range(TOP_K):
            s = t * TOP_K + j
            row = pl.multiple_of(li_ref[0, s], TILE_ROWS)
            acc = acc + gate_ref[0, s] * buf[pl.ds(row, TILE_ROWS), :]
        acc_scr[pl.ds(pl.multiple_of(t * TILE_ROWS, TILE_ROWS), TILE_ROWS), :] = acc
        return carry

    lax.fori_loop(0, tm, token, 0, unroll=8)
    y = x_ref[...] + _load_token_tiles(acc_scr, tm)
    o_ref[...] = _rms(y, g_ref[...]) if final else y


def _combine_segments(x2d, gates, chunk_src, li, y_sorted, g, final):
    t, d = x2d.shape
    tm = COMBINE_TM
    steps = t // tm
    n_chunks = chunk_src.shape[-1]
    src3 = chunk_src.reshape(steps, 1, n_chunks)
    smem = lambda w, imap: pl.BlockSpec((None, 1, w), imap, memory_space=pltpu.SMEM)
    return pl.pallas_call(
        functools.partial(_combine_seg_kernel, tm=tm, final=final),
        grid=(steps,),
        in_specs=[smem(n_chunks, lambda i: (i, 0, 0)),
                  smem(n_chunks, lambda i: (jnp.minimum(i + 1, steps - 1), 0, 0)),
                  smem(tm * TOP_K, lambda i: (i, 0, 0)), smem(tm * TOP_K, lambda i: (i, 0, 0)),
                  pl.BlockSpec((tm, d), lambda i: (i, 0)),
                  pl.BlockSpec((1, d), lambda i: (0, 0)), pl.BlockSpec(memory_space=pl.ANY)],
        out_specs=pl.BlockSpec((tm, d), lambda i: (i, 0)),
        out_shape=jax.ShapeDtypeStruct((t, d), F32),
        scratch_shapes=[pltpu.VMEM((2 * n_chunks * SEG_CHUNK * TILE_ROWS, LANES), F32),
                        pltpu.VMEM((tm * TILE_ROWS, LANES), F32), pltpu.SemaphoreType.DMA((2,))],
        compiler_params=_params("arbitrary"),
        name="combine_norm",
    )(src3, src3, li.reshape(steps, 1, tm * TOP_K), gates.reshape(steps, 1, tm * TOP_K), x2d, g, y_sorted)


def _layer(x, mem, lyr, final, p):
    b, s, d = x.shape
    t = b * s
    c = RWKV_W
    rwkv_in = 3 * c + LORA_W

    w_in = p["w_in"][lyr]
    w_cat = jnp.concatenate([w_in[:, rwkv_in:], w_in[:, :3 * c], w_in[:, 3 * c:rwkv_in],
                             jnp.zeros((d, LORA_PAD - LORA_W), F32)], axis=1).astype(BF16)
    diff_cols = 3 * DIFF_W
    proj_diff, proj_rwkv = _norm_inproj(x.reshape(t, d), p["norm_mix_g"][lyr][None], w_cat, diff_cols)
    proj_diff, proj_rwkv = proj_diff.reshape(b, s, -1), proj_rwkv.reshape(b, s, -1)

    mu = p["shift_mu"][lyr]
    mu_l = jnp.zeros((1, LORA_PAD), F32).at[0, :LORA_W].set(mu[3 * c:])
    wl = jnp.zeros((LORA_PAD, 3 * c), F32)
    wl = wl.at[:DECAY_LORA, :c].set(p["w_decay_up"][lyr])
    wl = wl.at[DECAY_LORA:DECAY_LORA + ICLR_LORA, c:2 * c].set(p["w_iclr_up"][lyr])
    wl = wl.at[DECAY_LORA + ICLR_LORA:LORA_W, 2 * c:].set(p["w_gate_up"][lyr])
    vec = jnp.stack([p["w0"][lyr], p["a0"][lyr], p["k_k"][lyr], p["k_a"][lyr], p["r_k"][lyr].reshape(c),
                     p["lnx_g"][lyr], p["lnx_b"][lyr], jnp.zeros((c,), F32)])
    y_rwkv = _rwkv_group(proj_rwkv, mu[None, :3 * c], mu_l, wl.astype(BF16), vec,
                         rkv_block=0, lora_block=3 * c // LORA_PAD)

    lambda_init = 0.8 - 0.6 * math.exp(-0.3 * lyr)
    lam_vecs = jnp.stack([p["lambda_q1"][lyr], p["lambda_k1"][lyr], p["lambda_q2"][lyr], p["lambda_k2"][lyr]])
    y_diff = _diff_attention(proj_diff, lam_vecs, p["subln_g"][lyr][None], lambda_init)

    k_mem, v_mem = _mem_kv(mem, p["norm_mem_g"][lyr][None], p["w_ckv"][lyr].astype(BF16))
    x2, hn, route_i, route_g, counts = _mid_stage(
        x, y_rwkv, y_diff, p["w_out"][lyr], p["norm_cross_g"][lyr][None], p["w_cq"][lyr], k_mem, v_mem,
        p["w_co"][lyr], p["norm_ffn_g"][lyr][None], p["w_router"][lyr], p["b_router"][lyr])

    e = N_EXPERTS
    r = MOE_ROWS
    n_blocks = (t * TOP_K) // r + e + 1
    n_slots = COMBINE_TM * TOP_K // SEG_CHUNK + e
    n_rows = n_blocks * r + n_slots * SEG_CHUNK
    route_i = route_i.reshape(t, LANES)
    idx, rank = route_i[:, :TOP_K], route_i[:, TOP_K:2 * TOP_K]
    cnt = counts[0, :e].astype(I32)
    padded = (cnt + SEG_CHUNK - 1 + r - 1) // r * r
    pad_end = jnp.cumsum(padded)
    pad_start = pad_end - padded
    block_start = jnp.arange(n_blocks, dtype=I32) * r
    block_e = jnp.minimum(jnp.sum((block_start[:, None] >= pad_end[None, :]).astype(I32), axis=1), e - 1)
    n_used = pad_end[-1:] // r

    onehot = idx[:, :, None] == jnp.arange(e, dtype=I32)
    tmc = COMBINE_TM
    nt = t // tmc
    per_tile = jnp.sum(onehot.reshape(nt, tmc * TOP_K, e), axis=1, dtype=I32)
    before = jnp.cumsum(per_tile, axis=0) - per_tile
    n_chunks = (per_tile + SEG_CHUNK - 1) // SEG_CHUNK
    chunk_end = jnp.cumsum(n_chunks, axis=1)
    chunk_first = chunk_end - n_chunks
    shift = jnp.repeat(chunk_first * SEG_CHUNK - before, tmc, axis=0)
    local_row = (jnp.sum(jnp.where(onehot, shift[:, None, :], 0), axis=-1) + rank) * TILE_ROWS
    slot_rows = n_slots * SEG_CHUNK * TILE_ROWS
    local_row = (local_row.reshape(nt, -1) + (jnp.arange(nt, dtype=I32) % 2 * slot_rows)[:, None]).reshape(-1)
    slots = jnp.arange(n_slots, dtype=I32)
    owner = jnp.sum((slots[None, :, None] >= chunk_end[:, None, :]).astype(I32), axis=-1)
    own = owner[:, :, None] == jnp.arange(e, dtype=I32)
    seg_row = jnp.sum(jnp.where(own, (pad_start[None, :] + before - chunk_first * SEG_CHUNK)[:, None, :], 0), axis=-1)
    chunk_row = seg_row + slots[None, :] * SEG_CHUNK
    chunk_dst = jnp.where(owner < e, chunk_row, n_blocks * r + slots[None, :] * SEG_CHUNK) * TILE_ROWS
    chunk_src = jnp.where(owner < e, chunk_row, 0) * TILE_ROWS

    unused_lo = jnp.concatenate([pad_start + cnt, pad_end[-1:]])
    unused_hi = jnp.concatenate([pad_end, jnp.full((1,), n_rows, I32)])
    x_sorted = _dispatch(hn, chunk_dst, local_row, unused_lo, unused_hi, n_rows)
    y_sorted = _expert_ffn(x_sorted, block_e, padded // r, n_used, p["w1"][lyr], p["b1"][lyr], p["w2"][lyr],
                           p["b2"][lyr])
    gates = route_g.reshape(t, LANES)[:, :TOP_K]
    return _combine_segments(x2.reshape(t, d), gates, chunk_src, local_row, y_sorted,
                             p["norm_final_g"][None], final).reshape(b, s, d)


def kernel(x, mem, norm_mix_g, w_in, shift_mu, w0, w_decay_up, a0, w_iclr_up, w_gate_up, k_k, k_a, r_k,
           lnx_g, lnx_b, lambda_q1, lambda_k1, lambda_q2, lambda_k2, subln_g, w_out, norm_cross_g,
           norm_mem_g, w_cq, w_ckv, w_co, norm_ffn_g, w_router, b_router, w1, b1, w2, b2, norm_final_g):
    p = dict(norm_mix_g=norm_mix_g, w_in=w_in, shift_mu=shift_mu, w0=w0, w_decay_up=w_decay_up, a0=a0,
             w_iclr_up=w_iclr_up, w_gate_up=w_gate_up, k_k=k_k, k_a=k_a, r_k=r_k, lnx_g=lnx_g, lnx_b=lnx_b,
             lambda_q1=lambda_q1, lambda_k1=lambda_k1, lambda_q2=lambda_q2, lambda_k2=lambda_k2,
             subln_g=subln_g, w_out=w_out, norm_cross_g=norm_cross_g, norm_mem_g=norm_mem_g, w_cq=w_cq,
             w_ckv=w_ckv, w_co=w_co, norm_ffn_g=norm_ffn_g, w_router=w_router, b_router=b_router,
             w1=w1, b1=b1, w2=w2, b2=b2, norm_final_g=norm_final_g)
    depth = w_in.shape[0]
    b, s, d = x.shape
    assert d == TILE_ROWS * LANES and w_in.shape[2] == 3 * RWKV_W + LORA_W + 3 * DIFF_W
    assert w_router.shape[2] == N_EXPERTS and w1.shape[2:] == (d, 2 * w2.shape[2])
    assert s % max(ATT_BLOCK, MID_TM, RWKV_CHUNK) == 0 and (b * s) % PROJ_TM == 0
    for lyr in range(depth):
        x = _layer(x, mem, lyr, lyr == depth - 1, p)
    return x
```
